```python
import math
import jax, jax.numpy as jnp
from jax import lax
import numpy as np

D_MODEL = 1024
BATCH = 1
SEQ = 16384
DEPTH = 1
DEC_BATCH = 8
DEC_SEQ = 16
PAST_LEN = 4096

CHUNK = 64
N_META = 16
Q_BLOCK = 128
H_SB = 16
DH_SB = 64
W_SB = H_SB * DH_SB
H_DIFF = 8
DH_DIFF = 64
DV_DIFF = 2 * DH_DIFF
W_DIFF_QK = H_DIFF * 2 * DH_DIFF
W_DIFF_V = H_DIFF * DV_DIFF
N_IN = 3 * W_SB + 2 * W_DIFF_QK + W_DIFF_V + 2 * D_MODEL
N_EXPERTS = 32
TOP_K = 4
D_FF = 1024
SWIGLU_LIMIT = 7.0
SWIGLU_ALPHA = 1.702
MOE_BLOCK = 128
NORM_EPS = 1e-5

kernel_name = "streaming_stickbreak_diffattn_moe_step"


def rms_norm(x, g):
    xf = x.astype(jnp.float32)
    y = xf * lax.rsqrt(jnp.mean(xf * xf, axis=-1, keepdims=True) + NORM_EPS)
    return (y * g.astype(jnp.float32)).astype(x.dtype)


def split_heads(a, n_heads):
    return a.reshape(*a.shape[:-1], n_heads, a.shape[-1] // n_heads)


def alibi_slopes():
    return jnp.exp2(-8.0 * jnp.arange(1, H_DIFF + 1, dtype=jnp.float32) / H_DIFF)


def project_mixers(h, g_norm, w_in):
    xn = rms_norm(h, g_norm)
    proj = xn @ w_in
    sizes = [W_SB, W_SB, W_SB, W_DIFF_QK, W_DIFF_QK, W_DIFF_V, 2 * D_MODEL]
    points, acc = [], 0
    for sz in sizes[:-1]:
        acc += sz
        points.append(acc)
    parts = jnp.split(proj, points, axis=-1)
    q_sb = split_heads(parts[0], H_SB)
    k_sb = split_heads(parts[1], H_SB)
    v_sb = split_heads(parts[2], H_SB)
    q_d = split_heads(parts[3], H_DIFF)
    k_d = split_heads(parts[4], H_DIFF)
    v_d = split_heads(parts[5], H_DIFF)
    gates = jax.nn.sigmoid(parts[6]).reshape(*h.shape[:-1], 2, D_MODEL)
    return q_sb, k_sb, v_sb, q_d, k_d, v_d, gates


def stick_breaking(q, k, v, q_pos, k_pos):
    z = jnp.einsum('bqhd,bkhd->bhqk', q, k).astype(jnp.float32) * (DH_SB ** -0.5)
    visible = k_pos[None, :] < q_pos[:, None]
    log_keep = jnp.where(visible, jax.nn.log_sigmoid(-z), 0.0)
    later = lax.cumsum(log_keep, axis=3, reverse=True) - log_keep
    w = jnp.where(visible, jnp.exp(jax.nn.log_sigmoid(z) + later), 0.0)
    return jnp.einsum('bhqk,bkhd->bqhd', w.astype(v.dtype), v)


def diff_lambda_value(lam_p, lambda_init):
    lp = lam_p.astype(jnp.float32)
    return jnp.exp(jnp.sum(lp[0] * lp[1])) - jnp.exp(jnp.sum(lp[2] * lp[3])) + lambda_init


def diff_attention(q, k, v, q_pos, k_pos, q_chunk, k_chunk, lam, subln_g, lambda_init):
    q1, q2 = jnp.split(q, 2, axis=-1)
    k1, k2 = jnp.split(k, 2, axis=-1)
    dist = jnp.abs(q_pos[:, None] - k_pos[None, :]).astype(jnp.float32)
    bias = -alibi_slopes()[:, None, None] * dist
    visible = k_chunk[None, :] <= q_chunk[:, None]

    def softmax_map(qq, kk):
        s = jnp.einsum('bqhd,bkhd->bhqk', qq, kk).astype(jnp.float32) * (DH_DIFF ** -0.5) + bias
        return jax.nn.softmax(jnp.where(visible, s, -jnp.inf), axis=-1)

    a = softmax_map(q1, k1) - lam * softmax_map(q2, k2)
    o = jnp.einsum('bhqk,bkhd->bqhd', a.astype(v.dtype), v)
    return rms_norm(o, subln_g) * (1.0 - lambda_init)


def prompt_attention(q_sb, k_sb, v_sb, q_d, k_d, v_d, lam, subln_g, lambda_init):
    b, t = q_sb.shape[:2]
    n_blk = -(-t // Q_BLOCK)
    t_pad = n_blk * Q_BLOCK
    k_pos = jnp.arange(t)
    k_chunk = (k_pos - N_META) // CHUNK

    def to_blocks(a):
        a = jnp.pad(a, ((0, 0), (0, t_pad - t), (0, 0), (0, 0)))
        return jnp.moveaxis(a.reshape(b, n_blk, Q_BLOCK, *a.shape[2:]), 1, 0)

    def from_blocks(o):
        return jnp.moveaxis(o, 0, 1).reshape(b, t_pad, *o.shape[3:])[:, :t]

    q_pos = jnp.arange(t_pad).reshape(n_blk, Q_BLOCK)

    def one_block(args):
        qs, qd, qp = args
        o_sb = stick_breaking(qs, k_sb, v_sb, qp, k_pos)
        o_d = diff_attention(qd, k_d, v_d, qp, k_pos, (qp - N_META) // CHUNK, k_chunk,
                             lam, subln_g, lambda_init)
        return o_sb, o_d

    o_sb, o_d = lax.map(one_block, (to_blocks(q_sb), to_blocks(q_d), q_pos))
    return from_blocks(o_sb), from_blocks(o_d)


def merge_branches(h, o_sb, o_d, gates, w_br_sb, w_br_diff, w_out):
    y_sb = o_sb.reshape(*o_sb.shape[:2], -1) @ w_br_sb
    y_d = o_d.reshape(*o_d.shape[:2], -1) @ w_br_diff
    return h + (gates[..., 0, :] * y_sb + gates[..., 1, :] * y_d) @ w_out


def moe_ffn(x, w_router, b_router, w_gu, b_gu, w_down, b_down):
    n = x.shape[0]
    logits = (x @ w_router + b_router).astype(jnp.float32)
    top_logit, top_e = lax.top_k(logits, TOP_K)
    gate = jax.nn.softmax(top_logit, axis=-1)
    flat_e = top_e.reshape(-1)
    flat_tok = jnp.arange(n * TOP_K, dtype=jnp.int32) // TOP_K
    flat_gate = gate.reshape(-1)
    order = jnp.argsort(flat_e)
    se = flat_e[order]
    counts = jnp.zeros((N_EXPERTS,), jnp.int32).at[flat_e].add(1)
    padded = ((counts + MOE_BLOCK - 1) // MOE_BLOCK) * MOE_BLOCK
    pend = jnp.cumsum(padded)
    pstart = pend - padded
    start = jnp.cumsum(counts) - counts
    rank = jnp.arange(n * TOP_K, dtype=jnp.int32) - start[se]
    dest = pstart[se] + rank
    n_rows = -(-(n * TOP_K + N_EXPERTS * (MOE_BLOCK - 1)) // MOE_BLOCK) * MOE_BLOCK
    n_blocks = n_rows // MOE_BLOCK
    row_tok = jnp.full((n_rows,), n, jnp.int32).at[dest].set(flat_tok[order])
    row_gate = jnp.zeros((n_rows,), x.dtype).at[dest].set(flat_gate[order].astype(x.dtype))
    block_e = jnp.clip(jnp.searchsorted(pend, jnp.arange(n_blocks, dtype=jnp.int32) * MOE_BLOCK,
                                        side='right'), 0, N_EXPERTS - 1)
    x_pad = jnp.concatenate([x, jnp.zeros((1, x.shape[1]), x.dtype)], axis=0)
    xb = x_pad[row_tok].reshape(n_blocks, MOE_BLOCK, x.shape[1])

    def expert_block(args):
        xblk, e = args
        gu = xblk @ w_gu[e] + b_gu[e]
        g, u = jnp.split(gu, 2, axis=-1)
        g = jnp.minimum(g, SWIGLU_LIMIT)
        u = jnp.clip(u, -SWIGLU_LIMIT, SWIGLU_LIMIT)
        return ((u + 1.0) * (g * jax.nn.sigmoid(SWIGLU_ALPHA * g))) @ w_down[e] + b_down[e]

    yb = lax.map(expert_block, (xb, block_e)).reshape(n_rows, x.shape[1])
    return jax.ops.segment_sum(yb * row_gate[:, None], row_tok, num_segments=n + 1)[:n]


def ffn_block(h, g, w_router, b_router, w_gu, b_gu, w_down, b_down):
    hn = rms_norm(h, g)
    y = moe_ffn(hn.reshape(-1, h.shape[-1]), w_router, b_router, w_gu, b_gu, w_down, b_down)
    return h + y.reshape(h.shape)


def setup_inputs(seed: int = 0) -> dict:
    key = jax.random.key(seed)
    ks = jax.random.split(key, 24)
    f32 = jnp.float32
    L, D, E, F = DEPTH, D_MODEL, N_EXPERTS, D_FF

    def nrm(k, shape, scale):
        return jax.random.normal(k, shape, f32) * scale

    def gain(k, shape):
        return 1.0 + 0.02 * jax.random.normal(k, shape, f32)

    return {
        "x_prompt": nrm(ks[0], (BATCH, SEQ, D), 1.0),
        "x_sample": nrm(ks[1], (DEC_BATCH, DEC_SEQ, D), 1.0),
        "cache_sb_k": nrm(ks[2], (L, DEC_BATCH, PAST_LEN, H_SB, DH_SB), 1.0),
        "cache_sb_v": nrm(ks[3], (L, DEC_BATCH, PAST_LEN, H_SB, DH_SB), 1.0),
        "cache_diff_k": nrm(ks[4], (L, DEC_BATCH, PAST_LEN, H_DIFF, 2 * DH_DIFF), 1.0),
        "cache_diff_v": nrm(ks[5], (L, DEC_BATCH, PAST_LEN, H_DIFF, DV_DIFF), 1.0),
        "meta_tokens": nrm(ks[6], (N_META, D), 1.0),
        "norm_mix_g": gain(ks[7], (L, D)),
        "w_in": nrm(ks[8], (L, D, N_IN), D ** -0.5),
        "diff_lambda": nrm(ks[9], (L, 4, DH_DIFF), 0.1),
        "diff_subln_g": gain(ks[10], (L, DV_DIFF)),
        "w_br_sb": nrm(ks[11], (L, W_SB, D), W_SB ** -0.5),
        "w_br_diff": nrm(ks[12], (L, W_DIFF_V, D), W_DIFF_V ** -0.5),
        "w_out": nrm(ks[13], (L, D, D), D ** -0.5),
        "norm_ffn_g": gain(ks[14], (L, D)),
        "w_router": nrm(ks[15], (L, D, E), D ** -0.5),
        "b_router": nrm(ks[16], (L, E), 0.01),
        "w_gate_up": nrm(ks[17], (L, E, D, 2 * F), D ** -0.5),
        "b_gate_up": nrm(ks[18], (L, E, 2 * F), 0.01),
        "w_down": nrm(ks[19], (L, E, F, D), F ** -0.5),
        "b_down": nrm(ks[20], (L, E, D), 0.01),
        "final_norm_g": gain(ks[21], (D,)),
    }


def reference(x_prompt, x_sample, cache_sb_k, cache_sb_v, cache_diff_k, cache_diff_v,
              meta_tokens, norm_mix_g, w_in, diff_lambda, diff_subln_g, w_br_sb, w_br_diff,
              w_out, norm_ffn_g, w_router, b_router, w_gate_up, b_gate_up, w_down, b_down,
              final_norm_g):
    b = x_prompt.shape[0]
    hp = jnp.concatenate(
        [jnp.broadcast_to(meta_tokens.astype(x_prompt.dtype)[None], (b, N_META, D_MODEL)), x_prompt],
        axis=1)
    hs = x_sample
    past = cache_sb_k.shape[2]
    s_len = hs.shape[1]
    s_kpos = jnp.arange(past + s_len)
    s_qpos = past + jnp.arange(s_len)

    p_sbk, p_sbv, p_dk, p_dv = [], [], [], []
    s_sbk, s_sbv, s_dk, s_dv = [], [], [], []
    for l in range(DEPTH):
        lambda_init = 0.8 - 0.6 * math.exp(-0.3 * l)
        lam = diff_lambda_value(diff_lambda[l], lambda_init)

        q_sb, k_sb, v_sb, q_d, k_d, v_d, gates = project_mixers(hp, norm_mix_g[l], w_in[l])
        o_sb, o_d = prompt_attention(q_sb, k_sb, v_sb, q_d, k_d, v_d, lam, diff_subln_g[l], lambda_init)
        hp = merge_branches(hp, o_sb, o_d, gates, w_br_sb[l], w_br_diff[l], w_out[l])
        hp = ffn_block(hp, norm_ffn_g[l], w_router[l], b_router[l], w_gate_up[l], b_gate_up[l],
                       w_down[l], b_down[l])
        p_sbk.append(k_sb); p_sbv.append(v_sb); p_dk.append(k_d); p_dv.append(v_d)

        q_sb, k_sb, v_sb, q_d, k_d, v_d, gates = project_mixers(hs, norm_mix_g[l], w_in[l])
        k_sb_all = jnp.concatenate([cache_sb_k[l], k_sb], axis=1)
        v_sb_all = jnp.concatenate([cache_sb_v[l], v_sb], axis=1)
        k_d_all = jnp.concatenate([cache_diff_k[l], k_d], axis=1)
        v_d_all = jnp.concatenate([cache_diff_v[l], v_d], axis=1)
        o_sb = stick_breaking(q_sb, k_sb_all, v_sb_all, s_qpos, s_kpos)
        o_d = diff_attention(q_d, k_d_all, v_d_all, s_qpos, s_kpos, s_qpos // CHUNK, s_kpos // CHUNK,
                             lam, diff_subln_g[l], lambda_init)
        hs = merge_branches(hs, o_sb, o_d, gates, w_br_sb[l], w_br_diff[l], w_out[l])
        hs = ffn_block(hs, norm_ffn_g[l], w_router[l], b_router[l], w_gate_up[l], b_gate_up[l],
                       w_down[l], b_down[l])
        s_sbk.append(k_sb); s_sbv.append(v_sb); s_dk.append(k_d); s_dv.append(v_d)

    y_prompt = rms_norm(hp, final_norm_g)[:, N_META:]
    y_sample = rms_norm(hs, final_norm_g)
    return (y_prompt, y_sample,
            jnp.stack(p_sbk), jnp.stack(p_sbv), jnp.stack(p_dk), jnp.stack(p_dv),
            jnp.stack(s_sbk), jnp.stack(s_sbv), jnp.stack(s_dk), jnp.stack(s_dv))
```

```python
import functools
import math

import jax
import jax.numpy as jnp
from jax import lax
from jax.experimental import pallas as pl
from jax.experimental.pallas import tpu as pltpu

F32 = jnp.float32
BF16 = jnp.bfloat16

D_MODEL = 1024
CHUNK = 64
CHUNK_SHIFT = 6
N_META = 16
H_SB = 16
DH_SB = 64
H_DIFF = 8
DH_DIFF = 64
DV_DIFF = 128
N_EXPERTS = 32
TOP_K = 4
D_FF = 1024
SWIGLU_LIMIT = 7.0
SWIGLU_ALPHA = 1.702
NORM_EPS = 1e-5
LAMBDA_INIT = 0.8 - 0.6 * math.exp(-0.3 * 0)

LANES = 128
SEC = 1024
KEY_TILE = 256
Q_TILE_PROMPT = 256
Q_TILE_DECODE = 128
MOE_ROWS = 256
VMEM_LIMIT = 56 * 1024 * 1024
NEG_INF = float("-inf")


def _cparams(sem):
    return pltpu.CompilerParams(dimension_semantics=sem, vmem_limit_bytes=VMEM_LIMIT)


def _rmsnorm_kernel(x_ref, g_ref, o_ref):
    x = x_ref[...]
    y = x * lax.rsqrt(jnp.mean(x * x, axis=-1, keepdims=True) + NORM_EPS)
    o_ref[...] = (y * g_ref[...]).astype(o_ref.dtype)


def _rmsnorm(x, g, tm):
    r, d = x.shape
    return pl.pallas_call(
        _rmsnorm_kernel,
        grid=(r // tm,),
        in_specs=[pl.BlockSpec((tm, d), lambda i: (i, 0)),
                  pl.BlockSpec((1, d), lambda i: (0, 0))],
        out_specs=pl.BlockSpec((tm, d), lambda i: (i, 0)),
        out_shape=jax.ShapeDtypeStruct((r, d), BF16),
        compiler_params=_cparams(("parallel",)),
        name="rmsnorm",
    )(x, g.reshape(1, d))


def _proj_kernel(x_ref, w_ref, *out_refs, mode, scale, kt):
    acc = jnp.dot(x_ref[...], w_ref[...], preferred_element_type=F32)
    if mode == "q":
        out_refs[0][0] = (acc * scale).astype(BF16)
    elif mode == "k":
        out_refs[0][0] = acc
        out_refs[1][0] = acc.astype(BF16)
    elif mode == "v":
        out_refs[0][0] = acc
        tm = acc.shape[0]
        for c in range(tm // kt):
            for hp in range(SEC // LANES):
                tile = acc[c * kt:(c + 1) * kt, hp * LANES:(hp + 1) * LANES]
                out_refs[1][0, hp, c] = tile.T.astype(BF16)
    else:
        out_refs[0][...] = jax.nn.sigmoid(acc)


def _proj(xn, w2, mode, tm, kt=KEY_TILE, scale=1.0):
    r, d = xn.shape
    grid = (2, r // tm)
    in_specs = [pl.BlockSpec((tm, d), lambda j, i: (i, 0)),
                pl.BlockSpec((d, SEC), lambda j, i: (0, j))]
    row_spec = pl.BlockSpec((1, tm, SEC), lambda j, i: (j, i, 0))
    if mode == "q":
        out_specs = [row_spec]
        out_shape = [jax.ShapeDtypeStruct((2, r, SEC), BF16)]
    elif mode == "k":
        out_specs = [row_spec, row_spec]
        out_shape = [jax.ShapeDtypeStruct((2, r, SEC), F32),
                     jax.ShapeDtypeStruct((2, r, SEC), BF16)]
    elif mode == "v":
        out_specs = [row_spec,
                     pl.BlockSpec((1, SEC // LANES, tm // kt, LANES, kt),
                                  lambda j, i: (j, 0, i, 0, 0))]
        out_shape = [jax.ShapeDtypeStruct((2, r, SEC), F32),
                     jax.ShapeDtypeStruct((2, SEC // LANES, r // kt, LANES, kt), BF16)]
    else:
        out_specs = [pl.BlockSpec((tm, SEC), lambda j, i: (i, j))]
        out_shape = [jax.ShapeDtypeStruct((r, 2 * SEC), F32)]
    return pl.pallas_call(
        functools.partial(_proj_kernel, mode=mode, scale=scale, kt=kt),
        grid=grid, in_specs=in_specs, out_specs=out_specs, out_shape=out_shape,
        compiler_params=_cparams(("parallel", "parallel")),
        name="proj_" + mode,
    )(xn, w2)


def _sb_kernel(q_ref, k_ref, vt_ref, tri_ref, o_ref, acc_ref, *, tq, kt, q_pos0):
    qi = pl.program_id(2)
    q0 = q_pos0 + qi * tq
    jm = q0 // kt
    q = q_ref[0]
    lane = lax.broadcasted_iota(jnp.int32, (tq, LANES), 1)
    kpos_l = lax.broadcasted_iota(jnp.int32, (kt, tq), 0)
    qpos_l = lax.broadcasted_iota(jnp.int32, (kt, tq), 1)
    tri = tri_ref[...]

    for hh in range(2):
        in_head = (lane >= DH_SB) if hh == 1 else (lane < DH_SB)
        qz = jnp.where(in_head, q, jnp.zeros_like(q))
        acc_ref[hh] = jnp.zeros((LANES, tq), F32)

        def tile(j, carry, masked, qz=qz, hh=hh):
            off = pl.multiple_of(j * kt, kt)
            kk = k_ref[0, pl.ds(off, kt), :]
            z = lax.dot_general(kk, qz, (((1,), (1,)), ((), ())),
                                preferred_element_type=F32)
            sp = jnp.maximum(z, 0.0) + jnp.log(1.0 + jnp.exp(-jnp.abs(z)))
            if masked:
                vis = (kpos_l + j * kt) < (qpos_l + q0)
                lk = jnp.where(vis, -sp, 0.0)
            else:
                lk = -sp
            later = jnp.dot(tri, lk.astype(BF16), preferred_element_type=F32)
            w = jnp.exp(z + lk + later + carry)
            if masked:
                w = jnp.where(vis, w, 0.0)
            acc_ref[hh] += jnp.dot(vt_ref[0, 0, j], w.astype(BF16),
                                   preferred_element_type=F32)
            return carry + jnp.sum(lk, axis=0, keepdims=True)

        carry = tile(jm, jnp.zeros((1, tq), F32), True)

        def body(t, carry, tile=tile):
            return tile(jm - 1 - t, carry, False)

        lax.fori_loop(0, jm, body, carry)

    row = lax.broadcasted_iota(jnp.int32, (LANES, tq), 0)
    o_t = jnp.where(row < DH_SB, acc_ref[0], acc_ref[1])
    o_ref[0] = o_t.T.astype(o_ref.dtype)


def _sb_attention(q, k, vt, tri, *, batch, nq, tq, kt, q_pos0, q_sec=0, k_sec=0, v_sec=0):
    rk = k.shape[1]
    nk = rk // kt
    nhp = SEC // LANES
    return pl.pallas_call(
        functools.partial(_sb_kernel, tq=tq, kt=kt, q_pos0=q_pos0),
        grid=(batch, nhp, nq),
        in_specs=[
            pl.BlockSpec((1, tq, LANES), lambda b, h, i: (b + q_sec, i, h)),
            pl.BlockSpec((1, rk, LANES), lambda b, h, i: (b + k_sec, 0, h)),
            pl.BlockSpec((1, 1, nk, LANES, kt), lambda b, h, i: (b + v_sec, h, 0, 0, 0)),
            pl.BlockSpec((kt, kt), lambda b, h, i: (0, 0)),
        ],
        out_specs=pl.BlockSpec((1, tq, LANES), lambda b, h, i: (b, i, h)),
        out_shape=jax.ShapeDtypeStruct((batch, nq * tq, SEC), BF16),
        scratch_shapes=[pltpu.VMEM((2, LANES, tq), F32)],
        compiler_params=_cparams(("parallel", "parallel", "arbitrary")),
        name="sb_attention",
    )(q, k, vt, tri)


def _diff_kernel(slopes_ref, lam_ref, g_ref, q_ref, k_ref, vt_ref, o_ref,
                 acc_ref, b0_ref, *, tq, kt, q_pos0, coff, n_keys, nk):
    h = pl.program_id(1)
    qi = pl.program_id(2)
    slope = slopes_ref[h]
    q0 = q_pos0 + qi * tq
    c_lo = (q0 - coff + CHUNK) // CHUNK - 1
    c_hi = (q0 + tq - 1 - coff + CHUNK) // CHUNK - 1
    full_end = jnp.minimum(coff + CHUNK * (c_lo + 1), n_keys)
    vis_end = jnp.minimum(coff + CHUNK * (c_hi + 1), n_keys)
    n_full = full_end // kt
    n_vis = jnp.minimum((vis_end + kt - 1) // kt, nk)

    q = q_ref[0]
    lane = lax.broadcasted_iota(jnp.int32, (tq, LANES), 1)
    kpos_l = lax.broadcasted_iota(jnp.int32, (kt, tq), 0)
    qpos_l = lax.broadcasted_iota(jnp.int32, (kt, tq), 1)
    b0_ref[...] = slope * kpos_l.astype(F32)

    stats = []
    for mm in range(2):
        in_map = (lane >= DH_DIFF) if mm == 1 else (lane < DH_DIFF)
        qz = jnp.where(in_map, q, jnp.zeros_like(q))
        acc_ref[mm] = jnp.zeros((LANES, tq), F32)

        def update(j, s, sj, m, l, mm=mm):
            m_new = jnp.maximum(m, jnp.max(s, axis=0, keepdims=True) + sj)
            m_safe = jnp.where(m_new == NEG_INF, 0.0, m_new)
            alpha = jnp.exp(m - m_safe)
            p = jnp.exp(s - (m_safe - sj))
            l = alpha * l + jnp.sum(p, axis=0, keepdims=True)
            acc_ref[mm] = alpha * acc_ref[mm] + jnp.dot(
                vt_ref[0, 0, j], p.astype(BF16), preferred_element_type=F32)
            return m_new, l

        def scores(j, qz=qz):
            off = pl.multiple_of(j * kt, kt)
            kk = k_ref[0, pl.ds(off, kt), :]
            return lax.dot_general(kk, qz, (((1,), (1,)), ((), ())),
                                   preferred_element_type=F32)

        def full_body(j, ml, scores=scores, update=update):
            m, l = ml
            s = scores(j) + b0_ref[...]
            sj = slope * (j * kt - q0).astype(F32)
            return update(j, s, sj, m, l)

        def masked_body(j, ml, scores=scores, update=update):
            m, l = ml
            kpos = kpos_l + j * kt
            qpos = qpos_l + q0
            vis = (((kpos - coff + CHUNK) >> CHUNK_SHIFT) <= ((qpos - coff + CHUNK) >> CHUNK_SHIFT)) \
                & (kpos < n_keys)
            bias = slope * (qpos_l - jnp.abs(qpos - kpos)).astype(F32)
            s = jnp.where(vis, scores(j) + bias, NEG_INF)
            return update(j, s, jnp.float32(0.0), m, l)

        ml = (jnp.full((1, tq), NEG_INF, F32), jnp.zeros((1, tq), F32))
        ml = lax.fori_loop(0, n_full, full_body, ml)
        ml = lax.fori_loop(n_full, n_vis, masked_body, ml)
        stats.append(ml)

    lp = lam_ref[...]
    lam = (jnp.exp(jnp.sum(lp[0:1] * lp[1:2], axis=-1, keepdims=True))
           - jnp.exp(jnp.sum(lp[2:3] * lp[3:4], axis=-1, keepdims=True)) + LAMBDA_INIT)
    l1 = stats[0][1]
    l2 = stats[1][1]
    l1 = jnp.where(l1 == 0.0, 1.0, l1)
    l2 = jnp.where(l2 == 0.0, 1.0, l2)
    o_t = acc_ref[0] / l1 - lam * (acc_ref[1] / l2)
    o = o_t.T
    y = o * lax.rsqrt(jnp.mean(o * o, axis=-1, keepdims=True) + NORM_EPS)
    o_ref[0] = ((y * g_ref[...]) * (1.0 - LAMBDA_INIT)).astype(o_ref.dtype)


def _diff_attention(q, k, vt, slopes, lam_p, subln_g, *, batch, nq, tq, kt, q_pos0, coff, n_keys,
                    q_sec=0, k_sec=0, v_sec=0):
    rk = k.shape[1]
    nk = rk // kt
    nh = SEC // LANES
    return pl.pallas_call(
        functools.partial(_diff_kernel, tq=tq, kt=kt, q_pos0=q_pos0, coff=coff, n_keys=n_keys, nk=nk),
        grid=(batch, nh, nq),
        in_specs=[
            pl.BlockSpec(memory_space=pltpu.SMEM),
            pl.BlockSpec((4, DH_DIFF), lambda b, h, i: (0, 0)),
            pl.BlockSpec((1, DV_DIFF), lambda b, h, i: (0, 0)),
            pl.BlockSpec((1, tq, LANES), lambda b, h, i: (b + q_sec, i, h)),
            pl.BlockSpec((1, rk, LANES), lambda b, h, i: (b + k_sec, 0, h)),
            pl.BlockSpec((1, 1, nk, LANES, kt), lambda b, h, i: (b + v_sec, h, 0, 0, 0)),
        ],
        out_specs=pl.BlockSpec((1, tq, LANES), lambda b, h, i: (b, i, h)),
        out_shape=jax.ShapeDtypeStruct((batch, nq * tq, SEC), BF16),
        scratch_shapes=[pltpu.VMEM((2, LANES, tq), F32), pltpu.VMEM((kt, tq), F32)],
        compiler_params=_cparams(("parallel", "parallel", "arbitrary")),
        name="diff_attention",
    )(slopes, lam_p, subln_g.reshape(1, DV_DIFF), q, k, vt)


def _merge_kernel(h_ref, osb_ref, od_ref, gt_ref, wsb_ref, wd_ref, wo_ref, gn_ref, wr_ref, br_ref,
                  h2_ref, hn_ref, te_ref, tg_ref):
    y_sb = jnp.dot(osb_ref[...], wsb_ref[...], preferred_element_type=F32)
    y_d = jnp.dot(od_ref[...], wd_ref[...], preferred_element_type=F32)
    gt = gt_ref[...]
    mix = gt[:, :D_MODEL] * y_sb + gt[:, D_MODEL:] * y_d
    h2 = h_ref[...] + jnp.dot(mix.astype(BF16), wo_ref[...], preferred_element_type=F32)
    h2_ref[...] = h2
    hn = (h2 * lax.rsqrt(jnp.mean(h2 * h2, axis=-1, keepdims=True) + NORM_EPS)) * gn_ref[...]
    hn_ref[...] = hn.astype(BF16)
    logits = jnp.dot(hn, wr_ref[...], preferred_element_type=F32,
                     precision=lax.Precision.HIGHEST) + br_ref[...]
    tm = logits.shape[0]
    lane = lax.broadcasted_iota(jnp.int32, (tm, LANES), 1).astype(F32)
    tops, idxs = [], []
    l = logits
    for _ in range(TOP_K):
        m = jnp.max(l, axis=-1, keepdims=True)
        idx = jnp.min(jnp.where(l == m, lane, float(LANES)), axis=-1, keepdims=True)
        tops.append(m)
        idxs.append(idx)
        l = jnp.where(lane == idx, NEG_INF, l)
    ex = [jnp.exp(t - tops[0]) for t in tops]
    den = ex[0] + ex[1] + ex[2] + ex[3]
    te = jnp.zeros((tm, LANES), F32)
    tg = jnp.zeros((tm, LANES), F32)
    for k in range(TOP_K):
        te = jnp.where(lane == float(k), idxs[k], te)
        tg = jnp.where(lane == float(k), ex[k] / den, tg)
    te_ref[...] = te.astype(jnp.int32)
    tg_ref[...] = tg


def _merge(h, o_sb, o_d, gates, w_sb, w_d, w_o, g_ffn, w_r, b_r, tm):
    r, d = h.shape
    row = lambda w: pl.BlockSpec((tm, w), lambda i: (i, 0))
    full = lambda a, b: pl.BlockSpec((a, b), lambda i: (0, 0))
    return pl.pallas_call(
        _merge_kernel,
        grid=(r // tm,),
        in_specs=[row(d), row(d), row(d), row(2 * d), full(d, d), full(d, d), full(d, d),
                  full(1, d), full(d, LANES), full(1, LANES)],
        out_specs=[row(d), row(d), row(LANES), row(LANES)],
        out_shape=[jax.ShapeDtypeStruct((r, d), F32), jax.ShapeDtypeStruct((r, d), BF16),
                   jax.ShapeDtypeStruct((r, LANES), jnp.int32), jax.ShapeDtypeStruct((r, LANES), F32)],
        compiler_params=_cparams(("parallel",)),
        name="merge_router",
    )(h, o_sb, o_d, gates, w_sb, w_d, w_o, g_ffn.reshape(1, d), w_r, b_r)


def _moe_kernel(be_ref, nb_ref, x_ref, gate_ref, wgu_ref, bgu_ref, wdn_ref, bdn_ref, o_ref):
    b = pl.program_id(0)

    @pl.when(b < nb_ref[0])
    def _():
        gu = jnp.dot(x_ref[...], wgu_ref[0], preferred_element_type=F32) + bgu_ref[0]
        g = jnp.minimum(gu[:, :D_FF], SWIGLU_LIMIT)
        u = jnp.clip(gu[:, D_FF:], -SWIGLU_LIMIT, SWIGLU_LIMIT)
        act = (u + 1.0) * (g * jax.nn.sigmoid(SWIGLU_ALPHA * g))
        y = jnp.dot(act.astype(BF16), wdn_ref[0], preferred_element_type=F32) + bdn_ref[0]
        o_ref[...] = y * gate_ref[...]

    @pl.when(b >= nb_ref[0])
    def _():
        o_ref[...] = jnp.zeros_like(o_ref)


def _moe_gmm(block_e, n_used, xs, row_gate, w_gu, b_gu, w_dn, b_dn, bm):
    n_rows, d = xs.shape
    nb = n_rows // bm
    grid_spec = pltpu.PrefetchScalarGridSpec(
        num_scalar_prefetch=2,
        grid=(nb,),
        in_specs=[
            pl.BlockSpec((bm, d), lambda b, be, nu: (b, 0)),
            pl.BlockSpec((bm, 1), lambda b, be, nu: (b, 0)),
            pl.BlockSpec((1, d, 2 * D_FF), lambda b, be, nu: (be[b], 0, 0)),
            pl.BlockSpec((1, 1, 2 * D_FF), lambda b, be, nu: (be[b], 0, 0)),
            pl.BlockSpec((1, D_FF, d), lambda b, be, nu: (be[b], 0, 0)),
            pl.BlockSpec((1, 1, d), lambda b, be, nu: (be[b], 0, 0)),
        ],
        out_specs=pl.BlockSpec((bm, d), lambda b, be, nu: (b, 0)),
    )
    return pl.pallas_call(
        _moe_kernel,
        grid_spec=grid_spec,
        out_shape=jax.ShapeDtypeStruct((n_rows, d), F32),
        compiler_params=_cparams(("arbitrary",)),
        name="moe_experts",
    )(block_e, n_used, xs, row_gate, w_gu, b_gu, w_dn, b_dn)


def _final_kernel(h_ref, c_ref, g_ref, o_ref):
    c = c_ref[...]
    d = h_ref.shape[1]
    y = c[:, 0:d]
    for k in range(1, TOP_K):
        y = y + c[:, k * d:(k + 1) * d]
    x = h_ref[...] + y
    o_ref[...] = (x * lax.rsqrt(jnp.mean(x * x, axis=-1, keepdims=True) + NORM_EPS)) * g_ref[...]


def _final(h2, contrib, g, tm):
    r, d = h2.shape
    return pl.pallas_call(
        _final_kernel,
        grid=(r // tm,),
        in_specs=[pl.BlockSpec((tm, d), lambda i: (i, 0)),
                  pl.BlockSpec((tm, TOP_K * d), lambda i: (i, 0)),
                  pl.BlockSpec((1, d), lambda i: (0, 0))],
        out_specs=pl.BlockSpec((tm, d), lambda i: (i, 0)),
        out_shape=jax.ShapeDtypeStruct((r, d), F32),
        compiler_params=_cparams(("parallel",)),
        name="combine_final_norm",
    )(h2, contrib, g.reshape(1, d))


def _route(top_e, top_g, bm):
    n = top_e.shape[0]
    a = n * TOP_K
    flat_e = top_e.reshape(-1)
    order = jnp.argsort(flat_e)
    se = flat_e[order]
    counts = jnp.zeros((N_EXPERTS,), jnp.int32).at[flat_e].add(1)
    padded = ((counts + bm - 1) // bm) * bm
    pend = jnp.cumsum(padded)
    pstart = pend - padded
    start = jnp.cumsum(counts) - counts
    rank = jnp.arange(a, dtype=jnp.int32) - start[se]
    dest_sorted = pstart[se] + rank
    n_rows = -(-(a + N_EXPERTS * (bm - 1)) // bm) * bm
    nb = n_rows // bm
    row_tok = jnp.full((n_rows,), n, jnp.int32).at[dest_sorted].set(order.astype(jnp.int32) // TOP_K)
    row_gate = jnp.zeros((n_rows,), F32).at[dest_sorted].set(top_g.reshape(-1)[order])
    pos = jnp.zeros((a,), jnp.int32).at[order].set(dest_sorted).reshape(n, TOP_K)
    block_e = jnp.clip(jnp.searchsorted(pend, jnp.arange(nb, dtype=jnp.int32) * bm, side="right"),
                       0, N_EXPERTS - 1).astype(jnp.int32)
    n_used = (pend[-1] // bm).astype(jnp.int32).reshape(1)
    return row_tok, row_gate, pos, block_e, n_used


def _tri(kt):
    s = lax.broadcasted_iota(jnp.int32, (kt, kt), 0)
    j = lax.broadcasted_iota(jnp.int32, (kt, kt), 1)
    return (j > s).astype(BF16)


def _tiles_t(v, kt):
    b, r, _ = v.shape
    return v.reshape(b, r // kt, kt, SEC // LANES, LANES).transpose(0, 3, 1, 4, 2)


def kernel(x_prompt, x_sample, cache_sb_k, cache_sb_v, cache_diff_k, cache_diff_v, meta_tokens,
           norm_mix_g, w_in, diff_lambda, diff_subln_g, w_br_sb, w_br_diff, w_out, norm_ffn_g,
           w_router, b_router, w_gate_up, b_gate_up, w_down, b_down, final_norm_g):
    assert x_prompt.shape[0] == 1 and w_in.shape[0] == 1
    d = D_MODEL
    seq = x_prompt.shape[1]
    t = N_META + seq
    tq_p, kt = Q_TILE_PROMPT, KEY_TILE
    tp = -(-t // tq_p) * tq_p
    nq_p = tp // tq_p
    nb_s, s_len = x_sample.shape[:2]
    past = cache_sb_k.shape[2]
    n_s = nb_s * s_len
    scale = DH_SB ** -0.5

    w0 = w_in[0]
    wq = jnp.concatenate([w0[:, 0:SEC], w0[:, 3 * SEC:4 * SEC]], axis=1).astype(BF16)
    wk = jnp.concatenate([w0[:, SEC:2 * SEC], w0[:, 4 * SEC:5 * SEC]], axis=1).astype(BF16)
    wv = jnp.concatenate([w0[:, 2 * SEC:3 * SEC], w0[:, 5 * SEC:6 * SEC]], axis=1).astype(BF16)
    wg = w0[:, 6 * SEC:8 * SEC].astype(BF16)
    w_sb = w_br_sb[0].astype(BF16)
    w_d = w_br_diff[0].astype(BF16)
    w_o = w_out[0].astype(BF16)
    w_r = jnp.pad(w_router[0], ((0, 0), (0, LANES - N_EXPERTS)))
    b_r = jnp.pad(b_router[0], (0, LANES - N_EXPERTS), constant_values=NEG_INF).reshape(1, LANES)
    w_gu = w_gate_up[0].astype(BF16)
    w_dn = w_down[0].astype(BF16)
    b_gu = b_gate_up[0].reshape(N_EXPERTS, 1, 2 * D_FF)
    b_dn = b_down[0].reshape(N_EXPERTS, 1, d)
    slopes = jnp.exp2(-8.0 * jnp.arange(1, H_DIFF + 1, dtype=F32) / H_DIFF)
    tri = _tri(kt)
    lam_p = diff_lambda[0]
    g_sub = diff_subln_g[0]

    hp = jnp.concatenate([meta_tokens.astype(F32), x_prompt[0], jnp.zeros((tp - t, d), F32)], axis=0)
    tm_p = tp // 13 if tp % 13 == 0 and (tp // 13) % kt == 0 else tq_p
    xn = _rmsnorm(hp, norm_mix_g[0], tm_p)
    (q2,) = _proj(xn, wq, "q", tm_p, scale=scale)
    kf, kb = _proj(xn, wk, "k", tm_p)
    vf, vt = _proj(xn, wv, "v", tm_p)
    (gates,) = _proj(xn, wg, "g", tm_p)
    o_sb = _sb_attention(q2, kb, vt, tri, batch=1, nq=nq_p, tq=tq_p, kt=kt, q_pos0=0)
    o_d = _diff_attention(q2, kb, vt, slopes, lam_p, g_sub, batch=1, nq=nq_p, tq=tq_p, kt=kt,
                          q_pos0=0, coff=N_META, n_keys=t, q_sec=1, k_sec=1, v_sec=1)
    h2, hn, te, tg = _merge(hp, o_sb[0], o_d[0], gates, w_sb, w_d, w_o, norm_ffn_g[0], w_r, b_r, tq_p)

    tq_s = Q_TILE_DECODE
    hs = x_sample.reshape(n_s, d)
    xn_s = _rmsnorm(hs, norm_mix_g[0], n_s)
    (q2_s,) = _proj(xn_s, wq, "q", n_s, scale=scale)
    kf_s, kb_s = _proj(xn_s, wk, "k", n_s)
    vf_s, _ = _proj(xn_s, wv, "v", n_s, kt=n_s)
    (gates_s,) = _proj(xn_s, wg, "g", n_s)
    rk_s = -(-(past + s_len) // kt) * kt

    def with_cache(cache, new):
        c = cache[0].reshape(nb_s, past, SEC).astype(BF16)
        nw = new.reshape(nb_s, s_len, SEC).astype(BF16)
        z = jnp.zeros((nb_s, rk_s - past - s_len, SEC), BF16)
        return jnp.concatenate([c, nw, z], axis=1)

    def pad_q(qs):
        return jnp.pad(qs.reshape(nb_s, s_len, SEC), ((0, 0), (0, tq_s - s_len), (0, 0)))

    k_sb_all = with_cache(cache_sb_k, kf_s[0])
    vt_sb_all = _tiles_t(with_cache(cache_sb_v, vf_s[0]), kt)
    k_d_all = with_cache(cache_diff_k, kf_s[1])
    vt_d_all = _tiles_t(with_cache(cache_diff_v, vf_s[1]), kt)
    o_sb_s = _sb_attention(pad_q(q2_s[0]), k_sb_all, vt_sb_all, tri, batch=nb_s, nq=1, tq=tq_s, kt=kt,
                           q_pos0=past)
    o_d_s = _diff_attention(pad_q(q2_s[1]), k_d_all, vt_d_all, slopes, lam_p, g_sub, batch=nb_s, nq=1,
                            tq=tq_s, kt=kt, q_pos0=past, coff=0, n_keys=past + s_len)
    o_sb_s = o_sb_s[:, :s_len].reshape(n_s, SEC)
    o_d_s = o_d_s[:, :s_len].reshape(n_s, SEC)
    h2_s, hn_s, te_s, tg_s = _merge(hs, o_sb_s, o_d_s, gates_s, w_sb, w_d, w_o, norm_ffn_g[0], w_r, b_r,
                                    n_s)

    n_tok = t + n_s
    top_e = jnp.concatenate([te[:t, :TOP_K], te_s[:, :TOP_K]], axis=0)
    top_g = jnp.concatenate([tg[:t, :TOP_K], tg_s[:, :TOP_K]], axis=0)
    row_tok, row_gate, pos, block_e, n_used = _route(top_e, top_g, MOE_ROWS)
    hn_all = jnp.concatenate([hn[:t], hn_s, jnp.zeros((1, d), BF16)], axis=0)
    xs = hn_all[row_tok]
    yb = _moe_gmm(block_e, n_used, xs, row_gate.reshape(-1, 1), w_gu, b_gu, w_dn, b_dn, MOE_ROWS)
    pos_p = jnp.pad(pos[:t], ((0, tp - t), (0, 0)))
    contrib_p = yb[pos_p].reshape(tp, TOP_K * d)
    contrib_s = yb[pos[t:]].reshape(n_s, TOP_K * d)
    y_p = _final(h2, contrib_p, final_norm_g, tq_p)
    y_s = _final(h2_s, contrib_s, final_norm_g, n_s)

    y_prompt = y_p[N_META:t][None]
    y_sample = y_s.reshape(nb_s, s_len, d)

    def heads(a, nh):
        return a.reshape(1, *a.shape[:-1], nh, a.shape[-1] // nh)

    return (y_prompt, y_sample,
            heads(kf[0, :t][None], H_SB), heads(vf[0, :t][None], H_SB),
            heads(kf[1, :t][None], H_DIFF), heads(vf[1, :t][None], H_DIFF),
            heads(kf_s[0].reshape(nb_s, s_len, SEC), H_SB), heads(vf_s[0].reshape(nb_s, s_len, SEC), H_SB),
            heads(kf_s[1].reshape(nb_s, s_len, SEC), H_DIFF), heads(vf_s[1].reshape(nb_s, s_len, SEC), H_DIFF))
```

```python
import functools
import math

import jax
import jax.numpy as jnp
from jax import lax
from jax.experimental import pallas as pl
from jax.experimental.pallas import tpu as pltpu

F32 = jnp.float32
BF16 = jnp.bfloat16

D_MODEL = 1024
CHUNK = 64
CHUNK_SHIFT = 6
N_META = 16
H_SB = 16
DH_SB = 64
H_DIFF = 8
DH_DIFF = 64
DV_DIFF = 128
N_EXPERTS = 32
TOP_K = 4
D_FF = 1024
SWIGLU_LIMIT = 7.0
SWIGLU_ALPHA = 1.702
NORM_EPS = 1e-5
LAMBDA_INIT = 0.8 - 0.6 * math.exp(-0.3 * 0)

LANES = 128
SEC = 1024
KEY_TILE = 256
Q_TILE_PROMPT = 256
Q_TILE_DECODE = 128
MOE_ROWS = 256
VMEM_LIMIT = 56 * 1024 * 1024
NEG_INF = float("-inf")
SB_SKIP_LOG = 104.0
SB_ABSENT = 1e30
DIFF_GROUP = 2
DIFF_TAIL_GROUPS = 2


def _cparams(sem):
    return pltpu.CompilerParams(dimension_semantics=sem, vmem_limit_bytes=VMEM_LIMIT)


def _rmsnorm_kernel(x_ref, g_ref, o_ref):
    x = x_ref[...]
    y = x * lax.rsqrt(jnp.mean(x * x, axis=-1, keepdims=True) + NORM_EPS)
    o_ref[...] = (y * g_ref[...]).astype(o_ref.dtype)


def _rmsnorm(x, g, tm):
    r, d = x.shape
    return pl.pallas_call(
        _rmsnorm_kernel,
        grid=(r // tm,),
        in_specs=[pl.BlockSpec((tm, d), lambda i: (i, 0)),
                  pl.BlockSpec((1, d), lambda i: (0, 0))],
        out_specs=pl.BlockSpec((tm, d), lambda i: (i, 0)),
        out_shape=jax.ShapeDtypeStruct((r, d), BF16),
        compiler_params=_cparams(("parallel",)),
        name="rmsnorm",
    )(x, g.reshape(1, d))


def _proj_kernel(x_ref, w_ref, *out_refs, mode, scale, kt):
    acc = jnp.dot(x_ref[...], w_ref[...], preferred_element_type=F32)
    if mode == "q":
        out_refs[0][0] = (acc * scale).astype(BF16)
    elif mode == "k":
        out_refs[0][0] = acc
        out_refs[1][0] = acc.astype(BF16)
    elif mode == "v":
        out_refs[0][0] = acc
        tm = acc.shape[0]
        for c in range(tm // kt):
            for hp in range(SEC // LANES):
                tile = acc[c * kt:(c + 1) * kt, hp * LANES:(hp + 1) * LANES]
                out_refs[1][0, hp, c] = tile.T.astype(BF16)
    else:
        out_refs[0][...] = jax.nn.sigmoid(acc)


def _proj(xn, w2, mode, tm, kt=KEY_TILE, scale=1.0):
    r, d = xn.shape
    grid = (2, r // tm)
    in_specs = [pl.BlockSpec((tm, d), lambda j, i: (i, 0)),
                pl.BlockSpec((d, SEC), lambda j, i: (0, j))]
    row_spec = pl.BlockSpec((1, tm, SEC), lambda j, i: (j, i, 0))
    if mode == "q":
        out_specs = [row_spec]
        out_shape = [jax.ShapeDtypeStruct((2, r, SEC), BF16)]
    elif mode == "k":
        out_specs = [row_spec, row_spec]
        out_shape = [jax.ShapeDtypeStruct((2, r, SEC), F32),
                     jax.ShapeDtypeStruct((2, r, SEC), BF16)]
    elif mode == "v":
        out_specs = [row_spec,
                     pl.BlockSpec((1, SEC // LANES, tm // kt, LANES, kt),
                                  lambda j, i: (j, 0, i, 0, 0))]
        out_shape = [jax.ShapeDtypeStruct((2, r, SEC), F32),
                     jax.ShapeDtypeStruct((2, SEC // LANES, r // kt, LANES, kt), BF16)]
    else:
        out_specs = [pl.BlockSpec((tm, SEC), lambda j, i: (i, j))]
        out_shape = [jax.ShapeDtypeStruct((r, 2 * SEC), F32)]
    return pl.pallas_call(
        functools.partial(_proj_kernel, mode=mode, scale=scale, kt=kt),
        grid=grid, in_specs=in_specs, out_specs=out_specs, out_shape=out_shape,
        compiler_params=_cparams(("parallel", "parallel")),
        name="proj_" + mode,
    )(xn, w2)


def _sb_kernel(q_ref, k_ref, vt_ref, tri_ref, o_ref, acc_ref, *, tq, kt, q_pos0):
    qi = pl.program_id(2)
    q0 = q_pos0 + qi * tq
    jm = q0 // kt
    q = q_ref[0]
    lane = lax.broadcasted_iota(jnp.int32, (tq, LANES), 1)
    kpos_l = lax.broadcasted_iota(jnp.int32, (kt, tq), 0)
    qpos_l = lax.broadcasted_iota(jnp.int32, (kt, tq), 1)
    tri = tri_ref[...]
    zero = jnp.zeros_like(q)
    qz = (jnp.where(lane < DH_SB, q, zero), jnp.where(lane >= DH_SB, q, zero))
    acc_ref[...] = jnp.zeros_like(acc_ref)

    def tile(j, hh, carry, masked):
        off = pl.multiple_of(j * kt, kt)
        kk = k_ref[0, pl.ds(off, kt), :]
        z = lax.dot_general(kk, qz[hh], (((1,), (1,)), ((), ())),
                            preferred_element_type=F32)
        sp = jnp.maximum(z, 0.0) + jnp.log(1.0 + jnp.exp(-jnp.abs(z)))
        if masked:
            vis = (kpos_l + j * kt) < (qpos_l + q0)
            lk = jnp.where(vis, -sp, 0.0)
        else:
            lk = -sp
        later = jnp.dot(tri, lk.astype(BF16), preferred_element_type=F32)
        w = jnp.exp(z + lk + later + carry)
        if masked:
            w = jnp.where(vis, w, 0.0)
        acc_ref[hh] += jnp.dot(vt_ref[0, 0, j], w.astype(BF16), preferred_element_type=F32)
        return jnp.sum(lk, axis=0, keepdims=True)

    none_later = jnp.zeros((1, tq), F32)
    carries = tuple(tile(jm, hh, none_later, True) for hh in range(2))

    def cond(state):
        t, ca, cb = state
        return jnp.logical_and(jm - 1 - 2 * t >= 0,
                               jnp.max(jnp.maximum(ca, cb)) > -SB_SKIP_LOG)

    def body(state):
        t, ca, cb = state
        j0 = jm - 1 - 2 * t
        j1 = j0 - 1
        j1c = jnp.maximum(j1, 0)
        out = []
        for hh, carry in ((0, ca), (1, cb)):
            s0 = tile(j0, hh, carry, False)
            carry1 = jnp.where(j1 >= 0, carry + s0, -SB_ABSENT)
            s1 = tile(j1c, hh, carry1, False)
            out.append(carry + s0 + s1)
        return t + 1, out[0], out[1]

    lax.while_loop(cond, body, (jnp.int32(0), carries[0], carries[1]))

    row = lax.broadcasted_iota(jnp.int32, (LANES, tq), 0)
    o_t = jnp.where(row < DH_SB, acc_ref[0], acc_ref[1])
    o_ref[0] = o_t.T.astype(o_ref.dtype)


def _sb_attention(q, k, vt, tri, *, batch, nq, tq, kt, q_pos0, q_sec=0, k_sec=0, v_sec=0):
    rk = k.shape[1]
    nk = rk // kt
    nhp = SEC // LANES
    return pl.pallas_call(
        functools.partial(_sb_kernel, tq=tq, kt=kt, q_pos0=q_pos0),
        grid=(batch, nhp, nq),
        in_specs=[
            pl.BlockSpec((1, tq, LANES), lambda b, h, i: (b + q_sec, i, h)),
            pl.BlockSpec((1, rk, LANES), lambda b, h, i: (b + k_sec, 0, h)),
            pl.BlockSpec((1, 1, nk, LANES, kt), lambda b, h, i: (b + v_sec, h, 0, 0, 0)),
            pl.BlockSpec((kt, kt), lambda b, h, i: (0, 0)),
        ],
        out_specs=pl.BlockSpec((1, tq, LANES), lambda b, h, i: (b, i, h)),
        out_shape=jax.ShapeDtypeStruct((batch, nq * tq, SEC), BF16),
        scratch_shapes=[pltpu.VMEM((2, LANES, tq), F32)],
        compiler_params=_cparams(("parallel", "parallel", "arbitrary")),
        name="sb_attention",
    )(q, k, vt, tri)


def _diff_kernel(slopes_ref, lam_ref, g_ref, q_ref, k_ref, vt_ref, o_ref,
                 acc_ref, bt_ref, s0_ref, s1_ref, p0_ref, p1_ref, *, tq, kt, q_pos0, coff, n_keys, nk):
    h = pl.program_id(1)
    qi = pl.program_id(2)
    slope = slopes_ref[h]
    q0 = q_pos0 + qi * tq
    c_lo = (q0 - coff + CHUNK) // CHUNK - 1
    c_hi = (q0 + tq - 1 - coff + CHUNK) // CHUNK - 1
    full_end = jnp.minimum(coff + CHUNK * (c_lo + 1), n_keys)
    vis_end = jnp.minimum(coff + CHUNK * (c_hi + 1), n_keys)
    n_full = full_end // kt
    n_vis = jnp.minimum((vis_end + kt - 1) // kt, nk)

    q = q_ref[0]
    lane = lax.broadcasted_iota(jnp.int32, (tq, LANES), 1)
    kpos_l = lax.broadcasted_iota(jnp.int32, (kt, tq), 0)
    qpos_l = lax.broadcasted_iota(jnp.int32, (kt, tq), 1)
    gsz = DIFF_GROUP
    m_tiles = (n_full // gsz) * gsz
    n_fullg = m_tiles // gsz
    n_groups = n_fullg + DIFF_TAIL_GROUPS
    n_trips = (n_groups + 1) // 2
    n_tail = DIFF_TAIL_GROUPS * gsz
    absent = gsz + n_tail

    for u in range(gsz):
        bt_ref[u] = slope * (kpos_l + u * kt).astype(F32)
    for u in range(n_tail):
        kpos = kpos_l + (m_tiles + u) * kt
        qpos = qpos_l + q0
        vis = (((kpos - coff + CHUNK) >> CHUNK_SHIFT) <= ((qpos - coff + CHUNK) >> CHUNK_SHIFT)) \
            & (kpos < n_keys)
        bias = slope * (qpos_l - jnp.abs(qpos - kpos)).astype(F32)
        bt_ref[gsz + u] = jnp.where(vis, bias, NEG_INF)
    bt_ref[absent] = jnp.full((kt, tq), NEG_INF, F32)

    zero = jnp.zeros_like(q)
    qz = (jnp.where(lane < DH_DIFF, q, zero), jnp.where(lane >= DH_DIFF, q, zero))
    acc_ref[...] = jnp.zeros_like(acc_ref)
    p1_ref[...] = jnp.zeros_like(p1_ref)

    def tiles_of(g):
        return [jnp.clip(g * gsz + u, 0, nk - 1) for u in range(gsz)]

    def scores_stage(g, s_ref):
        js = tiles_of(g)
        for mm in range(2):
            for u in range(gsz):
                off = pl.multiple_of(js[u] * kt, kt)
                kk = k_ref[0, pl.ds(off, kt), :]
                s_ref[mm, u] = lax.dot_general(kk, qz[mm], (((1,), (1,)), ((), ())),
                                               preferred_element_type=F32)

    def softmax_stage(g, s_ref, p_ref, state):
        full = g < n_fullg
        sj = jnp.where(full, slope * (g * gsz * kt - q0).astype(F32), 0.0)
        bidx = [jnp.where(full, u, jnp.minimum(gsz + g * gsz + u - m_tiles, absent)) for u in range(gsz)]
        new_state, alphas = [], []
        for mm in range(2):
            m, l = state[2 * mm], state[2 * mm + 1]
            ss = [s_ref[mm, u] + bt_ref[bidx[u]] for u in range(gsz)]
            cmax = jnp.max(ss[0], axis=0, keepdims=True)
            for s in ss[1:]:
                cmax = jnp.maximum(cmax, jnp.max(s, axis=0, keepdims=True))
            m_new = jnp.maximum(m, cmax + sj)
            m_safe = jnp.where(m_new == NEG_INF, 0.0, m_new)
            alpha = jnp.exp(m - m_safe)
            r = m_safe - sj
            psum = jnp.zeros((1, tq), F32)
            for u in range(gsz):
                p = jnp.exp(ss[u] - r)
                psum = psum + jnp.sum(p, axis=0, keepdims=True)
                p_ref[mm, u] = p.astype(BF16)
            new_state += [m_new, alpha * l + psum]
            alphas.append(alpha)
        return tuple(new_state), tuple(alphas)

    def value_stage(g, p_ref, alphas):
        js = tiles_of(g)
        for mm in range(2):
            pv = jnp.dot(vt_ref[0, 0, js[0]], p_ref[mm, 0], preferred_element_type=F32)
            for u in range(1, gsz):
                pv = pv + jnp.dot(vt_ref[0, 0, js[u]], p_ref[mm, u], preferred_element_type=F32)
            acc_ref[mm] = alphas[mm] * acc_ref[mm] + pv

    def trip(t, carry):
        state, alpha_prev = carry
        g0 = 2 * t
        scores_stage(g0 + 1, s1_ref)
        state, alpha0 = softmax_stage(g0, s0_ref, p0_ref, state)
        value_stage(g0 - 1, p1_ref, alpha_prev)
        scores_stage(g0 + 2, s0_ref)
        state, alpha1 = softmax_stage(g0 + 1, s1_ref, p1_ref, state)
        value_stage(g0, p0_ref, alpha0)
        return state, alpha1

    neg = jnp.full((1, tq), NEG_INF, F32)
    zer = jnp.zeros((1, tq), F32)
    one = jnp.ones((1, tq), F32)
    scores_stage(0, s0_ref)
    state, alpha_last = lax.fori_loop(0, n_trips, trip, ((neg, zer, neg, zer), (one, one)))
    value_stage(2 * n_trips - 1, p1_ref, alpha_last)
    stats = [(state[0], state[1]), (state[2], state[3])]

    lp = lam_ref[...]
    lam = (jnp.exp(jnp.sum(lp[0:1] * lp[1:2], axis=-1, keepdims=True))
           - jnp.exp(jnp.sum(lp[2:3] * lp[3:4], axis=-1, keepdims=True)) + LAMBDA_INIT)
    l1 = stats[0][1]
    l2 = stats[1][1]
    l1 = jnp.where(l1 == 0.0, 1.0, l1)
    l2 = jnp.where(l2 == 0.0, 1.0, l2)
    o_t = acc_ref[0] / l1 - lam * (acc_ref[1] / l2)
    o = o_t.T
    y = o * lax.rsqrt(jnp.mean(o * o, axis=-1, keepdims=True) + NORM_EPS)
    o_ref[0] = ((y * g_ref[...]) * (1.0 - LAMBDA_INIT)).astype(o_ref.dtype)


def _diff_attention(q, k, vt, slopes, lam_p, subln_g, *, batch, nq, tq, kt, q_pos0, coff, n_keys,
                    q_sec=0, k_sec=0, v_sec=0):
    rk = k.shape[1]
    nk = rk // kt
    nh = SEC // LANES
    return pl.pallas_call(
        functools.partial(_diff_kernel, tq=tq, kt=kt, q_pos0=q_pos0, coff=coff, n_keys=n_keys, nk=nk),
        grid=(batch, nh, nq),
        in_specs=[
            pl.BlockSpec(memory_space=pltpu.SMEM),
            pl.BlockSpec((4, DH_DIFF), lambda b, h, i: (0, 0)),
            pl.BlockSpec((1, DV_DIFF), lambda b, h, i: (0, 0)),
            pl.BlockSpec((1, tq, LANES), lambda b, h, i: (b + q_sec, i, h)),
            pl.BlockSpec((1, rk, LANES), lambda b, h, i: (b + k_sec, 0, h)),
            pl.BlockSpec((1, 1, nk, LANES, kt), lambda b, h, i: (b + v_sec, h, 0, 0, 0)),
        ],
        out_specs=pl.BlockSpec((1, tq, LANES), lambda b, h, i: (b, i, h)),
        out_shape=jax.ShapeDtypeStruct((batch, nq * tq, SEC), BF16),
        scratch_shapes=[pltpu.VMEM((2, LANES, tq), F32),
                        pltpu.VMEM((DIFF_GROUP * (1 + DIFF_TAIL_GROUPS) + 1, kt, tq), F32),
                        pltpu.VMEM((2, DIFF_GROUP, kt, tq), F32),
                        pltpu.VMEM((2, DIFF_GROUP, kt, tq), F32),
                        pltpu.VMEM((2, DIFF_GROUP, kt, tq), BF16),
                        pltpu.VMEM((2, DIFF_GROUP, kt, tq), BF16)],
        compiler_params=_cparams(("parallel", "parallel", "arbitrary")),
        name="diff_attention",
    )(slopes, lam_p, subln_g.reshape(1, DV_DIFF), q, k, vt)


def _merge_kernel(h_ref, osb_ref, od_ref, gt_ref, wsb_ref, wd_ref, wo_ref, gn_ref, wr_ref, br_ref,
                  h2_ref, hn_ref, te_ref, tg_ref):
    y_sb = jnp.dot(osb_ref[...], wsb_ref[...], preferred_element_type=F32)
    y_d = jnp.dot(od_ref[...], wd_ref[...], preferred_element_type=F32)
    gt = gt_ref[...]
    mix = gt[:, :D_MODEL] * y_sb + gt[:, D_MODEL:] * y_d
    h2 = h_ref[...] + jnp.dot(mix.astype(BF16), wo_ref[...], preferred_element_type=F32)
    h2_ref[...] = h2
    hn = (h2 * lax.rsqrt(jnp.mean(h2 * h2, axis=-1, keepdims=True) + NORM_EPS)) * gn_ref[...]
    hn_ref[...] = hn.astype(BF16)
    logits = jnp.dot(hn, wr_ref[...], preferred_element_type=F32,
                     precision=lax.Precision.HIGHEST) + br_ref[...]
    tm = logits.shape[0]
    lane = lax.broadcasted_iota(jnp.int32, (tm, LANES), 1).astype(F32)
    tops, idxs = [], []
    l = logits
    for _ in range(TOP_K):
        m = jnp.max(l, axis=-1, keepdims=True)
        idx = jnp.min(jnp.where(l == m, lane, float(LANES)), axis=-1, keepdims=True)
        tops.append(m)
        idxs.append(idx)
        l = jnp.where(lane == idx, NEG_INF, l)
    ex = [jnp.exp(t - tops[0]) for t in tops]
    den = ex[0] + ex[1] + ex[2] + ex[3]
    te = jnp.zeros((tm, LANES), F32)
    tg = jnp.zeros((tm, LANES), F32)
    for k in range(TOP_K):
        te = jnp.where(lane == float(k), idxs[k], te)
        tg = jnp.where(lane == float(k), ex[k] / den, tg)
    te_ref[...] = te.astype(jnp.int32)
    tg_ref[...] = tg


def _merge(h, o_sb, o_d, gates, w_sb, w_d, w_o, g_ffn, w_r, b_r, tm):
    r, d = h.shape
    row = lambda w: pl.BlockSpec((tm, w), lambda i: (i, 0))
    full = lambda a, b: pl.BlockSpec((a, b), lambda i: (0, 0))
    return pl.pallas_call(
        _merge_kernel,
        grid=(r // tm,),
        in_specs=[row(d), row(d), row(d), row(2 * d), full(d, d), full(d, d), full(d, d),
                  full(1, d), full(d, LANES), full(1, LANES)],
        out_specs=[row(d), row(d), row(LANES), row(LANES)],
        out_shape=[jax.ShapeDtypeStruct((r, d), F32), jax.ShapeDtypeStruct((r, d), BF16),
                   jax.ShapeDtypeStruct((r, LANES), jnp.int32), jax.ShapeDtypeStruct((r, LANES), F32)],
        compiler_params=_cparams(("parallel",)),
        name="merge_router",
    )(h, o_sb, o_d, gates, w_sb, w_d, w_o, g_ffn.reshape(1, d), w_r, b_r)


def _moe_kernel(be_ref, nb_ref, x_ref, gate_ref, wgu_ref, bgu_ref, wdn_ref, bdn_ref, o_ref):
    b = pl.program_id(0)

    @pl.when(b < nb_ref[0])
    def _():
        gu = jnp.dot(x_ref[...], wgu_ref[0], preferred_element_type=F32) + bgu_ref[0]
        g = jnp.minimum(gu[:, :D_FF], SWIGLU_LIMIT)
        u = jnp.clip(gu[:, D_FF:], -SWIGLU_LIMIT, SWIGLU_LIMIT)
        act = (u + 1.0) * (g * jax.nn.sigmoid(SWIGLU_ALPHA * g))
        y = jnp.dot(act.astype(BF16), wdn_ref[0], preferred_element_type=F32) + bdn_ref[0]
        o_ref[...] = y * gate_ref[...]

    @pl.when(b >= nb_ref[0])
    def _():
        o_ref[...] = jnp.zeros_like(o_ref)


def _moe_gmm(block_e, n_used, xs, row_gate, w_gu, b_gu, w_dn, b_dn, bm):
    n_rows, d = xs.shape
    nb = n_rows // bm
    grid_spec = pltpu.PrefetchScalarGridSpec(
        num_scalar_prefetch=2,
        grid=(nb,),
        in_specs=[
            pl.BlockSpec((bm, d), lambda b, be, nu: (b, 0)),
            pl.BlockSpec((bm, 1), lambda b, be, nu: (b, 0)),
            pl.BlockSpec((1, d, 2 * D_FF), lambda b, be, nu: (be[b], 0, 0)),
            pl.BlockSpec((1, 1, 2 * D_FF), lambda b, be, nu: (be[b], 0, 0)),
            pl.BlockSpec((1, D_FF, d), lambda b, be, nu: (be[b], 0, 0)),
            pl.BlockSpec((1, 1, d), lambda b, be, nu: (be[b], 0, 0)),
        ],
        out_specs=pl.BlockSpec((bm, d), lambda b, be, nu: (b, 0)),
    )
    return pl.pallas_call(
        _moe_kernel,
        grid_spec=grid_spec,
        out_shape=jax.ShapeDtypeStruct((n_rows, d), F32),
        compiler_params=_cparams(("arbitrary",)),
        name="moe_experts",
    )(block_e, n_used, xs, row_gate, w_gu, b_gu, w_dn, b_dn)


def _final_kernel(h_ref, c_ref, g_ref, o_ref):
    c = c_ref[...]
    d = h_ref.shape[1]
    y = c[:, 0:d]
    for k in range(1, TOP_K):
        y = y + c[:, k * d:(k + 1) * d]
    x = h_ref[...] + y
    o_ref[...] = (x * lax.rsqrt(jnp.mean(x * x, axis=-1, keepdims=True) + NORM_EPS)) * g_ref[...]


def _final(h2, contrib, g, tm):
    r, d = h2.shape
    return pl.pallas_call(
        _final_kernel,
        grid=(r // tm,),
        in_specs=[pl.BlockSpec((tm, d), lambda i: (i, 0)),
                  pl.BlockSpec((tm, TOP_K * d), lambda i: (i, 0)),
                  pl.BlockSpec((1, d), lambda i: (0, 0))],
        out_specs=pl.BlockSpec((tm, d), lambda i: (i, 0)),
        out_shape=jax.ShapeDtypeStruct((r, d), F32),
        compiler_params=_cparams(("parallel",)),
        name="combine_final_norm",
    )(h2, contrib, g.reshape(1, d))


def _route(top_e, top_g, bm):
    n = top_e.shape[0]
    a = n * TOP_K
    flat_e = top_e.reshape(-1)
    order = jnp.argsort(flat_e)
    se = flat_e[order]
    counts = jnp.zeros((N_EXPERTS,), jnp.int32).at[flat_e].add(1)
    padded = ((counts + bm - 1) // bm) * bm
    pend = jnp.cumsum(padded)
    pstart = pend - padded
    start = jnp.cumsum(counts) - counts
    rank = jnp.arange(a, dtype=jnp.int32) - start[se]
    dest_sorted = pstart[se] + rank
    n_rows = -(-(a + N_EXPERTS * (bm - 1)) // bm) * bm
    nb = n_rows // bm
    row_tok = jnp.full((n_rows,), n, jnp.int32).at[dest_sorted].set(order.astype(jnp.int32) // TOP_K)
    row_gate = jnp.zeros((n_rows,), F32).at[dest_sorted].set(top_g.reshape(-1)[order])
    pos = jnp.zeros((a,), jnp.int32).at[order].set(dest_sorted).reshape(n, TOP_K)
    block_e = jnp.clip(jnp.searchsorted(pend, jnp.arange(nb, dtype=jnp.int32) * bm, side="right"),
                       0, N_EXPERTS - 1).astype(jnp.int32)
    n_used = (pend[-1] // bm).astype(jnp.int32).reshape(1)
    return row_tok, row_gate, pos, block_e, n_used


def _tri(kt):
    s = lax.broadcasted_iota(jnp.int32, (kt, kt), 0)
    j = lax.broadcasted_iota(jnp.int32, (kt, kt), 1)
    return (j > s).astype(BF16)


def _tiles_t(v, kt):
    b, r, _ = v.shape
    return v.reshape(b, r // kt, kt, SEC // LANES, LANES).transpose(0, 3, 1, 4, 2)


def kernel(x_prompt, x_sample, cache_sb_k, cache_sb_v, cache_diff_k, cache_diff_v, meta_tokens,
           norm_mix_g, w_in, diff_lambda, diff_subln_g, w_br_sb, w_br_diff, w_out, norm_ffn_g,
           w_router, b_router, w_gate_up, b_gate_up, w_down, b_down, final_norm_g):
    assert x_prompt.shape[0] == 1 and w_in.shape[0] == 1
    d = D_MODEL
    seq = x_prompt.shape[1]
    t = N_META + seq
    tq_p, kt = Q_TILE_PROMPT, KEY_TILE
    tp = -(-t // tq_p) * tq_p
    nq_p = tp // tq_p
    nb_s, s_len = x_sample.shape[:2]
    past = cache_sb_k.shape[2]
    n_s = nb_s * s_len
    scale = DH_SB ** -0.5

    w0 = w_in[0]
    wq = jnp.concatenate([w0[:, 0:SEC], w0[:, 3 * SEC:4 * SEC]], axis=1).astype(BF16)
    wk = jnp.concatenate([w0[:, SEC:2 * SEC], w0[:, 4 * SEC:5 * SEC]], axis=1).astype(BF16)
    wv = jnp.concatenate([w0[:, 2 * SEC:3 * SEC], w0[:, 5 * SEC:6 * SEC]], axis=1).astype(BF16)
    wg = w0[:, 6 * SEC:8 * SEC].astype(BF16)
    w_sb = w_br_sb[0].astype(BF16)
    w_d = w_br_diff[0].astype(BF16)
    w_o = w_out[0].astype(BF16)
    w_r = jnp.pad(w_router[0], ((0, 0), (0, LANES - N_EXPERTS)))
    b_r = jnp.pad(b_router[0], (0, LANES - N_EXPERTS), constant_values=NEG_INF).reshape(1, LANES)
    w_gu = w_gate_up[0].astype(BF16)
    w_dn = w_down[0].astype(BF16)
    b_gu = b_gate_up[0].reshape(N_EXPERTS, 1, 2 * D_FF)
    b_dn = b_down[0].reshape(N_EXPERTS, 1, d)
    slopes = jnp.exp2(-8.0 * jnp.arange(1, H_DIFF + 1, dtype=F32) / H_DIFF)
    tri = _tri(kt)
    lam_p = diff_lambda[0]
    g_sub = diff_subln_g[0]

    hp = jnp.concatenate([meta_tokens.astype(F32), x_prompt[0], jnp.zeros((tp - t, d), F32)], axis=0)
    tm_p = tp // 13 if tp % 13 == 0 and (tp // 13) % kt == 0 else tq_p
    xn = _rmsnorm(hp, norm_mix_g[0], tm_p)
    (q2,) = _proj(xn, wq, "q", tm_p, scale=scale)
    kf, kb = _proj(xn, wk, "k", tm_p)
    vf, vt = _proj(xn, wv, "v", tm_p)
    (gates,) = _proj(xn, wg, "g", tm_p)
    o_sb = _sb_attention(q2, kb, vt, tri, batch=1, nq=nq_p, tq=tq_p, kt=kt, q_pos0=0)
    o_d = _diff_attention(q2, kb, vt, slopes, lam_p, g_sub, batch=1, nq=nq_p, tq=tq_p, kt=kt,
                          q_pos0=0, coff=N_META, n_keys=t, q_sec=1, k_sec=1, v_sec=1)
    h2, hn, te, tg = _merge(hp, o_sb[0], o_d[0], gates, w_sb, w_d, w_o, norm_ffn_g[0], w_r, b_r, tq_p)

    tq_s = Q_TILE_DECODE
    hs = x_sample.reshape(n_s, d)
    xn_s = _rmsnorm(hs, norm_mix_g[0], n_s)
    (q2_s,) = _proj(xn_s, wq, "q", n_s, scale=scale)
    kf_s, kb_s = _proj(xn_s, wk, "k", n_s)
    vf_s, _ = _proj(xn_s, wv, "v", n_s, kt=n_s)
    (gates_s,) = _proj(xn_s, wg, "g", n_s)
    rk_s = -(-(past + s_len) // kt) * kt

    def with_cache(cache, new):
        c = cache[0].reshape(nb_s, past, SEC).astype(BF16)
        nw = new.reshape(nb_s, s_len, SEC).astype(BF16)
        z = jnp.zeros((nb_s, rk_s - past - s_len, SEC), BF16)
        return jnp.concatenate([c, nw, z], axis=1)

    def pad_q(qs):
        return jnp.pad(qs.reshape(nb_s, s_len, SEC), ((0, 0), (0, tq_s - s_len), (0, 0)))

    k_sb_all = with_cache(cache_sb_k, kf_s[0])
    vt_sb_all = _tiles_t(with_cache(cache_sb_v, vf_s[0]), kt)
    k_d_all = with_cache(cache_diff_k, kf_s[1])
    vt_d_all = _tiles_t(with_cache(cache_diff_v, vf_s[1]), kt)
    o_sb_s = _sb_attention(pad_q(q2_s[0]), k_sb_all, vt_sb_all, tri, batch=nb_s, nq=1, tq=tq_s, kt=kt,
                           q_pos0=past)
    o_d_s = _diff_attention(pad_q(q2_s[1]), k_d_all, vt_d_all, slopes, lam_p, g_sub, batch=nb_s, nq=1,
                            tq=tq_s, kt=kt, q_pos0=past, coff=0, n_keys=past + s_len)
    o_sb_s = o_sb_s[:, :s_len].reshape(n_s, SEC)
    o_d_s = o_d_s[:, :s_len].reshape(n_s, SEC)
    h2_s, hn_s, te_s, tg_s = _merge(hs, o_sb_s, o_d_s, gates_s, w_sb, w_d, w_o, norm_ffn_g[0], w_r, b_r,
                                    n_s)

    n_tok = t + n_s
    top_e = jnp.concatenate([te[:t, :TOP_K], te_s[:, :TOP_K]], axis=0)
    top_g = jnp.concatenate([tg[:t, :TOP_K], tg_s[:, :TOP_K]], axis=0)
    row_tok, row_gate, pos, block_e, n_used = _route(top_e, top_g, MOE_ROWS)
    hn_all = jnp.concatenate([hn[:t], hn_s, jnp.zeros((1, d), BF16)], axis=0)
    xs = hn_all[row_tok]
    yb = _moe_gmm(block_e, n_used, xs, row_gate.reshape(-1, 1), w_gu, b_gu, w_dn, b_dn, MOE_ROWS)
    pos_p = jnp.pad(pos[:t], ((0, tp - t), (0, 0)))
    contrib_p = yb[pos_p].reshape(tp, TOP_K * d)
    contrib_s = yb[pos[t:]].reshape(n_s, TOP_K * d)
    y_p = _final(h2, contrib_p, final_norm_g, tq_p)
    y_s = _final(h2_s, contrib_s, final_norm_g, n_s)

    y_prompt = y_p[N_META:t][None]
    y_sample = y_s.reshape(nb_s, s_len, d)

    def heads(a, nh):
        return a.reshape(1, *a.shape[:-1], nh, a.shape[-1] // nh)

    return (y_prompt, y_sample,
            heads(kf[0, :t][None], H_SB), heads(vf[0, :t][None], H_SB),
            heads(kf[1, :t][None], H_DIFF), heads(vf[1, :t][None], H_DIFF),
            heads(kf_s[0].reshape(nb_s, s_len, SEC), H_SB), heads(vf_s[0].reshape(nb_s, s_len, SEC), H_SB),
            heads(kf_s[1].reshape(nb_s, s_len, SEC), H_DIFF), heads(vf_s[1].reshape(nb_s, s_len, SEC), H_DIFF))
```

```python
import functools
import math

import jax
import jax.numpy as jnp
from jax import lax
from jax.experimental import pallas as pl
from jax.experimental.pallas import tpu as pltpu

F32 = jnp.float32
BF16 = jnp.bfloat16

D_MODEL = 1024
CHUNK = 64
CHUNK_SHIFT = 6
N_META = 16
H_SB = 16
DH_SB = 64
H_DIFF = 8
DH_DIFF = 64
DV_DIFF = 128
N_EXPERTS = 32
TOP_K = 4
D_FF = 1024
SWIGLU_LIMIT = 7.0
SWIGLU_ALPHA = 1.702
NORM_EPS = 1e-5
LAMBDA_INIT = 0.8 - 0.6 * math.exp(-0.3 * 0)

LANES = 128
SEC = 1024
KEY_TILE = 256
Q_TILE_PROMPT = 256
Q_TILE_DECODE = 128
MOE_ROWS = 256
VMEM_LIMIT = 56 * 1024 * 1024
NEG_INF = float("-inf")
SB_SKIP_LOG = 104.0
SB_ABSENT = 1e30
DIFF_GROUP = 2
DIFF_TAIL_GROUPS = 2
DIFF_SKIP_LOG = 105.0
DIFF_NORM_SLACK = 1.01


def _cparams(sem):
    return pltpu.CompilerParams(dimension_semantics=sem, vmem_limit_bytes=VMEM_LIMIT)


def _rmsnorm_kernel(x_ref, g_ref, o_ref):
    x = x_ref[...]
    y = x * lax.rsqrt(jnp.mean(x * x, axis=-1, keepdims=True) + NORM_EPS)
    o_ref[...] = (y * g_ref[...]).astype(o_ref.dtype)


def _rmsnorm(x, g, tm):
    r, d = x.shape
    return pl.pallas_call(
        _rmsnorm_kernel,
        grid=(r // tm,),
        in_specs=[pl.BlockSpec((tm, d), lambda i: (i, 0)),
                  pl.BlockSpec((1, d), lambda i: (0, 0))],
        out_specs=pl.BlockSpec((tm, d), lambda i: (i, 0)),
        out_shape=jax.ShapeDtypeStruct((r, d), BF16),
        compiler_params=_cparams(("parallel",)),
        name="rmsnorm",
    )(x, g.reshape(1, d))


def _proj_kernel(x_ref, w_ref, *out_refs, mode, scale, kt):
    x = x_ref[...]
    for s in range(2):
        acc = jnp.dot(x, w_ref[:, s * SEC:(s + 1) * SEC], preferred_element_type=F32)
        if mode == "q":
            out_refs[s][...] = (acc * scale).astype(BF16)
        elif mode == "k":
            out_refs[2 * s][...] = acc
            out_refs[2 * s + 1][...] = acc.astype(BF16)
        elif mode == "v":
            out_refs[2 * s][...] = acc
            for c in range(acc.shape[0] // kt):
                for hp in range(SEC // LANES):
                    tile = acc[c * kt:(c + 1) * kt, hp * LANES:(hp + 1) * LANES]
                    out_refs[2 * s + 1][hp, c] = tile.T.astype(BF16)
        else:
            out_refs[0][:, s * SEC:(s + 1) * SEC] = jax.nn.sigmoid(acc)


def _proj(xn, w2, mode, tm, r_out=None, kt=KEY_TILE, scale=1.0):
    r, d = xn.shape
    r_out = r if r_out is None else r_out
    in_specs = [pl.BlockSpec((tm, d), lambda i: (i, 0)),
                pl.BlockSpec((d, 2 * SEC), lambda i: (0, 0))]
    row_spec = pl.BlockSpec((tm, SEC), lambda i: (i, 0))
    f32_out = jax.ShapeDtypeStruct((r_out, SEC), F32)
    bf16_out = jax.ShapeDtypeStruct((r, SEC), BF16)
    if mode == "q":
        out_specs = [row_spec, row_spec]
        out_shape = [bf16_out, bf16_out]
    elif mode == "k":
        out_specs = [row_spec] * 4
        out_shape = [f32_out, bf16_out, f32_out, bf16_out]
    elif mode == "v":
        t_spec = pl.BlockSpec((SEC // LANES, tm // kt, LANES, kt), lambda i: (0, i, 0, 0))
        t_out = jax.ShapeDtypeStruct((SEC // LANES, r // kt, LANES, kt), BF16)
        out_specs = [row_spec, t_spec, row_spec, t_spec]
        out_shape = [f32_out, t_out, f32_out, t_out]
    else:
        out_specs = [pl.BlockSpec((tm, 2 * SEC), lambda i: (i, 0))]
        out_shape = [jax.ShapeDtypeStruct((r, 2 * SEC), F32)]
    return pl.pallas_call(
        functools.partial(_proj_kernel, mode=mode, scale=scale, kt=kt),
        grid=(r // tm,), in_specs=in_specs, out_specs=out_specs, out_shape=out_shape,
        compiler_params=_cparams(("parallel",)),
        name="proj_" + mode,
    )(xn, w2)


def _sb_kernel(q_ref, k_ref, vt_ref, tri_ref, o_ref, acc_ref, *, tq, kt, q_pos0):
    qi = pl.program_id(2)
    q0 = q_pos0 + qi * tq
    jm = q0 // kt
    q = q_ref[0]
    lane = lax.broadcasted_iota(jnp.int32, (tq, LANES), 1)
    kpos_l = lax.broadcasted_iota(jnp.int32, (kt, tq), 0)
    qpos_l = lax.broadcasted_iota(jnp.int32, (kt, tq), 1)
    tri = tri_ref[...]
    zero = jnp.zeros_like(q)
    qz = (jnp.where(lane < DH_SB, q, zero), jnp.where(lane >= DH_SB, q, zero))
    acc_ref[...] = jnp.zeros_like(acc_ref)

    def tile(j, hh, carry, masked):
        off = pl.multiple_of(j * kt, kt)
        kk = k_ref[0, pl.ds(off, kt), :]
        z = lax.dot_general(kk, qz[hh], (((1,), (1,)), ((), ())),
                            preferred_element_type=F32)
        sp = jnp.maximum(z, 0.0) + jnp.log(1.0 + jnp.exp(-jnp.abs(z)))
        if masked:
            vis = (kpos_l + j * kt) < (qpos_l + q0)
            lk = jnp.where(vis, -sp, 0.0)
        else:
            lk = -sp
        later = jnp.dot(tri, lk.astype(BF16), preferred_element_type=F32)
        w = jnp.exp(z + lk + later + carry)
        if masked:
            w = jnp.where(vis, w, 0.0)
        acc_ref[hh] += jnp.dot(vt_ref[0, 0, j], w.astype(BF16), preferred_element_type=F32)
        return jnp.sum(lk, axis=0, keepdims=True)

    none_later = jnp.zeros((1, tq), F32)
    carries = tuple(tile(jm, hh, none_later, True) for hh in range(2))

    def cond(state):
        t, ca, cb = state
        return jnp.logical_and(jm - 1 - 2 * t >= 0,
                               jnp.max(jnp.maximum(ca, cb)) > -SB_SKIP_LOG)

    def body(state):
        t, ca, cb = state
        j0 = jm - 1 - 2 * t
        j1 = j0 - 1
        j1c = jnp.maximum(j1, 0)
        out = []
        for hh, carry in ((0, ca), (1, cb)):
            s0 = tile(j0, hh, carry, False)
            carry1 = jnp.where(j1 >= 0, carry + s0, -SB_ABSENT)
            s1 = tile(j1c, hh, carry1, False)
            out.append(carry + s0 + s1)
        return t + 1, out[0], out[1]

    lax.while_loop(cond, body, (jnp.int32(0), carries[0], carries[1]))

    row = lax.broadcasted_iota(jnp.int32, (LANES, tq), 0)
    o_t = jnp.where(row < DH_SB, acc_ref[0], acc_ref[1])
    o_ref[0] = o_t.T.astype(o_ref.dtype)


def _sb_attention(q, k, vt, tri, *, batch, nq, tq, kt, q_pos0, q_sec=0, k_sec=0, v_sec=0):
    rk = k.shape[1]
    nk = rk // kt
    nhp = SEC // LANES
    return pl.pallas_call(
        functools.partial(_sb_kernel, tq=tq, kt=kt, q_pos0=q_pos0),
        grid=(batch, nhp, nq),
        in_specs=[
            pl.BlockSpec((1, tq, LANES), lambda b, h, i: (b + q_sec, i, h)),
            pl.BlockSpec((1, rk, LANES), lambda b, h, i: (b + k_sec, 0, h)),
            pl.BlockSpec((1, 1, nk, LANES, kt), lambda b, h, i: (b + v_sec, h, 0, 0, 0)),
            pl.BlockSpec((kt, kt), lambda b, h, i: (0, 0)),
        ],
        out_specs=pl.BlockSpec((1, tq, LANES), lambda b, h, i: (b, i, h)),
        out_shape=jax.ShapeDtypeStruct((batch, nq * tq, SEC), BF16),
        scratch_shapes=[pltpu.VMEM((2, LANES, tq), F32)],
        compiler_params=_cparams(("parallel", "parallel", "arbitrary")),
        name="sb_attention",
    )(q, k, vt, tri)


def _diff_kernel(slopes_ref, lam_ref, g_ref, q_ref, k_ref, vt_ref, o_ref,
                 acc_ref, bt_ref, s0_ref, s1_ref, p0_ref, p1_ref, kn_ref,
                 *, tq, kt, q_pos0, coff, n_keys, nk):
    h = pl.program_id(1)
    qi = pl.program_id(2)
    slope = slopes_ref[0, h]
    inv_slope = slopes_ref[1, h]
    q0 = q_pos0 + qi * tq
    c_lo = (q0 - coff + CHUNK) // CHUNK - 1
    c_hi = (q0 + tq - 1 - coff + CHUNK) // CHUNK - 1
    full_end = jnp.minimum(coff + CHUNK * (c_lo + 1), n_keys)
    vis_end = jnp.minimum(coff + CHUNK * (c_hi + 1), n_keys)
    n_full = full_end // kt
    n_vis = jnp.minimum((vis_end + kt - 1) // kt, nk)

    q = q_ref[0]
    lane = lax.broadcasted_iota(jnp.int32, (tq, LANES), 1)
    kpos_l = lax.broadcasted_iota(jnp.int32, (kt, tq), 0)
    qpos_l = lax.broadcasted_iota(jnp.int32, (kt, tq), 1)
    gsz = DIFF_GROUP
    m_tiles = (n_full // gsz) * gsz
    n_fullg = m_tiles // gsz
    n_groups = n_fullg + DIFF_TAIL_GROUPS
    n_tail = DIFF_TAIL_GROUPS * gsz
    absent = gsz + n_tail

    for u in range(gsz):
        bt_ref[u] = slope * (kpos_l + u * kt).astype(F32)
    for u in range(n_tail):
        kpos = kpos_l + (m_tiles + u) * kt
        qpos = qpos_l + q0
        vis = (((kpos - coff + CHUNK) >> CHUNK_SHIFT) <= ((qpos - coff + CHUNK) >> CHUNK_SHIFT)) \
            & (kpos < n_keys)
        bias = slope * (qpos_l - jnp.abs(qpos - kpos)).astype(F32)
        bt_ref[gsz + u] = jnp.where(vis, bias, NEG_INF)
    bt_ref[absent] = jnp.full((kt, tq), NEG_INF, F32)

    zero = jnp.zeros_like(q)
    qz = (jnp.where(lane < DH_DIFF, q, zero), jnp.where(lane >= DH_DIFF, q, zero))
    acc_ref[...] = jnp.zeros_like(acc_ref)
    p1_ref[...] = jnp.zeros_like(p1_ref)

    hr = lax.broadcasted_iota(jnp.int32, (LANES, LANES), 0)
    hc = lax.broadcasted_iota(jnp.int32, (LANES, LANES), 1)
    half = (((hc == 0) & (hr < DH_DIFF)) | ((hc == 1) & (hr >= DH_DIFF))).astype(BF16)
    lane1 = lax.broadcasted_iota(jnp.int32, (1, LANES), 1)

    def max_half_norms(sq_max):
        nrm = jnp.sqrt(sq_max)
        return (jnp.max(jnp.where(lane1 == 0, nrm, 0.0)), jnp.max(jnp.where(lane1 == 1, nrm, 0.0)))

    @pl.when(qi == 0)
    def _():
        def body(j, mx):
            off = pl.multiple_of(j * kt, kt)
            kk = k_ref[0, pl.ds(off, kt), :]
            n2 = jnp.dot(kk * kk, half, preferred_element_type=F32)
            return jnp.maximum(mx, jnp.max(n2, axis=0, keepdims=True))
        kn = max_half_norms(lax.fori_loop(0, nk, body, jnp.zeros((1, LANES), F32)))
        kn_ref[0] = kn[0]
        kn_ref[1] = kn[1]

    qn = max_half_norms(jnp.max(jnp.dot(q * q, half, preferred_element_type=F32), axis=0, keepdims=True))
    qk_bound = [DIFF_NORM_SLACK * qn[mm] * kn_ref[mm] for mm in range(2)]
    q0_f = q0.astype(F32)

    def skip_below(state):
        need = jnp.maximum((DIFF_SKIP_LOG + qk_bound[0]) - state[0], (DIFF_SKIP_LOG + qk_bound[1]) - state[2])
        return q0_f - jnp.max(need) * inv_slope

    def seq_group(n):
        return n_groups - 1 - n

    def needed(g, thr):
        top = ((g + 1) * (gsz * kt)).astype(F32)
        return jnp.logical_and(g >= 0, jnp.logical_or(g >= n_fullg, top > thr))

    def tiles_of(g):
        return [jnp.clip(g * gsz + u, 0, nk - 1) for u in range(gsz)]

    def scores_stage(g, s_ref):
        js = tiles_of(g)
        for mm in range(2):
            for u in range(gsz):
                off = pl.multiple_of(js[u] * kt, kt)
                kk = k_ref[0, pl.ds(off, kt), :]
                s_ref[mm, u] = lax.dot_general(kk, qz[mm], (((1,), (1,)), ((), ())),
                                               preferred_element_type=F32)

    def softmax_stage(g, thr, s_ref, p_ref, state):
        ok = needed(g, thr)
        full = jnp.logical_and(ok, g < n_fullg)
        sj = jnp.where(full, slope * (g * gsz * kt - q0).astype(F32), 0.0)
        bidx = [jnp.where(full, u, jnp.where(ok, gsz + g * gsz + u - m_tiles, absent)) for u in range(gsz)]
        new_state, alphas = [], []
        for mm in range(2):
            m, l = state[2 * mm], state[2 * mm + 1]
            ss = [s_ref[mm, u] + bt_ref[bidx[u]] for u in range(gsz)]
            cmax = jnp.max(ss[0], axis=0, keepdims=True)
            for s in ss[1:]:
                cmax = jnp.maximum(cmax, jnp.max(s, axis=0, keepdims=True))
            m_new = jnp.maximum(m, cmax + sj)
            m_safe = jnp.where(m_new == NEG_INF, 0.0, m_new)
            alpha = jnp.exp(m - m_safe)
            r = m_safe - sj
            psum = jnp.zeros((1, tq), F32)
            for u in range(gsz):
                p = jnp.exp(ss[u] - r)
                psum = psum + jnp.sum(p, axis=0, keepdims=True)
                p_ref[mm, u] = p.astype(BF16)
            new_state += [m_new, alpha * l + psum]
            alphas.append(alpha)
        return tuple(new_state), tuple(alphas)

    def value_stage(g, p_ref, alphas):
        js = tiles_of(g)
        for mm in range(2):
            pv = jnp.dot(vt_ref[0, 0, js[0]], p_ref[mm, 0], preferred_element_type=F32)
            for u in range(1, gsz):
                pv = pv + jnp.dot(vt_ref[0, 0, js[u]], p_ref[mm, u], preferred_element_type=F32)
            acc_ref[mm] = alphas[mm] * acc_ref[mm] + pv

    def trip(carry):
        t, thr, state, alpha_prev = carry
        n0 = 2 * t
        scores_stage(seq_group(n0 + 1), s1_ref)
        state, alpha0 = softmax_stage(seq_group(n0), thr, s0_ref, p0_ref, state)
        value_stage(seq_group(n0 - 1), p1_ref, alpha_prev)
        scores_stage(seq_group(n0 + 2), s0_ref)
        state, alpha1 = softmax_stage(seq_group(n0 + 1), thr, s1_ref, p1_ref, state)
        value_stage(seq_group(n0), p0_ref, alpha0)
        return t + 1, skip_below(state), state, alpha1

    def more(carry):
        t, thr = carry[0], carry[1]
        return needed(seq_group(2 * t), thr)

    neg = jnp.full((1, tq), NEG_INF, F32)
    zer = jnp.zeros((1, tq), F32)
    one = jnp.ones((1, tq), F32)
    scores_stage(seq_group(0), s0_ref)
    n_done, _, state, alpha_last = lax.while_loop(
        more, trip, (jnp.int32(0), jnp.float32(NEG_INF), (neg, zer, neg, zer), (one, one)))
    value_stage(seq_group(2 * n_done - 1), p1_ref, alpha_last)
    stats = [(state[0], state[1]), (state[2], state[3])]

    lp = lam_ref[...]
    lam = (jnp.exp(jnp.sum(lp[0:1] * lp[1:2], axis=-1, keepdims=True))
           - jnp.exp(jnp.sum(lp[2:3] * lp[3:4], axis=-1, keepdims=True)) + LAMBDA_INIT)
    l1 = stats[0][1]
    l2 = stats[1][1]
    l1 = jnp.where(l1 == 0.0, 1.0, l1)
    l2 = jnp.where(l2 == 0.0, 1.0, l2)
    o_t = acc_ref[0] / l1 - lam * (acc_ref[1] / l2)
    o = o_t.T
    y = o * lax.rsqrt(jnp.mean(o * o, axis=-1, keepdims=True) + NORM_EPS)
    o_ref[0] = ((y * g_ref[...]) * (1.0 - LAMBDA_INIT)).astype(o_ref.dtype)


def _diff_attention(q, k, vt, slopes, lam_p, subln_g, *, batch, nq, tq, kt, q_pos0, coff, n_keys,
                    q_sec=0, k_sec=0, v_sec=0):
    rk = k.shape[1]
    nk = rk // kt
    nh = SEC // LANES
    return pl.pallas_call(
        functools.partial(_diff_kernel, tq=tq, kt=kt, q_pos0=q_pos0, coff=coff, n_keys=n_keys, nk=nk),
        grid=(batch, nh, nq),
        in_specs=[
            pl.BlockSpec(memory_space=pltpu.SMEM),
            pl.BlockSpec((4, DH_DIFF), lambda b, h, i: (0, 0)),
            pl.BlockSpec((1, DV_DIFF), lambda b, h, i: (0, 0)),
            pl.BlockSpec((1, tq, LANES), lambda b, h, i: (b + q_sec, i, h)),
            pl.BlockSpec((1, rk, LANES), lambda b, h, i: (b + k_sec, 0, h)),
            pl.BlockSpec((1, 1, nk, LANES, kt), lambda b, h, i: (b + v_sec, h, 0, 0, 0)),
        ],
        out_specs=pl.BlockSpec((1, tq, LANES), lambda b, h, i: (b, i, h)),
        out_shape=jax.ShapeDtypeStruct((batch, nq * tq, SEC), BF16),
        scratch_shapes=[pltpu.VMEM((2, LANES, tq), F32),
                        pltpu.VMEM((DIFF_GROUP * (1 + DIFF_TAIL_GROUPS) + 1, kt, tq), F32),
                        pltpu.VMEM((2, DIFF_GROUP, kt, tq), F32),
                        pltpu.VMEM((2, DIFF_GROUP, kt, tq), F32),
                        pltpu.VMEM((2, DIFF_GROUP, kt, tq), BF16),
                        pltpu.VMEM((2, DIFF_GROUP, kt, tq), BF16),
                        pltpu.SMEM((2,), F32)],
        compiler_params=_cparams(("parallel", "parallel", "arbitrary")),
        name="diff_attention",
    )(slopes, lam_p, subln_g.reshape(1, DV_DIFF), q, k, vt)


def _merge_kernel(h_ref, osb_ref, od_ref, gt_ref, wsb_ref, wd_ref, wo_ref, gn_ref, wr_ref, br_ref,
                  h2_ref, hn_ref, te_ref, tg_ref):
    y_sb = jnp.dot(osb_ref[...], wsb_ref[...], preferred_element_type=F32)
    y_d = jnp.dot(od_ref[...], wd_ref[...], preferred_element_type=F32)
    gt = gt_ref[...]
    mix = gt[:, :D_MODEL] * y_sb + gt[:, D_MODEL:] * y_d
    h2 = h_ref[...] + jnp.dot(mix.astype(BF16), wo_ref[...], preferred_element_type=F32)
    h2_ref[...] = h2
    hn = (h2 * lax.rsqrt(jnp.mean(h2 * h2, axis=-1, keepdims=True) + NORM_EPS)) * gn_ref[...]
    hn_ref[...] = hn.astype(BF16)
    logits = jnp.dot(hn, wr_ref[...], preferred_element_type=F32,
                     precision=lax.Precision.HIGHEST) + br_ref[...]
    tm = logits.shape[0]
    lane = lax.broadcasted_iota(jnp.int32, (tm, LANES), 1).astype(F32)
    tops, idxs = [], []
    l = logits
    for _ in range(TOP_K):
        m = jnp.max(l, axis=-1, keepdims=True)
        idx = jnp.min(jnp.where(l == m, lane, float(LANES)), axis=-1, keepdims=True)
        tops.append(m)
        idxs.append(idx)
        l = jnp.where(lane == idx, NEG_INF, l)
    ex = [jnp.exp(t - tops[0]) for t in tops]
    den = ex[0] + ex[1] + ex[2] + ex[3]
    te = jnp.zeros((tm, LANES), F32)
    tg = jnp.zeros((tm, LANES), F32)
    for k in range(TOP_K):
        te = jnp.where(lane == float(k), idxs[k], te)
        tg = jnp.where(lane == float(k), ex[k] / den, tg)
    te_ref[...] = te.astype(jnp.int32)
    tg_ref[...] = tg


def _merge(h, o_sb, o_d, gates, w_sb, w_d, w_o, g_ffn, w_r, b_r, tm):
    r, d = h.shape
    row = lambda w: pl.BlockSpec((tm, w), lambda i: (i, 0))
    full = lambda a, b: pl.BlockSpec((a, b), lambda i: (0, 0))
    return pl.pallas_call(
        _merge_kernel,
        grid=(r // tm,),
        in_specs=[row(d), row(d), row(d), row(2 * d), full(d, d), full(d, d), full(d, d),
                  full(1, d), full(d, LANES), full(1, LANES)],
        out_specs=[row(d), row(d), row(LANES), row(LANES)],
        out_shape=[jax.ShapeDtypeStruct((r, d), F32), jax.ShapeDtypeStruct((r, d), BF16),
                   jax.ShapeDtypeStruct((r, LANES), jnp.int32), jax.ShapeDtypeStruct((r, LANES), F32)],
        compiler_params=_cparams(("parallel",)),
        name="merge_router",
    )(h, o_sb, o_d, gates, w_sb, w_d, w_o, g_ffn.reshape(1, d), w_r, b_r)


def _moe_kernel(be_ref, nb_ref, x_ref, gate_ref, wgu_ref, bgu_ref, wdn_ref, bdn_ref, o_ref,
                wgu_bf, wdn_bf):
    b = pl.program_id(0)
    used = b < nb_ref[0]
    new_expert = jnp.logical_or(b == 0, be_ref[b] != be_ref[jnp.maximum(b - 1, 0)])

    @pl.when(jnp.logical_and(used, new_expert))
    def _():
        wgu_bf[...] = wgu_ref[0].astype(BF16)
        wdn_bf[...] = wdn_ref[0].astype(BF16)

    @pl.when(used)
    def _():
        gu = jnp.dot(x_ref[...], wgu_bf[...], preferred_element_type=F32) + bgu_ref[0]
        g = jnp.minimum(gu[:, :D_FF], SWIGLU_LIMIT)
        u = jnp.clip(gu[:, D_FF:], -SWIGLU_LIMIT, SWIGLU_LIMIT)
        act = (u + 1.0) * (g * jax.nn.sigmoid(SWIGLU_ALPHA * g))
        y = jnp.dot(act.astype(BF16), wdn_bf[...], preferred_element_type=F32) + bdn_ref[0]
        o_ref[...] = y * gate_ref[...]

    @pl.when(b >= nb_ref[0])
    def _():
        o_ref[...] = jnp.zeros_like(o_ref)


def _moe_gmm(block_e, n_used, xs, row_gate, w_gu, b_gu, w_dn, b_dn, bm):
    n_rows, d = xs.shape
    nb = n_rows // bm
    grid_spec = pltpu.PrefetchScalarGridSpec(
        num_scalar_prefetch=2,
        grid=(nb,),
        in_specs=[
            pl.BlockSpec((bm, d), lambda b, be, nu: (b, 0)),
            pl.BlockSpec((bm, 1), lambda b, be, nu: (b, 0)),
            pl.BlockSpec((1, d, 2 * D_FF), lambda b, be, nu: (be[b], 0, 0)),
            pl.BlockSpec((1, 1, 2 * D_FF), lambda b, be, nu: (be[b], 0, 0)),
            pl.BlockSpec((1, D_FF, d), lambda b, be, nu: (be[b], 0, 0)),
            pl.BlockSpec((1, 1, d), lambda b, be, nu: (be[b], 0, 0)),
        ],
        out_specs=pl.BlockSpec((bm, d), lambda b, be, nu: (b, 0)),
        scratch_shapes=[pltpu.VMEM((d, 2 * D_FF), BF16), pltpu.VMEM((D_FF, d), BF16)],
    )
    return pl.pallas_call(
        _moe_kernel,
        grid_spec=grid_spec,
        out_shape=jax.ShapeDtypeStruct((n_rows, d), F32),
        compiler_params=_cparams(("arbitrary",)),
        name="moe_experts",
    )(block_e, n_used, xs, row_gate, w_gu, b_gu, w_dn, b_dn)


def _final_kernel(h_ref, c_ref, g_ref, o_ref):
    c = c_ref[...]
    d = h_ref.shape[1]
    y = c[:, 0:d]
    for k in range(1, TOP_K):
        y = y + c[:, k * d:(k + 1) * d]
    x = h_ref[...] + y
    o_ref[...] = (x * lax.rsqrt(jnp.mean(x * x, axis=-1, keepdims=True) + NORM_EPS)) * g_ref[...]


def _final(h2, contrib, g, tm):
    r, d = h2.shape
    return pl.pallas_call(
        _final_kernel,
        grid=(r // tm,),
        in_specs=[pl.BlockSpec((tm, d), lambda i: (i, 0)),
                  pl.BlockSpec((tm, TOP_K * d), lambda i: (i, 0)),
                  pl.BlockSpec((1, d), lambda i: (0, 0))],
        out_specs=pl.BlockSpec((tm, d), lambda i: (i, 0)),
        out_shape=jax.ShapeDtypeStruct((r, d), F32),
        compiler_params=_cparams(("parallel",)),
        name="combine_final_norm",
    )(h2, contrib, g.reshape(1, d))


def _route(top_e, top_g, bm):
    n = top_e.shape[0]
    a = n * TOP_K
    flat_e = top_e.reshape(-1)
    order = jnp.argsort(flat_e).astype(jnp.int32)
    onehot = (flat_e[:, None] == jnp.arange(N_EXPERTS, dtype=jnp.int32)[None, :]).astype(jnp.int32)
    csum = jnp.cumsum(onehot, axis=0)
    counts = csum[-1]
    rank = jnp.sum(onehot * csum, axis=1) - 1
    padded = ((counts + bm - 1) // bm) * bm
    pend = jnp.cumsum(padded)
    pstart = pend - padded
    start = jnp.cumsum(counts) - counts
    pos = (jnp.sum(onehot * pstart[None, :], axis=1) + rank).reshape(n, TOP_K)
    n_rows = -(-(a + N_EXPERTS * (bm - 1)) // bm) * bm
    nb = n_rows // bm
    block_start = jnp.arange(nb, dtype=jnp.int32) * bm
    block_e = jnp.minimum(jnp.sum((pend[None, :] <= block_start[:, None]).astype(jnp.int32), axis=1),
                          N_EXPERTS - 1)
    row = jnp.arange(n_rows, dtype=jnp.int32)
    row_e = jnp.repeat(block_e, bm, total_repeat_length=n_rows)
    row_rank = row - pstart[row_e]
    valid = jnp.logical_and(row_rank >= 0, row_rank < counts[row_e])
    src = order[jnp.clip(start[row_e] + row_rank, 0, a - 1)]
    row_tok = jnp.where(valid, src // TOP_K, n)
    row_gate = jnp.where(valid, top_g.reshape(-1)[src], 0.0)
    n_used = (pend[-1] // bm).astype(jnp.int32).reshape(1)
    return row_tok, row_gate, pos, block_e, n_used


def _tri(kt):
    s = lax.broadcasted_iota(jnp.int32, (kt, kt), 0)
    j = lax.broadcasted_iota(jnp.int32, (kt, kt), 1)
    return (j > s).astype(BF16)


def _tiles_t(v, kt):
    b, r, _ = v.shape
    return v.reshape(b, r // kt, kt, SEC // LANES, LANES).transpose(0, 3, 1, 4, 2)


def kernel(x_prompt, x_sample, cache_sb_k, cache_sb_v, cache_diff_k, cache_diff_v, meta_tokens,
           norm_mix_g, w_in, diff_lambda, diff_subln_g, w_br_sb, w_br_diff, w_out, norm_ffn_g,
           w_router, b_router, w_gate_up, b_gate_up, w_down, b_down, final_norm_g):
    assert x_prompt.shape[0] == 1 and w_in.shape[0] == 1
    d = D_MODEL
    seq = x_prompt.shape[1]
    t = N_META + seq
    tq_p, kt = Q_TILE_PROMPT, KEY_TILE
    tp = -(-t // tq_p) * tq_p
    nq_p = tp // tq_p
    nb_s, s_len = x_sample.shape[:2]
    past = cache_sb_k.shape[2]
    n_s = nb_s * s_len
    scale = DH_SB ** -0.5

    w0 = w_in[0]
    wq = jnp.concatenate([w0[:, 0:SEC], w0[:, 3 * SEC:4 * SEC]], axis=1).astype(BF16)
    wk = jnp.concatenate([w0[:, SEC:2 * SEC], w0[:, 4 * SEC:5 * SEC]], axis=1).astype(BF16)
    wv = jnp.concatenate([w0[:, 2 * SEC:3 * SEC], w0[:, 5 * SEC:6 * SEC]], axis=1).astype(BF16)
    wg = w0[:, 6 * SEC:8 * SEC].astype(BF16)
    w_sb = w_br_sb[0].astype(BF16)
    w_d = w_br_diff[0].astype(BF16)
    w_o = w_out[0].astype(BF16)
    w_r = jnp.pad(w_router[0], ((0, 0), (0, LANES - N_EXPERTS)))
    b_r = jnp.pad(b_router[0], (0, LANES - N_EXPERTS), constant_values=NEG_INF).reshape(1, LANES)
    w_gu = w_gate_up[0]
    w_dn = w_down[0]
    b_gu = b_gate_up[0].reshape(N_EXPERTS, 1, 2 * D_FF)
    b_dn = b_down[0].reshape(N_EXPERTS, 1, d)
    slope = jnp.exp2(-8.0 * jnp.arange(1, H_DIFF + 1, dtype=F32) / H_DIFF)
    slopes = jnp.stack([slope, 1.0 / slope])
    tri = _tri(kt)
    lam_p = diff_lambda[0]
    g_sub = diff_subln_g[0]

    hp = jnp.concatenate([meta_tokens.astype(F32), x_prompt[0], jnp.zeros((tp - t, d), F32)], axis=0)
    tm_p = tq_p
    xn = _rmsnorm(hp, norm_mix_g[0], tm_p)
    q_sb, q_d = _proj(xn, wq, "q", tm_p, scale=scale)
    kf_sb, kb_sb, kf_d, kb_d = _proj(xn, wk, "k", tm_p, r_out=t)
    vf_sb, vt_sb, vf_d, vt_d = _proj(xn, wv, "v", tm_p, r_out=t)
    (gates,) = _proj(xn, wg, "g", tm_p)
    o_sb = _sb_attention(q_sb[None], kb_sb[None], vt_sb[None], tri, batch=1, nq=nq_p, tq=tq_p, kt=kt,
                         q_pos0=0)
    o_d = _diff_attention(q_d[None], kb_d[None], vt_d[None], slopes, lam_p, g_sub, batch=1, nq=nq_p,
                          tq=tq_p, kt=kt, q_pos0=0, coff=N_META, n_keys=t)
    h2, hn, te, tg = _merge(hp, o_sb[0], o_d[0], gates, w_sb, w_d, w_o, norm_ffn_g[0], w_r, b_r, tq_p)

    tq_s = Q_TILE_DECODE
    hs = x_sample.reshape(n_s, d)
    xn_s = _rmsnorm(hs, norm_mix_g[0], n_s)
    q2_s = _proj(xn_s, wq, "q", n_s, scale=scale)
    kf_s0, _, kf_s1, _ = _proj(xn_s, wk, "k", n_s)
    vf_s0, _, vf_s1, _ = _proj(xn_s, wv, "v", n_s, kt=n_s)
    kf_s = (kf_s0, kf_s1)
    vf_s = (vf_s0, vf_s1)
    (gates_s,) = _proj(xn_s, wg, "g", n_s)
    rk_s = -(-(past + s_len) // kt) * kt

    def with_cache(cache, new):
        c = cache[0].reshape(nb_s, past, SEC).astype(BF16)
        nw = new.reshape(nb_s, s_len, SEC).astype(BF16)
        z = jnp.zeros((nb_s, rk_s - past - s_len, SEC), BF16)
        return jnp.concatenate([c, nw, z], axis=1)

    def pad_q(qs):
        return jnp.pad(qs.reshape(nb_s, s_len, SEC), ((0, 0), (0, tq_s - s_len), (0, 0)))

    k_sb_all = with_cache(cache_sb_k, kf_s[0])
    vt_sb_all = _tiles_t(with_cache(cache_sb_v, vf_s[0]), kt)
    k_d_all = with_cache(cache_diff_k, kf_s[1])
    vt_d_all = _tiles_t(with_cache(cache_diff_v, vf_s[1]), kt)
    o_sb_s = _sb_attention(pad_q(q2_s[0]), k_sb_all, vt_sb_all, tri, batch=nb_s, nq=1, tq=tq_s, kt=kt,
                           q_pos0=past)
    o_d_s = _diff_attention(pad_q(q2_s[1]), k_d_all, vt_d_all, slopes, lam_p, g_sub, batch=nb_s, nq=1,
                            tq=tq_s, kt=kt, q_pos0=past, coff=0, n_keys=past + s_len)
    o_sb_s = o_sb_s[:, :s_len].reshape(n_s, SEC)
    o_d_s = o_d_s[:, :s_len].reshape(n_s, SEC)
    h2_s, hn_s, te_s, tg_s = _merge(hs, o_sb_s, o_d_s, gates_s, w_sb, w_d, w_o, norm_ffn_g[0], w_r, b_r,
                                    n_s)

    n_tok = t + n_s
    top_e = jnp.concatenate([te[:t, :TOP_K], te_s[:, :TOP_K]], axis=0)
    top_g = jnp.concatenate([tg[:t, :TOP_K], tg_s[:, :TOP_K]], axis=0)
    row_tok, row_gate, pos, block_e, n_used = _route(top_e, top_g, MOE_ROWS)
    hn_all = jnp.concatenate([hn[:t], hn_s, jnp.zeros((1, d), BF16)], axis=0)
    xs = hn_all[row_tok]
    yb = _moe_gmm(block_e, n_used, xs, row_gate.reshape(-1, 1), w_gu, b_gu, w_dn, b_dn, MOE_ROWS)
    pos_p = jnp.pad(pos[:t], ((0, tp - t), (0, 0)))
    contrib_p = yb[pos_p].reshape(tp, TOP_K * d)
    contrib_s = yb[pos[t:]].reshape(n_s, TOP_K * d)
    y_p = _final(h2, contrib_p, final_norm_g, tq_p)
    y_s = _final(h2_s, contrib_s, final_norm_g, n_s)

    y_prompt = y_p[N_META:t][None]
    y_sample = y_s.reshape(nb_s, s_len, d)

    def heads(a, nh):
        return a.reshape(1, *a.shape[:-1], nh, a.shape[-1] // nh)

    return (y_prompt, y_sample,
            heads(kf_sb[None], H_SB), heads(vf_sb[None], H_SB),
            heads(kf_d[None], H_DIFF), heads(vf_d[None], H_DIFF),
            heads(kf_s[0].reshape(nb_s, s_len, SEC), H_SB), heads(vf_s[0].reshape(nb_s, s_len, SEC), H_SB),
            heads(kf_s[1].reshape(nb_s, s_len, SEC), H_DIFF), heads(vf_s[1].reshape(nb_s, s_len, SEC), H_DIFF))
```

```python
import functools
import math

import jax
import jax.numpy as jnp
from jax import lax
from jax.experimental import pallas as pl
from jax.experimental.pallas import tpu as pltpu
from jax.experimental.pallas import tpu_sc as plsc

F32 = jnp.float32
BF16 = jnp.bfloat16

D_MODEL = 1024
CHUNK = 64
CHUNK_SHIFT = 6
N_META = 16
H_SB = 16
DH_SB = 64
H_DIFF = 8
DH_DIFF = 64
DV_DIFF = 128
N_EXPERTS = 32
TOP_K = 4
D_FF = 1024
SWIGLU_LIMIT = 7.0
SWIGLU_ALPHA = 1.702
NORM_EPS = 1e-5
LAMBDA_INIT = 0.8 - 0.6 * math.exp(-0.3 * 0)

LANES = 128
SEC = 1024
KEY_TILE = 256
Q_TILE_PROMPT = 256
Q_TILE_DECODE = 128
MOE_ROWS = 256
VMEM_LIMIT = 56 * 1024 * 1024
NEG_INF = float("-inf")
SB_SKIP_LOG = 104.0
SB_ABSENT = 1e30
DIFF_GROUP = 2
DIFF_TAIL_GROUPS = 2
DIFF_SKIP_LOG = 105.0
DIFF_NORM_SLACK = 1.01
SC_GATHER_WINDOW = 128
SC_ROW_SPLIT = 4


def _cparams(sem):
    return pltpu.CompilerParams(dimension_semantics=sem, vmem_limit_bytes=VMEM_LIMIT)


def _rmsnorm_kernel(x_ref, g_ref, o_ref):
    x = x_ref[...]
    y = x * lax.rsqrt(jnp.mean(x * x, axis=-1, keepdims=True) + NORM_EPS)
    o_ref[...] = (y * g_ref[...]).astype(o_ref.dtype)


def _rmsnorm(x, g, tm):
    r, d = x.shape
    return pl.pallas_call(
        _rmsnorm_kernel,
        grid=(r // tm,),
        in_specs=[pl.BlockSpec((tm, d), lambda i: (i, 0)),
                  pl.BlockSpec((1, d), lambda i: (0, 0))],
        out_specs=pl.BlockSpec((tm, d), lambda i: (i, 0)),
        out_shape=jax.ShapeDtypeStruct((r, d), BF16),
        compiler_params=_cparams(("parallel",)),
        name="rmsnorm",
    )(x, g.reshape(1, d))


def _proj_kernel(x_ref, w_ref, *out_refs, mode, scale, kt):
    x = x_ref[...]
    for s in range(2):
        acc = jnp.dot(x, w_ref[:, s * SEC:(s + 1) * SEC], preferred_element_type=F32)
        if mode == "q":
            out_refs[s][...] = (acc * scale).astype(BF16)
        elif mode == "k":
            out_refs[2 * s][...] = acc
            out_refs[2 * s + 1][...] = acc.astype(BF16)
        elif mode == "v":
            out_refs[2 * s][...] = acc
            for c in range(acc.shape[0] // kt):
                for hp in range(SEC // LANES):
                    tile = acc[c * kt:(c + 1) * kt, hp * LANES:(hp + 1) * LANES]
                    out_refs[2 * s + 1][hp, c] = tile.T.astype(BF16)
        else:
            out_refs[0][:, s * SEC:(s + 1) * SEC] = jax.nn.sigmoid(acc)


def _proj(xn, w2, mode, tm, r_out=None, kt=KEY_TILE, scale=1.0):
    r, d = xn.shape
    r_out = r if r_out is None else r_out
    in_specs = [pl.BlockSpec((tm, d), lambda i: (i, 0)),
                pl.BlockSpec((d, 2 * SEC), lambda i: (0, 0))]
    row_spec = pl.BlockSpec((tm, SEC), lambda i: (i, 0))
    f32_out = jax.ShapeDtypeStruct((r_out, SEC), F32)
    bf16_out = jax.ShapeDtypeStruct((r, SEC), BF16)
    if mode == "q":
        out_specs = [row_spec, row_spec]
        out_shape = [bf16_out, bf16_out]
    elif mode == "k":
        out_specs = [row_spec] * 4
        out_shape = [f32_out, bf16_out, f32_out, bf16_out]
    elif mode == "v":
        t_spec = pl.BlockSpec((SEC // LANES, tm // kt, LANES, kt), lambda i: (0, i, 0, 0))
        t_out = jax.ShapeDtypeStruct((SEC // LANES, r // kt, LANES, kt), BF16)
        out_specs = [row_spec, t_spec, row_spec, t_spec]
        out_shape = [f32_out, t_out, f32_out, t_out]
    else:
        out_specs = [pl.BlockSpec((tm, 2 * SEC), lambda i: (i, 0))]
        out_shape = [jax.ShapeDtypeStruct((r, 2 * SEC), F32)]
    return pl.pallas_call(
        functools.partial(_proj_kernel, mode=mode, scale=scale, kt=kt),
        grid=(r // tm,), in_specs=in_specs, out_specs=out_specs, out_shape=out_shape,
        compiler_params=_cparams(("parallel",)),
        name="proj_" + mode,
    )(xn, w2)


def _sb_kernel(q_ref, k_ref, vt_ref, tri_ref, o_ref, acc_ref, *, tq, kt, q_pos0):
    qi = pl.program_id(2)
    q0 = q_pos0 + qi * tq
    jm = q0 // kt
    q = q_ref[0]
    lane = lax.broadcasted_iota(jnp.int32, (tq, LANES), 1)
    kpos_l = lax.broadcasted_iota(jnp.int32, (kt, tq), 0)
    qpos_l = lax.broadcasted_iota(jnp.int32, (kt, tq), 1)
    tri = tri_ref[...]
    zero = jnp.zeros_like(q)
    qz = (jnp.where(lane < DH_SB, q, zero), jnp.where(lane >= DH_SB, q, zero))
    acc_ref[...] = jnp.zeros_like(acc_ref)

    def tile(j, hh, carry, masked):
        off = pl.multiple_of(j * kt, kt)
        kk = k_ref[0, pl.ds(off, kt), :]
        z = lax.dot_general(kk, qz[hh], (((1,), (1,)), ((), ())),
                            preferred_element_type=F32)
        sp = jnp.maximum(z, 0.0) + jnp.log(1.0 + jnp.exp(-jnp.abs(z)))
        if masked:
            vis = (kpos_l + j * kt) < (qpos_l + q0)
            lk = jnp.where(vis, -sp, 0.0)
        else:
            lk = -sp
        later = jnp.dot(tri, lk.astype(BF16), preferred_element_type=F32)
        w = jnp.exp(z + lk + later + carry)
        if masked:
            w = jnp.where(vis, w, 0.0)
        acc_ref[hh] += jnp.dot(vt_ref[0, 0, j], w.astype(BF16), preferred_element_type=F32)
        return jnp.sum(lk, axis=0, keepdims=True)

    none_later = jnp.zeros((1, tq), F32)
    carries = tuple(tile(jm, hh, none_later, True) for hh in range(2))

    def cond(state):
        t, ca, cb = state
        return jnp.logical_and(jm - 1 - 2 * t >= 0,
                               jnp.max(jnp.maximum(ca, cb)) > -SB_SKIP_LOG)

    def body(state):
        t, ca, cb = state
        j0 = jm - 1 - 2 * t
        j1 = j0 - 1
        j1c = jnp.maximum(j1, 0)
        out = []
        for hh, carry in ((0, ca), (1, cb)):
            s0 = tile(j0, hh, carry, False)
            carry1 = jnp.where(j1 >= 0, carry + s0, -SB_ABSENT)
            s1 = tile(j1c, hh, carry1, False)
            out.append(carry + s0 + s1)
        return t + 1, out[0], out[1]

    lax.while_loop(cond, body, (jnp.int32(0), carries[0], carries[1]))

    row = lax.broadcasted_iota(jnp.int32, (LANES, tq), 0)
    o_t = jnp.where(row < DH_SB, acc_ref[0], acc_ref[1])
    o_ref[0] = o_t.T.astype(o_ref.dtype)


def _sb_attention(q, k, vt, tri, *, batch, nq, tq, kt, q_pos0, q_sec=0, k_sec=0, v_sec=0):
    rk = k.shape[1]
    nk = rk // kt
    nhp = SEC // LANES
    return pl.pallas_call(
        functools.partial(_sb_kernel, tq=tq, kt=kt, q_pos0=q_pos0),
        grid=(batch, nhp, nq),
        in_specs=[
            pl.BlockSpec((1, tq, LANES), lambda b, h, i: (b + q_sec, i, h)),
            pl.BlockSpec((1, rk, LANES), lambda b, h, i: (b + k_sec, 0, h)),
            pl.BlockSpec((1, 1, nk, LANES, kt), lambda b, h, i: (b + v_sec, h, 0, 0, 0)),
            pl.BlockSpec((kt, kt), lambda b, h, i: (0, 0)),
        ],
        out_specs=pl.BlockSpec((1, tq, LANES), lambda b, h, i: (b, i, h)),
        out_shape=jax.ShapeDtypeStruct((batch, nq * tq, SEC), BF16),
        scratch_shapes=[pltpu.VMEM((2, LANES, tq), F32)],
        compiler_params=_cparams(("parallel", "parallel", "arbitrary")),
        name="sb_attention",
    )(q, k, vt, tri)


def _diff_kernel(slopes_ref, lam_ref, g_ref, q_ref, k_ref, vt_ref, o_ref,
                 acc_ref, bt_ref, s0_ref, s1_ref, p0_ref, p1_ref, kn_ref,
                 *, tq, kt, q_pos0, coff, n_keys, nk):
    h = pl.program_id(1)
    qi = pl.program_id(2)
    slope = slopes_ref[0, h]
    inv_slope = slopes_ref[1, h]
    q0 = q_pos0 + qi * tq
    c_lo = (q0 - coff + CHUNK) // CHUNK - 1
    c_hi = (q0 + tq - 1 - coff + CHUNK) // CHUNK - 1
    full_end = jnp.minimum(coff + CHUNK * (c_lo + 1), n_keys)
    vis_end = jnp.minimum(coff + CHUNK * (c_hi + 1), n_keys)
    n_full = full_end // kt
    n_vis = jnp.minimum((vis_end + kt - 1) // kt, nk)

    q = q_ref[0]
    lane = lax.broadcasted_iota(jnp.int32, (tq, LANES), 1)
    kpos_l = lax.broadcasted_iota(jnp.int32, (kt, tq), 0)
    qpos_l = lax.broadcasted_iota(jnp.int32, (kt, tq), 1)
    gsz = DIFF_GROUP
    m_tiles = (n_full // gsz) * gsz
    n_fullg = m_tiles // gsz
    n_groups = n_fullg + DIFF_TAIL_GROUPS
    n_tail = DIFF_TAIL_GROUPS * gsz
    absent = gsz + n_tail

    for u in range(gsz):
        bt_ref[u] = slope * (kpos_l + u * kt).astype(F32)
    for u in range(n_tail):
        kpos = kpos_l + (m_tiles + u) * kt
        qpos = qpos_l + q0
        vis = (((kpos - coff + CHUNK) >> CHUNK_SHIFT) <= ((qpos - coff + CHUNK) >> CHUNK_SHIFT)) \
            & (kpos < n_keys)
        bias = slope * (qpos_l - jnp.abs(qpos - kpos)).astype(F32)
        bt_ref[gsz + u] = jnp.where(vis, bias, NEG_INF)
    bt_ref[absent] = jnp.full((kt, tq), NEG_INF, F32)

    zero = jnp.zeros_like(q)
    qz = (jnp.where(lane < DH_DIFF, q, zero), jnp.where(lane >= DH_DIFF, q, zero))
    acc_ref[...] = jnp.zeros_like(acc_ref)
    p1_ref[...] = jnp.zeros_like(p1_ref)

    hr = lax.broadcasted_iota(jnp.int32, (LANES, LANES), 0)
    hc = lax.broadcasted_iota(jnp.int32, (LANES, LANES), 1)
    half = (((hc == 0) & (hr < DH_DIFF)) | ((hc == 1) & (hr >= DH_DIFF))).astype(BF16)
    lane1 = lax.broadcasted_iota(jnp.int32, (1, LANES), 1)

    def max_half_norms(sq_max):
        nrm = jnp.sqrt(sq_max)
        return (jnp.max(jnp.where(lane1 == 0, nrm, 0.0)), jnp.max(jnp.where(lane1 == 1, nrm, 0.0)))

    @pl.when(qi == 0)
    def _():
        def body(j, mx):
            off = pl.multiple_of(j * kt, kt)
            kk = k_ref[0, pl.ds(off, kt), :]
            n2 = jnp.dot(kk * kk, half, preferred_element_type=F32)
            return jnp.maximum(mx, jnp.max(n2, axis=0, keepdims=True))
        kn = max_half_norms(lax.fori_loop(0, nk, body, jnp.zeros((1, LANES), F32)))
        kn_ref[0] = kn[0]
        kn_ref[1] = kn[1]

    qn = max_half_norms(jnp.max(jnp.dot(q * q, half, preferred_element_type=F32), axis=0, keepdims=True))
    qk_bound = [DIFF_NORM_SLACK * qn[mm] * kn_ref[mm] for mm in range(2)]
    q0_f = q0.astype(F32)

    def skip_below(state):
        need = jnp.maximum((DIFF_SKIP_LOG + qk_bound[0]) - state[0], (DIFF_SKIP_LOG + qk_bound[1]) - state[2])
        return q0_f - jnp.max(need) * inv_slope

    def seq_group(n):
        return n_groups - 1 - n

    def needed(g, thr):
        top = ((g + 1) * (gsz * kt)).astype(F32)
        return jnp.logical_and(g >= 0, jnp.logical_or(g >= n_fullg, top > thr))

    def tiles_of(g):
        return [jnp.clip(g * gsz + u, 0, nk - 1) for u in range(gsz)]

    def scores_stage(g, s_ref):
        js = tiles_of(g)
        for mm in range(2):
            for u in range(gsz):
                off = pl.multiple_of(js[u] * kt, kt)
                kk = k_ref[0, pl.ds(off, kt), :]
                s_ref[mm, u] = lax.dot_general(kk, qz[mm], (((1,), (1,)), ((), ())),
                                               preferred_element_type=F32)

    def softmax_stage(g, thr, s_ref, p_ref, state):
        ok = needed(g, thr)
        full = jnp.logical_and(ok, g < n_fullg)
        sj = jnp.where(full, slope * (g * gsz * kt - q0).astype(F32), 0.0)
        bidx = [jnp.where(full, u, jnp.where(ok, gsz + g * gsz + u - m_tiles, absent)) for u in range(gsz)]
        new_state, alphas = [], []
        for mm in range(2):
            m, l = state[2 * mm], state[2 * mm + 1]
            ss = [s_ref[mm, u] + bt_ref[bidx[u]] for u in range(gsz)]
            cmax = jnp.max(ss[0], axis=0, keepdims=True)
            for s in ss[1:]:
                cmax = jnp.maximum(cmax, jnp.max(s, axis=0, keepdims=True))
            m_new = jnp.maximum(m, cmax + sj)
            m_safe = jnp.where(m_new == NEG_INF, 0.0, m_new)
            alpha = jnp.exp(m - m_safe)
            r = m_safe - sj
            psum = jnp.zeros((1, tq), F32)
            for u in range(gsz):
                p = jnp.exp(ss[u] - r)
                psum = psum + jnp.sum(p, axis=0, keepdims=True)
                p_ref[mm, u] = p.astype(BF16)
            new_state += [m_new, alpha * l + psum]
            alphas.append(alpha)
        return tuple(new_state), tuple(alphas)

    def value_stage(g, p_ref, alphas):
        js = tiles_of(g)
        for mm in range(2):
            pv = jnp.dot(vt_ref[0, 0, js[0]], p_ref[mm, 0], preferred_element_type=F32)
            for u in range(1, gsz):
                pv = pv + jnp.dot(vt_ref[0, 0, js[u]], p_ref[mm, u], preferred_element_type=F32)
            acc_ref[mm] = alphas[mm] * acc_ref[mm] + pv

    def trip(carry):
        t, thr, state, alpha_prev = carry
        n0 = 2 * t
        scores_stage(seq_group(n0 + 1), s1_ref)
        state, alpha0 = softmax_stage(seq_group(n0), thr, s0_ref, p0_ref, state)
        value_stage(seq_group(n0 - 1), p1_ref, alpha_prev)
        scores_stage(seq_group(n0 + 2), s0_ref)
        state, alpha1 = softmax_stage(seq_group(n0 + 1), thr, s1_ref, p1_ref, state)
        value_stage(seq_group(n0), p0_ref, alpha0)
        return t + 1, skip_below(state), state, alpha1

    def more(carry):
        t, thr = carry[0], carry[1]
        return needed(seq_group(2 * t), thr)

    neg = jnp.full((1, tq), NEG_INF, F32)
    zer = jnp.zeros((1, tq), F32)
    one = jnp.ones((1, tq), F32)
    scores_stage(seq_group(0), s0_ref)
    n_done, _, state, alpha_last = lax.while_loop(
        more, trip, (jnp.int32(0), jnp.float32(NEG_INF), (neg, zer, neg, zer), (one, one)))
    value_stage(seq_group(2 * n_done - 1), p1_ref, alpha_last)
    stats = [(state[0], state[1]), (state[2], state[3])]

    lp = lam_ref[...]
    lam = (jnp.exp(jnp.sum(lp[0:1] * lp[1:2], axis=-1, keepdims=True))
           - jnp.exp(jnp.sum(lp[2:3] * lp[3:4], axis=-1, keepdims=True)) + LAMBDA_INIT)
    l1 = stats[0][1]
    l2 = stats[1][1]
    l1 = jnp.where(l1 == 0.0, 1.0, l1)
    l2 = jnp.where(l2 == 0.0, 1.0, l2)
    o_t = acc_ref[0] / l1 - lam * (acc_ref[1] / l2)
    o = o_t.T
    y = o * lax.rsqrt(jnp.mean(o * o, axis=-1, keepdims=True) + NORM_EPS)
    o_ref[0] = ((y * g_ref[...]) * (1.0 - LAMBDA_INIT)).astype(o_ref.dtype)


def _diff_attention(q, k, vt, slopes, lam_p, subln_g, *, batch, nq, tq, kt, q_pos0, coff, n_keys,
                    q_sec=0, k_sec=0, v_sec=0):
    rk = k.shape[1]
    nk = rk // kt
    nh = SEC // LANES
    return pl.pallas_call(
        functools.partial(_diff_kernel, tq=tq, kt=kt, q_pos0=q_pos0, coff=coff, n_keys=n_keys, nk=nk),
        grid=(batch, nh, nq),
        in_specs=[
            pl.BlockSpec(memory_space=pltpu.SMEM),
            pl.BlockSpec((4, DH_DIFF), lambda b, h, i: (0, 0)),
            pl.BlockSpec((1, DV_DIFF), lambda b, h, i: (0, 0)),
            pl.BlockSpec((1, tq, LANES), lambda b, h, i: (b + q_sec, i, h)),
            pl.BlockSpec((1, rk, LANES), lambda b, h, i: (b + k_sec, 0, h)),
            pl.BlockSpec((1, 1, nk, LANES, kt), lambda b, h, i: (b + v_sec, h, 0, 0, 0)),
        ],
        out_specs=pl.BlockSpec((1, tq, LANES), lambda b, h, i: (b, i, h)),
        out_shape=jax.ShapeDtypeStruct((batch, nq * tq, SEC), BF16),
        scratch_shapes=[pltpu.VMEM((2, LANES, tq), F32),
                        pltpu.VMEM((DIFF_GROUP * (1 + DIFF_TAIL_GROUPS) + 1, kt, tq), F32),
                        pltpu.VMEM((2, DIFF_GROUP, kt, tq), F32),
                        pltpu.VMEM((2, DIFF_GROUP, kt, tq), F32),
                        pltpu.VMEM((2, DIFF_GROUP, kt, tq), BF16),
                        pltpu.VMEM((2, DIFF_GROUP, kt, tq), BF16),
                        pltpu.SMEM((2,), F32)],
        compiler_params=_cparams(("parallel", "parallel", "arbitrary")),
        name="diff_attention",
    )(slopes, lam_p, subln_g.reshape(1, DV_DIFF), q, k, vt)


def _merge_kernel(h_ref, osb_ref, od_ref, gt_ref, wsb_ref, wd_ref, wo_ref, gn_ref, wr_ref, br_ref,
                  h2_ref, hn_ref, te_ref, tg_ref):
    y_sb = jnp.dot(osb_ref[...], wsb_ref[...], preferred_element_type=F32)
    y_d = jnp.dot(od_ref[...], wd_ref[...], preferred_element_type=F32)
    gt = gt_ref[...]
    mix = gt[:, :D_MODEL] * y_sb + gt[:, D_MODEL:] * y_d
    h2 = h_ref[...] + jnp.dot(mix.astype(BF16), wo_ref[...], preferred_element_type=F32)
    h2_ref[...] = h2
    hn = (h2 * lax.rsqrt(jnp.mean(h2 * h2, axis=-1, keepdims=True) + NORM_EPS)) * gn_ref[...]
    hn_ref[...] = hn
    logits = jnp.dot(hn, wr_ref[...], preferred_element_type=F32,
                     precision=lax.Precision.HIGHEST) + br_ref[...]
    tm = logits.shape[0]
    lane = lax.broadcasted_iota(jnp.int32, (tm, LANES), 1).astype(F32)
    tops, idxs = [], []
    l = logits
    for _ in range(TOP_K):
        m = jnp.max(l, axis=-1, keepdims=True)
        idx = jnp.min(jnp.where(l == m, lane, float(LANES)), axis=-1, keepdims=True)
        tops.append(m)
        idxs.append(idx)
        l = jnp.where(lane == idx, NEG_INF, l)
    ex = [jnp.exp(t - tops[0]) for t in tops]
    den = ex[0] + ex[1] + ex[2] + ex[3]
    te = jnp.zeros((tm, LANES), F32)
    tg = jnp.zeros((tm, LANES), F32)
    for k in range(TOP_K):
        te = jnp.where(lane == float(k), idxs[k], te)
        tg = jnp.where(lane == float(k), ex[k] / den, tg)
    te_ref[...] = te.astype(jnp.int32)
    tg_ref[...] = tg


def _merge(h, o_sb, o_d, gates, w_sb, w_d, w_o, g_ffn, w_r, b_r, tm):
    r, d = h.shape
    row = lambda w: pl.BlockSpec((tm, w), lambda i: (i, 0))
    full = lambda a, b: pl.BlockSpec((a, b), lambda i: (0, 0))
    return pl.pallas_call(
        _merge_kernel,
        grid=(r // tm,),
        in_specs=[row(d), row(d), row(d), row(2 * d), full(d, d), full(d, d), full(d, d),
                  full(1, d), full(d, LANES), full(1, LANES)],
        out_specs=[row(d), row(d), row(LANES), row(LANES)],
        out_shape=[jax.ShapeDtypeStruct((r, d), F32), jax.ShapeDtypeStruct((r, d), F32),
                   jax.ShapeDtypeStruct((r, LANES), jnp.int32), jax.ShapeDtypeStruct((r, LANES), F32)],
        compiler_params=_cparams(("parallel",)),
        name="merge_router",
    )(h, o_sb, o_d, gates, w_sb, w_d, w_o, g_ffn.reshape(1, d), w_r, b_r)


def _moe_kernel(be_ref, nb_ref, x_ref, gate_ref, wgu_ref, bgu_ref, wdn_ref, bdn_ref, o_ref,
                wgu_bf, wdn_bf):
    b = pl.program_id(0)
    used = b < nb_ref[0]
    new_expert = jnp.logical_or(b == 0, be_ref[b] != be_ref[jnp.maximum(b - 1, 0)])

    @pl.when(jnp.logical_and(used, new_expert))
    def _():
        wgu_bf[...] = wgu_ref[0].astype(BF16)
        wdn_bf[...] = wdn_ref[0].astype(BF16)

    @pl.when(used)
    def _():
        gu = jnp.dot(x_ref[...].astype(BF16), wgu_bf[...], preferred_element_type=F32) + bgu_ref[0]
        g = jnp.minimum(gu[:, :D_FF], SWIGLU_LIMIT)
        u = jnp.clip(gu[:, D_FF:], -SWIGLU_LIMIT, SWIGLU_LIMIT)
        act = (u + 1.0) * (g * jax.nn.sigmoid(SWIGLU_ALPHA * g))
        y = jnp.dot(act.astype(BF16), wdn_bf[...], preferred_element_type=F32) + bdn_ref[0]
        o_ref[...] = y * gate_ref[...]

    @pl.when(b >= nb_ref[0])
    def _():
        o_ref[...] = jnp.zeros_like(o_ref)


def _moe_gmm(block_e, n_used, xs, row_gate, w_gu, b_gu, w_dn, b_dn, bm):
    n_rows = row_gate.shape[0]
    d = xs.shape[1]
    nb = n_rows // bm
    grid_spec = pltpu.PrefetchScalarGridSpec(
        num_scalar_prefetch=2,
        grid=(nb,),
        in_specs=[
            pl.BlockSpec((bm, d), lambda b, be, nu: (b, 0)),
            pl.BlockSpec((bm, 1), lambda b, be, nu: (b, 0)),
            pl.BlockSpec((1, d, 2 * D_FF), lambda b, be, nu: (be[b], 0, 0)),
            pl.BlockSpec((1, 1, 2 * D_FF), lambda b, be, nu: (be[b], 0, 0)),
            pl.BlockSpec((1, D_FF, d), lambda b, be, nu: (be[b], 0, 0)),
            pl.BlockSpec((1, 1, d), lambda b, be, nu: (be[b], 0, 0)),
        ],
        out_specs=pl.BlockSpec((bm, d), lambda b, be, nu: (b, 0)),
        scratch_shapes=[pltpu.VMEM((d, 2 * D_FF), BF16), pltpu.VMEM((D_FF, d), BF16)],
    )
    return pl.pallas_call(
        _moe_kernel,
        grid_spec=grid_spec,
        out_shape=jax.ShapeDtypeStruct((n_rows, d), F32),
        compiler_params=_cparams(("arbitrary",)),
        name="moe_experts",
    )(block_e, n_used, xs, row_gate, w_gu, b_gu, w_dn, b_dn)


def _final_kernel(h_ref, c_ref, g_ref, o_ref):
    c = c_ref[...]
    d = h_ref.shape[1]
    y = c[:, 0:d]
    for k in range(1, TOP_K):
        y = y + c[:, k * d:(k + 1) * d]
    x = h_ref[...] + y
    o_ref[...] = (x * lax.rsqrt(jnp.mean(x * x, axis=-1, keepdims=True) + NORM_EPS)) * g_ref[...]


def _final(h2, contrib, g, tm, c_row0=0):
    r, d = h2.shape
    c_blk0 = c_row0 // tm
    return pl.pallas_call(
        _final_kernel,
        grid=(r // tm,),
        in_specs=[pl.BlockSpec((tm, d), lambda i: (i, 0)),
                  pl.BlockSpec((tm, TOP_K * d), lambda i: (i + c_blk0, 0)),
                  pl.BlockSpec((1, d), lambda i: (0, 0))],
        out_specs=pl.BlockSpec((tm, d), lambda i: (i, 0)),
        out_shape=jax.ShapeDtypeStruct((r, d), F32),
        compiler_params=_cparams(("parallel",)),
        name="combine_final_norm",
    )(h2, contrib, g.reshape(1, d))


def _route(top_e, top_g, bm):
    n = top_e.shape[0]
    a = n * TOP_K
    flat_e = top_e.reshape(-1)
    order = jnp.argsort(flat_e).astype(jnp.int32)
    onehot = (flat_e[:, None] == jnp.arange(N_EXPERTS, dtype=jnp.int32)[None, :]).astype(jnp.int32)
    csum = jnp.cumsum(onehot, axis=0)
    counts = csum[-1]
    rank = jnp.sum(onehot * csum, axis=1) - 1
    padded = ((counts + bm - 1) // bm) * bm
    pend = jnp.cumsum(padded)
    pstart = pend - padded
    start = jnp.cumsum(counts) - counts
    pos = (jnp.sum(onehot * pstart[None, :], axis=1) + rank).reshape(n, TOP_K)
    n_rows = -(-(a + N_EXPERTS * (bm - 1)) // bm) * bm
    nb = n_rows // bm
    block_start = jnp.arange(nb, dtype=jnp.int32) * bm
    block_e = jnp.minimum(jnp.sum((pend[None, :] <= block_start[:, None]).astype(jnp.int32), axis=1),
                          N_EXPERTS - 1)
    row = jnp.arange(n_rows, dtype=jnp.int32)
    row_e = jnp.repeat(block_e, bm, total_repeat_length=n_rows)
    row_rank = row - pstart[row_e]
    valid = jnp.logical_and(row_rank >= 0, row_rank < counts[row_e])
    src = order[jnp.clip(start[row_e] + row_rank, 0, a - 1)]
    row_tok = jnp.where(valid, src // TOP_K, n)
    row_gate = jnp.where(valid, top_g.reshape(-1)[src], 0.0)
    n_used = (pend[-1] // bm).astype(jnp.int32).reshape(1)
    return row_tok, row_gate, pos, block_e, n_used


def _sc_gather(table, idx):
    n_rows, d_full = table.shape
    table = table.reshape(n_rows * SC_ROW_SPLIT, d_full // SC_ROW_SPLIT)
    idx = (idx[:, None] * SC_ROW_SPLIT + jnp.arange(SC_ROW_SPLIT, dtype=idx.dtype)[None, :]).reshape(-1)
    n_idx = idx.shape[0]
    d = table.shape[1]
    mesh = plsc.VectorSubcoreMesh(core_axis_name="c", subcore_axis_name="s")

    @pl.kernel(out_type=jax.ShapeDtypeStruct((n_idx, d), table.dtype), mesh=mesh)
    def gather_kernel(x_hbm, i_hbm, o_hbm):
        def body(i_vmem, o_vmem):
            pltpu.sync_copy(x_hbm.at[i_vmem.at[0]], o_vmem)

        pltpu.emit_pipeline(
            body,
            grid=(n_idx // SC_GATHER_WINDOW,),
            in_specs=[pl.BlockSpec((1, SC_GATHER_WINDOW), index_map=lambda i: (0, i))],
            out_specs=[pl.BlockSpec((SC_GATHER_WINDOW, d), index_map=lambda i: (i, 0))],
            core_axis_name=("c", "s"),
            dimension_semantics=(pltpu.PARALLEL,),
        )(i_hbm, o_hbm)

    return gather_kernel(table, idx.reshape(1, n_idx)).reshape(n_idx // SC_ROW_SPLIT, d_full)


def _pad_to(a, mult, value):
    pad = (-a.shape[0]) % mult
    return a if pad == 0 else jnp.concatenate([a, jnp.full((pad,), value, a.dtype)])


def _tri(kt):
    s = lax.broadcasted_iota(jnp.int32, (kt, kt), 0)
    j = lax.broadcasted_iota(jnp.int32, (kt, kt), 1)
    return (j > s).astype(BF16)


def _tiles_t(v, kt):
    b, r, _ = v.shape
    return v.reshape(b, r // kt, kt, SEC // LANES, LANES).transpose(0, 3, 1, 4, 2)


def kernel(x_prompt, x_sample, cache_sb_k, cache_sb_v, cache_diff_k, cache_diff_v, meta_tokens,
           norm_mix_g, w_in, diff_lambda, diff_subln_g, w_br_sb, w_br_diff, w_out, norm_ffn_g,
           w_router, b_router, w_gate_up, b_gate_up, w_down, b_down, final_norm_g):
    assert x_prompt.shape[0] == 1 and w_in.shape[0] == 1
    d = D_MODEL
    seq = x_prompt.shape[1]
    t = N_META + seq
    tq_p, kt = Q_TILE_PROMPT, KEY_TILE
    tp = -(-t // tq_p) * tq_p
    nq_p = tp // tq_p
    nb_s, s_len = x_sample.shape[:2]
    past = cache_sb_k.shape[2]
    n_s = nb_s * s_len
    scale = DH_SB ** -0.5

    w0 = w_in[0]
    wq = jnp.concatenate([w0[:, 0:SEC], w0[:, 3 * SEC:4 * SEC]], axis=1).astype(BF16)
    wk = jnp.concatenate([w0[:, SEC:2 * SEC], w0[:, 4 * SEC:5 * SEC]], axis=1).astype(BF16)
    wv = jnp.concatenate([w0[:, 2 * SEC:3 * SEC], w0[:, 5 * SEC:6 * SEC]], axis=1).astype(BF16)
    wg = w0[:, 6 * SEC:8 * SEC].astype(BF16)
    w_sb = w_br_sb[0].astype(BF16)
    w_d = w_br_diff[0].astype(BF16)
    w_o = w_out[0].astype(BF16)
    w_r = jnp.pad(w_router[0], ((0, 0), (0, LANES - N_EXPERTS)))
    b_r = jnp.pad(b_router[0], (0, LANES - N_EXPERTS), constant_values=NEG_INF).reshape(1, LANES)
    w_gu = w_gate_up[0]
    w_dn = w_down[0]
    b_gu = b_gate_up[0].reshape(N_EXPERTS, 1, 2 * D_FF)
    b_dn = b_down[0].reshape(N_EXPERTS, 1, d)
    slope = jnp.exp2(-8.0 * jnp.arange(1, H_DIFF + 1, dtype=F32) / H_DIFF)
    slopes = jnp.stack([slope, 1.0 / slope])
    tri = _tri(kt)
    lam_p = diff_lambda[0]
    g_sub = diff_subln_g[0]

    hp = jnp.concatenate([meta_tokens.astype(F32), x_prompt[0], jnp.zeros((tp - t, d), F32)], axis=0)
    tm_p = tq_p
    xn = _rmsnorm(hp, norm_mix_g[0], tm_p)
    q_sb, q_d = _proj(xn, wq, "q", tm_p, scale=scale)
    kf_sb, kb_sb, kf_d, kb_d = _proj(xn, wk, "k", tm_p, r_out=t)
    vf_sb, vt_sb, vf_d, vt_d = _proj(xn, wv, "v", tm_p, r_out=t)
    (gates,) = _proj(xn, wg, "g", tm_p)
    o_sb = _sb_attention(q_sb[None], kb_sb[None], vt_sb[None], tri, batch=1, nq=nq_p, tq=tq_p, kt=kt,
                         q_pos0=0)
    o_d = _diff_attention(q_d[None], kb_d[None], vt_d[None], slopes, lam_p, g_sub, batch=1, nq=nq_p,
                          tq=tq_p, kt=kt, q_pos0=0, coff=N_META, n_keys=t)
    h2, hn, te, tg = _merge(hp, o_sb[0], o_d[0], gates, w_sb, w_d, w_o, norm_ffn_g[0], w_r, b_r, tq_p)

    tq_s = Q_TILE_DECODE
    hs = x_sample.reshape(n_s, d)
    xn_s = _rmsnorm(hs, norm_mix_g[0], n_s)
    q2_s = _proj(xn_s, wq, "q", n_s, scale=scale)
    kf_s0, _, kf_s1, _ = _proj(xn_s, wk, "k", n_s)
    vf_s0, _, vf_s1, _ = _proj(xn_s, wv, "v", n_s, kt=n_s)
    kf_s = (kf_s0, kf_s1)
    vf_s = (vf_s0, vf_s1)
    (gates_s,) = _proj(xn_s, wg, "g", n_s)
    rk_s = -(-(past + s_len) // kt) * kt

    def with_cache(cache, new):
        c = cache[0].reshape(nb_s, past, SEC).astype(BF16)
        nw = new.reshape(nb_s, s_len, SEC).astype(BF16)
        z = jnp.zeros((nb_s, rk_s - past - s_len, SEC), BF16)
        return jnp.concatenate([c, nw, z], axis=1)

    def pad_q(qs):
        return jnp.pad(qs.reshape(nb_s, s_len, SEC), ((0, 0), (0, tq_s - s_len), (0, 0)))

    k_sb_all = with_cache(cache_sb_k, kf_s[0])
    vt_sb_all = _tiles_t(with_cache(cache_sb_v, vf_s[0]), kt)
    k_d_all = with_cache(cache_diff_k, kf_s[1])
    vt_d_all = _tiles_t(with_cache(cache_diff_v, vf_s[1]), kt)
    o_sb_s = _sb_attention(pad_q(q2_s[0]), k_sb_all, vt_sb_all, tri, batch=nb_s, nq=1, tq=tq_s, kt=kt,
                           q_pos0=past)
    o_d_s = _diff_attention(pad_q(q2_s[1]), k_d_all, vt_d_all, slopes, lam_p, g_sub, batch=nb_s, nq=1,
                            tq=tq_s, kt=kt, q_pos0=past, coff=0, n_keys=past + s_len)
    o_sb_s = o_sb_s[:, :s_len].reshape(n_s, SEC)
    o_d_s = o_d_s[:, :s_len].reshape(n_s, SEC)
    h2_s, hn_s, te_s, tg_s = _merge(hs, o_sb_s, o_d_s, gates_s, w_sb, w_d, w_o, norm_ffn_g[0], w_r, b_r,
                                    n_s)

    n_tok = t + n_s
    top_e = jnp.concatenate([te[:t, :TOP_K], te_s[:, :TOP_K]], axis=0)
    top_g = jnp.concatenate([tg[:t, :TOP_K], tg_s[:, :TOP_K]], axis=0)
    row_tok, row_gate, pos, block_e, n_used = _route(top_e, top_g, MOE_ROWS)
    gather_mult = SC_GATHER_WINDOW * 32 // SC_ROW_SPLIT
    hn_all = jnp.concatenate([hn[:t], hn_s, jnp.zeros((8, d), F32)], axis=0)
    xs = _sc_gather(hn_all, _pad_to(row_tok, gather_mult, n_tok))
    yb = _moe_gmm(block_e, n_used, xs, row_gate.reshape(-1, 1), w_gu, b_gu, w_dn, b_dn, MOE_ROWS)
    pos_all = jnp.concatenate([pos[:t].reshape(-1), jnp.zeros(((tp - t) * TOP_K,), jnp.int32),
                               pos[t:].reshape(-1)])
    contrib = _sc_gather(yb, _pad_to(pos_all, gather_mult, 0))
    contrib = contrib.reshape(-1, TOP_K * d)
    y_p = _final(h2, contrib, final_norm_g, tq_p)
    y_s = _final(h2_s, contrib, final_norm_g, n_s, c_row0=tp)

    y_prompt = y_p[N_META:t][None]
    y_sample = y_s.reshape(nb_s, s_len, d)

    def heads(a, nh):
        return a.reshape(1, *a.shape[:-1], nh, a.shape[-1] // nh)

    return (y_prompt, y_sample,
            heads(kf_sb[None], H_SB), heads(vf_sb[None], H_SB),
            heads(kf_d[None], H_DIFF), heads(vf_d[None], H_DIFF),
            heads(kf_s[0].reshape(nb_s, s_len, SEC), H_SB), heads(vf_s[0].reshape(nb_s, s_len, SEC), H_SB),
            heads(kf_s[1].reshape(nb_s, s_len, SEC), H_DIFF), heads(vf_s[1].reshape(nb_s, s_len, SEC), H_DIFF))
```

```python
import functools
import math

import jax
import jax.numpy as jnp
from jax import lax
from jax.experimental import pallas as pl
from jax.experimental.pallas import tpu as pltpu
from jax.experimental.pallas import tpu_sc as plsc

F32 = jnp.float32
BF16 = jnp.bfloat16

D_MODEL = 1024
CHUNK = 64
CHUNK_SHIFT = 6
N_META = 16
H_SB = 16
DH_SB = 64
H_DIFF = 8
DH_DIFF = 64
DV_DIFF = 128
N_EXPERTS = 32
TOP_K = 4
D_FF = 1024
SWIGLU_LIMIT = 7.0
SWIGLU_ALPHA = 1.702
NORM_EPS = 1e-5
LAMBDA_INIT = 0.8 - 0.6 * math.exp(-0.3 * 0)

LANES = 128
SEC = 1024
KEY_TILE = 256
Q_TILE_PROMPT = 256
Q_TILE_DECODE = 128
MOE_ROWS = 256
VMEM_LIMIT = 56 * 1024 * 1024
NEG_INF = float("-inf")
SB_SKIP_LOG = 104.0
SB_ABSENT = 1e30
DIFF_GROUP = 2
DIFF_TAIL_GROUPS = 2
DIFF_SKIP_LOG = 105.0
DIFF_NORM_SLACK = 1.01
SC_GATHER_WINDOW = 128
SC_SUB_WINDOWS = 4


def _cparams(sem):
    return pltpu.CompilerParams(dimension_semantics=sem, vmem_limit_bytes=VMEM_LIMIT)


def _rmsnorm_kernel(x_ref, g_ref, o_ref):
    x = x_ref[...]
    y = x * lax.rsqrt(jnp.mean(x * x, axis=-1, keepdims=True) + NORM_EPS)
    o_ref[...] = (y * g_ref[...]).astype(o_ref.dtype)


def _rmsnorm(x, g, tm):
    r, d = x.shape
    return pl.pallas_call(
        _rmsnorm_kernel,
        grid=(r // tm,),
        in_specs=[pl.BlockSpec((tm, d), lambda i: (i, 0)),
                  pl.BlockSpec((1, d), lambda i: (0, 0))],
        out_specs=pl.BlockSpec((tm, d), lambda i: (i, 0)),
        out_shape=jax.ShapeDtypeStruct((r, d), BF16),
        compiler_params=_cparams(("parallel",)),
        name="rmsnorm",
    )(x, g.reshape(1, d))


def _proj_kernel(x_ref, w_ref, *out_refs, mode, scale, kt):
    x = x_ref[...]
    for s in range(2):
        acc = jnp.dot(x, w_ref[:, s * SEC:(s + 1) * SEC], preferred_element_type=F32)
        if mode == "q":
            out_refs[s][...] = (acc * scale).astype(BF16)
        elif mode == "k":
            out_refs[2 * s][...] = acc
            out_refs[2 * s + 1][...] = acc.astype(BF16)
        elif mode == "v":
            out_refs[2 * s][...] = acc
            for c in range(acc.shape[0] // kt):
                for hp in range(SEC // LANES):
                    tile = acc[c * kt:(c + 1) * kt, hp * LANES:(hp + 1) * LANES]
                    out_refs[2 * s + 1][hp, c] = tile.T.astype(BF16)
        else:
            out_refs[0][:, s * SEC:(s + 1) * SEC] = jax.nn.sigmoid(acc)


def _proj(xn, w2, mode, tm, r_out=None, kt=KEY_TILE, scale=1.0):
    r, d = xn.shape
    r_out = r if r_out is None else r_out
    in_specs = [pl.BlockSpec((tm, d), lambda i: (i, 0)),
                pl.BlockSpec((d, 2 * SEC), lambda i: (0, 0))]
    row_spec = pl.BlockSpec((tm, SEC), lambda i: (i, 0))
    f32_out = jax.ShapeDtypeStruct((r_out, SEC), F32)
    bf16_out = jax.ShapeDtypeStruct((r, SEC), BF16)
    if mode == "q":
        out_specs = [row_spec, row_spec]
        out_shape = [bf16_out, bf16_out]
    elif mode == "k":
        out_specs = [row_spec] * 4
        out_shape = [f32_out, bf16_out, f32_out, bf16_out]
    elif mode == "v":
        t_spec = pl.BlockSpec((SEC // LANES, tm // kt, LANES, kt), lambda i: (0, i, 0, 0))
        t_out = jax.ShapeDtypeStruct((SEC // LANES, r // kt, LANES, kt), BF16)
        out_specs = [row_spec, t_spec, row_spec, t_spec]
        out_shape = [f32_out, t_out, f32_out, t_out]
    else:
        out_specs = [pl.BlockSpec((tm, 2 * SEC), lambda i: (i, 0))]
        out_shape = [jax.ShapeDtypeStruct((r, 2 * SEC), F32)]
    return pl.pallas_call(
        functools.partial(_proj_kernel, mode=mode, scale=scale, kt=kt),
        grid=(r // tm,), in_specs=in_specs, out_specs=out_specs, out_shape=out_shape,
        compiler_params=_cparams(("parallel",)),
        name="proj_" + mode,
    )(xn, w2)


def _sb_kernel(q_ref, k_ref, vt_ref, tri_ref, o_ref, acc_ref, *, tq, kt, q_pos0):
    qi = pl.program_id(2)
    q0 = q_pos0 + qi * tq
    jm = q0 // kt
    q = q_ref[0]
    lane = lax.broadcasted_iota(jnp.int32, (tq, LANES), 1)
    kpos_l = lax.broadcasted_iota(jnp.int32, (kt, tq), 0)
    qpos_l = lax.broadcasted_iota(jnp.int32, (kt, tq), 1)
    tri = tri_ref[...]
    zero = jnp.zeros_like(q)
    qz = (jnp.where(lane < DH_SB, q, zero), jnp.where(lane >= DH_SB, q, zero))
    acc_ref[...] = jnp.zeros_like(acc_ref)

    def tile(j, hh, carry, masked):
        off = pl.multiple_of(j * kt, kt)
        kk = k_ref[0, pl.ds(off, kt), :]
        z = lax.dot_general(kk, qz[hh], (((1,), (1,)), ((), ())),
                            preferred_element_type=F32)
        sp = jnp.maximum(z, 0.0) + jnp.log(1.0 + jnp.exp(-jnp.abs(z)))
        if masked:
            vis = (kpos_l + j * kt) < (qpos_l + q0)
            lk = jnp.where(vis, -sp, 0.0)
        else:
            lk = -sp
        later = jnp.dot(tri, lk.astype(BF16), preferred_element_type=F32)
        w = jnp.exp(z + lk + later + carry)
        if masked:
            w = jnp.where(vis, w, 0.0)
        acc_ref[hh] += jnp.dot(vt_ref[0, 0, j], w.astype(BF16), preferred_element_type=F32)
        return jnp.sum(lk, axis=0, keepdims=True)

    none_later = jnp.zeros((1, tq), F32)
    carries = tuple(tile(jm, hh, none_later, True) for hh in range(2))

    def cond(state):
        t, ca, cb = state
        return jnp.logical_and(jm - 1 - 2 * t >= 0,
                               jnp.max(jnp.maximum(ca, cb)) > -SB_SKIP_LOG)

    def body(state):
        t, ca, cb = state
        j0 = jm - 1 - 2 * t
        j1 = j0 - 1
        j1c = jnp.maximum(j1, 0)
        out = []
        for hh, carry in ((0, ca), (1, cb)):
            s0 = tile(j0, hh, carry, False)
            carry1 = jnp.where(j1 >= 0, carry + s0, -SB_ABSENT)
            s1 = tile(j1c, hh, carry1, False)
            out.append(carry + s0 + s1)
        return t + 1, out[0], out[1]

    lax.while_loop(cond, body, (jnp.int32(0), carries[0], carries[1]))

    row = lax.broadcasted_iota(jnp.int32, (LANES, tq), 0)
    o_t = jnp.where(row < DH_SB, acc_ref[0], acc_ref[1])
    o_ref[0] = o_t.T.astype(o_ref.dtype)


def _sb_attention(q, k, vt, tri, *, batch, nq, tq, kt, q_pos0, q_sec=0, k_sec=0, v_sec=0):
    rk = k.shape[1]
    nk = rk // kt
    nhp = SEC // LANES
    return pl.pallas_call(
        functools.partial(_sb_kernel, tq=tq, kt=kt, q_pos0=q_pos0),
        grid=(batch, nhp, nq),
        in_specs=[
            pl.BlockSpec((1, tq, LANES), lambda b, h, i: (b + q_sec, i, h)),
            pl.BlockSpec((1, rk, LANES), lambda b, h, i: (b + k_sec, 0, h)),
            pl.BlockSpec((1, 1, nk, LANES, kt), lambda b, h, i: (b + v_sec, h, 0, 0, 0)),
            pl.BlockSpec((kt, kt), lambda b, h, i: (0, 0)),
        ],
        out_specs=pl.BlockSpec((1, tq, LANES), lambda b, h, i: (b, i, h)),
        out_shape=jax.ShapeDtypeStruct((batch, nq * tq, SEC), BF16),
        scratch_shapes=[pltpu.VMEM((2, LANES, tq), F32)],
        compiler_params=_cparams(("parallel", "parallel", "arbitrary")),
        name="sb_attention",
    )(q, k, vt, tri)


def _diff_kernel(slopes_ref, lam_ref, g_ref, q_ref, k_ref, vt_ref, o_ref,
                 acc_ref, bt_ref, s0_ref, s1_ref, p0_ref, p1_ref, kn_ref,
                 *, tq, kt, q_pos0, coff, n_keys, nk):
    h = pl.program_id(1)
    qi = pl.program_id(2)
    slope = slopes_ref[0, h]
    inv_slope = slopes_ref[1, h]
    q0 = q_pos0 + qi * tq
    c_lo = (q0 - coff + CHUNK) // CHUNK - 1
    c_hi = (q0 + tq - 1 - coff + CHUNK) // CHUNK - 1
    full_end = jnp.minimum(coff + CHUNK * (c_lo + 1), n_keys)
    vis_end = jnp.minimum(coff + CHUNK * (c_hi + 1), n_keys)
    n_full = full_end // kt
    n_vis = jnp.minimum((vis_end + kt - 1) // kt, nk)

    q = q_ref[0]
    lane = lax.broadcasted_iota(jnp.int32, (tq, LANES), 1)
    kpos_l = lax.broadcasted_iota(jnp.int32, (kt, tq), 0)
    qpos_l = lax.broadcasted_iota(jnp.int32, (kt, tq), 1)
    gsz = DIFF_GROUP
    m_tiles = (n_full // gsz) * gsz
    n_fullg = m_tiles // gsz
    n_groups = n_fullg + DIFF_TAIL_GROUPS
    n_tail = DIFF_TAIL_GROUPS * gsz
    absent = gsz + n_tail

    for u in range(gsz):
        bt_ref[u] = slope * (kpos_l + u * kt).astype(F32)
    for u in range(n_tail):
        kpos = kpos_l + (m_tiles + u) * kt
        qpos = qpos_l + q0
        vis = (((kpos - coff + CHUNK) >> CHUNK_SHIFT) <= ((qpos - coff + CHUNK) >> CHUNK_SHIFT)) \
            & (kpos < n_keys)
        bias = slope * (qpos_l - jnp.abs(qpos - kpos)).astype(F32)
        bt_ref[gsz + u] = jnp.where(vis, bias, NEG_INF)
    bt_ref[absent] = jnp.full((kt, tq), NEG_INF, F32)

    zero = jnp.zeros_like(q)
    qz = (jnp.where(lane < DH_DIFF, q, zero), jnp.where(lane >= DH_DIFF, q, zero))
    acc_ref[...] = jnp.zeros_like(acc_ref)
    p1_ref[...] = jnp.zeros_like(p1_ref)

    hr = lax.broadcasted_iota(jnp.int32, (LANES, LANES), 0)
    hc = lax.broadcasted_iota(jnp.int32, (LANES, LANES), 1)
    half = (((hc == 0) & (hr < DH_DIFF)) | ((hc == 1) & (hr >= DH_DIFF))).astype(BF16)
    lane1 = lax.broadcasted_iota(jnp.int32, (1, LANES), 1)

    def max_half_norms(sq_max):
        nrm = jnp.sqrt(sq_max)
        return (jnp.max(jnp.where(lane1 == 0, nrm, 0.0)), jnp.max(jnp.where(lane1 == 1, nrm, 0.0)))

    @pl.when(qi == 0)
    def _():
        def body(j, mx):
            off = pl.multiple_of(j * kt, kt)
            kk = k_ref[0, pl.ds(off, kt), :]
            n2 = jnp.dot(kk * kk, half, preferred_element_type=F32)
            return jnp.maximum(mx, jnp.max(n2, axis=0, keepdims=True))
        kn = max_half_norms(lax.fori_loop(0, nk, body, jnp.zeros((1, LANES), F32)))
        kn_ref[0] = kn[0]
        kn_ref[1] = kn[1]

    qn = max_half_norms(jnp.max(jnp.dot(q * q, half, preferred_element_type=F32), axis=0, keepdims=True))
    qk_bound = [DIFF_NORM_SLACK * qn[mm] * kn_ref[mm] for mm in range(2)]
    q0_f = q0.astype(F32)

    def skip_below(state):
        need = jnp.maximum((DIFF_SKIP_LOG + qk_bound[0]) - state[0], (DIFF_SKIP_LOG + qk_bound[1]) - state[2])
        return q0_f - jnp.max(need) * inv_slope

    def seq_group(n):
        return n_groups - 1 - n

    def needed(g, thr):
        top = ((g + 1) * (gsz * kt)).astype(F32)
        return jnp.logical_and(g >= 0, jnp.logical_or(g >= n_fullg, top > thr))

    def tiles_of(g):
        return [jnp.clip(g * gsz + u, 0, nk - 1) for u in range(gsz)]

    def scores_stage(g, s_ref):
        js = tiles_of(g)
        for mm in range(2):
            for u in range(gsz):
                off = pl.multiple_of(js[u] * kt, kt)
                kk = k_ref[0, pl.ds(off, kt), :]
                s_ref[mm, u] = lax.dot_general(kk, qz[mm], (((1,), (1,)), ((), ())),
                                               preferred_element_type=F32)

    def softmax_stage(g, thr, s_ref, p_ref, state):
        ok = needed(g, thr)
        full = jnp.logical_and(ok, g < n_fullg)
        sj = jnp.where(full, slope * (g * gsz * kt - q0).astype(F32), 0.0)
        bidx = [jnp.where(full, u, jnp.where(ok, gsz + g * gsz + u - m_tiles, absent)) for u in range(gsz)]
        new_state, alphas = [], []
        for mm in range(2):
            m, l = state[2 * mm], state[2 * mm + 1]
            ss = [s_ref[mm, u] + bt_ref[bidx[u]] for u in range(gsz)]
            cmax = jnp.max(ss[0], axis=0, keepdims=True)
            for s in ss[1:]:
                cmax = jnp.maximum(cmax, jnp.max(s, axis=0, keepdims=True))
            m_new = jnp.maximum(m, cmax + sj)
            m_safe = jnp.where(m_new == NEG_INF, 0.0, m_new)
            alpha = jnp.exp(m - m_safe)
            r = m_safe - sj
            psum = jnp.zeros((1, tq), F32)
            for u in range(gsz):
                p = jnp.exp(ss[u] - r)
                psum = psum + jnp.sum(p, axis=0, keepdims=True)
                p_ref[mm, u] = p.astype(BF16)
            new_state += [m_new, alpha * l + psum]
            alphas.append(alpha)
        return tuple(new_state), tuple(alphas)

    def value_stage(g, p_ref, alphas):
        js = tiles_of(g)
        for mm in range(2):
            pv = jnp.dot(vt_ref[0, 0, js[0]], p_ref[mm, 0], preferred_element_type=F32)
            for u in range(1, gsz):
                pv = pv + jnp.dot(vt_ref[0, 0, js[u]], p_ref[mm, u], preferred_element_type=F32)
            acc_ref[mm] = alphas[mm] * acc_ref[mm] + pv

    def trip(carry):
        t, thr, state, alpha_prev = carry
        n0 = 2 * t
        scores_stage(seq_group(n0 + 1), s1_ref)
        state, alpha0 = softmax_stage(seq_group(n0), thr, s0_ref, p0_ref, state)
        value_stage(seq_group(n0 - 1), p1_ref, alpha_prev)
        scores_stage(seq_group(n0 + 2), s0_ref)
        state, alpha1 = softmax_stage(seq_group(n0 + 1), thr, s1_ref, p1_ref, state)
        value_stage(seq_group(n0), p0_ref, alpha0)
        return t + 1, skip_below(state), state, alpha1

    def more(carry):
        t, thr = carry[0], carry[1]
        return needed(seq_group(2 * t), thr)

    neg = jnp.full((1, tq), NEG_INF, F32)
    zer = jnp.zeros((1, tq), F32)
    one = jnp.ones((1, tq), F32)
    scores_stage(seq_group(0), s0_ref)
    n_done, _, state, alpha_last = lax.while_loop(
        more, trip, (jnp.int32(0), jnp.float32(NEG_INF), (neg, zer, neg, zer), (one, one)))
    value_stage(seq_group(2 * n_done - 1), p1_ref, alpha_last)
    stats = [(state[0], state[1]), (state[2], state[3])]

    lp = lam_ref[...]
    lam = (jnp.exp(jnp.sum(lp[0:1] * lp[1:2], axis=-1, keepdims=True))
           - jnp.exp(jnp.sum(lp[2:3] * lp[3:4], axis=-1, keepdims=True)) + LAMBDA_INIT)
    l1 = stats[0][1]
    l2 = stats[1][1]
    l1 = jnp.where(l1 == 0.0, 1.0, l1)
    l2 = jnp.where(l2 == 0.0, 1.0, l2)
    o_t = acc_ref[0] / l1 - lam * (acc_ref[1] / l2)
    o = o_t.T
    y = o * lax.rsqrt(jnp.mean(o * o, axis=-1, keepdims=True) + NORM_EPS)
    o_ref[0] = ((y * g_ref[...]) * (1.0 - LAMBDA_INIT)).astype(o_ref.dtype)


def _diff_attention(q, k, vt, slopes, lam_p, subln_g, *, batch, nq, tq, kt, q_pos0, coff, n_keys,
                    q_sec=0, k_sec=0, v_sec=0):
    rk = k.shape[1]
    nk = rk // kt
    nh = SEC // LANES
    return pl.pallas_call(
        functools.partial(_diff_kernel, tq=tq, kt=kt, q_pos0=q_pos0, coff=coff, n_keys=n_keys, nk=nk),
        grid=(batch, nh, nq),
        in_specs=[
            pl.BlockSpec(memory_space=pltpu.SMEM),
            pl.BlockSpec((4, DH_DIFF), lambda b, h, i: (0, 0)),
            pl.BlockSpec((1, DV_DIFF), lambda b, h, i: (0, 0)),
            pl.BlockSpec((1, tq, LANES), lambda b, h, i: (b + q_sec, i, h)),
            pl.BlockSpec((1, rk, LANES), lambda b, h, i: (b + k_sec, 0, h)),
            pl.BlockSpec((1, 1, nk, LANES, kt), lambda b, h, i: (b + v_sec, h, 0, 0, 0)),
        ],
        out_specs=pl.BlockSpec((1, tq, LANES), lambda b, h, i: (b, i, h)),
        out_shape=jax.ShapeDtypeStruct((batch, nq * tq, SEC), BF16),
        scratch_shapes=[pltpu.VMEM((2, LANES, tq), F32),
                        pltpu.VMEM((DIFF_GROUP * (1 + DIFF_TAIL_GROUPS) + 1, kt, tq), F32),
                        pltpu.VMEM((2, DIFF_GROUP, kt, tq), F32),
                        pltpu.VMEM((2, DIFF_GROUP, kt, tq), F32),
                        pltpu.VMEM((2, DIFF_GROUP, kt, tq), BF16),
                        pltpu.VMEM((2, DIFF_GROUP, kt, tq), BF16),
                        pltpu.SMEM((2,), F32)],
        compiler_params=_cparams(("parallel", "parallel", "arbitrary")),
        name="diff_attention",
    )(slopes, lam_p, subln_g.reshape(1, DV_DIFF), q, k, vt)


def _merge_kernel(h_ref, osb_ref, od_ref, gt_ref, wsb_ref, wd_ref, wo_ref, gn_ref, wr_ref, br_ref,
                  h2_ref, hn_ref, te_ref, tg_ref):
    y_sb = jnp.dot(osb_ref[...], wsb_ref[...], preferred_element_type=F32)
    y_d = jnp.dot(od_ref[...], wd_ref[...], preferred_element_type=F32)
    gt = gt_ref[...]
    mix = gt[:, :D_MODEL] * y_sb + gt[:, D_MODEL:] * y_d
    h2 = h_ref[...] + jnp.dot(mix.astype(BF16), wo_ref[...], preferred_element_type=F32)
    h2_ref[...] = h2
    hn = (h2 * lax.rsqrt(jnp.mean(h2 * h2, axis=-1, keepdims=True) + NORM_EPS)) * gn_ref[...]
    hn_ref[...] = hn
    logits = jnp.dot(hn, wr_ref[...], preferred_element_type=F32,
                     precision=lax.Precision.HIGHEST) + br_ref[...]
    tm = logits.shape[0]
    lane = lax.broadcasted_iota(jnp.int32, (tm, LANES), 1).astype(F32)
    tops, idxs = [], []
    l = logits
    for _ in range(TOP_K):
        m = jnp.max(l, axis=-1, keepdims=True)
        idx = jnp.min(jnp.where(l == m, lane, float(LANES)), axis=-1, keepdims=True)
        tops.append(m)
        idxs.append(idx)
        l = jnp.where(lane == idx, NEG_INF, l)
    ex = [jnp.exp(t - tops[0]) for t in tops]
    den = ex[0] + ex[1] + ex[2] + ex[3]
    te = jnp.zeros((tm, LANES), F32)
    tg = jnp.zeros((tm, LANES), F32)
    for k in range(TOP_K):
        te = jnp.where(lane == float(k), idxs[k], te)
        tg = jnp.where(lane == float(k), ex[k] / den, tg)
    te_ref[...] = te.astype(jnp.int32)
    tg_ref[...] = tg


def _merge(h, o_sb, o_d, gates, w_sb, w_d, w_o, g_ffn, w_r, b_r, tm):
    r, d = h.shape
    row = lambda w: pl.BlockSpec((tm, w), lambda i: (i, 0))
    full = lambda a, b: pl.BlockSpec((a, b), lambda i: (0, 0))
    return pl.pallas_call(
        _merge_kernel,
        grid=(r // tm,),
        in_specs=[row(d), row(d), row(d), row(2 * d), full(d, d), full(d, d), full(d, d),
                  full(1, d), full(d, LANES), full(1, LANES)],
        out_specs=[row(d), row(d), row(LANES), row(LANES)],
        out_shape=[jax.ShapeDtypeStruct((r, d), F32), jax.ShapeDtypeStruct((r, d), F32),
                   jax.ShapeDtypeStruct((r, LANES), jnp.int32), jax.ShapeDtypeStruct((r, LANES), F32)],
        compiler_params=_cparams(("parallel",)),
        name="merge_router",
    )(h, o_sb, o_d, gates, w_sb, w_d, w_o, g_ffn.reshape(1, d), w_r, b_r)


def _moe_kernel(be_ref, nb_ref, x_ref, gate_ref, wgu_ref, bgu_ref, wdn_ref, bdn_ref, o_ref,
                wgu_bf, wdn_bf):
    b = pl.program_id(0)
    used = b < nb_ref[0]
    new_expert = jnp.logical_or(b == 0, be_ref[b] != be_ref[jnp.maximum(b - 1, 0)])

    @pl.when(jnp.logical_and(used, new_expert))
    def _():
        wgu_bf[...] = wgu_ref[0].astype(BF16)
        wdn_bf[...] = wdn_ref[0].astype(BF16)

    @pl.when(used)
    def _():
        gu = jnp.dot(x_ref[...].astype(BF16), wgu_bf[...], preferred_element_type=F32) + bgu_ref[0]
        g = jnp.minimum(gu[:, :D_FF], SWIGLU_LIMIT)
        u = jnp.clip(gu[:, D_FF:], -SWIGLU_LIMIT, SWIGLU_LIMIT)
        act = (u + 1.0) * (g * jax.nn.sigmoid(SWIGLU_ALPHA * g))
        y = jnp.dot(act.astype(BF16), wdn_bf[...], preferred_element_type=F32) + bdn_ref[0]
        o_ref[...] = y * gate_ref[...]

    @pl.when(b >= nb_ref[0])
    def _():
        o_ref[...] = jnp.zeros_like(o_ref)


def _moe_gmm(block_e, n_used, xs, row_gate, w_gu, b_gu, w_dn, b_dn, bm):
    n_rows = row_gate.shape[0]
    d = xs.shape[1]
    nb = n_rows // bm
    grid_spec = pltpu.PrefetchScalarGridSpec(
        num_scalar_prefetch=2,
        grid=(nb,),
        in_specs=[
            pl.BlockSpec((bm, d), lambda b, be, nu: (b, 0)),
            pl.BlockSpec((bm, 1), lambda b, be, nu: (b, 0)),
            pl.BlockSpec((1, d, 2 * D_FF), lambda b, be, nu: (be[b], 0, 0)),
            pl.BlockSpec((1, 1, 2 * D_FF), lambda b, be, nu: (be[b], 0, 0)),
            pl.BlockSpec((1, D_FF, d), lambda b, be, nu: (be[b], 0, 0)),
            pl.BlockSpec((1, 1, d), lambda b, be, nu: (be[b], 0, 0)),
        ],
        out_specs=pl.BlockSpec((bm, d), lambda b, be, nu: (b, 0)),
        scratch_shapes=[pltpu.VMEM((d, 2 * D_FF), BF16), pltpu.VMEM((D_FF, d), BF16)],
    )
    return pl.pallas_call(
        _moe_kernel,
        grid_spec=grid_spec,
        out_shape=jax.ShapeDtypeStruct((n_rows, d), F32),
        compiler_params=_cparams(("arbitrary",)),
        name="moe_experts",
    )(block_e, n_used, xs, row_gate, w_gu, b_gu, w_dn, b_dn)


def _final_kernel(h_ref, c_ref, g_ref, o_ref):
    c = c_ref[...]
    d = h_ref.shape[1]
    y = c[:, 0:d]
    for k in range(1, TOP_K):
        y = y + c[:, k * d:(k + 1) * d]
    x = h_ref[...] + y
    o_ref[...] = (x * lax.rsqrt(jnp.mean(x * x, axis=-1, keepdims=True) + NORM_EPS)) * g_ref[...]


def _final(h2, contrib, g, tm, c_row0=0):
    r, d = h2.shape
    c_blk0 = c_row0 // tm
    return pl.pallas_call(
        _final_kernel,
        grid=(r // tm,),
        in_specs=[pl.BlockSpec((tm, d), lambda i: (i, 0)),
                  pl.BlockSpec((tm, TOP_K * d), lambda i: (i + c_blk0, 0)),
                  pl.BlockSpec((1, d), lambda i: (0, 0))],
        out_specs=pl.BlockSpec((tm, d), lambda i: (i, 0)),
        out_shape=jax.ShapeDtypeStruct((r, d), F32),
        compiler_params=_cparams(("parallel",)),
        name="combine_final_norm",
    )(h2, contrib, g.reshape(1, d))


def _route(top_e, top_g, bm):
    n = top_e.shape[0]
    a = n * TOP_K
    flat_e = top_e.reshape(-1)
    order = jnp.argsort(flat_e).astype(jnp.int32)
    onehot = (flat_e[:, None] == jnp.arange(N_EXPERTS, dtype=jnp.int32)[None, :]).astype(jnp.int32)
    csum = jnp.cumsum(onehot, axis=0)
    counts = csum[-1]
    rank = jnp.sum(onehot * csum, axis=1) - 1
    padded = ((counts + bm - 1) // bm) * bm
    pend = jnp.cumsum(padded)
    pstart = pend - padded
    start = jnp.cumsum(counts) - counts
    pos = (jnp.sum(onehot * pstart[None, :], axis=1) + rank).reshape(n, TOP_K)
    n_rows = -(-(a + N_EXPERTS * (bm - 1)) // bm) * bm
    nb = n_rows // bm
    block_start = jnp.arange(nb, dtype=jnp.int32) * bm
    block_e = jnp.minimum(jnp.sum((pend[None, :] <= block_start[:, None]).astype(jnp.int32), axis=1),
                          N_EXPERTS - 1)
    row = jnp.arange(n_rows, dtype=jnp.int32)
    row_e = jnp.repeat(block_e, bm, total_repeat_length=n_rows)
    row_rank = row - pstart[row_e]
    valid = jnp.logical_and(row_rank >= 0, row_rank < counts[row_e])
    src = order[jnp.clip(start[row_e] + row_rank, 0, a - 1)]
    row_tok = jnp.where(valid, src // TOP_K, n)
    row_gate = jnp.where(valid, top_g.reshape(-1)[src], 0.0)
    n_used = (pend[-1] // bm).astype(jnp.int32).reshape(1)
    return row_tok, row_gate, pos, block_e, n_used


def _sc_gather(table, idx):
    n_idx = idx.shape[0]
    d = table.shape[1]
    sub = SC_GATHER_WINDOW // SC_SUB_WINDOWS
    mesh = plsc.VectorSubcoreMesh(core_axis_name="c", subcore_axis_name="s")

    @pl.kernel(out_type=jax.ShapeDtypeStruct((n_idx, d), table.dtype), mesh=mesh)
    def gather_kernel(x_hbm, i_hbm, o_hbm):
        def body(i_vmem, o_vmem):
            j = pl.program_id(1)
            pltpu.sync_copy(x_hbm.at[i_vmem.at[0, pl.ds(j * sub, sub)]], o_vmem)

        pltpu.emit_pipeline(
            body,
            grid=(n_idx // SC_GATHER_WINDOW, SC_SUB_WINDOWS),
            in_specs=[pl.BlockSpec((1, SC_GATHER_WINDOW), index_map=lambda i, j: (0, i))],
            out_specs=[pl.BlockSpec((sub, d), index_map=lambda i, j: (i * SC_SUB_WINDOWS + j, 0))],
            core_axis_name=("c", "s"),
            dimension_semantics=(pltpu.PARALLEL, pltpu.ARBITRARY),
        )(i_hbm, o_hbm)

    return gather_kernel(table, idx.reshape(1, n_idx))


def _pad_to(a, mult, value):
    pad = (-a.shape[0]) % mult
    return a if pad == 0 else jnp.concatenate([a, jnp.full((pad,), value, a.dtype)])


def _tri(kt):
    s = lax.broadcasted_iota(jnp.int32, (kt, kt), 0)
    j = lax.broadcasted_iota(jnp.int32, (kt, kt), 1)
    return (j > s).astype(BF16)


def _tiles_t(v, kt):
    b, r, _ = v.shape
    return v.reshape(b, r // kt, kt, SEC // LANES, LANES).transpose(0, 3, 1, 4, 2)


def kernel(x_prompt, x_sample, cache_sb_k, cache_sb_v, cache_diff_k, cache_diff_v, meta_tokens,
           norm_mix_g, w_in, diff_lambda, diff_subln_g, w_br_sb, w_br_diff, w_out, norm_ffn_g,
           w_router, b_router, w_gate_up, b_gate_up, w_down, b_down, final_norm_g):
    assert x_prompt.shape[0] == 1 and w_in.shape[0] == 1
    d = D_MODEL
    seq = x_prompt.shape[1]
    t = N_META + seq
    tq_p, kt = Q_TILE_PROMPT, KEY_TILE
    tp = -(-t // tq_p) * tq_p
    nq_p = tp // tq_p
    nb_s, s_len = x_sample.shape[:2]
    past = cache_sb_k.shape[2]
    n_s = nb_s * s_len
    scale = DH_SB ** -0.5

    w0 = w_in[0]
    wq = jnp.concatenate([w0[:, 0:SEC], w0[:, 3 * SEC:4 * SEC]], axis=1).astype(BF16)
    wk = jnp.concatenate([w0[:, SEC:2 * SEC], w0[:, 4 * SEC:5 * SEC]], axis=1).astype(BF16)
    wv = jnp.concatenate([w0[:, 2 * SEC:3 * SEC], w0[:, 5 * SEC:6 * SEC]], axis=1).astype(BF16)
    wg = w0[:, 6 * SEC:8 * SEC].astype(BF16)
    w_sb = w_br_sb[0].astype(BF16)
    w_d = w_br_diff[0].astype(BF16)
    w_o = w_out[0].astype(BF16)
    w_r = jnp.pad(w_router[0], ((0, 0), (0, LANES - N_EXPERTS)))
    b_r = jnp.pad(b_router[0], (0, LANES - N_EXPERTS), constant_values=NEG_INF).reshape(1, LANES)
    w_gu = w_gate_up[0]
    w_dn = w_down[0]
    b_gu = b_gate_up[0].reshape(N_EXPERTS, 1, 2 * D_FF)
    b_dn = b_down[0].reshape(N_EXPERTS, 1, d)
    slope = jnp.exp2(-8.0 * jnp.arange(1, H_DIFF + 1, dtype=F32) / H_DIFF)
    slopes = jnp.stack([slope, 1.0 / slope])
    tri = _tri(kt)
    lam_p = diff_lambda[0]
    g_sub = diff_subln_g[0]

    hp = jnp.concatenate([meta_tokens.astype(F32), x_prompt[0], jnp.zeros((tp - t, d), F32)], axis=0)
    tm_p = tq_p
    xn = _rmsnorm(hp, norm_mix_g[0], tm_p)
    q_sb, q_d = _proj(xn, wq, "q", tm_p, scale=scale)
    kf_sb, kb_sb, kf_d, kb_d = _proj(xn, wk, "k", tm_p, r_out=t)
    vf_sb, vt_sb, vf_d, vt_d = _proj(xn, wv, "v", tm_p, r_out=t)
    (gates,) = _proj(xn, wg, "g", tm_p)
    o_sb = _sb_attention(q_sb[None], kb_sb[None], vt_sb[None], tri, batch=1, nq=nq_p, tq=tq_p, kt=kt,
                         q_pos0=0)
    o_d = _diff_attention(q_d[None], kb_d[None], vt_d[None], slopes, lam_p, g_sub, batch=1, nq=nq_p,
                          tq=tq_p, kt=kt, q_pos0=0, coff=N_META, n_keys=t)
    h2, hn, te, tg = _merge(hp, o_sb[0], o_d[0], gates, w_sb, w_d, w_o, norm_ffn_g[0], w_r, b_r, tq_p)

    tq_s = Q_TILE_DECODE
    hs = x_sample.reshape(n_s, d)
    xn_s = _rmsnorm(hs, norm_mix_g[0], n_s)
    q2_s = _proj(xn_s, wq, "q", n_s, scale=scale)
    kf_s0, _, kf_s1, _ = _proj(xn_s, wk, "k", n_s)
    vf_s0, _, vf_s1, _ = _proj(xn_s, wv, "v", n_s, kt=n_s)
    kf_s = (kf_s0, kf_s1)
    vf_s = (vf_s0, vf_s1)
    (gates_s,) = _proj(xn_s, wg, "g", n_s)
    rk_s = -(-(past + s_len) // kt) * kt

    def with_cache(cache, new):
        c = cache[0].reshape(nb_s, past, SEC).astype(BF16)
        nw = new.reshape(nb_s, s_len, SEC).astype(BF16)
        z = jnp.zeros((nb_s, rk_s - past - s_len, SEC), BF16)
        return jnp.concatenate([c, nw, z], axis=1)

    def pad_q(qs):
        return jnp.pad(qs.reshape(nb_s, s_len, SEC), ((0, 0), (0, tq_s - s_len), (0, 0)))

    k_sb_all = with_cache(cache_sb_k, kf_s[0])
    vt_sb_all = _tiles_t(with_cache(cache_sb_v, vf_s[0]), kt)
    k_d_all = with_cache(cache_diff_k, kf_s[1])
    vt_d_all = _tiles_t(with_cache(cache_diff_v, vf_s[1]), kt)
    o_sb_s = _sb_attention(pad_q(q2_s[0]), k_sb_all, vt_sb_all, tri, batch=nb_s, nq=1, tq=tq_s, kt=kt,
                           q_pos0=past)
    o_d_s = _diff_attention(pad_q(q2_s[1]), k_d_all, vt_d_all, slopes, lam_p, g_sub, batch=nb_s, nq=1,
                            tq=tq_s, kt=kt, q_pos0=past, coff=0, n_keys=past + s_len)
    o_sb_s = o_sb_s[:, :s_len].reshape(n_s, SEC)
    o_d_s = o_d_s[:, :s_len].reshape(n_s, SEC)
    h2_s, hn_s, te_s, tg_s = _merge(hs, o_sb_s, o_d_s, gates_s, w_sb, w_d, w_o, norm_ffn_g[0], w_r, b_r,
                                    n_s)

    n_tok = t + n_s
    top_e = jnp.concatenate([te[:t, :TOP_K], te_s[:, :TOP_K]], axis=0)
    top_g = jnp.concatenate([tg[:t, :TOP_K], tg_s[:, :TOP_K]], axis=0)
    row_tok, row_gate, pos, block_e, n_used = _route(top_e, top_g, MOE_ROWS)
    gather_mult = SC_GATHER_WINDOW * 8
    hn_all = jnp.concatenate([hn[:t], hn_s, jnp.zeros((8, d), F32)], axis=0)
    xs = _sc_gather(hn_all, _pad_to(row_tok, gather_mult, n_tok))
    yb = _moe_gmm(block_e, n_used, xs, row_gate.reshape(-1, 1), w_gu, b_gu, w_dn, b_dn, MOE_ROWS)
    pos_all = jnp.concatenate([pos[:t].reshape(-1), jnp.zeros(((tp - t) * TOP_K,), jnp.int32),
                               pos[t:].reshape(-1)])
    contrib = _sc_gather(yb, _pad_to(pos_all, gather_mult, 0))
    contrib = contrib.reshape(-1, TOP_K * d)
    y_p = _final(h2, contrib, final_norm_g, tq_p)
    y_s = _final(h2_s, contrib, final_norm_g, n_s, c_row0=tp)

    y_prompt = y_p[N_META:t][None]
    y_sample = y_s.reshape(nb_s, s_len, d)

    def heads(a, nh):
        return a.reshape(1, *a.shape[:-1], nh, a.shape[-1] // nh)

    return (y_prompt, y_sample,
            heads(kf_sb[None], H_SB), heads(vf_sb[None], H_SB),
            heads(kf_d[None], H_DIFF), heads(vf_d[None], H_DIFF),
            heads(kf_s[0].reshape(nb_s, s_len, SEC), H_SB), heads(vf_s[0].reshape(nb_s, s_len, SEC), H_SB),
            heads(kf_s[1].reshape(nb_s, s_len, SEC), H_DIFF), heads(vf_s[1].reshape(nb_s, s_len, SEC), H_DIFF))
```

```python
import functools
import math

import jax
import jax.numpy as jnp
from jax import lax
from jax.experimental import pallas as pl
from jax.experimental.pallas import tpu as pltpu
from jax.experimental.pallas import tpu_sc as plsc

F32 = jnp.float32
BF16 = jnp.bfloat16

D_MODEL = 1024
CHUNK = 64
CHUNK_SHIFT = 6
N_META = 16
H_SB = 16
DH_SB = 64
H_DIFF = 8
DH_DIFF = 64
DV_DIFF = 128
N_EXPERTS = 32
TOP_K = 4
D_FF = 1024
SWIGLU_LIMIT = 7.0
SWIGLU_ALPHA = 1.702
NORM_EPS = 1e-5
LAMBDA_INIT = 0.8 - 0.6 * math.exp(-0.3 * 0)

LANES = 128
SEC = 1024
KEY_TILE = 256
Q_TILE_PROMPT = 256
Q_TILE_DECODE = 128
MOE_ROWS = 256
VMEM_LIMIT = 56 * 1024 * 1024
NEG_INF = float("-inf")
SB_SKIP_LOG = 104.0
SB_ABSENT = 1e30
DIFF_GROUP = 2
DIFF_TAIL_GROUPS = 2
DIFF_SKIP_LOG = 105.0
DIFF_NORM_SLACK = 1.01
SC_GATHER_WINDOW = 128
SC_SUB_WINDOWS = 4


def _cparams(sem):
    return pltpu.CompilerParams(dimension_semantics=sem, vmem_limit_bytes=VMEM_LIMIT)


def _rmsnorm_kernel(x_ref, g_ref, o_ref):
    x = x_ref[...]
    y = x * lax.rsqrt(jnp.mean(x * x, axis=-1, keepdims=True) + NORM_EPS)
    o_ref[...] = (y * g_ref[...]).astype(o_ref.dtype)


def _rmsnorm(x, g, tm):
    r, d = x.shape
    return pl.pallas_call(
        _rmsnorm_kernel,
        grid=(r // tm,),
        in_specs=[pl.BlockSpec((tm, d), lambda i: (i, 0)),
                  pl.BlockSpec((1, d), lambda i: (0, 0))],
        out_specs=pl.BlockSpec((tm, d), lambda i: (i, 0)),
        out_shape=jax.ShapeDtypeStruct((r, d), BF16),
        compiler_params=_cparams(("parallel",)),
        name="rmsnorm",
    )(x, g.reshape(1, d))


def _proj_kernel(x_ref, w_ref, *out_refs, mode, scale, kt):
    x = x_ref[...]
    for s in range(2):
        acc = jnp.dot(x, w_ref[:, s * SEC:(s + 1) * SEC], preferred_element_type=F32)
        if mode == "q":
            out_refs[s][...] = (acc * scale).astype(BF16)
        elif mode == "k":
            out_refs[2 * s][...] = acc
            out_refs[2 * s + 1][...] = acc.astype(BF16)
        elif mode == "v":
            out_refs[2 * s][...] = acc
            for c in range(acc.shape[0] // kt):
                for hp in range(SEC // LANES):
                    tile = acc[c * kt:(c + 1) * kt, hp * LANES:(hp + 1) * LANES]
                    out_refs[2 * s + 1][hp, c] = tile.T.astype(BF16)
        else:
            out_refs[0][:, s * SEC:(s + 1) * SEC] = jax.nn.sigmoid(acc)


def _proj(xn, w2, mode, tm, r_out=None, kt=KEY_TILE, scale=1.0):
    r, d = xn.shape
    r_out = r if r_out is None else r_out
    in_specs = [pl.BlockSpec((tm, d), lambda i: (i, 0)),
                pl.BlockSpec((d, 2 * SEC), lambda i: (0, 0))]
    row_spec = pl.BlockSpec((tm, SEC), lambda i: (i, 0))
    f32_out = jax.ShapeDtypeStruct((r_out, SEC), F32)
    bf16_out = jax.ShapeDtypeStruct((r, SEC), BF16)
    if mode == "q":
        out_specs = [row_spec, row_spec]
        out_shape = [bf16_out, bf16_out]
    elif mode == "k":
        out_specs = [row_spec] * 4
        out_shape = [f32_out, bf16_out, f32_out, bf16_out]
    elif mode == "v":
        t_spec = pl.BlockSpec((SEC // LANES, tm // kt, LANES, kt), lambda i: (0, i, 0, 0))
        t_out = jax.ShapeDtypeStruct((SEC // LANES, r // kt, LANES, kt), BF16)
        out_specs = [row_spec, t_spec, row_spec, t_spec]
        out_shape = [f32_out, t_out, f32_out, t_out]
    else:
        out_specs = [pl.BlockSpec((tm, 2 * SEC), lambda i: (i, 0))]
        out_shape = [jax.ShapeDtypeStruct((r, 2 * SEC), F32)]
    return pl.pallas_call(
        functools.partial(_proj_kernel, mode=mode, scale=scale, kt=kt),
        grid=(r // tm,), in_specs=in_specs, out_specs=out_specs, out_shape=out_shape,
        compiler_params=_cparams(("parallel",)),
        name="proj_" + mode,
    )(xn, w2)


def _sb_kernel(q_ref, k_ref, vt_ref, tri_ref, o_ref, acc_ref, *, tq, kt, q_pos0):
    qi = pl.program_id(2)
    q0 = q_pos0 + qi * tq
    jm = q0 // kt
    q = q_ref[0]
    lane = lax.broadcasted_iota(jnp.int32, (tq, LANES), 1)
    kpos_l = lax.broadcasted_iota(jnp.int32, (kt, tq), 0)
    qpos_l = lax.broadcasted_iota(jnp.int32, (kt, tq), 1)
    tri = tri_ref[...]
    zero = jnp.zeros_like(q)
    qz = (jnp.where(lane < DH_SB, q, zero), jnp.where(lane >= DH_SB, q, zero))
    acc_ref[...] = jnp.zeros_like(acc_ref)

    def tile(j, hh, carry, masked):
        off = pl.multiple_of(j * kt, kt)
        kk = k_ref[0, pl.ds(off, kt), :]
        z = lax.dot_general(kk, qz[hh], (((1,), (1,)), ((), ())),
                            preferred_element_type=F32)
        sp = jnp.maximum(z, 0.0) + jnp.log(1.0 + jnp.exp(-jnp.abs(z)))
        if masked:
            vis = (kpos_l + j * kt) < (qpos_l + q0)
            lk = jnp.where(vis, -sp, 0.0)
        else:
            lk = -sp
        later = jnp.dot(tri, lk.astype(BF16), preferred_element_type=F32)
        w = jnp.exp(z + lk + later + carry)
        if masked:
            w = jnp.where(vis, w, 0.0)
        acc_ref[hh] += jnp.dot(vt_ref[0, 0, j], w.astype(BF16), preferred_element_type=F32)
        return jnp.sum(lk, axis=0, keepdims=True)

    none_later = jnp.zeros((1, tq), F32)
    carries = tuple(tile(jm, hh, none_later, True) for hh in range(2))

    def cond(state):
        t, ca, cb = state
        return jnp.logical_and(jm - 1 - 2 * t >= 0,
                               jnp.max(jnp.maximum(ca, cb)) > -SB_SKIP_LOG)

    def pair(j0, ca, cb):
        j1 = j0 - 1
        out = []
        for hh, carry in ((0, ca), (1, cb)):
            s0 = tile(jnp.maximum(j0, 0), hh, jnp.where(j0 >= 0, carry, -SB_ABSENT), False)
            s1 = tile(jnp.maximum(j1, 0), hh, jnp.where(j1 >= 0, carry + s0, -SB_ABSENT), False)
            out.append(carry + s0 + s1)
        return out

    def body(state):
        t, ca, cb = state
        ca, cb = pair(jm - 1 - 2 * t, ca, cb)
        return t + 1, ca, cb

    ca, cb = pair(jm - 1, carries[0], carries[1])
    lax.while_loop(cond, body, (jnp.int32(1), ca, cb))

    row = lax.broadcasted_iota(jnp.int32, (LANES, tq), 0)
    o_t = jnp.where(row < DH_SB, acc_ref[0], acc_ref[1])
    o_ref[0] = o_t.T.astype(o_ref.dtype)


def _sb_attention(q, k, vt, tri, *, batch, nq, tq, kt, q_pos0, q_sec=0, k_sec=0, v_sec=0):
    rk = k.shape[1]
    nk = rk // kt
    nhp = SEC // LANES
    return pl.pallas_call(
        functools.partial(_sb_kernel, tq=tq, kt=kt, q_pos0=q_pos0),
        grid=(batch, nhp, nq),
        in_specs=[
            pl.BlockSpec((1, tq, LANES), lambda b, h, i: (b + q_sec, i, h)),
            pl.BlockSpec((1, rk, LANES), lambda b, h, i: (b + k_sec, 0, h)),
            pl.BlockSpec((1, 1, nk, LANES, kt), lambda b, h, i: (b + v_sec, h, 0, 0, 0)),
            pl.BlockSpec((kt, kt), lambda b, h, i: (0, 0)),
        ],
        out_specs=pl.BlockSpec((1, tq, LANES), lambda b, h, i: (b, i, h)),
        out_shape=jax.ShapeDtypeStruct((batch, nq * tq, SEC), BF16),
        scratch_shapes=[pltpu.VMEM((2, LANES, tq), F32)],
        compiler_params=_cparams(("parallel", "parallel", "arbitrary")),
        name="sb_attention",
    )(q, k, vt, tri)


def _diff_kernel(slopes_ref, lam_ref, g_ref, q_ref, k_ref, vt_ref, o_ref,
                 acc_ref, bt_ref, s0_ref, s1_ref, p0_ref, p1_ref, kn_ref,
                 *, tq, kt, q_pos0, coff, n_keys, nk):
    h = pl.program_id(1)
    qi = pl.program_id(2)
    slope = slopes_ref[0, h]
    inv_slope = slopes_ref[1, h]
    q0 = q_pos0 + qi * tq
    c_lo = (q0 - coff + CHUNK) // CHUNK - 1
    c_hi = (q0 + tq - 1 - coff + CHUNK) // CHUNK - 1
    full_end = jnp.minimum(coff + CHUNK * (c_lo + 1), n_keys)
    vis_end = jnp.minimum(coff + CHUNK * (c_hi + 1), n_keys)
    n_full = full_end // kt
    n_vis = jnp.minimum((vis_end + kt - 1) // kt, nk)

    q = q_ref[0]
    lane = lax.broadcasted_iota(jnp.int32, (tq, LANES), 1)
    kpos_l = lax.broadcasted_iota(jnp.int32, (kt, tq), 0)
    qpos_l = lax.broadcasted_iota(jnp.int32, (kt, tq), 1)
    gsz = DIFF_GROUP
    m_tiles = (n_full // gsz) * gsz
    n_fullg = m_tiles // gsz
    n_groups = n_fullg + DIFF_TAIL_GROUPS
    n_tail = DIFF_TAIL_GROUPS * gsz
    absent = gsz + n_tail

    for u in range(gsz):
        bt_ref[u] = slope * (kpos_l + u * kt).astype(F32)
    for u in range(n_tail):
        kpos = kpos_l + (m_tiles + u) * kt
        qpos = qpos_l + q0
        vis = (((kpos - coff + CHUNK) >> CHUNK_SHIFT) <= ((qpos - coff + CHUNK) >> CHUNK_SHIFT)) \
            & (kpos < n_keys)
        bias = slope * (qpos_l - jnp.abs(qpos - kpos)).astype(F32)
        bt_ref[gsz + u] = jnp.where(vis, bias, NEG_INF)
    bt_ref[absent] = jnp.full((kt, tq), NEG_INF, F32)

    zero = jnp.zeros_like(q)
    qz = (jnp.where(lane < DH_DIFF, q, zero), jnp.where(lane >= DH_DIFF, q, zero))
    acc_ref[...] = jnp.zeros_like(acc_ref)
    p1_ref[...] = jnp.zeros_like(p1_ref)

    hr = lax.broadcasted_iota(jnp.int32, (LANES, LANES), 0)
    hc = lax.broadcasted_iota(jnp.int32, (LANES, LANES), 1)
    half = (((hc == 0) & (hr < DH_DIFF)) | ((hc == 1) & (hr >= DH_DIFF))).astype(BF16)
    lane1 = lax.broadcasted_iota(jnp.int32, (1, LANES), 1)

    def max_half_norms(sq_max):
        nrm = jnp.sqrt(sq_max)
        return (jnp.max(jnp.where(lane1 == 0, nrm, 0.0)), jnp.max(jnp.where(lane1 == 1, nrm, 0.0)))

    @pl.when(qi == 0)
    def _():
        def body(j, mx):
            off = pl.multiple_of(j * kt, kt)
            kk = k_ref[0, pl.ds(off, kt), :]
            n2 = jnp.dot(kk * kk, half, preferred_element_type=F32)
            return jnp.maximum(mx, jnp.max(n2, axis=0, keepdims=True))
        kn = max_half_norms(lax.fori_loop(0, nk, body, jnp.zeros((1, LANES), F32)))
        kn_ref[0] = kn[0]
        kn_ref[1] = kn[1]

    qn = max_half_norms(jnp.max(jnp.dot(q * q, half, preferred_element_type=F32), axis=0, keepdims=True))
    qk_bound = [DIFF_NORM_SLACK * qn[mm] * kn_ref[mm] for mm in range(2)]
    q0_f = q0.astype(F32)

    def skip_below(state):
        need = jnp.maximum((DIFF_SKIP_LOG + qk_bound[0]) - state[0], (DIFF_SKIP_LOG + qk_bound[1]) - state[2])
        return q0_f - jnp.max(need) * inv_slope

    def seq_group(n):
        return n_groups - 1 - n

    def needed(g, thr):
        top = ((g + 1) * (gsz * kt)).astype(F32)
        return jnp.logical_and(g >= 0, jnp.logical_or(g >= n_fullg, top > thr))

    def tiles_of(g):
        return [jnp.clip(g * gsz + u, 0, nk - 1) for u in range(gsz)]

    def scores_stage(g, s_ref):
        js = tiles_of(g)
        for mm in range(2):
            for u in range(gsz):
                off = pl.multiple_of(js[u] * kt, kt)
                kk = k_ref[0, pl.ds(off, kt), :]
                s_ref[mm, u] = lax.dot_general(kk, qz[mm], (((1,), (1,)), ((), ())),
                                               preferred_element_type=F32)

    def softmax_stage(g, thr, s_ref, p_ref, state):
        ok = needed(g, thr)
        full = jnp.logical_and(ok, g < n_fullg)
        sj = jnp.where(full, slope * (g * gsz * kt - q0).astype(F32), 0.0)
        bidx = [jnp.where(full, u, jnp.where(ok, gsz + g * gsz + u - m_tiles, absent)) for u in range(gsz)]
        new_state, alphas = [], []
        for mm in range(2):
            m, l = state[2 * mm], state[2 * mm + 1]
            ss = [s_ref[mm, u] + bt_ref[bidx[u]] for u in range(gsz)]
            cmax = jnp.max(ss[0], axis=0, keepdims=True)
            for s in ss[1:]:
                cmax = jnp.maximum(cmax, jnp.max(s, axis=0, keepdims=True))
            m_new = jnp.maximum(m, cmax + sj)
            m_safe = jnp.where(m_new == NEG_INF, 0.0, m_new)
            alpha = jnp.exp(m - m_safe)
            r = m_safe - sj
            psum = jnp.zeros((1, tq), F32)
            for u in range(gsz):
                p = jnp.exp(ss[u] - r)
                psum = psum + jnp.sum(p, axis=0, keepdims=True)
                p_ref[mm, u] = p.astype(BF16)
            new_state += [m_new, alpha * l + psum]
            alphas.append(alpha)
        return tuple(new_state), tuple(alphas)

    def value_stage(g, p_ref, alphas):
        js = tiles_of(g)
        for mm in range(2):
            pv = jnp.dot(vt_ref[0, 0, js[0]], p_ref[mm, 0], preferred_element_type=F32)
            for u in range(1, gsz):
                pv = pv + jnp.dot(vt_ref[0, 0, js[u]], p_ref[mm, u], preferred_element_type=F32)
            acc_ref[mm] = alphas[mm] * acc_ref[mm] + pv

    def trip(carry):
        t, thr, state, alpha_prev = carry
        n0 = 2 * t
        scores_stage(seq_group(n0 + 1), s1_ref)
        state, alpha0 = softmax_stage(seq_group(n0), thr, s0_ref, p0_ref, state)
        value_stage(seq_group(n0 - 1), p1_ref, alpha_prev)
        scores_stage(seq_group(n0 + 2), s0_ref)
        state, alpha1 = softmax_stage(seq_group(n0 + 1), thr, s1_ref, p1_ref, state)
        value_stage(seq_group(n0), p0_ref, alpha0)
        return t + 1, skip_below(state), state, alpha1

    def more(carry):
        t, thr = carry[0], carry[1]
        return needed(seq_group(2 * t), thr)

    neg = jnp.full((1, tq), NEG_INF, F32)
    zer = jnp.zeros((1, tq), F32)
    one = jnp.ones((1, tq), F32)
    scores_stage(seq_group(0), s0_ref)
    n_done, _, state, alpha_last = lax.while_loop(
        more, trip, (jnp.int32(0), jnp.float32(NEG_INF), (neg, zer, neg, zer), (one, one)))
    value_stage(seq_group(2 * n_done - 1), p1_ref, alpha_last)
    stats = [(state[0], state[1]), (state[2], state[3])]

    lp = lam_ref[...]
    lam = (jnp.exp(jnp.sum(lp[0:1] * lp[1:2], axis=-1, keepdims=True))
           - jnp.exp(jnp.sum(lp[2:3] * lp[3:4], axis=-1, keepdims=True)) + LAMBDA_INIT)
    l1 = stats[0][1]
    l2 = stats[1][1]
    l1 = jnp.where(l1 == 0.0, 1.0, l1)
    l2 = jnp.where(l2 == 0.0, 1.0, l2)
    o_t = acc_ref[0] / l1 - lam * (acc_ref[1] / l2)
    o = o_t.T
    y = o * lax.rsqrt(jnp.mean(o * o, axis=-1, keepdims=True) + NORM_EPS)
    o_ref[0] = ((y * g_ref[...]) * (1.0 - LAMBDA_INIT)).astype(o_ref.dtype)


def _diff_attention(q, k, vt, slopes, lam_p, subln_g, *, batch, nq, tq, kt, q_pos0, coff, n_keys,
                    q_sec=0, k_sec=0, v_sec=0):
    rk = k.shape[1]
    nk = rk // kt
    nh = SEC // LANES
    return pl.pallas_call(
        functools.partial(_diff_kernel, tq=tq, kt=kt, q_pos0=q_pos0, coff=coff, n_keys=n_keys, nk=nk),
        grid=(batch, nh, nq),
        in_specs=[
            pl.BlockSpec(memory_space=pltpu.SMEM),
            pl.BlockSpec((4, DH_DIFF), lambda b, h, i: (0, 0)),
            pl.BlockSpec((1, DV_DIFF), lambda b, h, i: (0, 0)),
            pl.BlockSpec((1, tq, LANES), lambda b, h, i: (b + q_sec, i, h)),
            pl.BlockSpec((1, rk, LANES), lambda b, h, i: (b + k_sec, 0, h)),
            pl.BlockSpec((1, 1, nk, LANES, kt), lambda b, h, i: (b + v_sec, h, 0, 0, 0)),
        ],
        out_specs=pl.BlockSpec((1, tq, LANES), lambda b, h, i: (b, i, h)),
        out_shape=jax.ShapeDtypeStruct((batch, nq * tq, SEC), BF16),
        scratch_shapes=[pltpu.VMEM((2, LANES, tq), F32),
                        pltpu.VMEM((DIFF_GROUP * (1 + DIFF_TAIL_GROUPS) + 1, kt, tq), F32),
                        pltpu.VMEM((2, DIFF_GROUP, kt, tq), F32),
                        pltpu.VMEM((2, DIFF_GROUP, kt, tq), F32),
                        pltpu.VMEM((2, DIFF_GROUP, kt, tq), BF16),
                        pltpu.VMEM((2, DIFF_GROUP, kt, tq), BF16),
                        pltpu.SMEM((2,), F32)],
        compiler_params=_cparams(("parallel", "parallel", "arbitrary")),
        name="diff_attention",
    )(slopes, lam_p, subln_g.reshape(1, DV_DIFF), q, k, vt)


def _merge_kernel(h_ref, osb_ref, od_ref, gt_ref, wsb_ref, wd_ref, wo_ref, gn_ref, wr_ref, br_ref,
                  h2_ref, hn_ref, te_ref, tg_ref):
    y_sb = jnp.dot(osb_ref[...], wsb_ref[...], preferred_element_type=F32)
    y_d = jnp.dot(od_ref[...], wd_ref[...], preferred_element_type=F32)
    gt = gt_ref[...]
    mix = gt[:, :D_MODEL] * y_sb + gt[:, D_MODEL:] * y_d
    h2 = h_ref[...] + jnp.dot(mix.astype(BF16), wo_ref[...], preferred_element_type=F32)
    h2_ref[...] = h2
    hn = (h2 * lax.rsqrt(jnp.mean(h2 * h2, axis=-1, keepdims=True) + NORM_EPS)) * gn_ref[...]
    hn_ref[...] = hn
    logits = jnp.dot(hn, wr_ref[...], preferred_element_type=F32,
                     precision=lax.Precision.HIGHEST) + br_ref[...]
    tm = logits.shape[0]
    lane = lax.broadcasted_iota(jnp.int32, (tm, LANES), 1).astype(F32)
    tops, idxs = [], []
    l = logits
    for _ in range(TOP_K):
        m = jnp.max(l, axis=-1, keepdims=True)
        idx = jnp.min(jnp.where(l == m, lane, float(LANES)), axis=-1, keepdims=True)
        tops.append(m)
        idxs.append(idx)
        l = jnp.where(lane == idx, NEG_INF, l)
    ex = [jnp.exp(t - tops[0]) for t in tops]
    den = ex[0] + ex[1] + ex[2] + ex[3]
    te = jnp.zeros((tm, LANES), F32)
    tg = jnp.zeros((tm, LANES), F32)
    for k in range(TOP_K):
        te = jnp.where(lane == float(k), idxs[k], te)
        tg = jnp.where(lane == float(k), ex[k] / den, tg)
    te_ref[...] = te.astype(jnp.int32)
    tg_ref[...] = tg


def _merge(h, o_sb, o_d, gates, w_sb, w_d, w_o, g_ffn, w_r, b_r, tm):
    r, d = h.shape
    row = lambda w: pl.BlockSpec((tm, w), lambda i: (i, 0))
    full = lambda a, b: pl.BlockSpec((a, b), lambda i: (0, 0))
    return pl.pallas_call(
        _merge_kernel,
        grid=(r // tm,),
        in_specs=[row(d), row(d), row(d), row(2 * d), full(d, d), full(d, d), full(d, d),
                  full(1, d), full(d, LANES), full(1, LANES)],
        out_specs=[row(d), row(d), row(LANES), row(LANES)],
        out_shape=[jax.ShapeDtypeStruct((r, d), F32), jax.ShapeDtypeStruct((r, d), F32),
                   jax.ShapeDtypeStruct((r, LANES), jnp.int32), jax.ShapeDtypeStruct((r, LANES), F32)],
        compiler_params=_cparams(("parallel",)),
        name="merge_router",
    )(h, o_sb, o_d, gates, w_sb, w_d, w_o, g_ffn.reshape(1, d), w_r, b_r)


def _moe_kernel(be_ref, nv_ref, x_ref, wgu_ref, bgu_ref, wdn_ref, bdn_ref, o_ref, wgu_bf, wdn_bf):
    b = pl.program_id(0)
    used = nv_ref[b] > 0
    new_expert = jnp.logical_or(b == 0, be_ref[b] != be_ref[jnp.maximum(b - 1, 0)])

    @pl.when(jnp.logical_and(used, new_expert))
    def _():
        wgu_bf[...] = wgu_ref[0].astype(BF16)
        wdn_bf[...] = wdn_ref[0].astype(BF16)

    @pl.when(used)
    def _():
        row = lax.broadcasted_iota(jnp.int32, x_ref.shape, 0)
        x = jnp.where(row < nv_ref[b], x_ref[...], 0.0).astype(BF16)
        gu = jnp.dot(x, wgu_bf[...], preferred_element_type=F32) + bgu_ref[0]
        g = jnp.minimum(gu[:, :D_FF], SWIGLU_LIMIT)
        u = jnp.clip(gu[:, D_FF:], -SWIGLU_LIMIT, SWIGLU_LIMIT)
        act = (u + 1.0) * (g * jax.nn.sigmoid(SWIGLU_ALPHA * g))
        o_ref[...] = jnp.dot(act.astype(BF16), wdn_bf[...], preferred_element_type=F32) + bdn_ref[0]

    @pl.when(jnp.logical_not(used))
    def _():
        o_ref[...] = jnp.zeros_like(o_ref)


def _moe_gmm(block_e, n_valid, xs, w_gu, b_gu, w_dn, b_dn, bm):
    nb = block_e.shape[0]
    n_rows = nb * bm
    d = xs.shape[1]
    grid_spec = pltpu.PrefetchScalarGridSpec(
        num_scalar_prefetch=2,
        grid=(nb,),
        in_specs=[
            pl.BlockSpec((bm, d), lambda b, be, nu: (b, 0)),
            pl.BlockSpec((1, d, 2 * D_FF), lambda b, be, nu: (be[b], 0, 0)),
            pl.BlockSpec((1, 1, 2 * D_FF), lambda b, be, nu: (be[b], 0, 0)),
            pl.BlockSpec((1, D_FF, d), lambda b, be, nu: (be[b], 0, 0)),
            pl.BlockSpec((1, 1, d), lambda b, be, nu: (be[b], 0, 0)),
        ],
        out_specs=pl.BlockSpec((bm, d), lambda b, be, nu: (b, 0)),
        scratch_shapes=[pltpu.VMEM((d, 2 * D_FF), BF16), pltpu.VMEM((D_FF, d), BF16)],
    )
    return pl.pallas_call(
        _moe_kernel,
        grid_spec=grid_spec,
        out_shape=jax.ShapeDtypeStruct((n_rows, d), F32),
        compiler_params=_cparams(("arbitrary",)),
        name="moe_experts",
    )(block_e, n_valid, xs, w_gu, b_gu, w_dn, b_dn)


def _final_kernel(h_ref, c_ref, tg_ref, g_ref, o_ref):
    c = c_ref[...]
    tg = tg_ref[...]
    d = h_ref.shape[1]
    y = c[:, 0:d] * tg[:, 0:1]
    for k in range(1, TOP_K):
        y = y + c[:, k * d:(k + 1) * d] * tg[:, k:k + 1]
    x = h_ref[...] + y
    o_ref[...] = (x * lax.rsqrt(jnp.mean(x * x, axis=-1, keepdims=True) + NORM_EPS)) * g_ref[...]


def _final(h2, contrib, tg, g, tm, c_row0=0):
    r, d = h2.shape
    c_blk0 = c_row0 // tm
    return pl.pallas_call(
        _final_kernel,
        grid=(r // tm,),
        in_specs=[pl.BlockSpec((tm, d), lambda i: (i, 0)),
                  pl.BlockSpec((tm, TOP_K * d), lambda i: (i + c_blk0, 0)),
                  pl.BlockSpec((tm, LANES), lambda i: (i, 0)),
                  pl.BlockSpec((1, d), lambda i: (0, 0))],
        out_specs=pl.BlockSpec((tm, d), lambda i: (i, 0)),
        out_shape=jax.ShapeDtypeStruct((r, d), F32),
        compiler_params=_cparams(("parallel",)),
        name="combine_final_norm",
    )(h2, contrib, tg, g.reshape(1, d))


def _rank_kernel(te_ref, tril_ref, rank_ref, cnt_ref, run_ref):
    @pl.when(pl.program_id(0) == 0)
    def _():
        run_ref[...] = jnp.zeros_like(run_ref)

    te = te_ref[...]
    tm = te.shape[0]
    lane = lax.broadcasted_iota(jnp.int32, (tm, LANES), 1)
    onehots = [(lane == te[:, k:k + 1]).astype(F32) for k in range(TOP_K)]
    member = onehots[0] + onehots[1] + onehots[2] + onehots[3]
    before = jnp.dot(tril_ref[...], member.astype(BF16), preferred_element_type=F32) + run_ref[...]
    rank = jnp.zeros((tm, LANES), F32)
    for k in range(TOP_K):
        r_k = jnp.sum(onehots[k] * before, axis=-1, keepdims=True)
        rank = jnp.where(lane == k, r_k, rank)
    rank_ref[...] = rank.astype(jnp.int32)
    run_ref[...] += jnp.sum(member, axis=0, keepdims=True)
    cnt_ref[...] = run_ref[...]


def _ranks(te, tm):
    r = te.shape[0]
    i = lax.broadcasted_iota(jnp.int32, (tm, tm), 0)
    j = lax.broadcasted_iota(jnp.int32, (tm, tm), 1)
    tril = (j < i).astype(BF16)
    return pl.pallas_call(
        _rank_kernel,
        grid=(r // tm,),
        in_specs=[pl.BlockSpec((tm, LANES), lambda b: (b, 0)), pl.BlockSpec((tm, tm), lambda b: (0, 0))],
        out_specs=[pl.BlockSpec((tm, LANES), lambda b: (b, 0)), pl.BlockSpec((1, LANES), lambda b: (0, 0))],
        out_shape=[jax.ShapeDtypeStruct((r, LANES), jnp.int32), jax.ShapeDtypeStruct((1, LANES), F32)],
        scratch_shapes=[pltpu.VMEM((1, LANES), F32)],
        compiler_params=_cparams(("arbitrary",)),
        name="route_ranks",
    )(te, tril)


def _route(te, n_rows, bm):
    rank, cnt = _ranks(te, Q_TILE_PROMPT)
    counts = cnt[0, :N_EXPERTS].astype(jnp.int32)
    padded = ((counts + bm - 1) // bm) * bm
    pend = jnp.cumsum(padded)
    pstart = pend - padded
    top_e = te[:, :TOP_K]
    onehot = top_e[:, :, None] == jnp.arange(N_EXPERTS, dtype=jnp.int32)[None, None, :]
    pos = jnp.sum(jnp.where(onehot, pstart[None, None, :], 0), axis=-1) + rank[:, :TOP_K]
    nb = n_rows // bm
    block_start = jnp.arange(nb, dtype=jnp.int32) * bm
    block_e = jnp.minimum(jnp.sum((pend[None, :] <= block_start[:, None]).astype(jnp.int32), axis=1),
                          N_EXPERTS - 1)
    e_hot = block_e[:, None] == jnp.arange(N_EXPERTS, dtype=jnp.int32)[None, :]
    last = jnp.sum(jnp.where(e_hot, (pstart + counts)[None, :], 0), axis=1)
    n_valid = jnp.clip(last - block_start, 0, bm).astype(jnp.int32)
    return pos, block_e.astype(jnp.int32), n_valid


def _sc_scatter_rows(src, idx_t, n_out):
    n_src, d = src.shape
    sub = SC_GATHER_WINDOW // SC_SUB_WINDOWS
    n_tiles = n_src // SC_GATHER_WINDOW
    idx_t = idx_t.reshape(1, TOP_K * n_src)
    mesh = plsc.VectorSubcoreMesh(core_axis_name="c", subcore_axis_name="s")

    @pl.kernel(out_type=jax.ShapeDtypeStruct((n_out, d), src.dtype), mesh=mesh)
    def scatter_kernel(x_hbm, i_hbm, o_hbm):
        def body(x_vmem, i_vmem):
            s = pl.program_id(2)
            pltpu.sync_copy(x_vmem, o_hbm.at[i_vmem.at[0, pl.ds(s * sub, sub)]])

        pltpu.emit_pipeline(
            body,
            grid=(n_tiles, TOP_K, SC_SUB_WINDOWS),
            in_specs=[pl.BlockSpec((sub, d), index_map=lambda i, k, s: (i * SC_SUB_WINDOWS + s, 0)),
                      pl.BlockSpec((1, SC_GATHER_WINDOW), index_map=lambda i, k, s: (0, k * n_tiles + i))],
            out_specs=[],
            core_axis_name=("c", "s"),
            dimension_semantics=(pltpu.PARALLEL, pltpu.ARBITRARY, pltpu.ARBITRARY),
        )(x_hbm, i_hbm)

    return scatter_kernel(src, idx_t)


def _sc_gather(table, idx):
    n_idx = idx.shape[0]
    d = table.shape[1]
    sub = SC_GATHER_WINDOW // SC_SUB_WINDOWS
    mesh = plsc.VectorSubcoreMesh(core_axis_name="c", subcore_axis_name="s")

    @pl.kernel(out_type=jax.ShapeDtypeStruct((n_idx, d), table.dtype), mesh=mesh)
    def gather_kernel(x_hbm, i_hbm, o_hbm):
        def body(i_vmem, o_vmem):
            j = pl.program_id(1)
            pltpu.sync_copy(x_hbm.at[i_vmem.at[0, pl.ds(j * sub, sub)]], o_vmem)

        pltpu.emit_pipeline(
            body,
            grid=(n_idx // SC_GATHER_WINDOW, SC_SUB_WINDOWS),
            in_specs=[pl.BlockSpec((1, SC_GATHER_WINDOW), index_map=lambda i, j: (0, i))],
            out_specs=[pl.BlockSpec((sub, d), index_map=lambda i, j: (i * SC_SUB_WINDOWS + j, 0))],
            core_axis_name=("c", "s"),
            dimension_semantics=(pltpu.PARALLEL, pltpu.ARBITRARY),
        )(i_hbm, o_hbm)

    return gather_kernel(table, idx.reshape(1, n_idx))


def _pad_to(a, mult, value):
    pad = (-a.shape[0]) % mult
    return a if pad == 0 else jnp.concatenate([a, jnp.full((pad,), value, a.dtype)])


def _tri(kt):
    s = lax.broadcasted_iota(jnp.int32, (kt, kt), 0)
    j = lax.broadcasted_iota(jnp.int32, (kt, kt), 1)
    return (j > s).astype(BF16)


def _tiles_t(v, kt):
    b, r, _ = v.shape
    return v.reshape(b, r // kt, kt, SEC // LANES, LANES).transpose(0, 3, 1, 4, 2)


def kernel(x_prompt, x_sample, cache_sb_k, cache_sb_v, cache_diff_k, cache_diff_v, meta_tokens,
           norm_mix_g, w_in, diff_lambda, diff_subln_g, w_br_sb, w_br_diff, w_out, norm_ffn_g,
           w_router, b_router, w_gate_up, b_gate_up, w_down, b_down, final_norm_g):
    assert x_prompt.shape[0] == 1 and w_in.shape[0] == 1
    d = D_MODEL
    seq = x_prompt.shape[1]
    t = N_META + seq
    tq_p, kt = Q_TILE_PROMPT, KEY_TILE
    tp = -(-t // tq_p) * tq_p
    nq_p = tp // tq_p
    nb_s, s_len = x_sample.shape[:2]
    past = cache_sb_k.shape[2]
    n_s = nb_s * s_len
    scale = DH_SB ** -0.5

    w0 = w_in[0]
    wq = jnp.concatenate([w0[:, 0:SEC], w0[:, 3 * SEC:4 * SEC]], axis=1).astype(BF16)
    wk = jnp.concatenate([w0[:, SEC:2 * SEC], w0[:, 4 * SEC:5 * SEC]], axis=1).astype(BF16)
    wv = jnp.concatenate([w0[:, 2 * SEC:3 * SEC], w0[:, 5 * SEC:6 * SEC]], axis=1).astype(BF16)
    wg = w0[:, 6 * SEC:8 * SEC].astype(BF16)
    w_sb = w_br_sb[0].astype(BF16)
    w_d = w_br_diff[0].astype(BF16)
    w_o = w_out[0].astype(BF16)
    w_r = jnp.pad(w_router[0], ((0, 0), (0, LANES - N_EXPERTS)))
    b_r = jnp.pad(b_router[0], (0, LANES - N_EXPERTS), constant_values=NEG_INF).reshape(1, LANES)
    w_gu = w_gate_up[0]
    w_dn = w_down[0]
    b_gu = b_gate_up[0].reshape(N_EXPERTS, 1, 2 * D_FF)
    b_dn = b_down[0].reshape(N_EXPERTS, 1, d)
    slope = jnp.exp2(-8.0 * jnp.arange(1, H_DIFF + 1, dtype=F32) / H_DIFF)
    slopes = jnp.stack([slope, 1.0 / slope])
    tri = _tri(kt)
    lam_p = diff_lambda[0]
    g_sub = diff_subln_g[0]

    hp = jnp.concatenate([meta_tokens.astype(F32), x_prompt[0], jnp.zeros((tp - t, d), F32)], axis=0)
    tm_p = tq_p
    xn = _rmsnorm(hp, norm_mix_g[0], tm_p)
    q_sb, q_d = _proj(xn, wq, "q", tm_p, scale=scale)
    kf_sb, kb_sb, kf_d, kb_d = _proj(xn, wk, "k", tm_p, r_out=t)
    vf_sb, vt_sb, vf_d, vt_d = _proj(xn, wv, "v", tm_p, r_out=t)
    (gates,) = _proj(xn, wg, "g", tm_p)
    o_sb = _sb_attention(q_sb[None], kb_sb[None], vt_sb[None], tri, batch=1, nq=nq_p, tq=tq_p, kt=kt,
                         q_pos0=0)
    o_d = _diff_attention(q_d[None], kb_d[None], vt_d[None], slopes, lam_p, g_sub, batch=1, nq=nq_p,
                          tq=tq_p, kt=kt, q_pos0=0, coff=N_META, n_keys=t)
    h2, hn, te, tg = _merge(hp, o_sb[0], o_d[0], gates, w_sb, w_d, w_o, norm_ffn_g[0], w_r, b_r, tq_p)

    tq_s = Q_TILE_DECODE
    hs = x_sample.reshape(n_s, d)
    xn_s = _rmsnorm(hs, norm_mix_g[0], n_s)
    q2_s = _proj(xn_s, wq, "q", n_s, scale=scale)
    kf_s0, _, kf_s1, _ = _proj(xn_s, wk, "k", n_s)
    vf_s0, _, vf_s1, _ = _proj(xn_s, wv, "v", n_s, kt=n_s)
    kf_s = (kf_s0, kf_s1)
    vf_s = (vf_s0, vf_s1)
    (gates_s,) = _proj(xn_s, wg, "g", n_s)
    rk_s = -(-(past + s_len) // kt) * kt

    def with_cache(cache, new):
        c = cache[0].reshape(nb_s, past, SEC).astype(BF16)
        nw = new.reshape(nb_s, s_len, SEC).astype(BF16)
        z = jnp.zeros((nb_s, rk_s - past - s_len, SEC), BF16)
        return jnp.concatenate([c, nw, z], axis=1)

    def pad_q(qs):
        return jnp.pad(qs.reshape(nb_s, s_len, SEC), ((0, 0), (0, tq_s - s_len), (0, 0)))

    k_sb_all = with_cache(cache_sb_k, kf_s[0])
    vt_sb_all = _tiles_t(with_cache(cache_sb_v, vf_s[0]), kt)
    k_d_all = with_cache(cache_diff_k, kf_s[1])
    vt_d_all = _tiles_t(with_cache(cache_diff_v, vf_s[1]), kt)
    o_sb_s = _sb_attention(pad_q(q2_s[0]), k_sb_all, vt_sb_all, tri, batch=nb_s, nq=1, tq=tq_s, kt=kt,
                           q_pos0=past)
    o_d_s = _diff_attention(pad_q(q2_s[1]), k_d_all, vt_d_all, slopes, lam_p, g_sub, batch=nb_s, nq=1,
                            tq=tq_s, kt=kt, q_pos0=past, coff=0, n_keys=past + s_len)
    o_sb_s = o_sb_s[:, :s_len].reshape(n_s, SEC)
    o_d_s = o_d_s[:, :s_len].reshape(n_s, SEC)
    h2_s, hn_s, te_s, tg_s = _merge(hs, o_sb_s, o_d_s, gates_s, w_sb, w_d, w_o, norm_ffn_g[0], w_r, b_r,
                                    n_s)

    n_tok = t + n_s
    r_tok = -(-n_tok // tq_p) * tq_p
    bm = MOE_ROWS
    n_rows = -(-(n_tok * TOP_K + N_EXPERTS * (bm - 1)) // bm) * bm
    te_all = jnp.concatenate([te[:t], te_s, jnp.full((r_tok - n_tok, LANES), -1, jnp.int32)], axis=0)
    hn_all = jnp.concatenate([hn[:t], hn_s, jnp.zeros((r_tok - n_tok, d), F32)], axis=0)
    pos, block_e, n_valid = _route(te_all, n_rows, bm)
    pos_d = jnp.where(te_all[:, :1] >= 0, pos, n_rows)
    xs = _sc_scatter_rows(hn_all, pos_d.T, n_rows + 8)
    yb = _moe_gmm(block_e, n_valid, xs, w_gu, b_gu, w_dn, b_dn, bm)
    pos_all = jnp.concatenate([pos[:t].reshape(-1), jnp.zeros(((tp - t) * TOP_K,), jnp.int32),
                               pos[t:n_tok].reshape(-1)])
    contrib = _sc_gather(yb, _pad_to(pos_all, SC_GATHER_WINDOW * 8, 0))
    contrib = contrib.reshape(-1, TOP_K * d)
    y_p = _final(h2, contrib, tg, final_norm_g, tq_p)
    y_s = _final(h2_s, contrib, tg_s, final_norm_g, n_s, c_row0=tp)

    y_prompt = y_p[N_META:t][None]
    y_sample = y_s.reshape(nb_s, s_len, d)

    def heads(a, nh):
        return a.reshape(1, *a.shape[:-1], nh, a.shape[-1] // nh)

    return (y_prompt, y_sample,
            heads(kf_sb[None], H_SB), heads(vf_sb[None], H_SB),
            heads(kf_d[None], H_DIFF), heads(vf_d[None], H_DIFF),
            heads(kf_s[0].reshape(nb_s, s_len, SEC), H_SB), heads(vf_s[0].reshape(nb_s, s_len, SEC), H_SB),
            heads(kf_s[1].reshape(nb_s, s_len, SEC), H_DIFF), heads(vf_s[1].reshape(nb_s, s_len, SEC), H_DIFF))
```

```python
import functools
import math

import jax
import jax.numpy as jnp
from jax import lax
from jax.experimental import pallas as pl
from jax.experimental.pallas import tpu as pltpu
from jax.experimental.pallas import tpu_sc as plsc

F32 = jnp.float32
BF16 = jnp.bfloat16

D_MODEL = 1024
CHUNK = 64
CHUNK_SHIFT = 6
N_META = 16
H_SB = 16
DH_SB = 64
H_DIFF = 8
DH_DIFF = 64
DV_DIFF = 128
N_EXPERTS = 32
TOP_K = 4
D_FF = 1024
SWIGLU_LIMIT = 7.0
SWIGLU_ALPHA = 1.702
NORM_EPS = 1e-5
LAMBDA_INIT = 0.8 - 0.6 * math.exp(-0.3 * 0)

LANES = 128
SEC = 1024
KEY_TILE = 256
Q_TILE_PROMPT = 256
Q_TILE_DECODE = 128
MOE_ROWS = 256
VMEM_LIMIT = 56 * 1024 * 1024
NEG_INF = float("-inf")
SB_SKIP_LOG = 104.0
SB_ABSENT = 1e30
DIFF_GROUP = 2
DIFF_TAIL_GROUPS = 2
DIFF_SKIP_LOG = 105.0
DIFF_NORM_SLACK = 1.01
SC_GATHER_WINDOW = 128
SC_SUB_WINDOWS = 4


def _cparams(sem):
    return pltpu.CompilerParams(dimension_semantics=sem, vmem_limit_bytes=VMEM_LIMIT)


def _rmsnorm_kernel(x_ref, g_ref, o_ref):
    x = x_ref[...]
    y = x * lax.rsqrt(jnp.mean(x * x, axis=-1, keepdims=True) + NORM_EPS)
    o_ref[...] = (y * g_ref[...]).astype(o_ref.dtype)


def _rmsnorm(x, g, tm):
    r, d = x.shape
    return pl.pallas_call(
        _rmsnorm_kernel,
        grid=(r // tm,),
        in_specs=[pl.BlockSpec((tm, d), lambda i: (i, 0)),
                  pl.BlockSpec((1, d), lambda i: (0, 0))],
        out_specs=pl.BlockSpec((tm, d), lambda i: (i, 0)),
        out_shape=jax.ShapeDtypeStruct((r, d), BF16),
        compiler_params=_cparams(("parallel",)),
        name="rmsnorm",
    )(x, g.reshape(1, d))


def _proj_kernel(x_ref, w_ref, *out_refs, mode, scale, kt):
    x = x_ref[...]
    for s in range(2):
        acc = jnp.dot(x, w_ref[:, s * SEC:(s + 1) * SEC], preferred_element_type=F32)
        if mode == "q":
            out_refs[s][...] = (acc * scale).astype(BF16)
        elif mode == "k":
            out_refs[2 * s][...] = acc
            out_refs[2 * s + 1][...] = acc.astype(BF16)
        elif mode == "v":
            out_refs[2 * s][...] = acc
            for c in range(acc.shape[0] // kt):
                for hp in range(SEC // LANES):
                    tile = acc[c * kt:(c + 1) * kt, hp * LANES:(hp + 1) * LANES]
                    out_refs[2 * s + 1][hp, c] = tile.T.astype(BF16)
        else:
            out_refs[0][:, s * SEC:(s + 1) * SEC] = jax.nn.sigmoid(acc)


def _proj(xn, w2, mode, tm, r_out=None, kt=KEY_TILE, scale=1.0):
    r, d = xn.shape
    r_out = r if r_out is None else r_out
    in_specs = [pl.BlockSpec((tm, d), lambda i: (i, 0)),
                pl.BlockSpec((d, 2 * SEC), lambda i: (0, 0))]
    row_spec = pl.BlockSpec((tm, SEC), lambda i: (i, 0))
    f32_out = jax.ShapeDtypeStruct((r_out, SEC), F32)
    bf16_out = jax.ShapeDtypeStruct((r, SEC), BF16)
    if mode == "q":
        out_specs = [row_spec, row_spec]
        out_shape = [bf16_out, bf16_out]
    elif mode == "k":
        out_specs = [row_spec] * 4
        out_shape = [f32_out, bf16_out, f32_out, bf16_out]
    elif mode == "v":
        t_spec = pl.BlockSpec((SEC // LANES, tm // kt, LANES, kt), lambda i: (0, i, 0, 0))
        t_out = jax.ShapeDtypeStruct((SEC // LANES, r // kt, LANES, kt), BF16)
        out_specs = [row_spec, t_spec, row_spec, t_spec]
        out_shape = [f32_out, t_out, f32_out, t_out]
    else:
        out_specs = [pl.BlockSpec((tm, 2 * SEC), lambda i: (i, 0))]
        out_shape = [jax.ShapeDtypeStruct((r, 2 * SEC), F32)]
    return pl.pallas_call(
        functools.partial(_proj_kernel, mode=mode, scale=scale, kt=kt),
        grid=(r // tm,), in_specs=in_specs, out_specs=out_specs, out_shape=out_shape,
        compiler_params=_cparams(("parallel",)),
        name="proj_" + mode,
    )(xn, w2)


def _stage_cache(kc_ref, vc_ref, kn_ref, vn_ref, k_st, vt_st, *, kt):
    n_cache = kc_ref.shape[1] // kt
    s_len = kn_ref.shape[1]

    def body(j, c):
        off = pl.multiple_of(j * kt, kt)
        k_st[pl.ds(off, kt), :] = kc_ref[0, pl.ds(off, kt), :].astype(BF16)
        vt_st[j] = vc_ref[0, pl.ds(off, kt), :].T.astype(BF16)
        return c

    lax.fori_loop(0, n_cache, body, 0)
    pad = jnp.zeros((kt - s_len, LANES), F32)
    k_st[pl.ds(n_cache * kt, kt), :] = jnp.concatenate([kn_ref[0], pad], axis=0).astype(BF16)
    vt_st[n_cache] = jnp.concatenate([vn_ref[0], pad], axis=0).T.astype(BF16)


def _sb_kernel(q_ref, k_ref, vt_ref, tri_ref, o_ref, acc_ref, **cfg):
    _sb_body(q_ref, lambda off, kt: k_ref[0, pl.ds(off, kt), :], lambda j: vt_ref[0, 0, j],
             tri_ref, o_ref, acc_ref, **cfg)


def _sb_kernel_decode(q_ref, kc_ref, vc_ref, kn_ref, vn_ref, tri_ref, o_ref, acc_ref, k_st, vt_st, **cfg):
    _stage_cache(kc_ref, vc_ref, kn_ref, vn_ref, k_st, vt_st, kt=cfg["kt"])
    _sb_body(q_ref, lambda off, kt: k_st[pl.ds(off, kt), :], lambda j: vt_st[j],
             tri_ref, o_ref, acc_ref, **cfg)


def _sb_body(q_ref, k_rows, vt_tile, tri_ref, o_ref, acc_ref, *, tq, kt, q_pos0):
    qi = pl.program_id(2)
    q0 = q_pos0 + qi * tq
    jm = q0 // kt
    q = q_ref[0]
    lane = lax.broadcasted_iota(jnp.int32, (tq, LANES), 1)
    kpos_l = lax.broadcasted_iota(jnp.int32, (kt, tq), 0)
    qpos_l = lax.broadcasted_iota(jnp.int32, (kt, tq), 1)
    tri = tri_ref[...]
    zero = jnp.zeros_like(q)
    qz = (jnp.where(lane < DH_SB, q, zero), jnp.where(lane >= DH_SB, q, zero))
    acc_ref[...] = jnp.zeros_like(acc_ref)

    def tile(j, hh, carry, masked):
        off = pl.multiple_of(j * kt, kt)
        kk = k_rows(off, kt)
        z = lax.dot_general(kk, qz[hh], (((1,), (1,)), ((), ())),
                            preferred_element_type=F32)
        sp = jnp.maximum(z, 0.0) + jnp.log(1.0 + jnp.exp(-jnp.abs(z)))
        if masked:
            vis = (kpos_l + j * kt) < (qpos_l + q0)
            lk = jnp.where(vis, -sp, 0.0)
        else:
            lk = -sp
        later = jnp.dot(tri, lk.astype(BF16), preferred_element_type=F32)
        w = jnp.exp(z + lk + later + carry)
        if masked:
            w = jnp.where(vis, w, 0.0)
        acc_ref[hh] += jnp.dot(vt_tile(j), w.astype(BF16), preferred_element_type=F32)
        return jnp.sum(lk, axis=0, keepdims=True)

    none_later = jnp.zeros((1, tq), F32)
    carries = tuple(tile(jm, hh, none_later, True) for hh in range(2))

    def cond(state):
        t, ca, cb = state
        return jnp.logical_and(jm - 1 - 2 * t >= 0,
                               jnp.max(jnp.maximum(ca, cb)) > -SB_SKIP_LOG)

    def pair(j0, ca, cb):
        j1 = j0 - 1
        out = []
        for hh, carry in ((0, ca), (1, cb)):
            s0 = tile(jnp.maximum(j0, 0), hh, jnp.where(j0 >= 0, carry, -SB_ABSENT), False)
            s1 = tile(jnp.maximum(j1, 0), hh, jnp.where(j1 >= 0, carry + s0, -SB_ABSENT), False)
            out.append(carry + s0 + s1)
        return out

    def body(state):
        t, ca, cb = state
        ca, cb = pair(jm - 1 - 2 * t, ca, cb)
        return t + 1, ca, cb

    ca, cb = pair(jm - 1, carries[0], carries[1])
    lax.while_loop(cond, body, (jnp.int32(1), ca, cb))

    row = lax.broadcasted_iota(jnp.int32, (LANES, tq), 0)
    o_t = jnp.where(row < DH_SB, acc_ref[0], acc_ref[1])
    o_ref[0] = o_t.T.astype(o_ref.dtype)


def _cache_specs(k_cache, k_new, kt):
    past, s_len = k_cache.shape[1], k_new.shape[1]
    assert past % kt == 0 and s_len <= kt
    nk = past // kt + 1
    specs = [pl.BlockSpec((1, past, LANES), lambda b, h, i: (b, 0, h)),
             pl.BlockSpec((1, past, LANES), lambda b, h, i: (b, 0, h)),
             pl.BlockSpec((1, s_len, LANES), lambda b, h, i: (b, 0, h)),
             pl.BlockSpec((1, s_len, LANES), lambda b, h, i: (b, 0, h))]
    scratch = [pltpu.VMEM((nk * kt, LANES), BF16), pltpu.VMEM((nk, LANES, kt), BF16)]
    return specs, scratch, nk


def _sb_attention(q, k, vt, tri, *, batch, nq, tq, kt, q_pos0):
    rk = k.shape[1]
    nk = rk // kt
    nhp = SEC // LANES
    return pl.pallas_call(
        functools.partial(_sb_kernel, tq=tq, kt=kt, q_pos0=q_pos0),
        grid=(batch, nhp, nq),
        in_specs=[
            pl.BlockSpec((1, tq, LANES), lambda b, h, i: (b, i, h)),
            pl.BlockSpec((1, rk, LANES), lambda b, h, i: (b, 0, h)),
            pl.BlockSpec((1, 1, nk, LANES, kt), lambda b, h, i: (b, h, 0, 0, 0)),
            pl.BlockSpec((kt, kt), lambda b, h, i: (0, 0)),
        ],
        out_specs=pl.BlockSpec((1, tq, LANES), lambda b, h, i: (b, i, h)),
        out_shape=jax.ShapeDtypeStruct((batch, nq * tq, SEC), BF16),
        scratch_shapes=[pltpu.VMEM((2, LANES, tq), F32)],
        compiler_params=_cparams(("parallel", "parallel", "arbitrary")),
        name="sb_attention",
    )(q, k, vt, tri)


def _sb_attention_decode(q, k_cache, v_cache, k_new, v_new, tri, *, tq, kt):
    batch, past = k_cache.shape[:2]
    specs, scratch, _ = _cache_specs(k_cache, k_new, kt)
    return pl.pallas_call(
        functools.partial(_sb_kernel_decode, tq=tq, kt=kt, q_pos0=past),
        grid=(batch, SEC // LANES, 1),
        in_specs=[pl.BlockSpec((1, tq, LANES), lambda b, h, i: (b, i, h))] + specs
        + [pl.BlockSpec((kt, kt), lambda b, h, i: (0, 0))],
        out_specs=pl.BlockSpec((1, tq, LANES), lambda b, h, i: (b, i, h)),
        out_shape=jax.ShapeDtypeStruct((batch, tq, SEC), BF16),
        scratch_shapes=[pltpu.VMEM((2, LANES, tq), F32)] + scratch,
        compiler_params=_cparams(("parallel", "parallel", "arbitrary")),
        name="sb_attention_decode",
    )(q, k_cache, v_cache, k_new, v_new, tri)


def _diff_kernel(slopes_ref, lam_ref, g_ref, q_ref, k_ref, vt_ref, o_ref, *scratch, **cfg):
    _diff_body(slopes_ref, lam_ref, g_ref, q_ref, lambda off, kt: k_ref[0, pl.ds(off, kt), :],
               lambda j: vt_ref[0, 0, j], o_ref, *scratch, **cfg)


def _diff_kernel_decode(slopes_ref, lam_ref, g_ref, q_ref, kc_ref, vc_ref, kn_ref, vn_ref, o_ref,
                        k_st, vt_st, *scratch, **cfg):
    _stage_cache(kc_ref, vc_ref, kn_ref, vn_ref, k_st, vt_st, kt=cfg["kt"])
    _diff_body(slopes_ref, lam_ref, g_ref, q_ref, lambda off, kt: k_st[pl.ds(off, kt), :],
               lambda j: vt_st[j], o_ref, *scratch, **cfg)


def _diff_body(slopes_ref, lam_ref, g_ref, q_ref, k_rows, vt_tile, o_ref,
               acc_ref, bt_ref, s0_ref, s1_ref, p0_ref, p1_ref, kn_ref,
               *, tq, kt, q_pos0, coff, n_keys, nk):
    h = pl.program_id(1)
    qi = pl.program_id(2)
    slope = slopes_ref[0, h]
    inv_slope = slopes_ref[1, h]
    q0 = q_pos0 + qi * tq
    c_lo = (q0 - coff + CHUNK) // CHUNK - 1
    c_hi = (q0 + tq - 1 - coff + CHUNK) // CHUNK - 1
    full_end = jnp.minimum(coff + CHUNK * (c_lo + 1), n_keys)
    vis_end = jnp.minimum(coff + CHUNK * (c_hi + 1), n_keys)
    n_full = full_end // kt
    n_vis = jnp.minimum((vis_end + kt - 1) // kt, nk)

    q = q_ref[0]
    lane = lax.broadcasted_iota(jnp.int32, (tq, LANES), 1)
    kpos_l = lax.broadcasted_iota(jnp.int32, (kt, tq), 0)
    qpos_l = lax.broadcasted_iota(jnp.int32, (kt, tq), 1)
    gsz = DIFF_GROUP
    m_tiles = (n_full // gsz) * gsz
    n_fullg = m_tiles // gsz
    n_groups = n_fullg + DIFF_TAIL_GROUPS
    n_tail = DIFF_TAIL_GROUPS * gsz
    absent = gsz + n_tail

    for u in range(gsz):
        bt_ref[u] = slope * (kpos_l + u * kt).astype(F32)
    for u in range(n_tail):
        kpos = kpos_l + (m_tiles + u) * kt
        qpos = qpos_l + q0
        vis = (((kpos - coff + CHUNK) >> CHUNK_SHIFT) <= ((qpos - coff + CHUNK) >> CHUNK_SHIFT)) \
            & (kpos < n_keys)
        bias = slope * (qpos_l - jnp.abs(qpos - kpos)).astype(F32)
        bt_ref[gsz + u] = jnp.where(vis, bias, NEG_INF)
    bt_ref[absent] = jnp.full((kt, tq), NEG_INF, F32)

    zero = jnp.zeros_like(q)
    qz = (jnp.where(lane < DH_DIFF, q, zero), jnp.where(lane >= DH_DIFF, q, zero))
    acc_ref[...] = jnp.zeros_like(acc_ref)
    p1_ref[...] = jnp.zeros_like(p1_ref)

    hr = lax.broadcasted_iota(jnp.int32, (LANES, LANES), 0)
    hc = lax.broadcasted_iota(jnp.int32, (LANES, LANES), 1)
    half = (((hc == 0) & (hr < DH_DIFF)) | ((hc == 1) & (hr >= DH_DIFF))).astype(BF16)
    lane1 = lax.broadcasted_iota(jnp.int32, (1, LANES), 1)

    def max_half_norms(sq_max):
        nrm = jnp.sqrt(sq_max)
        return (jnp.max(jnp.where(lane1 == 0, nrm, 0.0)), jnp.max(jnp.where(lane1 == 1, nrm, 0.0)))

    @pl.when(qi == 0)
    def _():
        def body(j, mx):
            off = pl.multiple_of(j * kt, kt)
            kk = k_rows(off, kt)
            n2 = jnp.dot(kk * kk, half, preferred_element_type=F32)
            return jnp.maximum(mx, jnp.max(n2, axis=0, keepdims=True))
        kn = max_half_norms(lax.fori_loop(0, nk, body, jnp.zeros((1, LANES), F32)))
        kn_ref[0] = kn[0]
        kn_ref[1] = kn[1]

    qn = max_half_norms(jnp.max(jnp.dot(q * q, half, preferred_element_type=F32), axis=0, keepdims=True))
    qk_bound = [DIFF_NORM_SLACK * qn[mm] * kn_ref[mm] for mm in range(2)]
    q0_f = q0.astype(F32)

    def skip_below(state):
        need = jnp.maximum((DIFF_SKIP_LOG + qk_bound[0]) - state[0], (DIFF_SKIP_LOG + qk_bound[1]) - state[2])
        return q0_f - jnp.max(need) * inv_slope

    def seq_group(n):
        return n_groups - 1 - n

    def needed(g, thr):
        top = ((g + 1) * (gsz * kt)).astype(F32)
        return jnp.logical_and(g >= 0, jnp.logical_or(g >= n_fullg, top > thr))

    def tiles_of(g):
        return [jnp.clip(g * gsz + u, 0, nk - 1) for u in range(gsz)]

    def scores_stage(g, s_ref):
        js = tiles_of(g)
        for mm in range(2):
            for u in range(gsz):
                off = pl.multiple_of(js[u] * kt, kt)
                kk = k_rows(off, kt)
                s_ref[mm, u] =lax.dot_general(kk, qz[mm], (((1,), (1,)), ((), ())),
                                               preferred_element_type=F32)

    def softmax_stage(g, thr, s_ref, p_ref, state):
        ok = needed(g, thr)
        full = jnp.logical_and(ok, g < n_fullg)
        sj = jnp.where(full, slope * (g * gsz * kt - q0).astype(F32), 0.0)
        bidx = [jnp.where(full, u, jnp.where(ok, gsz + g * gsz + u - m_tiles, absent)) for u in range(gsz)]
        new_state, alphas = [], []
        for mm in range(2):
            m, l = state[2 * mm], state[2 * mm + 1]
            ss = [s_ref[mm, u] + bt_ref[bidx[u]] for u in range(gsz)]
            cmax = jnp.max(ss[0], axis=0, keepdims=True)
            for s in ss[1:]:
                cmax = jnp.maximum(cmax, jnp.max(s, axis=0, keepdims=True))
            m_new = jnp.maximum(m, cmax + sj)
            m_safe = jnp.where(m_new == NEG_INF, 0.0, m_new)
            alpha = jnp.exp(m - m_safe)
            r = m_safe - sj
            psum = jnp.zeros((1, tq), F32)
            for u in range(gsz):
                p = jnp.exp(ss[u] - r)
                psum = psum + jnp.sum(p, axis=0, keepdims=True)
                p_ref[mm, u] = p.astype(BF16)
            new_state += [m_new, alpha * l + psum]
            alphas.append(alpha)
        return tuple(new_state), tuple(alphas)

    def value_stage(g, p_ref, alphas):
        js = tiles_of(g)
        for mm in range(2):
            pv = jnp.dot(vt_tile(js[0]), p_ref[mm, 0], preferred_element_type=F32)
            for u in range(1, gsz):
                pv = pv + jnp.dot(vt_tile(js[u]), p_ref[mm, u], preferred_element_type=F32)
            acc_ref[mm] = alphas[mm] * acc_ref[mm] + pv

    def trip(carry):
        t, thr, state, alpha_prev = carry
        n0 = 2 * t
        scores_stage(seq_group(n0 + 1), s1_ref)
        state, alpha0 = softmax_stage(seq_group(n0), thr, s0_ref, p0_ref, state)
        value_stage(seq_group(n0 - 1), p1_ref, alpha_prev)
        scores_stage(seq_group(n0 + 2), s0_ref)
        state, alpha1 = softmax_stage(seq_group(n0 + 1), thr, s1_ref, p1_ref, state)
        value_stage(seq_group(n0), p0_ref, alpha0)
        return t + 1, skip_below(state), state, alpha1

    def more(carry):
        t, thr = carry[0], carry[1]
        return needed(seq_group(2 * t), thr)

    neg = jnp.full((1, tq), NEG_INF, F32)
    zer = jnp.zeros((1, tq), F32)
    one = jnp.ones((1, tq), F32)
    scores_stage(seq_group(0), s0_ref)
    n_done, _, state, alpha_last = lax.while_loop(
        more, trip, (jnp.int32(0), jnp.float32(NEG_INF), (neg, zer, neg, zer), (one, one)))
    value_stage(seq_group(2 * n_done - 1), p1_ref, alpha_last)
    stats = [(state[0], state[1]), (state[2], state[3])]

    lp = lam_ref[...]
    lam = (jnp.exp(jnp.sum(lp[0:1] * lp[1:2], axis=-1, keepdims=True))
           - jnp.exp(jnp.sum(lp[2:3] * lp[3:4], axis=-1, keepdims=True)) + LAMBDA_INIT)
    l1 = stats[0][1]
    l2 = stats[1][1]
    l1 = jnp.where(l1 == 0.0, 1.0, l1)
    l2 = jnp.where(l2 == 0.0, 1.0, l2)
    o_t = acc_ref[0] / l1 - lam * (acc_ref[1] / l2)
    o = o_t.T
    y = o * lax.rsqrt(jnp.mean(o * o, axis=-1, keepdims=True) + NORM_EPS)
    o_ref[0] = ((y * g_ref[...]) * (1.0 - LAMBDA_INIT)).astype(o_ref.dtype)


def _diff_scratch(tq, kt):
    return [pltpu.VMEM((2, LANES, tq), F32),
            pltpu.VMEM((DIFF_GROUP * (1 + DIFF_TAIL_GROUPS) + 1, kt, tq), F32),
            pltpu.VMEM((2, DIFF_GROUP, kt, tq), F32),
            pltpu.VMEM((2, DIFF_GROUP, kt, tq), F32),
            pltpu.VMEM((2, DIFF_GROUP, kt, tq), BF16),
            pltpu.VMEM((2, DIFF_GROUP, kt, tq), BF16),
            pltpu.SMEM((2,), F32)]


def _diff_param_specs():
    return [pl.BlockSpec(memory_space=pltpu.SMEM),
            pl.BlockSpec((4, DH_DIFF), lambda b, h, i: (0, 0)),
            pl.BlockSpec((1, DV_DIFF), lambda b, h, i: (0, 0))]


def _diff_attention(q, k, vt, slopes, lam_p, subln_g, *, batch, nq, tq, kt, q_pos0, coff, n_keys):
    rk = k.shape[1]
    nk = rk // kt
    return pl.pallas_call(
        functools.partial(_diff_kernel, tq=tq, kt=kt, q_pos0=q_pos0, coff=coff, n_keys=n_keys, nk=nk),
        grid=(batch, SEC // LANES, nq),
        in_specs=_diff_param_specs() + [
            pl.BlockSpec((1, tq, LANES), lambda b, h, i: (b, i, h)),
            pl.BlockSpec((1, rk, LANES), lambda b, h, i: (b, 0, h)),
            pl.BlockSpec((1, 1, nk, LANES, kt), lambda b, h, i: (b, h, 0, 0, 0)),
        ],
        out_specs=pl.BlockSpec((1, tq, LANES), lambda b, h, i: (b, i, h)),
        out_shape=jax.ShapeDtypeStruct((batch, nq * tq, SEC), BF16),
        scratch_shapes=_diff_scratch(tq, kt),
        compiler_params=_cparams(("parallel", "parallel", "arbitrary")),
        name="diff_attention",
    )(slopes, lam_p, subln_g.reshape(1, DV_DIFF), q, k, vt)


def _diff_attention_decode(q, k_cache, v_cache, k_new, v_new, slopes, lam_p, subln_g, *, tq, kt):
    batch, past = k_cache.shape[:2]
    specs, scratch, nk = _cache_specs(k_cache, k_new, kt)
    return pl.pallas_call(
        functools.partial(_diff_kernel_decode, tq=tq, kt=kt, q_pos0=past, coff=0,
                          n_keys=past + k_new.shape[1], nk=nk),
        grid=(batch, SEC // LANES, 1),
        in_specs=_diff_param_specs() + [pl.BlockSpec((1, tq, LANES), lambda b, h, i: (b, i, h))] + specs,
        out_specs=pl.BlockSpec((1, tq, LANES), lambda b, h, i: (b, i, h)),
        out_shape=jax.ShapeDtypeStruct((batch, tq, SEC), BF16),
        scratch_shapes=scratch + _diff_scratch(tq, kt),
        compiler_params=_cparams(("parallel", "parallel", "arbitrary")),
        name="diff_attention_decode",
    )(slopes, lam_p, subln_g.reshape(1, DV_DIFF), q, k_cache, v_cache, k_new, v_new)


def _merge_kernel(h_ref, osb_ref, od_ref, gt_ref, wsb_ref, wd_ref, wo_ref, gn_ref, wr_ref, br_ref,
                  h2_ref, hn_ref, te_ref, tg_ref):
    y_sb = jnp.dot(osb_ref[...], wsb_ref[...], preferred_element_type=F32)
    y_d = jnp.dot(od_ref[...], wd_ref[...], preferred_element_type=F32)
    gt = gt_ref[...]
    mix = gt[:, :D_MODEL] * y_sb + gt[:, D_MODEL:] * y_d
    h2 = h_ref[...] + jnp.dot(mix.astype(BF16), wo_ref[...], preferred_element_type=F32)
    h2_ref[...] = h2
    hn = (h2 * lax.rsqrt(jnp.mean(h2 * h2, axis=-1, keepdims=True) + NORM_EPS)) * gn_ref[...]
    hn_ref[...] = hn
    logits = jnp.dot(hn, wr_ref[...], preferred_element_type=F32,
                     precision=lax.Precision.HIGHEST) + br_ref[...]
    tm = logits.shape[0]
    lane = lax.broadcasted_iota(jnp.int32, (tm, LANES), 1).astype(F32)
    tops, idxs = [], []
    l = logits
    for _ in range(TOP_K):
        m = jnp.max(l, axis=-1, keepdims=True)
        idx = jnp.min(jnp.where(l == m, lane, float(LANES)), axis=-1, keepdims=True)
        tops.append(m)
        idxs.append(idx)
        l = jnp.where(lane == idx, NEG_INF, l)
    ex = [jnp.exp(t - tops[0]) for t in tops]
    den = ex[0] + ex[1] + ex[2] + ex[3]
    te = jnp.zeros((tm, LANES), F32)
    tg = jnp.zeros((tm, LANES), F32)
    for k in range(TOP_K):
        te = jnp.where(lane == float(k), idxs[k], te)
        tg = jnp.where(lane == float(k), ex[k] / den, tg)
    te_ref[...] = te.astype(jnp.int32)
    tg_ref[...] = tg


def _merge(h, o_sb, o_d, gates, w_sb, w_d, w_o, g_ffn, w_r, b_r, tm):
    r, d = h.shape
    row = lambda w: pl.BlockSpec((tm, w), lambda i: (i, 0))
    full = lambda a, b: pl.BlockSpec((a, b), lambda i: (0, 0))
    return pl.pallas_call(
        _merge_kernel,
        grid=(r // tm,),
        in_specs=[row(d), row(d), row(d), row(2 * d), full(d, d), full(d, d), full(d, d),
                  full(1, d), full(d, LANES), full(1, LANES)],
        out_specs=[row(d), row(d), row(LANES), row(LANES)],
        out_shape=[jax.ShapeDtypeStruct((r, d), F32), jax.ShapeDtypeStruct((r, d), F32),
                   jax.ShapeDtypeStruct((r, LANES), jnp.int32), jax.ShapeDtypeStruct((r, LANES), F32)],
        compiler_params=_cparams(("parallel",)),
        name="merge_router",
    )(h, o_sb, o_d, gates, w_sb, w_d, w_o, g_ffn.reshape(1, d), w_r, b_r)


def _moe_kernel(be_ref, nv_ref, x_ref, wgu_ref, bgu_ref, wdn_ref, bdn_ref, o_ref, wgu_bf, wdn_bf):
    b = pl.program_id(0)
    used = nv_ref[b] > 0
    new_expert = jnp.logical_or(b == 0, be_ref[b] != be_ref[jnp.maximum(b - 1, 0)])

    @pl.when(jnp.logical_and(used, new_expert))
    def _():
        wgu_bf[...] = wgu_ref[0].astype(BF16)
        wdn_bf[...] = wdn_ref[0].astype(BF16)

    @pl.when(used)
    def _():
        row = lax.broadcasted_iota(jnp.int32, x_ref.shape, 0)
        x = jnp.where(row < nv_ref[b], x_ref[...], 0.0).astype(BF16)
        gu = jnp.dot(x, wgu_bf[...], preferred_element_type=F32) + bgu_ref[0]
        g = jnp.minimum(gu[:, :D_FF], SWIGLU_LIMIT)
        u = jnp.clip(gu[:, D_FF:], -SWIGLU_LIMIT, SWIGLU_LIMIT)
        act = (u + 1.0) * (g * jax.nn.sigmoid(SWIGLU_ALPHA * g))
        o_ref[...] = jnp.dot(act.astype(BF16), wdn_bf[...], preferred_element_type=F32) + bdn_ref[0]

    @pl.when(jnp.logical_not(used))
    def _():
        o_ref[...] = jnp.zeros_like(o_ref)


def _moe_gmm(block_e, n_valid, xs, w_gu, b_gu, w_dn, b_dn, bm):
    nb = block_e.shape[0]
    n_rows = nb * bm
    d = xs.shape[1]
    grid_spec = pltpu.PrefetchScalarGridSpec(
        num_scalar_prefetch=2,
        grid=(nb,),
        in_specs=[
            pl.BlockSpec((bm, d), lambda b, be, nu: (b, 0)),
            pl.BlockSpec((1, d, 2 * D_FF), lambda b, be, nu: (be[b], 0, 0)),
            pl.BlockSpec((1, 1, 2 * D_FF), lambda b, be, nu: (be[b], 0, 0)),
            pl.BlockSpec((1, D_FF, d), lambda b, be, nu: (be[b], 0, 0)),
            pl.BlockSpec((1, 1, d), lambda b, be, nu: (be[b], 0, 0)),
        ],
        out_specs=pl.BlockSpec((bm, d), lambda b, be, nu: (b, 0)),
        scratch_shapes=[pltpu.VMEM((d, 2 * D_FF), BF16), pltpu.VMEM((D_FF, d), BF16)],
    )
    return pl.pallas_call(
        _moe_kernel,
        grid_spec=grid_spec,
        out_shape=jax.ShapeDtypeStruct((n_rows, d), F32),
        compiler_params=_cparams(("arbitrary",)),
        name="moe_experts",
    )(block_e, n_valid, xs, w_gu, b_gu, w_dn, b_dn)


def _final_kernel(h_ref, c_ref, tg_ref, g_ref, o_ref):
    c = c_ref[...]
    tg = tg_ref[...]
    d = h_ref.shape[1]
    y = c[:, 0:d] * tg[:, 0:1]
    for k in range(1, TOP_K):
        y = y + c[:, k * d:(k + 1) * d] * tg[:, k:k + 1]
    x = h_ref[...] + y
    o_ref[...] = (x * lax.rsqrt(jnp.mean(x * x, axis=-1, keepdims=True) + NORM_EPS)) * g_ref[...]


def _final(h2, contrib, tg, g, tm, c_row0=0):
    r, d = h2.shape
    c_blk0 = c_row0 // tm
    return pl.pallas_call(
        _final_kernel,
        grid=(r // tm,),
        in_specs=[pl.BlockSpec((tm, d), lambda i: (i, 0)),
                  pl.BlockSpec((tm, TOP_K * d), lambda i: (i + c_blk0, 0)),
                  pl.BlockSpec((tm, LANES), lambda i: (i, 0)),
                  pl.BlockSpec((1, d), lambda i: (0, 0))],
        out_specs=pl.BlockSpec((tm, d), lambda i: (i, 0)),
        out_shape=jax.ShapeDtypeStruct((r, d), F32),
        compiler_params=_cparams(("parallel",)),
        name="combine_final_norm",
    )(h2, contrib, tg, g.reshape(1, d))


def _rank_kernel(te_ref, tril_ref, rank_ref, cnt_ref, run_ref):
    @pl.when(pl.program_id(0) == 0)
    def _():
        run_ref[...] = jnp.zeros_like(run_ref)

    te = te_ref[...]
    tm = te.shape[0]
    lane = lax.broadcasted_iota(jnp.int32, (tm, LANES), 1)
    onehots = [(lane == te[:, k:k + 1]).astype(F32) for k in range(TOP_K)]
    member = onehots[0] + onehots[1] + onehots[2] + onehots[3]
    before = jnp.dot(tril_ref[...], member.astype(BF16), preferred_element_type=F32) + run_ref[...]
    rank = jnp.zeros((tm, LANES), F32)
    for k in range(TOP_K):
        r_k = jnp.sum(onehots[k] * before, axis=-1, keepdims=True)
        rank = jnp.where(lane == k, r_k, rank)
    rank_ref[...] = rank.astype(jnp.int32)
    run_ref[...] += jnp.sum(member, axis=0, keepdims=True)
    cnt_ref[...] = run_ref[...]


def _ranks(te, tm):
    r = te.shape[0]
    i = lax.broadcasted_iota(jnp.int32, (tm, tm), 0)
    j = lax.broadcasted_iota(jnp.int32, (tm, tm), 1)
    tril = (j < i).astype(BF16)
    return pl.pallas_call(
        _rank_kernel,
        grid=(r // tm,),
        in_specs=[pl.BlockSpec((tm, LANES), lambda b: (b, 0)), pl.BlockSpec((tm, tm), lambda b: (0, 0))],
        out_specs=[pl.BlockSpec((tm, LANES), lambda b: (b, 0)), pl.BlockSpec((1, LANES), lambda b: (0, 0))],
        out_shape=[jax.ShapeDtypeStruct((r, LANES), jnp.int32), jax.ShapeDtypeStruct((1, LANES), F32)],
        scratch_shapes=[pltpu.VMEM((1, LANES), F32)],
        compiler_params=_cparams(("arbitrary",)),
        name="route_ranks",
    )(te, tril)


def _route(te, n_rows, bm):
    rank, cnt = _ranks(te, Q_TILE_PROMPT)
    counts = cnt[0, :N_EXPERTS].astype(jnp.int32)
    padded = ((counts + bm - 1) // bm) * bm
    pend = jnp.cumsum(padded)
    pstart = pend - padded
    top_e = te[:, :TOP_K]
    onehot = top_e[:, :, None] == jnp.arange(N_EXPERTS, dtype=jnp.int32)[None, None, :]
    pos = jnp.sum(jnp.where(onehot, pstart[None, None, :], 0), axis=-1) + rank[:, :TOP_K]
    nb = n_rows // bm
    block_start = jnp.arange(nb, dtype=jnp.int32) * bm
    block_e = jnp.minimum(jnp.sum((pend[None, :] <= block_start[:, None]).astype(jnp.int32), axis=1),
                          N_EXPERTS - 1)
    e_hot = block_e[:, None] == jnp.arange(N_EXPERTS, dtype=jnp.int32)[None, :]
    last = jnp.sum(jnp.where(e_hot, (pstart + counts)[None, :], 0), axis=1)
    n_valid = jnp.clip(last - block_start, 0, bm).astype(jnp.int32)
    return pos, block_e.astype(jnp.int32), n_valid


def _sc_scatter_rows(src, idx_t, n_out):
    n_src, d = src.shape
    sub = SC_GATHER_WINDOW // SC_SUB_WINDOWS
    n_tiles = n_src // SC_GATHER_WINDOW
    idx_t = idx_t.reshape(1, TOP_K * n_src)
    mesh = plsc.VectorSubcoreMesh(core_axis_name="c", subcore_axis_name="s")

    @pl.kernel(out_type=jax.ShapeDtypeStruct((n_out, d), src.dtype), mesh=mesh)
    def scatter_kernel(x_hbm, i_hbm, o_hbm):
        def body(x_vmem, i_vmem):
            s = pl.program_id(2)
            pltpu.sync_copy(x_vmem, o_hbm.at[i_vmem.at[0, pl.ds(s * sub, sub)]])

        pltpu.emit_pipeline(
            body,
            grid=(n_tiles, TOP_K, SC_SUB_WINDOWS),
            in_specs=[pl.BlockSpec((sub, d), index_map=lambda i, k, s: (i * SC_SUB_WINDOWS + s, 0)),
                      pl.BlockSpec((1, SC_GATHER_WINDOW), index_map=lambda i, k, s: (0, k * n_tiles + i))],
            out_specs=[],
            core_axis_name=("c", "s"),
            dimension_semantics=(pltpu.PARALLEL, pltpu.ARBITRARY, pltpu.ARBITRARY),
        )(x_hbm, i_hbm)

    return scatter_kernel(src, idx_t)


def _sc_gather(table, idx):
    n_idx = idx.shape[0]
    d = table.shape[1]
    sub = SC_GATHER_WINDOW // SC_SUB_WINDOWS
    mesh = plsc.VectorSubcoreMesh(core_axis_name="c", subcore_axis_name="s")

    @pl.kernel(out_type=jax.ShapeDtypeStruct((n_idx, d), table.dtype), mesh=mesh)
    def gather_kernel(x_hbm, i_hbm, o_hbm):
        def body(i_vmem, o_vmem):
            j = pl.program_id(1)
            pltpu.sync_copy(x_hbm.at[i_vmem.at[0, pl.ds(j * sub, sub)]], o_vmem)

        pltpu.emit_pipeline(
            body,
            grid=(n_idx // SC_GATHER_WINDOW, SC_SUB_WINDOWS),
            in_specs=[pl.BlockSpec((1, SC_GATHER_WINDOW), index_map=lambda i, j: (0, i))],
            out_specs=[pl.BlockSpec((sub, d), index_map=lambda i, j: (i * SC_SUB_WINDOWS + j, 0))],
            core_axis_name=("c", "s"),
            dimension_semantics=(pltpu.PARALLEL, pltpu.ARBITRARY),
        )(i_hbm, o_hbm)

    return gather_kernel(table, idx.reshape(1, n_idx))


def _pad_to(a, mult, value):
    pad = (-a.shape[0]) % mult
    return a if pad == 0 else jnp.concatenate([a, jnp.full((pad,), value, a.dtype)])


def _tri(kt):
    s = lax.broadcasted_iota(jnp.int32, (kt, kt), 0)
    j = lax.broadcasted_iota(jnp.int32, (kt, kt), 1)
    return (j > s).astype(BF16)


def _tiles_t(v, kt):
    b, r, _ = v.shape
    return v.reshape(b, r // kt, kt, SEC // LANES, LANES).transpose(0, 3, 1, 4, 2)


def kernel(x_prompt, x_sample, cache_sb_k, cache_sb_v, cache_diff_k, cache_diff_v, meta_tokens,
           norm_mix_g, w_in, diff_lambda, diff_subln_g, w_br_sb, w_br_diff, w_out, norm_ffn_g,
           w_router, b_router, w_gate_up, b_gate_up, w_down, b_down, final_norm_g):
    assert x_prompt.shape[0] == 1 and w_in.shape[0] == 1
    d = D_MODEL
    seq = x_prompt.shape[1]
    t = N_META + seq
    tq_p, kt = Q_TILE_PROMPT, KEY_TILE
    tp = -(-t // tq_p) * tq_p
    nq_p = tp // tq_p
    nb_s, s_len = x_sample.shape[:2]
    past = cache_sb_k.shape[2]
    n_s = nb_s * s_len
    scale = DH_SB ** -0.5

    w0 = w_in[0]
    wq = jnp.concatenate([w0[:, 0:SEC], w0[:, 3 * SEC:4 * SEC]], axis=1).astype(BF16)
    wk = jnp.concatenate([w0[:, SEC:2 * SEC], w0[:, 4 * SEC:5 * SEC]], axis=1).astype(BF16)
    wv = jnp.concatenate([w0[:, 2 * SEC:3 * SEC], w0[:, 5 * SEC:6 * SEC]], axis=1).astype(BF16)
    wg = w0[:, 6 * SEC:8 * SEC].astype(BF16)
    w_sb = w_br_sb[0].astype(BF16)
    w_d = w_br_diff[0].astype(BF16)
    w_o = w_out[0].astype(BF16)
    w_r = jnp.pad(w_router[0], ((0, 0), (0, LANES - N_EXPERTS)))
    b_r = jnp.pad(b_router[0], (0, LANES - N_EXPERTS), constant_values=NEG_INF).reshape(1, LANES)
    w_gu = w_gate_up[0]
    w_dn = w_down[0]
    b_gu = b_gate_up[0].reshape(N_EXPERTS, 1, 2 * D_FF)
    b_dn = b_down[0].reshape(N_EXPERTS, 1, d)
    slope = jnp.exp2(-8.0 * jnp.arange(1, H_DIFF + 1, dtype=F32) / H_DIFF)
    slopes = jnp.stack([slope, 1.0 / slope])
    tri = _tri(kt)
    lam_p = diff_lambda[0]
    g_sub = diff_subln_g[0]

    hp = jnp.concatenate([meta_tokens.astype(F32), x_prompt[0], jnp.zeros((tp - t, d), F32)], axis=0)
    tm_p = tq_p
    xn = _rmsnorm(hp, norm_mix_g[0], tm_p)
    q_sb, q_d = _proj(xn, wq, "q", tm_p, scale=scale)
    kf_sb, kb_sb, kf_d, kb_d = _proj(xn, wk, "k", tm_p, r_out=t)
    vf_sb, vt_sb, vf_d, vt_d = _proj(xn, wv, "v", tm_p, r_out=t)
    (gates,) = _proj(xn, wg, "g", tm_p)
    o_sb = _sb_attention(q_sb[None], kb_sb[None], vt_sb[None], tri, batch=1, nq=nq_p, tq=tq_p, kt=kt,
                         q_pos0=0)
    o_d = _diff_attention(q_d[None], kb_d[None], vt_d[None], slopes, lam_p, g_sub, batch=1, nq=nq_p,
                          tq=tq_p, kt=kt, q_pos0=0, coff=N_META, n_keys=t)
    h2, hn, te, tg = _merge(hp, o_sb[0], o_d[0], gates, w_sb, w_d, w_o, norm_ffn_g[0], w_r, b_r, tq_p)

    tq_s = Q_TILE_DECODE
    hs = x_sample.reshape(n_s, d)
    xn_s = _rmsnorm(hs, norm_mix_g[0], n_s)
    q2_s = _proj(xn_s, wq, "q", n_s, scale=scale)
    kf_s0, _, kf_s1, _ = _proj(xn_s, wk, "k", n_s)
    vf_s0, _, vf_s1, _ = _proj(xn_s, wv, "v", n_s, kt=n_s)
    kf_s = (kf_s0, kf_s1)
    vf_s = (vf_s0, vf_s1)
    (gates_s,) = _proj(xn_s, wg, "g", n_s)

    def pad_q(qs):
        return jnp.pad(qs.reshape(nb_s, s_len, SEC), ((0, 0), (0, tq_s - s_len), (0, 0)))

    def per_stream(a):
        return a.reshape(nb_s, -1, SEC)

    o_sb_s = _sb_attention_decode(pad_q(q2_s[0]), per_stream(cache_sb_k[0]), per_stream(cache_sb_v[0]),
                                  per_stream(kf_s[0]), per_stream(vf_s[0]), tri, tq=tq_s, kt=kt)
    o_d_s = _diff_attention_decode(pad_q(q2_s[1]), per_stream(cache_diff_k[0]), per_stream(cache_diff_v[0]),
                                   per_stream(kf_s[1]), per_stream(vf_s[1]), slopes, lam_p, g_sub,
                                   tq=tq_s, kt=kt)
    o_sb_s = o_sb_s[:, :s_len].reshape(n_s, SEC)
    o_d_s = o_d_s[:, :s_len].reshape(n_s, SEC)
    h2_s, hn_s, te_s, tg_s = _merge(hs, o_sb_s, o_d_s, gates_s, w_sb, w_d, w_o, norm_ffn_g[0], w_r, b_r,
                                    n_s)

    n_tok = t + n_s
    r_tok = -(-n_tok // tq_p) * tq_p
    bm = MOE_ROWS
    n_rows = -(-(n_tok * TOP_K + N_EXPERTS * (bm - 1)) // bm) * bm
    te_all = jnp.concatenate([te[:t], te_s, jnp.full((r_tok - n_tok, LANES), -1, jnp.int32)], axis=0)
    hn_all = jnp.concatenate([hn[:t], hn_s, jnp.zeros((r_tok - n_tok, d), F32)], axis=0)
    pos, block_e, n_valid = _route(te_all, n_rows, bm)
    pos_d = jnp.where(te_all[:, :1] >= 0, pos, n_rows)
    xs = _sc_scatter_rows(hn_all, pos_d.T, n_rows + 8)
    yb = _moe_gmm(block_e, n_valid, xs, w_gu, b_gu, w_dn, b_dn, bm)
    pos_all = jnp.concatenate([pos[:t].reshape(-1), jnp.zeros(((tp - t) * TOP_K,), jnp.int32),
                               pos[t:n_tok].reshape(-1)])
    contrib = _sc_gather(yb, _pad_to(pos_all, SC_GATHER_WINDOW * 8, 0))
    contrib = contrib.reshape(-1, TOP_K * d)
    y_p = _final(h2, contrib, tg, final_norm_g, tq_p)
    y_s = _final(h2_s, contrib, tg_s, final_norm_g, n_s, c_row0=tp)

    y_prompt = y_p[N_META:t][None]
    y_sample = y_s.reshape(nb_s, s_len, d)

    def heads(a, nh):
        return a.reshape(1, *a.shape[:-1], nh, a.shape[-1] // nh)

    return (y_prompt, y_sample,
            heads(kf_sb[None], H_SB), heads(vf_sb[None], H_SB),
            heads(kf_d[None], H_DIFF), heads(vf_d[None], H_DIFF),
            heads(kf_s[0].reshape(nb_s, s_len, SEC), H_SB), heads(vf_s[0].reshape(nb_s, s_len, SEC), H_SB),
            heads(kf_s[1].reshape(nb_s, s_len, SEC), H_DIFF), heads(vf_s[1].reshape(nb_s, s_len, SEC), H_DIFF))
```

```python
import functools
import math

import jax
import jax.numpy as jnp
from jax import lax
from jax.experimental import pallas as pl
from jax.experimental.pallas import tpu as pltpu
from jax.experimental.pallas import tpu_sc as plsc

F32 = jnp.float32
BF16 = jnp.bfloat16

D_MODEL = 1024
CHUNK = 64
CHUNK_SHIFT = 6
N_META = 16
H_SB = 16
DH_SB = 64
H_DIFF = 8
DH_DIFF = 64
DV_DIFF = 128
N_EXPERTS = 32
TOP_K = 4
D_FF = 1024
SWIGLU_LIMIT = 7.0
SWIGLU_ALPHA = 1.702
NORM_EPS = 1e-5
LAMBDA_INIT = 0.8 - 0.6 * math.exp(-0.3 * 0)

LANES = 128
SEC = 1024
KEY_TILE = 256
Q_TILE_PROMPT = 256
Q_TILE_DECODE = 128
MOE_ROWS = 256
VMEM_LIMIT = 56 * 1024 * 1024
NEG_INF = float("-inf")
SB_SKIP_LOG = 104.0
SB_ABSENT = 1e30
DIFF_GROUP = 2
DIFF_TAIL_GROUPS = 2
DIFF_SKIP_LOG = 105.0
DIFF_NORM_SLACK = 1.01
SC_GATHER_WINDOW = 128
SC_SUB_WINDOWS = 4


def _cparams(sem):
    return pltpu.CompilerParams(dimension_semantics=sem, vmem_limit_bytes=VMEM_LIMIT)


def _rmsnorm_kernel(x_ref, g_ref, o_ref):
    x = x_ref[...]
    y = x * lax.rsqrt(jnp.mean(x * x, axis=-1, keepdims=True) + NORM_EPS)
    o_ref[...] = (y * g_ref[...]).astype(o_ref.dtype)


def _rmsnorm(x, g, tm):
    r, d = x.shape
    return pl.pallas_call(
        _rmsnorm_kernel,
        grid=(r // tm,),
        in_specs=[pl.BlockSpec((tm, d), lambda i: (i, 0)),
                  pl.BlockSpec((1, d), lambda i: (0, 0))],
        out_specs=pl.BlockSpec((tm, d), lambda i: (i, 0)),
        out_shape=jax.ShapeDtypeStruct((r, d), BF16),
        compiler_params=_cparams(("parallel",)),
        name="rmsnorm",
    )(x, g.reshape(1, d))


def _proj_kernel(x_ref, w_ref, *out_refs, mode, scale, kt):
    x = x_ref[...]
    for s in range(2):
        acc = jnp.dot(x, w_ref[:, s * SEC:(s + 1) * SEC], preferred_element_type=F32)
        if mode == "q":
            out_refs[s][...] = (acc * scale).astype(BF16)
        elif mode == "k":
            out_refs[2 * s][...] = acc
            out_refs[2 * s + 1][...] = acc.astype(BF16)
        elif mode == "v":
            out_refs[2 * s][...] = acc
            for c in range(acc.shape[0] // kt):
                for hp in range(SEC // LANES):
                    tile = acc[c * kt:(c + 1) * kt, hp * LANES:(hp + 1) * LANES]
                    out_refs[2 * s + 1][hp, c] = tile.T.astype(BF16)
        else:
            out_refs[0][:, s * SEC:(s + 1) * SEC] = jax.nn.sigmoid(acc)


def _proj(xn, w2, mode, tm, r_out=None, kt=KEY_TILE, scale=1.0):
    r, d = xn.shape
    r_out = r if r_out is None else r_out
    in_specs = [pl.BlockSpec((tm, d), lambda i: (i, 0)),
                pl.BlockSpec((d, 2 * SEC), lambda i: (0, 0))]
    row_spec = pl.BlockSpec((tm, SEC), lambda i: (i, 0))
    f32_out = jax.ShapeDtypeStruct((r_out, SEC), F32)
    bf16_out = jax.ShapeDtypeStruct((r, SEC), BF16)
    if mode == "q":
        out_specs = [row_spec, row_spec]
        out_shape = [bf16_out, bf16_out]
    elif mode == "k":
        out_specs = [row_spec] * 4
        out_shape = [f32_out, bf16_out, f32_out, bf16_out]
    elif mode == "v":
        t_spec = pl.BlockSpec((SEC // LANES, tm // kt, LANES, kt), lambda i: (0, i, 0, 0))
        t_out = jax.ShapeDtypeStruct((SEC // LANES, r // kt, LANES, kt), BF16)
        out_specs = [row_spec, t_spec, row_spec, t_spec]
        out_shape = [f32_out, t_out, f32_out, t_out]
    else:
        out_specs = [pl.BlockSpec((tm, 2 * SEC), lambda i: (i, 0))]
        out_shape = [jax.ShapeDtypeStruct((r, 2 * SEC), F32)]
    return pl.pallas_call(
        functools.partial(_proj_kernel, mode=mode, scale=scale, kt=kt),
        grid=(r // tm,), in_specs=in_specs, out_specs=out_specs, out_shape=out_shape,
        compiler_params=_cparams(("parallel",)),
        name="proj_" + mode,
    )(xn, w2)


def _cache_tiles_kernel(kc_ref, vc_ref, kn_ref, vn_ref, k_out, vt_out, kbuf, vbuf, *, kt):
    j = pl.program_id(1)
    n_cache = pl.num_programs(1) - 1
    nh, dh = kc_ref.shape[2], kc_ref.shape[3]

    @pl.when(j < n_cache)
    def _():
        for h in range(nh):
            kbuf[:, h * dh:(h + 1) * dh] = kc_ref[0, :, h, :]
            vbuf[:, h * dh:(h + 1) * dh] = vc_ref[0, :, h, :]

    @pl.when(j == n_cache)
    def _():
        s_len = kn_ref.shape[1]
        pad = jnp.zeros((kt - s_len, SEC), F32)
        kbuf[...] = jnp.concatenate([kn_ref[0], pad], axis=0)
        vbuf[...] = jnp.concatenate([vn_ref[0], pad], axis=0)

    k_out[0] = kbuf[...].astype(BF16)
    for hp in range(SEC // LANES):
        vt_out[0, hp, 0] = vbuf[:, hp * LANES:(hp + 1) * LANES].T.astype(BF16)


def _cache_tiles(k_cache, v_cache, k_new, v_new, kt):
    b, past, nh, dh = k_cache.shape
    s_len = k_new.shape[1]
    assert past % kt == 0 and s_len <= kt and nh * dh == SEC
    n_cache = past // kt
    cache_spec = pl.BlockSpec((1, kt, nh, dh), lambda i, j: (i, jnp.minimum(j, n_cache - 1), 0, 0))
    new_spec = pl.BlockSpec((1, s_len, SEC), lambda i, j: (i, 0, 0))
    return pl.pallas_call(
        functools.partial(_cache_tiles_kernel, kt=kt),
        grid=(b, n_cache + 1),
        in_specs=[cache_spec, cache_spec, new_spec, new_spec],
        out_specs=[pl.BlockSpec((1, kt, SEC), lambda i, j: (i, j, 0)),
                   pl.BlockSpec((1, SEC // LANES, 1, LANES, kt), lambda i, j: (i, 0, j, 0, 0))],
        out_shape=[jax.ShapeDtypeStruct((b, past + kt, SEC), BF16),
                   jax.ShapeDtypeStruct((b, SEC // LANES, n_cache + 1, LANES, kt), BF16)],
        scratch_shapes=[pltpu.VMEM((kt, SEC), F32), pltpu.VMEM((kt, SEC), F32)],
        compiler_params=_cparams(("parallel", "arbitrary")),
        name="cache_tiles",
    )(k_cache, v_cache, k_new, v_new)


def _sb_kernel(q_ref, k_ref, vt_ref, tri_ref, o_ref, acc_ref, *, tq, kt, q_pos0):
    qi = pl.program_id(2)
    q0 = q_pos0 + qi * tq
    jm = q0 // kt
    q = q_ref[0]
    lane = lax.broadcasted_iota(jnp.int32, (tq, LANES), 1)
    kpos_l = lax.broadcasted_iota(jnp.int32, (kt, tq), 0)
    qpos_l = lax.broadcasted_iota(jnp.int32, (kt, tq), 1)
    tri = tri_ref[...]
    zero = jnp.zeros_like(q)
    qz = (jnp.where(lane < DH_SB, q, zero), jnp.where(lane >= DH_SB, q, zero))
    acc_ref[...] = jnp.zeros_like(acc_ref)

    def tile(j, hh, carry, masked):
        off = pl.multiple_of(j * kt, kt)
        kk = k_ref[0, pl.ds(off, kt), :]
        z = lax.dot_general(kk, qz[hh], (((1,), (1,)), ((), ())),
                            preferred_element_type=F32)
        sp = jnp.maximum(z, 0.0) + jnp.log(1.0 + jnp.exp(-jnp.abs(z)))
        if masked:
            vis = (kpos_l + j * kt) < (qpos_l + q0)
            lk = jnp.where(vis, -sp, 0.0)
        else:
            lk = -sp
        later = jnp.dot(tri, lk.astype(BF16), preferred_element_type=F32)
        w = jnp.exp(z + lk + later + carry)
        if masked:
            w = jnp.where(vis, w, 0.0)
        acc_ref[hh] += jnp.dot(vt_ref[0, 0, j], w.astype(BF16), preferred_element_type=F32)
        return jnp.sum(lk, axis=0, keepdims=True)

    none_later = jnp.zeros((1, tq), F32)
    carries = tuple(tile(jm, hh, none_later, True) for hh in range(2))

    def pair(j0, ca, cb):
        j1 = j0 - 1
        out = []
        for hh, carry in ((0, ca), (1, cb)):
            s0 = tile(jnp.maximum(j0, 0), hh, jnp.where(j0 >= 0, carry, -SB_ABSENT), False)
            s1 = tile(jnp.maximum(j1, 0), hh, jnp.where(j1 >= 0, carry + s0, -SB_ABSENT), False)
            out.append(carry + s0 + s1)
        return out

    jb = jm - 1
    first = []
    for hh in range(2):
        s = tile(jnp.maximum(jb, 0), hh, jnp.where(jb >= 0, carries[hh], -SB_ABSENT), False)
        first.append(carries[hh] + s)

    def cond(state):
        t, ca, cb = state
        return jnp.logical_and(jm - 2 - 2 * t >= 0,
                               jnp.max(jnp.maximum(ca, cb)) > -SB_SKIP_LOG)

    def body(state):
        t, ca, cb = state
        ca, cb = pair(jm - 2 - 2 * t, ca, cb)
        return t + 1, ca, cb

    lax.while_loop(cond, body, (jnp.int32(0), first[0], first[1]))

    row = lax.broadcasted_iota(jnp.int32, (LANES, tq), 0)
    o_t = jnp.where(row < DH_SB, acc_ref[0], acc_ref[1])
    o_ref[0] = o_t.T.astype(o_ref.dtype)


def _sb_attention(q, k, vt, tri, *, batch, nq, tq, kt, q_pos0):
    rk = k.shape[1]
    nk = rk // kt
    nhp = SEC // LANES
    return pl.pallas_call(
        functools.partial(_sb_kernel, tq=tq, kt=kt, q_pos0=q_pos0),
        grid=(batch, nhp, nq),
        in_specs=[
            pl.BlockSpec((1, tq, LANES), lambda b, h, i: (b, i, h)),
            pl.BlockSpec((1, rk, LANES), lambda b, h, i: (b, 0, h)),
            pl.BlockSpec((1, 1, nk, LANES, kt), lambda b, h, i: (b, h, 0, 0, 0)),
            pl.BlockSpec((kt, kt), lambda b, h, i: (0, 0)),
        ],
        out_specs=pl.BlockSpec((1, tq, LANES), lambda b, h, i: (b, i, h)),
        out_shape=jax.ShapeDtypeStruct((batch, nq * tq, SEC), BF16),
        scratch_shapes=[pltpu.VMEM((2, LANES, tq), F32)],
        compiler_params=_cparams(("parallel", "parallel", "arbitrary")),
        name="sb_attention",
    )(q, k, vt, tri)


def _diff_kernel(slopes_ref, lam_ref, g_ref, q_ref, k_ref, vt_ref, o_ref,
                 acc_ref, bt_ref, s0_ref, s1_ref, p0_ref, p1_ref, kn_ref,
               *, tq, kt, q_pos0, coff, n_keys, nk):
    h = pl.program_id(1)
    qi = pl.program_id(2)
    slope = slopes_ref[0, h]
    inv_slope = slopes_ref[1, h]
    q0 = q_pos0 + qi * tq
    c_lo = (q0 - coff + CHUNK) // CHUNK - 1
    c_hi = (q0 + tq - 1 - coff + CHUNK) // CHUNK - 1
    full_end = jnp.minimum(coff + CHUNK * (c_lo + 1), n_keys)
    vis_end = jnp.minimum(coff + CHUNK * (c_hi + 1), n_keys)
    n_full = full_end // kt
    n_vis = jnp.minimum((vis_end + kt - 1) // kt, nk)

    q = q_ref[0]
    lane = lax.broadcasted_iota(jnp.int32, (tq, LANES), 1)
    kpos_l = lax.broadcasted_iota(jnp.int32, (kt, tq), 0)
    qpos_l = lax.broadcasted_iota(jnp.int32, (kt, tq), 1)
    gsz = DIFF_GROUP
    m_tiles = (n_full // gsz) * gsz
    n_fullg = m_tiles // gsz
    n_groups = n_fullg + DIFF_TAIL_GROUPS
    n_tail = DIFF_TAIL_GROUPS * gsz
    absent = gsz + n_tail

    for u in range(gsz):
        bt_ref[u] = slope * (kpos_l + u * kt).astype(F32)
    for u in range(n_tail):
        kpos = kpos_l + (m_tiles + u) * kt
        qpos = qpos_l + q0
        vis = (((kpos - coff + CHUNK) >> CHUNK_SHIFT) <= ((qpos - coff + CHUNK) >> CHUNK_SHIFT)) \
            & (kpos < n_keys)
        bias = slope * (qpos_l - jnp.abs(qpos - kpos)).astype(F32)
        bt_ref[gsz + u] = jnp.where(vis, bias, NEG_INF)
    bt_ref[absent] = jnp.full((kt, tq), NEG_INF, F32)

    zero = jnp.zeros_like(q)
    qz = (jnp.where(lane < DH_DIFF, q, zero), jnp.where(lane >= DH_DIFF, q, zero))
    acc_ref[...] = jnp.zeros_like(acc_ref)
    p1_ref[...] = jnp.zeros_like(p1_ref)

    hr = lax.broadcasted_iota(jnp.int32, (LANES, LANES), 0)
    hc = lax.broadcasted_iota(jnp.int32, (LANES, LANES), 1)
    half = (((hc == 0) & (hr < DH_DIFF)) | ((hc == 1) & (hr >= DH_DIFF))).astype(BF16)
    lane1 = lax.broadcasted_iota(jnp.int32, (1, LANES), 1)

    def max_half_norms(sq_max):
        nrm = jnp.sqrt(sq_max)
        return (jnp.max(jnp.where(lane1 == 0, nrm, 0.0)), jnp.max(jnp.where(lane1 == 1, nrm, 0.0)))

    @pl.when(qi == 0)
    def _():
        def body(j, mx):
            off = pl.multiple_of(j * kt, kt)
            kk = k_ref[0, pl.ds(off, kt), :]
            n2 = jnp.dot(kk * kk, half, preferred_element_type=F32)
            return jnp.maximum(mx, jnp.max(n2, axis=0, keepdims=True))
        kn = max_half_norms(lax.fori_loop(0, nk, body, jnp.zeros((1, LANES), F32)))
        kn_ref[0] = kn[0]
        kn_ref[1] = kn[1]

    qn = max_half_norms(jnp.max(jnp.dot(q * q, half, preferred_element_type=F32), axis=0, keepdims=True))
    qk_bound = [DIFF_NORM_SLACK * qn[mm] * kn_ref[mm] for mm in range(2)]
    q0_f = q0.astype(F32)

    def skip_below(state):
        need = jnp.maximum((DIFF_SKIP_LOG + qk_bound[0]) - state[0], (DIFF_SKIP_LOG + qk_bound[1]) - state[2])
        return q0_f - jnp.max(need) * inv_slope

    def seq_group(n):
        return n_groups - 1 - n

    def needed(g, thr):
        top = ((g + 1) * (gsz * kt)).astype(F32)
        return jnp.logical_and(g >= 0, jnp.logical_or(g >= n_fullg, top > thr))

    def tiles_of(g):
        return [jnp.clip(g * gsz + u, 0, nk - 1) for u in range(gsz)]

    def scores_stage(g, s_ref):
        js = tiles_of(g)
        for mm in range(2):
            for u in range(gsz):
                off = pl.multiple_of(js[u] * kt, kt)
                kk = k_ref[0, pl.ds(off, kt), :]
                s_ref[mm, u] =lax.dot_general(kk, qz[mm], (((1,), (1,)), ((), ())),
                                               preferred_element_type=F32)

    def softmax_stage(g, thr, s_ref, p_ref, state):
        ok = needed(g, thr)
        full = jnp.logical_and(ok, g < n_fullg)
        sj = jnp.where(full, slope * (g * gsz * kt - q0).astype(F32), 0.0)
        bidx = [jnp.where(full, u, jnp.where(ok, gsz + g * gsz + u - m_tiles, absent)) for u in range(gsz)]
        new_state, alphas = [], []
        for mm in range(2):
            m, l = state[2 * mm], state[2 * mm + 1]
            ss = [s_ref[mm, u] + bt_ref[bidx[u]] for u in range(gsz)]
            cmax = jnp.max(ss[0], axis=0, keepdims=True)
            for s in ss[1:]:
                cmax = jnp.maximum(cmax, jnp.max(s, axis=0, keepdims=True))
            m_new = jnp.maximum(m, cmax + sj)
            m_safe = jnp.where(m_new == NEG_INF, 0.0, m_new)
            alpha = jnp.exp(m - m_safe)
            r = m_safe - sj
            psum = jnp.zeros((1, tq), F32)
            for u in range(gsz):
                p = jnp.exp(ss[u] - r)
                psum = psum + jnp.sum(p, axis=0, keepdims=True)
                p_ref[mm, u] = p.astype(BF16)
            new_state += [m_new, alpha * l + psum]
            alphas.append(alpha)
        return tuple(new_state), tuple(alphas)

    def value_stage(g, p_ref, alphas):
        js = tiles_of(g)
        for mm in range(2):
            pv = jnp.dot(vt_ref[0, 0, js[0]], p_ref[mm, 0], preferred_element_type=F32)
            for u in range(1, gsz):
                pv = pv + jnp.dot(vt_ref[0, 0, js[u]], p_ref[mm, u], preferred_element_type=F32)
            acc_ref[mm] = alphas[mm] * acc_ref[mm] + pv

    def trip(carry):
        t, thr, state, alpha_prev = carry
        n0 = 2 * t
        scores_stage(seq_group(n0 + 1), s1_ref)
        state, alpha0 = softmax_stage(seq_group(n0), thr, s0_ref, p0_ref, state)
        value_stage(seq_group(n0 - 1), p1_ref, alpha_prev)
        scores_stage(seq_group(n0 + 2), s0_ref)
        state, alpha1 = softmax_stage(seq_group(n0 + 1), thr, s1_ref, p1_ref, state)
        value_stage(seq_group(n0), p0_ref, alpha0)
        return t + 1, skip_below(state), state, alpha1

    def more(carry):
        t, thr = carry[0], carry[1]
        return needed(seq_group(2 * t), thr)

    neg = jnp.full((1, tq), NEG_INF, F32)
    zer = jnp.zeros((1, tq), F32)
    one = jnp.ones((1, tq), F32)
    scores_stage(seq_group(0), s0_ref)
    n_done, _, state, alpha_last = lax.while_loop(
        more, trip, (jnp.int32(0), jnp.float32(NEG_INF), (neg, zer, neg, zer), (one, one)))
    value_stage(seq_group(2 * n_done - 1), p1_ref, alpha_last)
    stats = [(state[0], state[1]), (state[2], state[3])]

    lp = lam_ref[...]
    lam = (jnp.exp(jnp.sum(lp[0:1] * lp[1:2], axis=-1, keepdims=True))
           - jnp.exp(jnp.sum(lp[2:3] * lp[3:4], axis=-1, keepdims=True)) + LAMBDA_INIT)
    l1 = stats[0][1]
    l2 = stats[1][1]
    l1 = jnp.where(l1 == 0.0, 1.0, l1)
    l2 = jnp.where(l2 == 0.0, 1.0, l2)
    o_t = acc_ref[0] / l1 - lam * (acc_ref[1] / l2)
    o = o_t.T
    y = o * lax.rsqrt(jnp.mean(o * o, axis=-1, keepdims=True) + NORM_EPS)
    o_ref[0] = ((y * g_ref[...]) * (1.0 - LAMBDA_INIT)).astype(o_ref.dtype)


def _diff_scratch(tq, kt):
    return [pltpu.VMEM((2, LANES, tq), F32),
            pltpu.VMEM((DIFF_GROUP * (1 + DIFF_TAIL_GROUPS) + 1, kt, tq), F32),
            pltpu.VMEM((2, DIFF_GROUP, kt, tq), F32),
            pltpu.VMEM((2, DIFF_GROUP, kt, tq), F32),
            pltpu.VMEM((2, DIFF_GROUP, kt, tq), BF16),
            pltpu.VMEM((2, DIFF_GROUP, kt, tq), BF16),
            pltpu.SMEM((2,), F32)]


def _diff_param_specs():
    return [pl.BlockSpec(memory_space=pltpu.SMEM),
            pl.BlockSpec((4, DH_DIFF), lambda b, h, i: (0, 0)),
            pl.BlockSpec((1, DV_DIFF), lambda b, h, i: (0, 0))]


def _diff_attention(q, k, vt, slopes, lam_p, subln_g, *, batch, nq, tq, kt, q_pos0, coff, n_keys):
    rk = k.shape[1]
    nk = rk // kt
    return pl.pallas_call(
        functools.partial(_diff_kernel, tq=tq, kt=kt, q_pos0=q_pos0, coff=coff, n_keys=n_keys, nk=nk),
        grid=(batch, SEC // LANES, nq),
        in_specs=_diff_param_specs() + [
            pl.BlockSpec((1, tq, LANES), lambda b, h, i: (b, i, h)),
            pl.BlockSpec((1, rk, LANES), lambda b, h, i: (b, 0, h)),
            pl.BlockSpec((1, 1, nk, LANES, kt), lambda b, h, i: (b, h, 0, 0, 0)),
        ],
        out_specs=pl.BlockSpec((1, tq, LANES), lambda b, h, i: (b, i, h)),
        out_shape=jax.ShapeDtypeStruct((batch, nq * tq, SEC), BF16),
        scratch_shapes=_diff_scratch(tq, kt),
        compiler_params=_cparams(("parallel", "parallel", "arbitrary")),
        name="diff_attention",
    )(slopes, lam_p, subln_g.reshape(1, DV_DIFF), q, k, vt)


def _merge_kernel(h_ref, osb_ref, od_ref, gt_ref, wsb_ref, wd_ref, wo_ref, gn_ref, wr_ref, br_ref,
                  h2_ref, hn_ref, te_ref, tg_ref):
    y_sb = jnp.dot(osb_ref[...], wsb_ref[...], preferred_element_type=F32)
    y_d = jnp.dot(od_ref[...], wd_ref[...], preferred_element_type=F32)
    gt = gt_ref[...]
    mix = gt[:, :D_MODEL] * y_sb + gt[:, D_MODEL:] * y_d
    h2 = h_ref[...] + jnp.dot(mix.astype(BF16), wo_ref[...], preferred_element_type=F32)
    h2_ref[...] = h2
    hn = (h2 * lax.rsqrt(jnp.mean(h2 * h2, axis=-1, keepdims=True) + NORM_EPS)) * gn_ref[...]
    hn_ref[...] = hn
    logits = jnp.dot(hn, wr_ref[...], preferred_element_type=F32,
                     precision=lax.Precision.HIGHEST) + br_ref[...]
    tm = logits.shape[0]
    lane = lax.broadcasted_iota(jnp.int32, (tm, LANES), 1).astype(F32)
    tops, idxs = [], []
    l = logits
    for _ in range(TOP_K):
        m = jnp.max(l, axis=-1, keepdims=True)
        idx = jnp.min(jnp.where(l == m, lane, float(LANES)), axis=-1, keepdims=True)
        tops.append(m)
        idxs.append(idx)
        l = jnp.where(lane == idx, NEG_INF, l)
    ex = [jnp.exp(t - tops[0]) for t in tops]
    den = ex[0] + ex[1] + ex[2] + ex[3]
    te = jnp.zeros((tm, LANES), F32)
    tg = jnp.zeros((tm, LANES), F32)
    for k in range(TOP_K):
        te = jnp.where(lane == float(k), idxs[k], te)
        tg = jnp.where(lane == float(k), ex[k] / den, tg)
    te_ref[...] = te.astype(jnp.int32)
    tg_ref[...] = tg


def _merge(h, o_sb, o_d, gates, w_sb, w_d, w_o, g_ffn, w_r, b_r, tm):
    r, d = h.shape
    row = lambda w: pl.BlockSpec((tm, w), lambda i: (i, 0))
    full = lambda a, b: pl.BlockSpec((a, b), lambda i: (0, 0))
    return pl.pallas_call(
        _merge_kernel,
        grid=(r // tm,),
        in_specs=[row(d), row(d), row(d), row(2 * d), full(d, d), full(d, d), full(d, d),
                  full(1, d), full(d, LANES), full(1, LANES)],
        out_specs=[row(d), row(d), row(LANES), row(LANES)],
        out_shape=[jax.ShapeDtypeStruct((r, d), F32), jax.ShapeDtypeStruct((r, d), F32),
                   jax.ShapeDtypeStruct((r, LANES), jnp.int32), jax.ShapeDtypeStruct((r, LANES), F32)],
        compiler_params=_cparams(("parallel",)),
        name="merge_router",
    )(h, o_sb, o_d, gates, w_sb, w_d, w_o, g_ffn.reshape(1, d), w_r, b_r)


def _moe_kernel(be_ref, nv_ref, x_ref, wgu_ref, bgu_ref, wdn_ref, bdn_ref, o_ref, wgu_bf, wdn_bf):
    b = pl.program_id(0)
    used = nv_ref[b] > 0
    new_expert = jnp.logical_or(b == 0, be_ref[b] != be_ref[jnp.maximum(b - 1, 0)])

    @pl.when(jnp.logical_and(used, new_expert))
    def _():
        wgu_bf[...] = wgu_ref[0].astype(BF16)
        wdn_bf[...] = wdn_ref[0].astype(BF16)

    @pl.when(used)
    def _():
        row = lax.broadcasted_iota(jnp.int32, x_ref.shape, 0)
        x = jnp.where(row < nv_ref[b], x_ref[...], 0.0).astype(BF16)
        gu = jnp.dot(x, wgu_bf[...], preferred_element_type=F32) + bgu_ref[0]
        g = jnp.minimum(gu[:, :D_FF], SWIGLU_LIMIT)
        u = jnp.clip(gu[:, D_FF:], -SWIGLU_LIMIT, SWIGLU_LIMIT)
        act = (u + 1.0) * (g * jax.nn.sigmoid(SWIGLU_ALPHA * g))
        o_ref[...] = jnp.dot(act.astype(BF16), wdn_bf[...], preferred_element_type=F32) + bdn_ref[0]

    @pl.when(jnp.logical_not(used))
    def _():
        o_ref[...] = jnp.zeros_like(o_ref)


def _moe_gmm(block_e, n_valid, xs, w_gu, b_gu, w_dn, b_dn, bm):
    nb = block_e.shape[0]
    n_rows = nb * bm
    d = xs.shape[1]
    grid_spec = pltpu.PrefetchScalarGridSpec(
        num_scalar_prefetch=2,
        grid=(nb,),
        in_specs=[
            pl.BlockSpec((bm, d), lambda b, be, nu: (b, 0)),
            pl.BlockSpec((1, d, 2 * D_FF), lambda b, be, nu: (be[b], 0, 0)),
            pl.BlockSpec((1, 1, 2 * D_FF), lambda b, be, nu: (be[b], 0, 0)),
            pl.BlockSpec((1, D_FF, d), lambda b, be, nu: (be[b], 0, 0)),
            pl.BlockSpec((1, 1, d), lambda b, be, nu: (be[b], 0, 0)),
        ],
        out_specs=pl.BlockSpec((bm, d), lambda b, be, nu: (b, 0)),
        scratch_shapes=[pltpu.VMEM((d, 2 * D_FF), BF16), pltpu.VMEM((D_FF, d), BF16)],
    )
    return pl.pallas_call(
        _moe_kernel,
        grid_spec=grid_spec,
        out_shape=jax.ShapeDtypeStruct((n_rows, d), F32),
        compiler_params=_cparams(("arbitrary",)),
        name="moe_experts",
    )(block_e, n_valid, xs, w_gu, b_gu, w_dn, b_dn)


def _final_kernel(h_ref, c_ref, tg_ref, g_ref, o_ref):
    c = c_ref[...]
    tg = tg_ref[...]
    d = h_ref.shape[1]
    y = c[:, 0:d] * tg[:, 0:1]
    for k in range(1, TOP_K):
        y = y + c[:, k * d:(k + 1) * d] * tg[:, k:k + 1]
    x = h_ref[...] + y
    o_ref[...] = (x * lax.rsqrt(jnp.mean(x * x, axis=-1, keepdims=True) + NORM_EPS)) * g_ref[...]


def _final(h2, contrib, tg, g, tm, c_row0=0):
    r, d = h2.shape
    c_blk0 = c_row0 // tm
    return pl.pallas_call(
        _final_kernel,
        grid=(r // tm,),
        in_specs=[pl.BlockSpec((tm, d), lambda i: (i, 0)),
                  pl.BlockSpec((tm, TOP_K * d), lambda i: (i + c_blk0, 0)),
                  pl.BlockSpec((tm, LANES), lambda i: (i, 0)),
                  pl.BlockSpec((1, d), lambda i: (0, 0))],
        out_specs=pl.BlockSpec((tm, d), lambda i: (i, 0)),
        out_shape=jax.ShapeDtypeStruct((r, d), F32),
        compiler_params=_cparams(("parallel",)),
        name="combine_final_norm",
    )(h2, contrib, tg, g.reshape(1, d))


def _rank_kernel(te_ref, tril_ref, rank_ref, cnt_ref, run_ref):
    @pl.when(pl.program_id(0) == 0)
    def _():
        run_ref[...] = jnp.zeros_like(run_ref)

    te = te_ref[...]
    tm = te.shape[0]
    lane = lax.broadcasted_iota(jnp.int32, (tm, LANES), 1)
    onehots = [(lane == te[:, k:k + 1]).astype(F32) for k in range(TOP_K)]
    member = onehots[0] + onehots[1] + onehots[2] + onehots[3]
    before = jnp.dot(tril_ref[...], member.astype(BF16), preferred_element_type=F32) + run_ref[...]
    rank = jnp.zeros((tm, LANES), F32)
    for k in range(TOP_K):
        r_k = jnp.sum(onehots[k] * before, axis=-1, keepdims=True)
        rank = jnp.where(lane == k, r_k, rank)
    rank_ref[...] = rank.astype(jnp.int32)
    run_ref[...] += jnp.sum(member, axis=0, keepdims=True)
    cnt_ref[...] = run_ref[...]


def _ranks(te, tm):
    r = te.shape[0]
    i = lax.broadcasted_iota(jnp.int32, (tm, tm), 0)
    j = lax.broadcasted_iota(jnp.int32, (tm, tm), 1)
    tril = (j < i).astype(BF16)
    return pl.pallas_call(
        _rank_kernel,
        grid=(r // tm,),
        in_specs=[pl.BlockSpec((tm, LANES), lambda b: (b, 0)), pl.BlockSpec((tm, tm), lambda b: (0, 0))],
        out_specs=[pl.BlockSpec((tm, LANES), lambda b: (b, 0)), pl.BlockSpec((1, LANES), lambda b: (0, 0))],
        out_shape=[jax.ShapeDtypeStruct((r, LANES), jnp.int32), jax.ShapeDtypeStruct((1, LANES), F32)],
        scratch_shapes=[pltpu.VMEM((1, LANES), F32)],
        compiler_params=_cparams(("arbitrary",)),
        name="route_ranks",
    )(te, tril)


def _route(te, n_rows, bm):
    rank, cnt = _ranks(te, Q_TILE_PROMPT)
    counts = cnt[0, :N_EXPERTS].astype(jnp.int32)
    padded = ((counts + bm - 1) // bm) * bm
    pend = jnp.cumsum(padded)
    pstart = pend - padded
    top_e = te[:, :TOP_K]
    onehot = top_e[:, :, None] == jnp.arange(N_EXPERTS, dtype=jnp.int32)[None, None, :]
    pos = jnp.sum(jnp.where(onehot, pstart[None, None, :], 0), axis=-1) + rank[:, :TOP_K]
    nb = n_rows // bm
    block_start = jnp.arange(nb, dtype=jnp.int32) * bm
    block_e = jnp.minimum(jnp.sum((pend[None, :] <= block_start[:, None]).astype(jnp.int32), axis=1),
                          N_EXPERTS - 1)
    e_hot = block_e[:, None] == jnp.arange(N_EXPERTS, dtype=jnp.int32)[None, :]
    last = jnp.sum(jnp.where(e_hot, (pstart + counts)[None, :], 0), axis=1)
    n_valid = jnp.clip(last - block_start, 0, bm).astype(jnp.int32)
    return pos, block_e.astype(jnp.int32), n_valid


def _sc_scatter_rows(src, idx_t, n_out):
    n_src, d = src.shape
    sub = SC_GATHER_WINDOW // SC_SUB_WINDOWS
    n_tiles = n_src // SC_GATHER_WINDOW
    idx_t = idx_t.reshape(1, TOP_K * n_src)
    mesh = plsc.VectorSubcoreMesh(core_axis_name="c", subcore_axis_name="s")

    @pl.kernel(out_type=jax.ShapeDtypeStruct((n_out, d), src.dtype), mesh=mesh)
    def scatter_kernel(x_hbm, i_hbm, o_hbm):
        def body(x_vmem, i_vmem):
            s = pl.program_id(2)
            pltpu.sync_copy(x_vmem, o_hbm.at[i_vmem.at[0, pl.ds(s * sub, sub)]])

        pltpu.emit_pipeline(
            body,
            grid=(n_tiles, TOP_K, SC_SUB_WINDOWS),
            in_specs=[pl.BlockSpec((sub, d), index_map=lambda i, k, s: (i * SC_SUB_WINDOWS + s, 0)),
                      pl.BlockSpec((1, SC_GATHER_WINDOW), index_map=lambda i, k, s: (0, k * n_tiles + i))],
            out_specs=[],
            core_axis_name=("c", "s"),
            dimension_semantics=(pltpu.PARALLEL, pltpu.ARBITRARY, pltpu.ARBITRARY),
        )(x_hbm, i_hbm)

    return scatter_kernel(src, idx_t)


def _sc_gather(table, idx):
    n_idx = idx.shape[0]
    d = table.shape[1]
    sub = SC_GATHER_WINDOW // SC_SUB_WINDOWS
    mesh = plsc.VectorSubcoreMesh(core_axis_name="c", subcore_axis_name="s")

    @pl.kernel(out_type=jax.ShapeDtypeStruct((n_idx, d), table.dtype), mesh=mesh)
    def gather_kernel(x_hbm, i_hbm, o_hbm):
        def body(i_vmem, o_vmem):
            j = pl.program_id(1)
            pltpu.sync_copy(x_hbm.at[i_vmem.at[0, pl.ds(j * sub, sub)]], o_vmem)

        pltpu.emit_pipeline(
            body,
            grid=(n_idx // SC_GATHER_WINDOW, SC_SUB_WINDOWS),
            in_specs=[pl.BlockSpec((1, SC_GATHER_WINDOW), index_map=lambda i, j: (0, i))],
            out_specs=[pl.BlockSpec((sub, d), index_map=lambda i, j: (i * SC_SUB_WINDOWS + j, 0))],
            core_axis_name=("c", "s"),
            dimension_semantics=(pltpu.PARALLEL, pltpu.ARBITRARY),
        )(i_hbm, o_hbm)

    return gather_kernel(table, idx.reshape(1, n_idx))


def _pad_to(a, mult, value):
    pad = (-a.shape[0]) % mult
    return a if pad == 0 else jnp.concatenate([a, jnp.full((pad,), value, a.dtype)])


def _tri(kt):
    s = lax.broadcasted_iota(jnp.int32, (kt, kt), 0)
    j = lax.broadcasted_iota(jnp.int32, (kt, kt), 1)
    return (j > s).astype(BF16)


def kernel(x_prompt, x_sample, cache_sb_k, cache_sb_v, cache_diff_k, cache_diff_v, meta_tokens,
           norm_mix_g, w_in, diff_lambda, diff_subln_g, w_br_sb, w_br_diff, w_out, norm_ffn_g,
           w_router, b_router, w_gate_up, b_gate_up, w_down, b_down, final_norm_g):
    assert x_prompt.shape[0] == 1 and w_in.shape[0] == 1
    d = D_MODEL
    seq = x_prompt.shape[1]
    t = N_META + seq
    tq_p, kt = Q_TILE_PROMPT, KEY_TILE
    tp = -(-t // tq_p) * tq_p
    nq_p = tp // tq_p
    nb_s, s_len = x_sample.shape[:2]
    past = cache_sb_k.shape[2]
    n_s = nb_s * s_len
    scale = DH_SB ** -0.5

    w0 = w_in[0]
    wq = jnp.concatenate([w0[:, 0:SEC], w0[:, 3 * SEC:4 * SEC]], axis=1).astype(BF16)
    wk = jnp.concatenate([w0[:, SEC:2 * SEC], w0[:, 4 * SEC:5 * SEC]], axis=1).astype(BF16)
    wv = jnp.concatenate([w0[:, 2 * SEC:3 * SEC], w0[:, 5 * SEC:6 * SEC]], axis=1).astype(BF16)
    wg = w0[:, 6 * SEC:8 * SEC].astype(BF16)
    w_sb = w_br_sb[0].astype(BF16)
    w_d = w_br_diff[0].astype(BF16)
    w_o = w_out[0].astype(BF16)
    w_r = jnp.pad(w_router[0], ((0, 0), (0, LANES - N_EXPERTS)))
    b_r = jnp.pad(b_router[0], (0, LANES - N_EXPERTS), constant_values=NEG_INF).reshape(1, LANES)
    w_gu = w_gate_up[0]
    w_dn = w_down[0]
    b_gu = b_gate_up[0].reshape(N_EXPERTS, 1, 2 * D_FF)
    b_dn = b_down[0].reshape(N_EXPERTS, 1, d)
    slope = jnp.exp2(-8.0 * jnp.arange(1, H_DIFF + 1, dtype=F32) / H_DIFF)
    slopes = jnp.stack([slope, 1.0 / slope])
    tri = _tri(kt)
    lam_p = diff_lambda[0]
    g_sub = diff_subln_g[0]

    hp = jnp.concatenate([meta_tokens.astype(F32), x_prompt[0], jnp.zeros((tp - t, d), F32)], axis=0)
    tm_p = tq_p
    xn = _rmsnorm(hp, norm_mix_g[0], tm_p)
    q_sb, q_d = _proj(xn, wq, "q", tm_p, scale=scale)
    kf_sb, kb_sb, kf_d, kb_d = _proj(xn, wk, "k", tm_p, r_out=t)
    vf_sb, vt_sb, vf_d, vt_d = _proj(xn, wv, "v", tm_p, r_out=t)
    (gates,) = _proj(xn, wg, "g", tm_p)
    o_sb = _sb_attention(q_sb[None], kb_sb[None], vt_sb[None], tri, batch=1, nq=nq_p, tq=tq_p, kt=kt,
                         q_pos0=0)
    o_d = _diff_attention(q_d[None], kb_d[None], vt_d[None], slopes, lam_p, g_sub, batch=1, nq=nq_p,
                          tq=tq_p, kt=kt, q_pos0=0, coff=N_META, n_keys=t)
    h2, hn, te, tg = _merge(hp, o_sb[0], o_d[0], gates, w_sb, w_d, w_o, norm_ffn_g[0], w_r, b_r, tq_p)

    tq_s = Q_TILE_DECODE
    hs = x_sample.reshape(n_s, d)
    xn_s = _rmsnorm(hs, norm_mix_g[0], n_s)
    q2_s = _proj(xn_s, wq, "q", n_s, scale=scale)
    kf_s0, _, kf_s1, _ = _proj(xn_s, wk, "k", n_s)
    vf_s0, _, vf_s1, _ = _proj(xn_s, wv, "v", n_s, kt=n_s)
    kf_s = (kf_s0, kf_s1)
    vf_s = (vf_s0, vf_s1)
    (gates_s,) = _proj(xn_s, wg, "g", n_s)

    def pad_q(qs):
        return jnp.pad(qs.reshape(nb_s, s_len, SEC), ((0, 0), (0, tq_s - s_len), (0, 0)))

    def per_stream(a):
        return a.reshape(nb_s, -1, SEC)

    k_sb_all, vt_sb_all = _cache_tiles(cache_sb_k[0], cache_sb_v[0], per_stream(kf_s[0]),
                                       per_stream(vf_s[0]), kt)
    k_d_all, vt_d_all = _cache_tiles(cache_diff_k[0], cache_diff_v[0], per_stream(kf_s[1]),
                                     per_stream(vf_s[1]), kt)
    o_sb_s = _sb_attention(pad_q(q2_s[0]), k_sb_all, vt_sb_all, tri, batch=nb_s, nq=1, tq=tq_s, kt=kt,
                           q_pos0=past)
    o_d_s = _diff_attention(pad_q(q2_s[1]), k_d_all, vt_d_all, slopes, lam_p, g_sub, batch=nb_s, nq=1,
                            tq=tq_s, kt=kt, q_pos0=past, coff=0, n_keys=past + s_len)
    o_sb_s = o_sb_s[:, :s_len].reshape(n_s, SEC)
    o_d_s = o_d_s[:, :s_len].reshape(n_s, SEC)
    h2_s, hn_s, te_s, tg_s = _merge(hs, o_sb_s, o_d_s, gates_s, w_sb, w_d, w_o, norm_ffn_g[0], w_r, b_r,
                                    n_s)

    n_tok = t + n_s
    r_tok = -(-n_tok // tq_p) * tq_p
    bm = MOE_ROWS
    n_rows = -(-(n_tok * TOP_K + N_EXPERTS * (bm - 1)) // bm) * bm
    te_all = jnp.concatenate([te[:t], te_s, jnp.full((r_tok - n_tok, LANES), -1, jnp.int32)], axis=0)
    hn_all = jnp.concatenate([hn[:t], hn_s, jnp.zeros((r_tok - n_tok, d), F32)], axis=0)
    pos, block_e, n_valid = _route(te_all, n_rows, bm)
    pos_d = jnp.where(te_all[:, :1] >= 0, pos, n_rows)
    xs = _sc_scatter_rows(hn_all, pos_d.T, n_rows + 8)
    yb = _moe_gmm(block_e, n_valid, xs, w_gu, b_gu, w_dn, b_dn, bm)
    pos_all = jnp.concatenate([pos[:t].reshape(-1), jnp.zeros(((tp - t) * TOP_K,), jnp.int32),
                               pos[t:n_tok].reshape(-1)])
    contrib = _sc_gather(yb, _pad_to(pos_all, SC_GATHER_WINDOW * 8, 0))
    contrib = contrib.reshape(-1, TOP_K * d)
    y_p = _final(h2, contrib, tg, final_norm_g, tq_p)
    y_s = _final(h2_s, contrib, tg_s, final_norm_g, n_s, c_row0=tp)

    y_prompt = y_p[N_META:t][None]
    y_sample = y_s.reshape(nb_s, s_len, d)

    def heads(a, nh):
        return a.reshape(1, *a.shape[:-1], nh, a.shape[-1] // nh)

    return (y_prompt, y_sample,
            heads(kf_sb[None], H_SB), heads(vf_sb[None], H_SB),
            heads(kf_d[None], H_DIFF), heads(vf_d[None], H_DIFF),
            heads(kf_s[0].reshape(nb_s, s_len, SEC), H_SB), heads(vf_s[0].reshape(nb_s, s_len, SEC), H_SB),
            heads(kf_s[1].reshape(nb_s, s_len, SEC), H_DIFF), heads(vf_s[1].reshape(nb_s, s_len, SEC), H_DIFF))
```

```python
import functools
import math

import jax
import jax.numpy as jnp
from jax import lax
from jax.experimental import pallas as pl
from jax.experimental.pallas import tpu as pltpu
from jax.experimental.pallas import tpu_sc as plsc

F32 = jnp.float32
BF16 = jnp.bfloat16

D_MODEL = 1024
CHUNK = 64
CHUNK_SHIFT = 6
N_META = 16
H_SB = 16
DH_SB = 64
H_DIFF = 8
DH_DIFF = 64
DV_DIFF = 128
N_EXPERTS = 32
TOP_K = 4
D_FF = 1024
SWIGLU_LIMIT = 7.0
SWIGLU_ALPHA = 1.702
NORM_EPS = 1e-5
LAMBDA_INIT = 0.8 - 0.6 * math.exp(-0.3 * 0)

LANES = 128
SEC = 1024
KEY_TILE = 256
Q_TILE_PROMPT = 256
Q_TILE_DECODE = 128
MOE_ROWS = 256
VMEM_LIMIT = 56 * 1024 * 1024
NEG_INF = float("-inf")
SB_SKIP_LOG = 104.0
SB_ABSENT = 1e30
DIFF_GROUP = 2
DIFF_TAIL_GROUPS = 2
DIFF_SKIP_LOG = 105.0
DIFF_NORM_SLACK = 1.01
SC_GATHER_WINDOW = 128
SC_SUB_WINDOWS = 4


def _cparams(sem):
    return pltpu.CompilerParams(dimension_semantics=sem, vmem_limit_bytes=VMEM_LIMIT)


def _rmsnorm_kernel(x_ref, g_ref, o_ref):
    x = x_ref[...]
    y = x * lax.rsqrt(jnp.mean(x * x, axis=-1, keepdims=True) + NORM_EPS)
    o_ref[...] = (y * g_ref[...]).astype(o_ref.dtype)


def _rmsnorm(x, g, tm):
    r, d = x.shape
    return pl.pallas_call(
        _rmsnorm_kernel,
        grid=(r // tm,),
        in_specs=[pl.BlockSpec((tm, d), lambda i: (i, 0)),
                  pl.BlockSpec((1, d), lambda i: (0, 0))],
        out_specs=pl.BlockSpec((tm, d), lambda i: (i, 0)),
        out_shape=jax.ShapeDtypeStruct((r, d), BF16),
        compiler_params=_cparams(("parallel",)),
        name="rmsnorm",
    )(x, g.reshape(1, d))


def _proj_kernel(x_ref, w_ref, *out_refs, mode, scale, kt):
    x = x_ref[...]
    for s in range(2):
        acc = jnp.dot(x, w_ref[:, s * SEC:(s + 1) * SEC], preferred_element_type=F32)
        if mode == "q":
            out_refs[s][...] = (acc * scale).astype(BF16)
        elif mode == "k":
            out_refs[2 * s][...] = acc
            out_refs[2 * s + 1][...] = acc.astype(BF16)
        elif mode == "v":
            out_refs[2 * s][...] = acc
            for c in range(acc.shape[0] // kt):
                for hp in range(SEC // LANES):
                    tile = acc[c * kt:(c + 1) * kt, hp * LANES:(hp + 1) * LANES]
                    out_refs[2 * s + 1][hp, c] = tile.T.astype(BF16)
        else:
            out_refs[0][:, s * SEC:(s + 1) * SEC] = jax.nn.sigmoid(acc)


def _proj(xn, w2, mode, tm, r_out=None, kt=KEY_TILE, scale=1.0):
    r, d = xn.shape
    r_out = r if r_out is None else r_out
    in_specs = [pl.BlockSpec((tm, d), lambda i: (i, 0)),
                pl.BlockSpec((d, 2 * SEC), lambda i: (0, 0))]
    row_spec = pl.BlockSpec((tm, SEC), lambda i: (i, 0))
    f32_out = jax.ShapeDtypeStruct((r_out, SEC), F32)
    bf16_out = jax.ShapeDtypeStruct((r, SEC), BF16)
    if mode == "q":
        out_specs = [row_spec, row_spec]
        out_shape = [bf16_out, bf16_out]
    elif mode == "k":
        out_specs = [row_spec] * 4
        out_shape = [f32_out, bf16_out, f32_out, bf16_out]
    elif mode == "v":
        t_spec = pl.BlockSpec((SEC // LANES, tm // kt, LANES, kt), lambda i: (0, i, 0, 0))
        t_out = jax.ShapeDtypeStruct((SEC // LANES, r // kt, LANES, kt), BF16)
        out_specs = [row_spec, t_spec, row_spec, t_spec]
        out_shape = [f32_out, t_out, f32_out, t_out]
    else:
        out_specs = [pl.BlockSpec((tm, 2 * SEC), lambda i: (i, 0))]
        out_shape = [jax.ShapeDtypeStruct((r, 2 * SEC), F32)]
    return pl.pallas_call(
        functools.partial(_proj_kernel, mode=mode, scale=scale, kt=kt),
        grid=(r // tm,), in_specs=in_specs, out_specs=out_specs, out_shape=out_shape,
        compiler_params=_cparams(("parallel",)),
        name="proj_" + mode,
    )(xn, w2)


def _cache_tiles_kernel(kc_ref, vc_ref, kn_ref, vn_ref, k_out, vt_out, *, kt, keys_minor):
    j = pl.program_id(1)
    n_cache = pl.num_programs(1) - 1
    nhb = SEC // LANES

    @pl.when(j < n_cache)
    def _():
        for hb in range(nhb):
            cols = slice(hb * LANES, (hb + 1) * LANES)
            if keys_minor:
                k_out[0, :, cols] = kc_ref[0, cols, :].T.astype(BF16)
                vt_out[0, hb, 0] = vc_ref[0, cols, :].astype(BF16)
            else:
                k_out[0, :, cols] = kc_ref[0, :, hb, :].astype(BF16)
                vt_out[0, hb, 0] = vc_ref[0, :, hb, :].T.astype(BF16)

    @pl.when(j == n_cache)
    def _():
        s_len = kn_ref.shape[1]
        pad = jnp.zeros((kt - s_len, SEC), F32)
        k_out[0] = jnp.concatenate([kn_ref[0], pad], axis=0).astype(BF16)
        v_new = jnp.concatenate([vn_ref[0], pad], axis=0)
        for hb in range(nhb):
            vt_out[0, hb, 0] = v_new[:, hb * LANES:(hb + 1) * LANES].T.astype(BF16)


def _cache_tiles(k_cache, v_cache, k_new, v_new, kt, keys_minor):
    b = k_cache.shape[0]
    past = k_cache.shape[2] if keys_minor else k_cache.shape[1]
    s_len = k_new.shape[1]
    assert past % kt == 0 and s_len <= kt
    n_cache = past // kt
    if keys_minor:
        cache_spec = pl.BlockSpec((1, SEC, kt), lambda i, j: (i, 0, jnp.minimum(j, n_cache - 1)))
    else:
        cache_spec = pl.BlockSpec((1, kt, SEC // LANES, LANES),
                                  lambda i, j: (i, jnp.minimum(j, n_cache - 1), 0, 0))
    new_spec = pl.BlockSpec((1, s_len, SEC), lambda i, j: (i, 0, 0))
    return pl.pallas_call(
        functools.partial(_cache_tiles_kernel, kt=kt, keys_minor=keys_minor),
        grid=(b, n_cache + 1),
        in_specs=[cache_spec, cache_spec, new_spec, new_spec],
        out_specs=[pl.BlockSpec((1, kt, SEC), lambda i, j: (i, j, 0)),
                   pl.BlockSpec((1, SEC // LANES, 1, LANES, kt), lambda i, j: (i, 0, j, 0, 0))],
        out_shape=[jax.ShapeDtypeStruct((b, past + kt, SEC), BF16),
                   jax.ShapeDtypeStruct((b, SEC // LANES, n_cache + 1, LANES, kt), BF16)],
        compiler_params=_cparams(("parallel", "arbitrary")),
        name="cache_tiles",
    )(k_cache, v_cache, k_new, v_new)


def _sb_kernel(q_ref, k_ref, vt_ref, tri_ref, o_ref, acc_ref, *, tq, kt, q_pos0):
    qi = pl.program_id(2)
    q0 = q_pos0 + qi * tq
    jm = q0 // kt
    q = q_ref[0]
    lane = lax.broadcasted_iota(jnp.int32, (tq, LANES), 1)
    kpos_l = lax.broadcasted_iota(jnp.int32, (kt, tq), 0)
    qpos_l = lax.broadcasted_iota(jnp.int32, (kt, tq), 1)
    tri = tri_ref[...]
    zero = jnp.zeros_like(q)
    qz = (jnp.where(lane < DH_SB, q, zero), jnp.where(lane >= DH_SB, q, zero))
    acc_ref[...] = jnp.zeros_like(acc_ref)

    def tile(j, hh, carry, masked):
        off = pl.multiple_of(j * kt, kt)
        kk = k_ref[0, pl.ds(off, kt), :]
        z = lax.dot_general(kk, qz[hh], (((1,), (1,)), ((), ())),
                            preferred_element_type=F32)
        sp = jnp.maximum(z, 0.0) + jnp.log(1.0 + jnp.exp(-jnp.abs(z)))
        if masked:
            vis = (kpos_l + j * kt) < (qpos_l + q0)
            lk = jnp.where(vis, -sp, 0.0)
        else:
            lk = -sp
        later = jnp.dot(tri, lk.astype(BF16), preferred_element_type=F32)
        w = jnp.exp(z + lk + later + carry)
        if masked:
            w = jnp.where(vis, w, 0.0)
        acc_ref[hh] += jnp.dot(vt_ref[0, 0, j], w.astype(BF16), preferred_element_type=F32)
        return jnp.sum(lk, axis=0, keepdims=True)

    none_later = jnp.zeros((1, tq), F32)
    carries = tuple(tile(jm, hh, none_later, True) for hh in range(2))

    def pair(j0, ca, cb):
        j1 = j0 - 1
        out = []
        for hh, carry in ((0, ca), (1, cb)):
            s0 = tile(jnp.maximum(j0, 0), hh, jnp.where(j0 >= 0, carry, -SB_ABSENT), False)
            s1 = tile(jnp.maximum(j1, 0), hh, jnp.where(j1 >= 0, carry + s0, -SB_ABSENT), False)
            out.append(carry + s0 + s1)
        return out

    jb = jm - 1
    first = []
    for hh in range(2):
        s = tile(jnp.maximum(jb, 0), hh, jnp.where(jb >= 0, carries[hh], -SB_ABSENT), False)
        first.append(carries[hh] + s)

    def cond(state):
        t, ca, cb = state
        return jnp.logical_and(jm - 2 - 2 * t >= 0,
                               jnp.max(jnp.maximum(ca, cb)) > -SB_SKIP_LOG)

    def body(state):
        t, ca, cb = state
        ca, cb = pair(jm - 2 - 2 * t, ca, cb)
        return t + 1, ca, cb

    lax.while_loop(cond, body, (jnp.int32(0), first[0], first[1]))

    row = lax.broadcasted_iota(jnp.int32, (LANES, tq), 0)
    o_t = jnp.where(row < DH_SB, acc_ref[0], acc_ref[1])
    o_ref[0] = o_t.T.astype(o_ref.dtype)


def _sb_attention(q, k, vt, tri, *, batch, nq, tq, kt, q_pos0):
    rk = k.shape[1]
    nk = rk // kt
    nhp = SEC // LANES
    return pl.pallas_call(
        functools.partial(_sb_kernel, tq=tq, kt=kt, q_pos0=q_pos0),
        grid=(batch, nhp, nq),
        in_specs=[
            pl.BlockSpec((1, tq, LANES), lambda b, h, i: (b, i, h)),
            pl.BlockSpec((1, rk, LANES), lambda b, h, i: (b, 0, h)),
            pl.BlockSpec((1, 1, nk, LANES, kt), lambda b, h, i: (b, h, 0, 0, 0)),
            pl.BlockSpec((kt, kt), lambda b, h, i: (0, 0)),
        ],
        out_specs=pl.BlockSpec((1, tq, LANES), lambda b, h, i: (b, i, h)),
        out_shape=jax.ShapeDtypeStruct((batch, nq * tq, SEC), BF16),
        scratch_shapes=[pltpu.VMEM((2, LANES, tq), F32)],
        compiler_params=_cparams(("parallel", "parallel", "arbitrary")),
        name="sb_attention",
    )(q, k, vt, tri)


def _diff_kernel(slopes_ref, lam_ref, g_ref, q_ref, k_ref, vt_ref, o_ref,
                 acc_ref, bt_ref, s0_ref, s1_ref, p0_ref, p1_ref, kn_ref,
               *, tq, kt, q_pos0, coff, n_keys, nk):
    h = pl.program_id(1)
    qi = pl.program_id(2)
    slope = slopes_ref[0, h]
    inv_slope = slopes_ref[1, h]
    q0 = q_pos0 + qi * tq
    c_lo = (q0 - coff + CHUNK) // CHUNK - 1
    c_hi = (q0 + tq - 1 - coff + CHUNK) // CHUNK - 1
    full_end = jnp.minimum(coff + CHUNK * (c_lo + 1), n_keys)
    vis_end = jnp.minimum(coff + CHUNK * (c_hi + 1), n_keys)
    n_full = full_end // kt
    n_vis = jnp.minimum((vis_end + kt - 1) // kt, nk)

    q = q_ref[0]
    lane = lax.broadcasted_iota(jnp.int32, (tq, LANES), 1)
    kpos_l = lax.broadcasted_iota(jnp.int32, (kt, tq), 0)
    qpos_l = lax.broadcasted_iota(jnp.int32, (kt, tq), 1)
    gsz = DIFF_GROUP
    m_tiles = (n_full // gsz) * gsz
    n_fullg = m_tiles // gsz
    n_groups = n_fullg + DIFF_TAIL_GROUPS
    n_tail = DIFF_TAIL_GROUPS * gsz
    absent = gsz + n_tail

    for u in range(gsz):
        bt_ref[u] = slope * (kpos_l + u * kt).astype(F32)
    for u in range(n_tail):
        kpos = kpos_l + (m_tiles + u) * kt
        qpos = qpos_l + q0
        vis = (((kpos - coff + CHUNK) >> CHUNK_SHIFT) <= ((qpos - coff + CHUNK) >> CHUNK_SHIFT)) \
            & (kpos < n_keys)
        bias = slope * (qpos_l - jnp.abs(qpos - kpos)).astype(F32)
        bt_ref[gsz + u] = jnp.where(vis, bias, NEG_INF)
    bt_ref[absent] = jnp.full((kt, tq), NEG_INF, F32)

    zero = jnp.zeros_like(q)
    qz = (jnp.where(lane < DH_DIFF, q, zero), jnp.where(lane >= DH_DIFF, q, zero))
    acc_ref[...] = jnp.zeros_like(acc_ref)
    p1_ref[...] = jnp.zeros_like(p1_ref)

    hr = lax.broadcasted_iota(jnp.int32, (LANES, LANES), 0)
    hc = lax.broadcasted_iota(jnp.int32, (LANES, LANES), 1)
    half = (((hc == 0) & (hr < DH_DIFF)) | ((hc == 1) & (hr >= DH_DIFF))).astype(BF16)
    lane1 = lax.broadcasted_iota(jnp.int32, (1, LANES), 1)

    def max_half_norms(sq_max):
        nrm = jnp.sqrt(sq_max)
        return (jnp.max(jnp.where(lane1 == 0, nrm, 0.0)), jnp.max(jnp.where(lane1 == 1, nrm, 0.0)))

    @pl.when(qi == 0)
    def _():
        def body(j, mx):
            off = pl.multiple_of(j * kt, kt)
            kk = k_ref[0, pl.ds(off, kt), :]
            n2 = jnp.dot(kk * kk, half, preferred_element_type=F32)
            return jnp.maximum(mx, jnp.max(n2, axis=0, keepdims=True))
        kn = max_half_norms(lax.fori_loop(0, nk, body, jnp.zeros((1, LANES), F32)))
        kn_ref[0] = kn[0]
        kn_ref[1] = kn[1]

    qn = max_half_norms(jnp.max(jnp.dot(q * q, half, preferred_element_type=F32), axis=0, keepdims=True))
    qk_bound = [DIFF_NORM_SLACK * qn[mm] * kn_ref[mm] for mm in range(2)]
    q0_f = q0.astype(F32)

    def skip_below(state):
        need = jnp.maximum((DIFF_SKIP_LOG + qk_bound[0]) - state[0], (DIFF_SKIP_LOG + qk_bound[1]) - state[2])
        return q0_f - jnp.max(need) * inv_slope

    def seq_group(n):
        return n_groups - 1 - n

    def needed(g, thr):
        top = ((g + 1) * (gsz * kt)).astype(F32)
        return jnp.logical_and(g >= 0, jnp.logical_or(g >= n_fullg, top > thr))

    def tiles_of(g):
        return [jnp.clip(g * gsz + u, 0, nk - 1) for u in range(gsz)]

    def scores_stage(g, s_ref):
        js = tiles_of(g)
        for mm in range(2):
            for u in range(gsz):
                off = pl.multiple_of(js[u] * kt, kt)
                kk = k_ref[0, pl.ds(off, kt), :]
                s_ref[mm, u] =lax.dot_general(kk, qz[mm], (((1,), (1,)), ((), ())),
                                               preferred_element_type=F32)

    def softmax_stage(g, thr, s_ref, p_ref, state):
        ok = needed(g, thr)
        full = jnp.logical_and(ok, g < n_fullg)
        sj = jnp.where(full, slope * (g * gsz * kt - q0).astype(F32), 0.0)
        bidx = [jnp.where(full, u, jnp.where(ok, gsz + g * gsz + u - m_tiles, absent)) for u in range(gsz)]
        new_state, alphas = [], []
        for mm in range(2):
            m, l = state[2 * mm], state[2 * mm + 1]
            ss = [s_ref[mm, u] + bt_ref[bidx[u]] for u in range(gsz)]
            cmax = jnp.max(ss[0], axis=0, keepdims=True)
            for s in ss[1:]:
                cmax = jnp.maximum(cmax, jnp.max(s, axis=0, keepdims=True))
            m_new = jnp.maximum(m, cmax + sj)
            m_safe = jnp.where(m_new == NEG_INF, 0.0, m_new)
            alpha = jnp.exp(m - m_safe)
            r = m_safe - sj
            psum = jnp.zeros((1, tq), F32)
            for u in range(gsz):
                p = jnp.exp(ss[u] - r)
                psum = psum + jnp.sum(p, axis=0, keepdims=True)
                p_ref[mm, u] = p.astype(BF16)
            new_state += [m_new, alpha * l + psum]
            alphas.append(alpha)
        return tuple(new_state), tuple(alphas)

    def value_stage(g, p_ref, alphas):
        js = tiles_of(g)
        for mm in range(2):
            pv = jnp.dot(vt_ref[0, 0, js[0]], p_ref[mm, 0], preferred_element_type=F32)
            for u in range(1, gsz):
                pv = pv + jnp.dot(vt_ref[0, 0, js[u]], p_ref[mm, u], preferred_element_type=F32)
            acc_ref[mm] = alphas[mm] * acc_ref[mm] + pv

    def trip(carry):
        t, thr, state, alpha_prev = carry
        n0 = 2 * t
        scores_stage(seq_group(n0 + 1), s1_ref)
        state, alpha0 = softmax_stage(seq_group(n0), thr, s0_ref, p0_ref, state)
        value_stage(seq_group(n0 - 1), p1_ref, alpha_prev)
        scores_stage(seq_group(n0 + 2), s0_ref)
        state, alpha1 = softmax_stage(seq_group(n0 + 1), thr, s1_ref, p1_ref, state)
        value_stage(seq_group(n0), p0_ref, alpha0)
        return t + 1, skip_below(state), state, alpha1

    def more(carry):
        t, thr = carry[0], carry[1]
        return needed(seq_group(2 * t), thr)

    neg = jnp.full((1, tq), NEG_INF, F32)
    zer = jnp.zeros((1, tq), F32)
    one = jnp.ones((1, tq), F32)
    scores_stage(seq_group(0), s0_ref)
    n_done, _, state, alpha_last = lax.while_loop(
        more, trip, (jnp.int32(0), jnp.float32(NEG_INF), (neg, zer, neg, zer), (one, one)))
    value_stage(seq_group(2 * n_done - 1), p1_ref, alpha_last)
    stats = [(state[0], state[1]), (state[2], state[3])]

    lp = lam_ref[...]
    lam = (jnp.exp(jnp.sum(lp[0:1] * lp[1:2], axis=-1, keepdims=True))
           - jnp.exp(jnp.sum(lp[2:3] * lp[3:4], axis=-1, keepdims=True)) + LAMBDA_INIT)
    l1 = stats[0][1]
    l2 = stats[1][1]
    l1 = jnp.where(l1 == 0.0, 1.0, l1)
    l2 = jnp.where(l2 == 0.0, 1.0, l2)
    o_t = acc_ref[0] / l1 - lam * (acc_ref[1] / l2)
    o = o_t.T
    y = o * lax.rsqrt(jnp.mean(o * o, axis=-1, keepdims=True) + NORM_EPS)
    o_ref[0] = ((y * g_ref[...]) * (1.0 - LAMBDA_INIT)).astype(o_ref.dtype)


def _diff_scratch(tq, kt):
    return [pltpu.VMEM((2, LANES, tq), F32),
            pltpu.VMEM((DIFF_GROUP * (1 + DIFF_TAIL_GROUPS) + 1, kt, tq), F32),
            pltpu.VMEM((2, DIFF_GROUP, kt, tq), F32),
            pltpu.VMEM((2, DIFF_GROUP, kt, tq), F32),
            pltpu.VMEM((2, DIFF_GROUP, kt, tq), BF16),
            pltpu.VMEM((2, DIFF_GROUP, kt, tq), BF16),
            pltpu.SMEM((2,), F32)]


def _diff_param_specs():
    return [pl.BlockSpec(memory_space=pltpu.SMEM),
            pl.BlockSpec((4, DH_DIFF), lambda b, h, i: (0, 0)),
            pl.BlockSpec((1, DV_DIFF), lambda b, h, i: (0, 0))]


def _diff_attention(q, k, vt, slopes, lam_p, subln_g, *, batch, nq, tq, kt, q_pos0, coff, n_keys):
    rk = k.shape[1]
    nk = rk // kt
    return pl.pallas_call(
        functools.partial(_diff_kernel, tq=tq, kt=kt, q_pos0=q_pos0, coff=coff, n_keys=n_keys, nk=nk),
        grid=(batch, SEC // LANES, nq),
        in_specs=_diff_param_specs() + [
            pl.BlockSpec((1, tq, LANES), lambda b, h, i: (b, i, h)),
            pl.BlockSpec((1, rk, LANES), lambda b, h, i: (b, 0, h)),
            pl.BlockSpec((1, 1, nk, LANES, kt), lambda b, h, i: (b, h, 0, 0, 0)),
        ],
        out_specs=pl.BlockSpec((1, tq, LANES), lambda b, h, i: (b, i, h)),
        out_shape=jax.ShapeDtypeStruct((batch, nq * tq, SEC), BF16),
        scratch_shapes=_diff_scratch(tq, kt),
        compiler_params=_cparams(("parallel", "parallel", "arbitrary")),
        name="diff_attention",
    )(slopes, lam_p, subln_g.reshape(1, DV_DIFF), q, k, vt)


def _merge_kernel(h_ref, osb_ref, od_ref, gt_ref, wsb_ref, wd_ref, wo_ref, gn_ref, wr_ref, br_ref,
                  h2_ref, hn_ref, te_ref, tg_ref):
    y_sb = jnp.dot(osb_ref[...], wsb_ref[...], preferred_element_type=F32)
    y_d = jnp.dot(od_ref[...], wd_ref[...], preferred_element_type=F32)
    gt = gt_ref[...]
    mix = gt[:, :D_MODEL] * y_sb + gt[:, D_MODEL:] * y_d
    h2 = h_ref[...] + jnp.dot(mix.astype(BF16), wo_ref[...], preferred_element_type=F32)
    h2_ref[...] = h2
    hn = (h2 * lax.rsqrt(jnp.mean(h2 * h2, axis=-1, keepdims=True) + NORM_EPS)) * gn_ref[...]
    hn_ref[...] = hn
    logits = jnp.dot(hn, wr_ref[...], preferred_element_type=F32,
                     precision=lax.Precision.HIGHEST) + br_ref[...]
    tm = logits.shape[0]
    lane = lax.broadcasted_iota(jnp.int32, (tm, LANES), 1).astype(F32)
    tops, idxs = [], []
    l = logits
    for _ in range(TOP_K):
        m = jnp.max(l, axis=-1, keepdims=True)
        idx = jnp.min(jnp.where(l == m, lane, float(LANES)), axis=-1, keepdims=True)
        tops.append(m)
        idxs.append(idx)
        l = jnp.where(lane == idx, NEG_INF, l)
    ex = [jnp.exp(t - tops[0]) for t in tops]
    den = ex[0] + ex[1] + ex[2] + ex[3]
    te = jnp.zeros((tm, LANES), F32)
    tg = jnp.zeros((tm, LANES), F32)
    for k in range(TOP_K):
        te = jnp.where(lane == float(k), idxs[k], te)
        tg = jnp.where(lane == float(k), ex[k] / den, tg)
    te_ref[...] = te.astype(jnp.int32)
    tg_ref[...] = tg


def _merge(h, o_sb, o_d, gates, w_sb, w_d, w_o, g_ffn, w_r, b_r, tm):
    r, d = h.shape
    row = lambda w: pl.BlockSpec((tm, w), lambda i: (i, 0))
    full = lambda a, b: pl.BlockSpec((a, b), lambda i: (0, 0))
    return pl.pallas_call(
        _merge_kernel,
        grid=(r // tm,),
        in_specs=[row(d), row(d), row(d), row(2 * d), full(d, d), full(d, d), full(d, d),
                  full(1, d), full(d, LANES), full(1, LANES)],
        out_specs=[row(d), row(d), row(LANES), row(LANES)],
        out_shape=[jax.ShapeDtypeStruct((r, d), F32), jax.ShapeDtypeStruct((r, d), F32),
                   jax.ShapeDtypeStruct((r, LANES), jnp.int32), jax.ShapeDtypeStruct((r, LANES), F32)],
        compiler_params=_cparams(("parallel",)),
        name="merge_router",
    )(h, o_sb, o_d, gates, w_sb, w_d, w_o, g_ffn.reshape(1, d), w_r, b_r)


def _moe_kernel(be_ref, nv_ref, x_ref, wgu_ref, bgu_ref, wdn_ref, bdn_ref, o_ref, wgu_bf, wdn_bf):
    b = pl.program_id(0)
    used = nv_ref[b] > 0
    new_expert = jnp.logical_or(b == 0, be_ref[b] != be_ref[jnp.maximum(b - 1, 0)])

    @pl.when(jnp.logical_and(used, new_expert))
    def _():
        wgu_bf[...] = wgu_ref[0].astype(BF16)
        wdn_bf[...] = wdn_ref[0].astype(BF16)

    @pl.when(used)
    def _():
        row = lax.broadcasted_iota(jnp.int32, x_ref.shape, 0)
        x = jnp.where(row < nv_ref[b], x_ref[...], 0.0).astype(BF16)
        gu = jnp.dot(x, wgu_bf[...], preferred_element_type=F32) + bgu_ref[0]
        g = jnp.minimum(gu[:, :D_FF], SWIGLU_LIMIT)
        u = jnp.clip(gu[:, D_FF:], -SWIGLU_LIMIT, SWIGLU_LIMIT)
        act = (u + 1.0) * (g * jax.nn.sigmoid(SWIGLU_ALPHA * g))
        o_ref[...] = jnp.dot(act.astype(BF16), wdn_bf[...], preferred_element_type=F32) + bdn_ref[0]

    @pl.when(jnp.logical_not(used))
    def _():
        o_ref[...] = jnp.zeros_like(o_ref)


def _moe_gmm(block_e, n_valid, xs, w_gu, b_gu, w_dn, b_dn, bm):
    nb = block_e.shape[0]
    n_rows = nb * bm
    d = xs.shape[1]
    grid_spec = pltpu.PrefetchScalarGridSpec(
        num_scalar_prefetch=2,
        grid=(nb,),
        in_specs=[
            pl.BlockSpec((bm, d), lambda b, be, nu: (b, 0)),
            pl.BlockSpec((1, d, 2 * D_FF), lambda b, be, nu: (be[b], 0, 0)),
            pl.BlockSpec((1, 1, 2 * D_FF), lambda b, be, nu: (be[b], 0, 0)),
            pl.BlockSpec((1, D_FF, d), lambda b, be, nu: (be[b], 0, 0)),
            pl.BlockSpec((1, 1, d), lambda b, be, nu: (be[b], 0, 0)),
        ],
        out_specs=pl.BlockSpec((bm, d), lambda b, be, nu: (b, 0)),
        scratch_shapes=[pltpu.VMEM((d, 2 * D_FF), BF16), pltpu.VMEM((D_FF, d), BF16)],
    )
    return pl.pallas_call(
        _moe_kernel,
        grid_spec=grid_spec,
        out_shape=jax.ShapeDtypeStruct((n_rows, d), F32),
        compiler_params=_cparams(("arbitrary",)),
        name="moe_experts",
    )(block_e, n_valid, xs, w_gu, b_gu, w_dn, b_dn)


def _final_kernel(h_ref, *refs):
    c_refs, (tg_ref, g_ref, o_ref) = refs[:TOP_K], refs[TOP_K:]
    tg = tg_ref[...]
    y = c_refs[0][...] * tg[:, 0:1]
    for k in range(1, TOP_K):
        y = y + c_refs[k][...] * tg[:, k:k + 1]
    x = h_ref[...] + y
    o_ref[...] = (x * lax.rsqrt(jnp.mean(x * x, axis=-1, keepdims=True) + NORM_EPS)) * g_ref[...]


def _final(h2, contrib, tg, g, tm, slot_rows, c_row0=0):
    r, d = h2.shape

    def slot_spec(k):
        blk0 = (k * slot_rows + c_row0) // tm
        return pl.BlockSpec((tm, d), lambda i: (i + blk0, 0))

    return pl.pallas_call(
        _final_kernel,
        grid=(r // tm,),
        in_specs=[pl.BlockSpec((tm, d), lambda i: (i, 0))] + [slot_spec(k) for k in range(TOP_K)]
        + [pl.BlockSpec((tm, LANES), lambda i: (i, 0)), pl.BlockSpec((1, d), lambda i: (0, 0))],
        out_specs=pl.BlockSpec((tm, d), lambda i: (i, 0)),
        out_shape=jax.ShapeDtypeStruct((r, d), F32),
        compiler_params=_cparams(("parallel",)),
        name="combine_final_norm",
    )(h2, *([contrib] * TOP_K), tg, g.reshape(1, d))


def _rank_kernel(te_ref, tril_ref, rank_ref, cnt_ref, run_ref):
    @pl.when(pl.program_id(0) == 0)
    def _():
        run_ref[...] = jnp.zeros_like(run_ref)

    te = te_ref[...]
    tm = te.shape[0]
    lane = lax.broadcasted_iota(jnp.int32, (tm, LANES), 1)
    onehots = [(lane == te[:, k:k + 1]).astype(F32) for k in range(TOP_K)]
    member = onehots[0] + onehots[1] + onehots[2] + onehots[3]
    before = jnp.dot(tril_ref[...], member.astype(BF16), preferred_element_type=F32) + run_ref[...]
    rank = jnp.zeros((tm, LANES), F32)
    for k in range(TOP_K):
        r_k = jnp.sum(onehots[k] * before, axis=-1, keepdims=True)
        rank = jnp.where(lane == k, r_k, rank)
    rank_ref[...] = rank.astype(jnp.int32)
    run_ref[...] += jnp.sum(member, axis=0, keepdims=True)
    cnt_ref[...] = run_ref[...]


def _ranks(te, tm):
    r = te.shape[0]
    i = lax.broadcasted_iota(jnp.int32, (tm, tm), 0)
    j = lax.broadcasted_iota(jnp.int32, (tm, tm), 1)
    tril = (j < i).astype(BF16)
    return pl.pallas_call(
        _rank_kernel,
        grid=(r // tm,),
        in_specs=[pl.BlockSpec((tm, LANES), lambda b: (b, 0)), pl.BlockSpec((tm, tm), lambda b: (0, 0))],
        out_specs=[pl.BlockSpec((tm, LANES), lambda b: (b, 0)), pl.BlockSpec((1, LANES), lambda b: (0, 0))],
        out_shape=[jax.ShapeDtypeStruct((r, LANES), jnp.int32), jax.ShapeDtypeStruct((1, LANES), F32)],
        scratch_shapes=[pltpu.VMEM((1, LANES), F32)],
        compiler_params=_cparams(("arbitrary",)),
        name="route_ranks",
    )(te, tril)


def _route(te, n_rows, bm):
    rank, cnt = _ranks(te, Q_TILE_PROMPT)
    counts = cnt[0, :N_EXPERTS].astype(jnp.int32)
    padded = ((counts + bm - 1) // bm) * bm
    pend = jnp.cumsum(padded)
    pstart = pend - padded
    top_e = te[:, :TOP_K]
    onehot = top_e[:, :, None] == jnp.arange(N_EXPERTS, dtype=jnp.int32)[None, None, :]
    pos = jnp.sum(jnp.where(onehot, pstart[None, None, :], 0), axis=-1) + rank[:, :TOP_K]
    nb = n_rows // bm
    block_start = jnp.arange(nb, dtype=jnp.int32) * bm
    block_e = jnp.minimum(jnp.sum((pend[None, :] <= block_start[:, None]).astype(jnp.int32), axis=1),
                          N_EXPERTS - 1)
    e_hot = block_e[:, None] == jnp.arange(N_EXPERTS, dtype=jnp.int32)[None, :]
    last = jnp.sum(jnp.where(e_hot, (pstart + counts)[None, :], 0), axis=1)
    n_valid = jnp.clip(last - block_start, 0, bm).astype(jnp.int32)
    return pos, block_e.astype(jnp.int32), n_valid


def _sc_scatter_rows(src, idx_t, n_out):
    n_src, d = src.shape
    sub = SC_GATHER_WINDOW // SC_SUB_WINDOWS
    n_tiles = n_src // SC_GATHER_WINDOW
    idx_t = idx_t.reshape(1, TOP_K * n_src)
    mesh = plsc.VectorSubcoreMesh(core_axis_name="c", subcore_axis_name="s")

    @pl.kernel(out_type=jax.ShapeDtypeStruct((n_out, d), src.dtype), mesh=mesh)
    def scatter_kernel(x_hbm, i_hbm, o_hbm):
        def body(x_vmem, i_vmem):
            s = pl.program_id(2)
            pltpu.sync_copy(x_vmem, o_hbm.at[i_vmem.at[0, pl.ds(s * sub, sub)]])

        pltpu.emit_pipeline(
            body,
            grid=(n_tiles, TOP_K, SC_SUB_WINDOWS),
            in_specs=[pl.BlockSpec((sub, d), index_map=lambda i, k, s: (i * SC_SUB_WINDOWS + s, 0)),
                      pl.BlockSpec((1, SC_GATHER_WINDOW), index_map=lambda i, k, s: (0, k * n_tiles + i))],
            out_specs=[],
            core_axis_name=("c", "s"),
            dimension_semantics=(pltpu.PARALLEL, pltpu.ARBITRARY, pltpu.ARBITRARY),
        )(x_hbm, i_hbm)

    return scatter_kernel(src, idx_t)


def _sc_gather(table, idx):
    n_idx = idx.shape[0]
    d = table.shape[1]
    sub = SC_GATHER_WINDOW // SC_SUB_WINDOWS
    mesh = plsc.VectorSubcoreMesh(core_axis_name="c", subcore_axis_name="s")

    @pl.kernel(out_type=jax.ShapeDtypeStruct((n_idx, d), table.dtype), mesh=mesh)
    def gather_kernel(x_hbm, i_hbm, o_hbm):
        def body(i_vmem, o_vmem):
            j = pl.program_id(1)
            pltpu.sync_copy(x_hbm.at[i_vmem.at[0, pl.ds(j * sub, sub)]], o_vmem)

        pltpu.emit_pipeline(
            body,
            grid=(n_idx // SC_GATHER_WINDOW, SC_SUB_WINDOWS),
            in_specs=[pl.BlockSpec((1, SC_GATHER_WINDOW), index_map=lambda i, j: (0, i))],
            out_specs=[pl.BlockSpec((sub, d), index_map=lambda i, j: (i * SC_SUB_WINDOWS + j, 0))],
            core_axis_name=("c", "s"),
            dimension_semantics=(pltpu.PARALLEL, pltpu.ARBITRARY),
        )(i_hbm, o_hbm)

    return gather_kernel(table, idx.reshape(1, n_idx))


def _tri(kt):
    s = lax.broadcasted_iota(jnp.int32, (kt, kt), 0)
    j = lax.broadcasted_iota(jnp.int32, (kt, kt), 1)
    return (j > s).astype(BF16)


def kernel(x_prompt, x_sample, cache_sb_k, cache_sb_v, cache_diff_k, cache_diff_v, meta_tokens,
           norm_mix_g, w_in, diff_lambda, diff_subln_g, w_br_sb, w_br_diff, w_out, norm_ffn_g,
           w_router, b_router, w_gate_up, b_gate_up, w_down, b_down, final_norm_g):
    assert x_prompt.shape[0] == 1 and w_in.shape[0] == 1
    d = D_MODEL
    seq = x_prompt.shape[1]
    t = N_META + seq
    tq_p, kt = Q_TILE_PROMPT, KEY_TILE
    tp = -(-t // tq_p) * tq_p
    nq_p = tp // tq_p
    nb_s, s_len = x_sample.shape[:2]
    past = cache_sb_k.shape[2]
    n_s = nb_s * s_len
    scale = DH_SB ** -0.5

    w0 = w_in[0]
    wq = jnp.concatenate([w0[:, 0:SEC], w0[:, 3 * SEC:4 * SEC]], axis=1).astype(BF16)
    wk = jnp.concatenate([w0[:, SEC:2 * SEC], w0[:, 4 * SEC:5 * SEC]], axis=1).astype(BF16)
    wv = jnp.concatenate([w0[:, 2 * SEC:3 * SEC], w0[:, 5 * SEC:6 * SEC]], axis=1).astype(BF16)
    wg = w0[:, 6 * SEC:8 * SEC].astype(BF16)
    w_sb = w_br_sb[0].astype(BF16)
    w_d = w_br_diff[0].astype(BF16)
    w_o = w_out[0].astype(BF16)
    w_r = jnp.pad(w_router[0], ((0, 0), (0, LANES - N_EXPERTS)))
    b_r = jnp.pad(b_router[0], (0, LANES - N_EXPERTS), constant_values=NEG_INF).reshape(1, LANES)
    w_gu = w_gate_up[0]
    w_dn = w_down[0]
    b_gu = b_gate_up[0].reshape(N_EXPERTS, 1, 2 * D_FF)
    b_dn = b_down[0].reshape(N_EXPERTS, 1, d)
    slope = jnp.exp2(-8.0 * jnp.arange(1, H_DIFF + 1, dtype=F32) / H_DIFF)
    slopes = jnp.stack([slope, 1.0 / slope])
    tri = _tri(kt)
    lam_p = diff_lambda[0]
    g_sub = diff_subln_g[0]

    hp = jnp.concatenate([meta_tokens.astype(F32), x_prompt[0], jnp.zeros((tp - t, d), F32)], axis=0)
    tm_p = tq_p
    xn = _rmsnorm(hp, norm_mix_g[0], tm_p)
    q_sb, q_d = _proj(xn, wq, "q", tm_p, scale=scale)
    kf_sb, kb_sb, kf_d, kb_d = _proj(xn, wk, "k", tm_p, r_out=t)
    vf_sb, vt_sb, vf_d, vt_d = _proj(xn, wv, "v", tm_p, r_out=t)
    (gates,) = _proj(xn, wg, "g", tm_p)
    o_sb = _sb_attention(q_sb[None], kb_sb[None], vt_sb[None], tri, batch=1, nq=nq_p, tq=tq_p, kt=kt,
                         q_pos0=0)
    o_d = _diff_attention(q_d[None], kb_d[None], vt_d[None], slopes, lam_p, g_sub, batch=1, nq=nq_p,
                          tq=tq_p, kt=kt, q_pos0=0, coff=N_META, n_keys=t)
    h2, hn, te, tg = _merge(hp, o_sb[0], o_d[0], gates, w_sb, w_d, w_o, norm_ffn_g[0], w_r, b_r, tq_p)

    tq_s = Q_TILE_DECODE
    hs = x_sample.reshape(n_s, d)
    xn_s = _rmsnorm(hs, norm_mix_g[0], n_s)
    q2_s = _proj(xn_s, wq, "q", n_s, scale=scale)
    kf_s0, _, kf_s1, _ = _proj(xn_s, wk, "k", n_s)
    vf_s0, _, vf_s1, _ = _proj(xn_s, wv, "v", n_s, kt=n_s)
    kf_s = (kf_s0, kf_s1)
    vf_s = (vf_s0, vf_s1)
    (gates_s,) = _proj(xn_s, wg, "g", n_s)

    def pad_q(qs):
        return jnp.pad(qs.reshape(nb_s, s_len, SEC), ((0, 0), (0, tq_s - s_len), (0, 0)))

    def per_stream(a):
        return a.reshape(nb_s, -1, SEC)

    def keys_last(cache):
        return jnp.transpose(cache, (0, 2, 3, 1)).reshape(nb_s, SEC, past)

    k_sb_all, vt_sb_all = _cache_tiles(keys_last(cache_sb_k[0]), keys_last(cache_sb_v[0]),
                                       per_stream(kf_s[0]), per_stream(vf_s[0]), kt, True)
    k_d_all, vt_d_all = _cache_tiles(cache_diff_k[0], cache_diff_v[0], per_stream(kf_s[1]),
                                     per_stream(vf_s[1]), kt, False)
    o_sb_s = _sb_attention(pad_q(q2_s[0]), k_sb_all, vt_sb_all, tri, batch=nb_s, nq=1, tq=tq_s, kt=kt,
                           q_pos0=past)
    o_d_s = _diff_attention(pad_q(q2_s[1]), k_d_all, vt_d_all, slopes, lam_p, g_sub, batch=nb_s, nq=1,
                            tq=tq_s, kt=kt, q_pos0=past, coff=0, n_keys=past + s_len)
    o_sb_s = o_sb_s[:, :s_len].reshape(n_s, SEC)
    o_d_s = o_d_s[:, :s_len].reshape(n_s, SEC)
    h2_s, hn_s, te_s, tg_s = _merge(hs, o_sb_s, o_d_s, gates_s, w_sb, w_d, w_o, norm_ffn_g[0], w_r, b_r,
                                    n_s)

    n_tok = t + n_s
    r_tok = -(-n_tok // tq_p) * tq_p
    bm = MOE_ROWS
    n_rows = -(-(n_tok * TOP_K + N_EXPERTS * (bm - 1)) // bm) * bm
    te_all = jnp.concatenate([te[:t], te_s, jnp.full((r_tok - n_tok, LANES), -1, jnp.int32)], axis=0)
    hn_all = jnp.concatenate([hn[:t], hn_s, jnp.zeros((r_tok - n_tok, d), F32)], axis=0)
    pos, block_e, n_valid = _route(te_all, n_rows, bm)
    pos_d = jnp.where(te_all[:, :1] >= 0, pos, n_rows)
    xs = _sc_scatter_rows(hn_all, pos_d.T, n_rows + 8)
    yb = _moe_gmm(block_e, n_valid, xs, w_gu, b_gu, w_dn, b_dn, bm)
    slot_rows = -(-(tp + n_s) // (SC_GATHER_WINDOW * 8)) * (SC_GATHER_WINDOW * 8)
    pos_slots = jnp.concatenate([pos[:t], jnp.zeros((tp - t, TOP_K), jnp.int32), pos[t:n_tok],
                                 jnp.zeros((slot_rows - tp - n_s, TOP_K), jnp.int32)], axis=0)
    contrib = _sc_gather(yb, pos_slots.T.reshape(-1))
    y_p = _final(h2, contrib, tg, final_norm_g, tq_p, slot_rows)
    y_s = _final(h2_s, contrib, tg_s, final_norm_g, n_s, slot_rows, c_row0=tp)

    y_prompt = y_p[N_META:t][None]
    y_sample = y_s.reshape(nb_s, s_len, d)

    def heads(a, nh):
        return a.reshape(1, *a.shape[:-1], nh, a.shape[-1] // nh)

    return (y_prompt, y_sample,
            heads(kf_sb[None], H_SB), heads(vf_sb[None], H_SB),
            heads(kf_d[None], H_DIFF), heads(vf_d[None], H_DIFF),
            heads(kf_s[0].reshape(nb_s, s_len, SEC), H_SB), heads(vf_s[0].reshape(nb_s, s_len, SEC), H_SB),
            heads(kf_s[1].reshape(nb_s, s_len, SEC), H_DIFF), heads(vf_s[1].reshape(nb_s, s_len, SEC), H_DIFF))
```

```python
import functools
import math

import jax
import jax.numpy as jnp
from jax import lax
from jax.experimental import pallas as pl
from jax.experimental.pallas import tpu as pltpu
from jax.experimental.pallas import tpu_sc as plsc

F32 = jnp.float32
BF16 = jnp.bfloat16

D_MODEL = 1024
CHUNK = 64
CHUNK_SHIFT = 6
N_META = 16
H_SB = 16
DH_SB = 64
H_DIFF = 8
DH_DIFF = 64
DV_DIFF = 128
N_EXPERTS = 32
TOP_K = 4
D_FF = 1024
SWIGLU_LIMIT = 7.0
SWIGLU_ALPHA = 1.702
NORM_EPS = 1e-5
LAMBDA_INIT = 0.8 - 0.6 * math.exp(-0.3 * 0)

LANES = 128
SEC = 1024
KEY_TILE = 256
Q_TILE_PROMPT = 256
Q_TILE_DIFF = 512
Q_TILE_DECODE = 128
MOE_ROWS = 256
VMEM_LIMIT = 56 * 1024 * 1024
NEG_INF = float("-inf")
SB_SKIP_LOG = 104.0
SB_ABSENT = 1e30
DIFF_GROUP = 2
DIFF_TAIL_GROUPS = 2
DIFF_SKIP_LOG = 105.0
DIFF_NORM_SLACK = 1.01
SC_GATHER_WINDOW = 128
SC_SUB_WINDOWS = 4


def _cparams(sem):
    return pltpu.CompilerParams(dimension_semantics=sem, vmem_limit_bytes=VMEM_LIMIT)


def _rmsnorm_kernel(x_ref, g_ref, o_ref):
    x = x_ref[...]
    y = x * lax.rsqrt(jnp.mean(x * x, axis=-1, keepdims=True) + NORM_EPS)
    o_ref[...] = (y * g_ref[...]).astype(o_ref.dtype)


def _rmsnorm(x, g, tm):
    r, d = x.shape
    return pl.pallas_call(
        _rmsnorm_kernel,
        grid=(r // tm,),
        in_specs=[pl.BlockSpec((tm, d), lambda i: (i, 0)),
                  pl.BlockSpec((1, d), lambda i: (0, 0))],
        out_specs=pl.BlockSpec((tm, d), lambda i: (i, 0)),
        out_shape=jax.ShapeDtypeStruct((r, d), BF16),
        compiler_params=_cparams(("parallel",)),
        name="rmsnorm",
    )(x, g.reshape(1, d))


def _proj_kernel(x_ref, w_ref, *out_refs, mode, scale, kt):
    x = x_ref[...]
    for s in range(2):
        acc = jnp.dot(x, w_ref[:, s * SEC:(s + 1) * SEC], preferred_element_type=F32)
        if mode == "q":
            out_refs[s][...] = (acc * scale).astype(BF16)
        elif mode == "k":
            out_refs[2 * s][...] = acc
            out_refs[2 * s + 1][...] = acc.astype(BF16)
        elif mode == "v":
            out_refs[2 * s][...] = acc
            for c in range(acc.shape[0] // kt):
                for hp in range(SEC // LANES):
                    tile = acc[c * kt:(c + 1) * kt, hp * LANES:(hp + 1) * LANES]
                    out_refs[2 * s + 1][hp, c] = tile.T.astype(BF16)
        else:
            out_refs[0][:, s * SEC:(s + 1) * SEC] = jax.nn.sigmoid(acc)


def _proj(xn, w2, mode, tm, r_out=None, kt=KEY_TILE, scale=1.0):
    r, d = xn.shape
    r_out = r if r_out is None else r_out
    in_specs = [pl.BlockSpec((tm, d), lambda i: (i, 0)),
                pl.BlockSpec((d, 2 * SEC), lambda i: (0, 0))]
    row_spec = pl.BlockSpec((tm, SEC), lambda i: (i, 0))
    f32_out = jax.ShapeDtypeStruct((r_out, SEC), F32)
    bf16_out = jax.ShapeDtypeStruct((r, SEC), BF16)
    if mode == "q":
        out_specs = [row_spec, row_spec]
        out_shape = [bf16_out, bf16_out]
    elif mode == "k":
        out_specs = [row_spec] * 4
        out_shape = [f32_out, bf16_out, f32_out, bf16_out]
    elif mode == "v":
        t_spec = pl.BlockSpec((SEC // LANES, tm // kt, LANES, kt), lambda i: (0, i, 0, 0))
        t_out = jax.ShapeDtypeStruct((SEC // LANES, r // kt, LANES, kt), BF16)
        out_specs = [row_spec, t_spec, row_spec, t_spec]
        out_shape = [f32_out, t_out, f32_out, t_out]
    else:
        out_specs = [pl.BlockSpec((tm, 2 * SEC), lambda i: (i, 0))]
        out_shape = [jax.ShapeDtypeStruct((r, 2 * SEC), F32)]
    return pl.pallas_call(
        functools.partial(_proj_kernel, mode=mode, scale=scale, kt=kt),
        grid=(r // tm,), in_specs=in_specs, out_specs=out_specs, out_shape=out_shape,
        compiler_params=_cparams(("parallel",)),
        name="proj_" + mode,
    )(xn, w2)


def _cache_tiles_kernel(kc_ref, vc_ref, kn_ref, vn_ref, k_out, vt_out, *, kt, keys_minor):
    j = pl.program_id(1)
    n_cache = pl.num_programs(1) - 1
    nhb = SEC // LANES

    @pl.when(j < n_cache)
    def _():
        for hb in range(nhb):
            cols = slice(hb * LANES, (hb + 1) * LANES)
            if keys_minor:
                k_out[0, :, cols] = kc_ref[0, cols, :].T.astype(BF16)
                vt_out[0, hb, 0] = vc_ref[0, cols, :].astype(BF16)
            else:
                k_out[0, :, cols] = kc_ref[0, :, hb, :].astype(BF16)
                vt_out[0, hb, 0] = vc_ref[0, :, hb, :].T.astype(BF16)

    @pl.when(j == n_cache)
    def _():
        s_len = kn_ref.shape[1]
        pad = jnp.zeros((kt - s_len, SEC), F32)
        k_out[0] = jnp.concatenate([kn_ref[0], pad], axis=0).astype(BF16)
        v_new = jnp.concatenate([vn_ref[0], pad], axis=0)
        for hb in range(nhb):
            vt_out[0, hb, 0] = v_new[:, hb * LANES:(hb + 1) * LANES].T.astype(BF16)


def _cache_tiles(k_cache, v_cache, k_new, v_new, kt, keys_minor):
    b = k_cache.shape[0]
    past = k_cache.shape[2] if keys_minor else k_cache.shape[1]
    s_len = k_new.shape[1]
    assert past % kt == 0 and s_len <= kt
    n_cache = past // kt
    if keys_minor:
        cache_spec = pl.BlockSpec((1, SEC, kt), lambda i, j: (i, 0, jnp.minimum(j, n_cache - 1)))
    else:
        cache_spec = pl.BlockSpec((1, kt, SEC // LANES, LANES),
                                  lambda i, j: (i, jnp.minimum(j, n_cache - 1), 0, 0))
    new_spec = pl.BlockSpec((1, s_len, SEC), lambda i, j: (i, 0, 0))
    return pl.pallas_call(
        functools.partial(_cache_tiles_kernel, kt=kt, keys_minor=keys_minor),
        grid=(b, n_cache + 1),
        in_specs=[cache_spec, cache_spec, new_spec, new_spec],
        out_specs=[pl.BlockSpec((1, kt, SEC), lambda i, j: (i, j, 0)),
                   pl.BlockSpec((1, SEC // LANES, 1, LANES, kt), lambda i, j: (i, 0, j, 0, 0))],
        out_shape=[jax.ShapeDtypeStruct((b, past + kt, SEC), BF16),
                   jax.ShapeDtypeStruct((b, SEC // LANES, n_cache + 1, LANES, kt), BF16)],
        compiler_params=_cparams(("parallel", "arbitrary")),
        name="cache_tiles",
    )(k_cache, v_cache, k_new, v_new)


def _sb_kernel(q_ref, k_ref, vt_ref, tri_ref, o_ref, acc_ref, *, tq, kt, q_pos0):
    qi = pl.program_id(2)
    q0 = q_pos0 + qi * tq
    jm = q0 // kt
    q = q_ref[0]
    lane = lax.broadcasted_iota(jnp.int32, (tq, LANES), 1)
    kpos_l = lax.broadcasted_iota(jnp.int32, (kt, tq), 0)
    qpos_l = lax.broadcasted_iota(jnp.int32, (kt, tq), 1)
    tri = tri_ref[...]
    zero = jnp.zeros_like(q)
    qz = (jnp.where(lane < DH_SB, q, zero), jnp.where(lane >= DH_SB, q, zero))
    acc_ref[...] = jnp.zeros_like(acc_ref)

    def tile(j, hh, carry, masked):
        off = pl.multiple_of(j * kt, kt)
        kk = k_ref[0, pl.ds(off, kt), :]
        z = lax.dot_general(kk, qz[hh], (((1,), (1,)), ((), ())),
                            preferred_element_type=F32)
        sp = jnp.maximum(z, 0.0) + jnp.log(1.0 + jnp.exp(-jnp.abs(z)))
        if masked:
            vis = (kpos_l + j * kt) < (qpos_l + q0)
            lk = jnp.where(vis, -sp, 0.0)
        else:
            lk = -sp
        later = jnp.dot(tri, lk.astype(BF16), preferred_element_type=F32)
        w = jnp.exp(z + lk + later + carry)
        if masked:
            w = jnp.where(vis, w, 0.0)
        acc_ref[hh] += jnp.dot(vt_ref[0, 0, j], w.astype(BF16), preferred_element_type=F32)
        return jnp.sum(lk, axis=0, keepdims=True)

    none_later = jnp.zeros((1, tq), F32)
    carries = tuple(tile(jm, hh, none_later, True) for hh in range(2))

    def pair(j0, ca, cb):
        j1 = j0 - 1
        out = []
        for hh, carry in ((0, ca), (1, cb)):
            s0 = tile(jnp.maximum(j0, 0), hh, jnp.where(j0 >= 0, carry, -SB_ABSENT), False)
            s1 = tile(jnp.maximum(j1, 0), hh, jnp.where(j1 >= 0, carry + s0, -SB_ABSENT), False)
            out.append(carry + s0 + s1)
        return out

    jb = jm - 1
    first = []
    for hh in range(2):
        s = tile(jnp.maximum(jb, 0), hh, jnp.where(jb >= 0, carries[hh], -SB_ABSENT), False)
        first.append(carries[hh] + s)

    def cond(state):
        t, ca, cb = state
        return jnp.logical_and(jm - 2 - 2 * t >= 0,
                               jnp.max(jnp.maximum(ca, cb)) > -SB_SKIP_LOG)

    def body(state):
        t, ca, cb = state
        ca, cb = pair(jm - 2 - 2 * t, ca, cb)
        return t + 1, ca, cb

    lax.while_loop(cond, body, (jnp.int32(0), first[0], first[1]))

    row = lax.broadcasted_iota(jnp.int32, (LANES, tq), 0)
    o_t = jnp.where(row < DH_SB, acc_ref[0], acc_ref[1])
    o_ref[0] = o_t.T.astype(o_ref.dtype)


def _sb_attention(q, k, vt, tri, *, batch, nq, tq, kt, q_pos0):
    rk = k.shape[1]
    nk = rk // kt
    nhp = SEC // LANES
    return pl.pallas_call(
        functools.partial(_sb_kernel, tq=tq, kt=kt, q_pos0=q_pos0),
        grid=(batch, nhp, nq),
        in_specs=[
            pl.BlockSpec((1, tq, LANES), lambda b, h, i: (b, i, h)),
            pl.BlockSpec((1, rk, LANES), lambda b, h, i: (b, 0, h)),
            pl.BlockSpec((1, 1, nk, LANES, kt), lambda b, h, i: (b, h, 0, 0, 0)),
            pl.BlockSpec((kt, kt), lambda b, h, i: (0, 0)),
        ],
        out_specs=pl.BlockSpec((1, tq, LANES), lambda b, h, i: (b, i, h)),
        out_shape=jax.ShapeDtypeStruct((batch, nq * tq, SEC), BF16),
        scratch_shapes=[pltpu.VMEM((2, LANES, tq), F32)],
        compiler_params=_cparams(("parallel", "parallel", "arbitrary")),
        name="sb_attention",
    )(q, k, vt, tri)


def _diff_kernel(slopes_ref, lam_ref, g_ref, q_ref, k_ref, vt_ref, o_ref,
                 acc_ref, bt_ref, s0_ref, s1_ref, p0_ref, p1_ref, kn_ref,
                 *, tq, kt, q_pos0, coff, n_keys, nk, q_rows):
    h = pl.program_id(1)
    qi = pl.program_id(2)
    slope = slopes_ref[0, h]
    inv_slope = slopes_ref[1, h]
    q0 = q_pos0 + qi * tq
    c_lo = (q0 - coff + CHUNK) // CHUNK - 1
    c_hi = (q0 + tq - 1 - coff + CHUNK) // CHUNK - 1
    full_end = jnp.minimum(coff + CHUNK * (c_lo + 1), n_keys)
    vis_end = jnp.minimum(coff + CHUNK * (c_hi + 1), n_keys)
    n_full = full_end // kt
    n_vis = jnp.minimum((vis_end + kt - 1) // kt, nk)

    q = q_ref[0]
    qrow = lax.broadcasted_iota(jnp.int32, (tq, LANES), 0)
    q = jnp.where(qrow < q_rows - qi * tq, q, jnp.zeros_like(q))
    lane = lax.broadcasted_iota(jnp.int32, (tq, LANES), 1)
    kpos_l = lax.broadcasted_iota(jnp.int32, (kt, tq), 0)
    qpos_l = lax.broadcasted_iota(jnp.int32, (kt, tq), 1)
    gsz = DIFF_GROUP
    m_tiles = (n_full // gsz) * gsz
    n_fullg = m_tiles // gsz
    n_groups = n_fullg + DIFF_TAIL_GROUPS
    n_tail = DIFF_TAIL_GROUPS * gsz
    absent = gsz + n_tail

    for u in range(gsz):
        bt_ref[u] = slope * (kpos_l + u * kt).astype(F32)
    for u in range(n_tail):
        kpos = kpos_l + (m_tiles + u) * kt
        qpos = qpos_l + q0
        vis = (((kpos - coff + CHUNK) >> CHUNK_SHIFT) <= ((qpos - coff + CHUNK) >> CHUNK_SHIFT)) \
            & (kpos < n_keys)
        bias = slope * (qpos_l - jnp.abs(qpos - kpos)).astype(F32)
        bt_ref[gsz + u] = jnp.where(vis, bias, NEG_INF)
    bt_ref[absent] = jnp.full((kt, tq), NEG_INF, F32)

    zero = jnp.zeros_like(q)
    qz = (jnp.where(lane < DH_DIFF, q, zero), jnp.where(lane >= DH_DIFF, q, zero))
    acc_ref[...] = jnp.zeros_like(acc_ref)
    p1_ref[...] = jnp.zeros_like(p1_ref)

    hr = lax.broadcasted_iota(jnp.int32, (LANES, LANES), 0)
    hc = lax.broadcasted_iota(jnp.int32, (LANES, LANES), 1)
    half = (((hc == 0) & (hr < DH_DIFF)) | ((hc == 1) & (hr >= DH_DIFF))).astype(BF16)
    lane1 = lax.broadcasted_iota(jnp.int32, (1, LANES), 1)

    def max_half_norms(sq_max):
        nrm = jnp.sqrt(sq_max)
        return (jnp.max(jnp.where(lane1 == 0, nrm, 0.0)), jnp.max(jnp.where(lane1 == 1, nrm, 0.0)))

    @pl.when(qi == 0)
    def _():
        def body(j, mx):
            off = pl.multiple_of(j * kt, kt)
            kk = k_ref[0, pl.ds(off, kt), :]
            n2 = jnp.dot(kk * kk, half, preferred_element_type=F32)
            return jnp.maximum(mx, jnp.max(n2, axis=0, keepdims=True))
        kn = max_half_norms(lax.fori_loop(0, nk, body, jnp.zeros((1, LANES), F32)))
        kn_ref[0] = kn[0]
        kn_ref[1] = kn[1]

    qn = max_half_norms(jnp.max(jnp.dot(q * q, half, preferred_element_type=F32), axis=0, keepdims=True))
    qk_bound = [DIFF_NORM_SLACK * qn[mm] * kn_ref[mm] for mm in range(2)]
    q0_f = q0.astype(F32)

    def skip_below(state):
        need = jnp.maximum((DIFF_SKIP_LOG + qk_bound[0]) - state[0], (DIFF_SKIP_LOG + qk_bound[1]) - state[2])
        return q0_f - jnp.max(need) * inv_slope

    def seq_group(n):
        return n_groups - 1 - n

    def needed(g, thr):
        top = ((g + 1) * (gsz * kt)).astype(F32)
        return jnp.logical_and(g >= 0, jnp.logical_or(g >= n_fullg, top > thr))

    def tiles_of(g):
        return [jnp.clip(g * gsz + u, 0, nk - 1) for u in range(gsz)]

    def scores_stage(g, s_ref):
        js = tiles_of(g)
        for mm in range(2):
            for u in range(gsz):
                off = pl.multiple_of(js[u] * kt, kt)
                kk = k_ref[0, pl.ds(off, kt), :]
                s_ref[mm, u] =lax.dot_general(kk, qz[mm], (((1,), (1,)), ((), ())),
                                               preferred_element_type=F32)

    def softmax_stage(g, thr, s_ref, p_ref, state):
        ok = needed(g, thr)
        full = jnp.logical_and(ok, g < n_fullg)
        sj = jnp.where(full, slope * (g * gsz * kt - q0).astype(F32), 0.0)
        bidx = [jnp.where(full, u, jnp.where(ok, gsz + g * gsz + u - m_tiles, absent)) for u in range(gsz)]
        new_state, alphas = [], []
        for mm in range(2):
            m, l = state[2 * mm], state[2 * mm + 1]
            ss = [s_ref[mm, u] + bt_ref[bidx[u]] for u in range(gsz)]
            cmax = jnp.max(ss[0], axis=0, keepdims=True)
            for s in ss[1:]:
                cmax = jnp.maximum(cmax, jnp.max(s, axis=0, keepdims=True))
            m_new = jnp.maximum(m, cmax + sj)
            m_safe = jnp.where(m_new == NEG_INF, 0.0, m_new)
            alpha = jnp.exp(m - m_safe)
            r = m_safe - sj
            psum = jnp.zeros((1, tq), F32)
            for u in range(gsz):
                p = jnp.exp(ss[u] - r)
                psum = psum + jnp.sum(p, axis=0, keepdims=True)
                p_ref[mm, u] = p.astype(BF16)
            new_state += [m_new, alpha * l + psum]
            alphas.append(alpha)
        return tuple(new_state), tuple(alphas)

    def value_stage(g, p_ref, alphas):
        js = tiles_of(g)
        for mm in range(2):
            pv = jnp.dot(vt_ref[0, 0, js[0]], p_ref[mm, 0], preferred_element_type=F32)
            for u in range(1, gsz):
                pv = pv + jnp.dot(vt_ref[0, 0, js[u]], p_ref[mm, u], preferred_element_type=F32)
            acc_ref[mm] = alphas[mm] * acc_ref[mm] + pv

    def trip(carry):
        t, thr, state, alpha_prev = carry
        n0 = 2 * t
        scores_stage(seq_group(n0 + 1), s1_ref)
        state, alpha0 = softmax_stage(seq_group(n0), thr, s0_ref, p0_ref, state)
        value_stage(seq_group(n0 - 1), p1_ref, alpha_prev)
        scores_stage(seq_group(n0 + 2), s0_ref)
        state, alpha1 = softmax_stage(seq_group(n0 + 1), thr, s1_ref, p1_ref, state)
        value_stage(seq_group(n0), p0_ref, alpha0)
        return t + 1, skip_below(state), state, alpha1

    def more(carry):
        t, thr = carry[0], carry[1]
        return needed(seq_group(2 * t), thr)

    neg = jnp.full((1, tq), NEG_INF, F32)
    zer = jnp.zeros((1, tq), F32)
    one = jnp.ones((1, tq), F32)
    scores_stage(seq_group(0), s0_ref)
    n_done, _, state, alpha_last = lax.while_loop(
        more, trip, (jnp.int32(0), jnp.float32(NEG_INF), (neg, zer, neg, zer), (one, one)))
    value_stage(seq_group(2 * n_done - 1), p1_ref, alpha_last)
    stats = [(state[0], state[1]), (state[2], state[3])]

    lp = lam_ref[...]
    lam = (jnp.exp(jnp.sum(lp[0:1] * lp[1:2], axis=-1, keepdims=True))
           - jnp.exp(jnp.sum(lp[2:3] * lp[3:4], axis=-1, keepdims=True)) + LAMBDA_INIT)
    l1 = stats[0][1]
    l2 = stats[1][1]
    l1 = jnp.where(l1 == 0.0, 1.0, l1)
    l2 = jnp.where(l2 == 0.0, 1.0, l2)
    o_t = acc_ref[0] / l1 - lam * (acc_ref[1] / l2)
    o = o_t.T
    y = o * lax.rsqrt(jnp.mean(o * o, axis=-1, keepdims=True) + NORM_EPS)
    o_ref[0] = ((y * g_ref[...]) * (1.0 - LAMBDA_INIT)).astype(o_ref.dtype)


def _diff_scratch(tq, kt):
    return [pltpu.VMEM((2, LANES, tq), F32),
            pltpu.VMEM((DIFF_GROUP * (1 + DIFF_TAIL_GROUPS) + 1, kt, tq), F32),
            pltpu.VMEM((2, DIFF_GROUP, kt, tq), F32),
            pltpu.VMEM((2, DIFF_GROUP, kt, tq), F32),
            pltpu.VMEM((2, DIFF_GROUP, kt, tq), BF16),
            pltpu.VMEM((2, DIFF_GROUP, kt, tq), BF16),
            pltpu.SMEM((2,), F32)]


def _diff_param_specs():
    return [pl.BlockSpec(memory_space=pltpu.SMEM),
            pl.BlockSpec((4, DH_DIFF), lambda b, h, i: (0, 0)),
            pl.BlockSpec((1, DV_DIFF), lambda b, h, i: (0, 0))]


def _diff_attention(q, k, vt, slopes, lam_p, subln_g, *, batch, nq, tq, kt, q_pos0, coff, n_keys):
    rk = k.shape[1]
    nk = rk // kt
    q_rows = q.shape[1]
    assert (nq - 1) * tq < q_rows <= nq * tq
    return pl.pallas_call(
        functools.partial(_diff_kernel, tq=tq, kt=kt, q_pos0=q_pos0, coff=coff, n_keys=n_keys, nk=nk,
                          q_rows=q_rows),
        grid=(batch, SEC // LANES, nq),
        in_specs=_diff_param_specs() + [
            pl.BlockSpec((1, tq, LANES), lambda b, h, i: (b, i, h)),
            pl.BlockSpec((1, rk, LANES), lambda b, h, i: (b, 0, h)),
            pl.BlockSpec((1, 1, nk, LANES, kt), lambda b, h, i: (b, h, 0, 0, 0)),
        ],
        out_specs=pl.BlockSpec((1, tq, LANES), lambda b, h, i: (b, i, h)),
        out_shape=jax.ShapeDtypeStruct((batch, q_rows, SEC), BF16),
        scratch_shapes=_diff_scratch(tq, kt),
        compiler_params=_cparams(("parallel", "parallel", "arbitrary")),
        name="diff_attention",
    )(slopes, lam_p, subln_g.reshape(1, DV_DIFF), q, k, vt)


def _merge_kernel(h_ref, osb_ref, od_ref, gt_ref, wsb_ref, wd_ref, wo_ref, gn_ref, wr_ref, br_ref,
                  h2_ref, hn_ref, te_ref, tg_ref):
    y_sb = jnp.dot(osb_ref[...], wsb_ref[...], preferred_element_type=F32)
    y_d = jnp.dot(od_ref[...], wd_ref[...], preferred_element_type=F32)
    gt = gt_ref[...]
    mix = gt[:, :D_MODEL] * y_sb + gt[:, D_MODEL:] * y_d
    h2 = h_ref[...] + jnp.dot(mix.astype(BF16), wo_ref[...], preferred_element_type=F32)
    h2_ref[...] = h2
    hn = (h2 * lax.rsqrt(jnp.mean(h2 * h2, axis=-1, keepdims=True) + NORM_EPS)) * gn_ref[...]
    hn_ref[...] = hn
    logits = jnp.dot(hn, wr_ref[...], preferred_element_type=F32,
                     precision=lax.Precision.HIGHEST) + br_ref[...]
    tm = logits.shape[0]
    lane = lax.broadcasted_iota(jnp.int32, (tm, LANES), 1).astype(F32)
    tops, idxs = [], []
    l = logits
    for _ in range(TOP_K):
        m = jnp.max(l, axis=-1, keepdims=True)
        idx = jnp.min(jnp.where(l == m, lane, float(LANES)), axis=-1, keepdims=True)
        tops.append(m)
        idxs.append(idx)
        l = jnp.where(lane == idx, NEG_INF, l)
    ex = [jnp.exp(t - tops[0]) for t in tops]
    den = ex[0] + ex[1] + ex[2] + ex[3]
    te = jnp.zeros((tm, LANES), F32)
    tg = jnp.zeros((tm, LANES), F32)
    for k in range(TOP_K):
        te = jnp.where(lane == float(k), idxs[k], te)
        tg = jnp.where(lane == float(k), ex[k] / den, tg)
    te_ref[...] = te.astype(jnp.int32)
    tg_ref[...] = tg


def _merge(h, o_sb, o_d, gates, w_sb, w_d, w_o, g_ffn, w_r, b_r, tm):
    r, d = h.shape
    row = lambda w: pl.BlockSpec((tm, w), lambda i: (i, 0))
    full = lambda a, b: pl.BlockSpec((a, b), lambda i: (0, 0))
    return pl.pallas_call(
        _merge_kernel,
        grid=(r // tm,),
        in_specs=[row(d), row(d), row(d), row(2 * d), full(d, d), full(d, d), full(d, d),
                  full(1, d), full(d, LANES), full(1, LANES)],
        out_specs=[row(d), row(d), row(LANES), row(LANES)],
        out_shape=[jax.ShapeDtypeStruct((r, d), F32), jax.ShapeDtypeStruct((r, d), F32),
                   jax.ShapeDtypeStruct((r, LANES), jnp.int32), jax.ShapeDtypeStruct((r, LANES), F32)],
        compiler_params=_cparams(("parallel",)),
        name="merge_router",
    )(h, o_sb, o_d, gates, w_sb, w_d, w_o, g_ffn.reshape(1, d), w_r, b_r)


def _moe_kernel(be_ref, nv_ref, x_ref, wgu_ref, bgu_ref, wdn_ref, bdn_ref, o_ref, wgu_bf, wdn_bf):
    b = pl.program_id(0)
    used = nv_ref[b] > 0
    new_expert = jnp.logical_or(b == 0, be_ref[b] != be_ref[jnp.maximum(b - 1, 0)])

    @pl.when(jnp.logical_and(used, new_expert))
    def _():
        wgu_bf[...] = wgu_ref[0].astype(BF16)
        wdn_bf[...] = wdn_ref[0].astype(BF16)

    @pl.when(used)
    def _():
        row = lax.broadcasted_iota(jnp.int32, x_ref.shape, 0)
        x = jnp.where(row < nv_ref[b], x_ref[...], 0.0).astype(BF16)
        gu = jnp.dot(x, wgu_bf[...], preferred_element_type=F32) + bgu_ref[0]
        g = jnp.minimum(gu[:, :D_FF], SWIGLU_LIMIT)
        u = jnp.clip(gu[:, D_FF:], -SWIGLU_LIMIT, SWIGLU_LIMIT)
        act = (u + 1.0) * (g * jax.nn.sigmoid(SWIGLU_ALPHA * g))
        o_ref[...] = jnp.dot(act.astype(BF16), wdn_bf[...], preferred_element_type=F32) + bdn_ref[0]

    @pl.when(jnp.logical_not(used))
    def _():
        o_ref[...] = jnp.zeros_like(o_ref)


def _moe_gmm(block_e, n_valid, xs, w_gu, b_gu, w_dn, b_dn, bm):
    nb = block_e.shape[0]
    n_rows = nb * bm
    d = xs.shape[1]
    grid_spec = pltpu.PrefetchScalarGridSpec(
        num_scalar_prefetch=2,
        grid=(nb,),
        in_specs=[
            pl.BlockSpec((bm, d), lambda b, be, nu: (b, 0)),
            pl.BlockSpec((1, d, 2 * D_FF), lambda b, be, nu: (be[b], 0, 0)),
            pl.BlockSpec((1, 1, 2 * D_FF), lambda b, be, nu: (be[b], 0, 0)),
            pl.BlockSpec((1, D_FF, d), lambda b, be, nu: (be[b], 0, 0)),
            pl.BlockSpec((1, 1, d), lambda b, be, nu: (be[b], 0, 0)),
        ],
        out_specs=pl.BlockSpec((bm, d), lambda b, be, nu: (b, 0)),
        scratch_shapes=[pltpu.VMEM((d, 2 * D_FF), BF16), pltpu.VMEM((D_FF, d), BF16)],
    )
    return pl.pallas_call(
        _moe_kernel,
        grid_spec=grid_spec,
        out_shape=jax.ShapeDtypeStruct((n_rows, d), F32),
        compiler_params=_cparams(("arbitrary",)),
        name="moe_experts",
    )(block_e, n_valid, xs, w_gu, b_gu, w_dn, b_dn)


def _final_kernel(h_ref, *refs):
    c_refs, (tg_ref, g_ref, o_ref) = refs[:TOP_K], refs[TOP_K:]
    tg = tg_ref[...]
    y = c_refs[0][...] * tg[:, 0:1]
    for k in range(1, TOP_K):
        y = y + c_refs[k][...] * tg[:, k:k + 1]
    x = h_ref[...] + y
    o_ref[...] = (x * lax.rsqrt(jnp.mean(x * x, axis=-1, keepdims=True) + NORM_EPS)) * g_ref[...]


def _final(h2, contrib, tg, g, tm, slot_rows, c_row0=0):
    r, d = h2.shape

    def slot_spec(k):
        blk0 = (k * slot_rows + c_row0) // tm
        return pl.BlockSpec((tm, d), lambda i: (i + blk0, 0))

    return pl.pallas_call(
        _final_kernel,
        grid=(r // tm,),
        in_specs=[pl.BlockSpec((tm, d), lambda i: (i, 0))] + [slot_spec(k) for k in range(TOP_K)]
        + [pl.BlockSpec((tm, LANES), lambda i: (i, 0)), pl.BlockSpec((1, d), lambda i: (0, 0))],
        out_specs=pl.BlockSpec((tm, d), lambda i: (i, 0)),
        out_shape=jax.ShapeDtypeStruct((r, d), F32),
        compiler_params=_cparams(("parallel",)),
        name="combine_final_norm",
    )(h2, *([contrib] * TOP_K), tg, g.reshape(1, d))


def _rank_kernel(te_ref, tril_ref, rank_ref, cnt_ref, run_ref):
    @pl.when(pl.program_id(0) == 0)
    def _():
        run_ref[...] = jnp.zeros_like(run_ref)

    te = te_ref[...]
    tm = te.shape[0]
    lane = lax.broadcasted_iota(jnp.int32, (tm, LANES), 1)
    onehots = [(lane == te[:, k:k + 1]).astype(F32) for k in range(TOP_K)]
    member = onehots[0] + onehots[1] + onehots[2] + onehots[3]
    before = jnp.dot(tril_ref[...], member.astype(BF16), preferred_element_type=F32) + run_ref[...]
    rank = jnp.zeros((tm, LANES), F32)
    for k in range(TOP_K):
        r_k = jnp.sum(onehots[k] * before, axis=-1, keepdims=True)
        rank = jnp.where(lane == k, r_k, rank)
    rank_ref[...] = rank.astype(jnp.int32)
    run_ref[...] += jnp.sum(member, axis=0, keepdims=True)
    cnt_ref[...] = run_ref[...]


def _ranks(te, tm):
    r = te.shape[0]
    i = lax.broadcasted_iota(jnp.int32, (tm, tm), 0)
    j = lax.broadcasted_iota(jnp.int32, (tm, tm), 1)
    tril = (j < i).astype(BF16)
    return pl.pallas_call(
        _rank_kernel,
        grid=(r // tm,),
        in_specs=[pl.BlockSpec((tm, LANES), lambda b: (b, 0)), pl.BlockSpec((tm, tm), lambda b: (0, 0))],
        out_specs=[pl.BlockSpec((tm, LANES), lambda b: (b, 0)), pl.BlockSpec((1, LANES), lambda b: (0, 0))],
        out_shape=[jax.ShapeDtypeStruct((r, LANES), jnp.int32), jax.ShapeDtypeStruct((1, LANES), F32)],
        scratch_shapes=[pltpu.VMEM((1, LANES), F32)],
        compiler_params=_cparams(("arbitrary",)),
        name="route_ranks",
    )(te, tril)


def _route(te, n_rows, bm):
    rank, cnt = _ranks(te, Q_TILE_PROMPT)
    counts = cnt[0, :N_EXPERTS].astype(jnp.int32)
    padded = ((counts + bm - 1) // bm) * bm
    pend = jnp.cumsum(padded)
    pstart = pend - padded
    top_e = te[:, :TOP_K]
    onehot = top_e[:, :, None] == jnp.arange(N_EXPERTS, dtype=jnp.int32)[None, None, :]
    pos = jnp.sum(jnp.where(onehot, pstart[None, None, :], 0), axis=-1) + rank[:, :TOP_K]
    nb = n_rows // bm
    block_start = jnp.arange(nb, dtype=jnp.int32) * bm
    block_e = jnp.minimum(jnp.sum((pend[None, :] <= block_start[:, None]).astype(jnp.int32), axis=1),
                          N_EXPERTS - 1)
    e_hot = block_e[:, None] == jnp.arange(N_EXPERTS, dtype=jnp.int32)[None, :]
    last = jnp.sum(jnp.where(e_hot, (pstart + counts)[None, :], 0), axis=1)
    n_valid = jnp.clip(last - block_start, 0, bm).astype(jnp.int32)
    return pos, block_e.astype(jnp.int32), n_valid


def _sc_scatter_rows(src, idx_t, n_out):
    n_src, d = src.shape
    sub = SC_GATHER_WINDOW // SC_SUB_WINDOWS
    n_tiles = n_src // SC_GATHER_WINDOW
    idx_t = idx_t.reshape(1, TOP_K * n_src)
    mesh = plsc.VectorSubcoreMesh(core_axis_name="c", subcore_axis_name="s")

    @pl.kernel(out_type=jax.ShapeDtypeStruct((n_out, d), src.dtype), mesh=mesh)
    def scatter_kernel(x_hbm, i_hbm, o_hbm):
        def body(x_vmem, i_vmem):
            s = pl.program_id(2)
            pltpu.sync_copy(x_vmem, o_hbm.at[i_vmem.at[0, pl.ds(s * sub, sub)]])

        pltpu.emit_pipeline(
            body,
            grid=(n_tiles, TOP_K, SC_SUB_WINDOWS),
            in_specs=[pl.BlockSpec((sub, d), index_map=lambda i, k, s: (i * SC_SUB_WINDOWS + s, 0)),
                      pl.BlockSpec((1, SC_GATHER_WINDOW), index_map=lambda i, k, s: (0, k * n_tiles + i))],
            out_specs=[],
            core_axis_name=("c", "s"),
            dimension_semantics=(pltpu.PARALLEL, pltpu.ARBITRARY, pltpu.ARBITRARY),
        )(x_hbm, i_hbm)

    return scatter_kernel(src, idx_t)


def _sc_gather(table, idx):
    n_idx = idx.shape[0]
    d = table.shape[1]
    sub = SC_GATHER_WINDOW // SC_SUB_WINDOWS
    mesh = plsc.VectorSubcoreMesh(core_axis_name="c", subcore_axis_name="s")

    @pl.kernel(out_type=jax.ShapeDtypeStruct((n_idx, d), table.dtype), mesh=mesh)
    def gather_kernel(x_hbm, i_hbm, o_hbm):
        def body(i_vmem, o_vmem):
            j = pl.program_id(1)
            pltpu.sync_copy(x_hbm.at[i_vmem.at[0, pl.ds(j * sub, sub)]], o_vmem)

        pltpu.emit_pipeline(
            body,
            grid=(n_idx // SC_GATHER_WINDOW, SC_SUB_WINDOWS),
            in_specs=[pl.BlockSpec((1, SC_GATHER_WINDOW), index_map=lambda i, j: (0, i))],
            out_specs=[pl.BlockSpec((sub, d), index_map=lambda i, j: (i * SC_SUB_WINDOWS + j, 0))],
            core_axis_name=("c", "s"),
            dimension_semantics=(pltpu.PARALLEL, pltpu.ARBITRARY),
        )(i_hbm, o_hbm)

    return gather_kernel(table, idx.reshape(1, n_idx))


def _tri(kt):
    s = lax.broadcasted_iota(jnp.int32, (kt, kt), 0)
    j = lax.broadcasted_iota(jnp.int32, (kt, kt), 1)
    return (j > s).astype(BF16)


def kernel(x_prompt, x_sample, cache_sb_k, cache_sb_v, cache_diff_k, cache_diff_v, meta_tokens,
           norm_mix_g, w_in, diff_lambda, diff_subln_g, w_br_sb, w_br_diff, w_out, norm_ffn_g,
           w_router, b_router, w_gate_up, b_gate_up, w_down, b_down, final_norm_g):
    assert x_prompt.shape[0] == 1 and w_in.shape[0] == 1
    d = D_MODEL
    seq = x_prompt.shape[1]
    t = N_META + seq
    tq_p, kt = Q_TILE_PROMPT, KEY_TILE
    tp = -(-t // tq_p) * tq_p
    nq_p = tp // tq_p
    nb_s, s_len = x_sample.shape[:2]
    past = cache_sb_k.shape[2]
    n_s = nb_s * s_len
    scale = DH_SB ** -0.5

    w0 = w_in[0]
    wq = jnp.concatenate([w0[:, 0:SEC], w0[:, 3 * SEC:4 * SEC]], axis=1).astype(BF16)
    wk = jnp.concatenate([w0[:, SEC:2 * SEC], w0[:, 4 * SEC:5 * SEC]], axis=1).astype(BF16)
    wv = jnp.concatenate([w0[:, 2 * SEC:3 * SEC], w0[:, 5 * SEC:6 * SEC]], axis=1).astype(BF16)
    wg = w0[:, 6 * SEC:8 * SEC].astype(BF16)
    w_sb = w_br_sb[0].astype(BF16)
    w_d = w_br_diff[0].astype(BF16)
    w_o = w_out[0].astype(BF16)
    w_r = jnp.pad(w_router[0], ((0, 0), (0, LANES - N_EXPERTS)))
    b_r = jnp.pad(b_router[0], (0, LANES - N_EXPERTS), constant_values=NEG_INF).reshape(1, LANES)
    w_gu = w_gate_up[0]
    w_dn = w_down[0]
    b_gu = b_gate_up[0].reshape(N_EXPERTS, 1, 2 * D_FF)
    b_dn = b_down[0].reshape(N_EXPERTS, 1, d)
    slope = jnp.exp2(-8.0 * jnp.arange(1, H_DIFF + 1, dtype=F32) / H_DIFF)
    slopes = jnp.stack([slope, 1.0 / slope])
    tri = _tri(kt)
    lam_p = diff_lambda[0]
    g_sub = diff_subln_g[0]

    hp = jnp.concatenate([meta_tokens.astype(F32), x_prompt[0], jnp.zeros((tp - t, d), F32)], axis=0)
    tm_p = tq_p
    xn = _rmsnorm(hp, norm_mix_g[0], tm_p)
    q_sb, q_d = _proj(xn, wq, "q", tm_p, scale=scale)
    kf_sb, kb_sb, kf_d, kb_d = _proj(xn, wk, "k", tm_p, r_out=t)
    vf_sb, vt_sb, vf_d, vt_d = _proj(xn, wv, "v", tm_p, r_out=t)
    (gates,) = _proj(xn, wg, "g", tm_p)
    o_sb = _sb_attention(q_sb[None], kb_sb[None], vt_sb[None], tri, batch=1, nq=nq_p, tq=tq_p, kt=kt,
                         q_pos0=0)
    o_d = _diff_attention(q_d[None], kb_d[None], vt_d[None], slopes, lam_p, g_sub, batch=1,
                          nq=-(-tp // Q_TILE_DIFF), tq=Q_TILE_DIFF, kt=kt, q_pos0=0, coff=N_META, n_keys=t)
    h2, hn, te, tg = _merge(hp, o_sb[0], o_d[0], gates, w_sb, w_d, w_o, norm_ffn_g[0], w_r, b_r, tq_p)

    tq_s = Q_TILE_DECODE
    hs = x_sample.reshape(n_s, d)
    xn_s = _rmsnorm(hs, norm_mix_g[0], n_s)
    q2_s = _proj(xn_s, wq, "q", n_s, scale=scale)
    kf_s0, _, kf_s1, _ = _proj(xn_s, wk, "k", n_s)
    vf_s0, _, vf_s1, _ = _proj(xn_s, wv, "v", n_s, kt=n_s)
    kf_s = (kf_s0, kf_s1)
    vf_s = (vf_s0, vf_s1)
    (gates_s,) = _proj(xn_s, wg, "g", n_s)

    def pad_q(qs):
        return jnp.pad(qs.reshape(nb_s, s_len, SEC), ((0, 0), (0, tq_s - s_len), (0, 0)))

    def per_stream(a):
        return a.reshape(nb_s, -1, SEC)

    def keys_last(cache):
        return jnp.transpose(cache, (0, 2, 3, 1)).reshape(nb_s, SEC, past)

    k_sb_all, vt_sb_all = _cache_tiles(keys_last(cache_sb_k[0]), keys_last(cache_sb_v[0]),
                                       per_stream(kf_s[0]), per_stream(vf_s[0]), kt, True)
    k_d_all, vt_d_all = _cache_tiles(cache_diff_k[0], cache_diff_v[0], per_stream(kf_s[1]),
                                     per_stream(vf_s[1]), kt, False)
    o_sb_s = _sb_attention(pad_q(q2_s[0]), k_sb_all, vt_sb_all, tri, batch=nb_s, nq=1, tq=tq_s, kt=kt,
                           q_pos0=past)
    o_d_s = _diff_attention(pad_q(q2_s[1]), k_d_all, vt_d_all, slopes, lam_p, g_sub, batch=nb_s, nq=1,
                            tq=tq_s, kt=kt, q_pos0=past, coff=0, n_keys=past + s_len)
    o_sb_s = o_sb_s[:, :s_len].reshape(n_s, SEC)
    o_d_s = o_d_s[:, :s_len].reshape(n_s, SEC)
    h2_s, hn_s, te_s, tg_s = _merge(hs, o_sb_s, o_d_s, gates_s, w_sb, w_d, w_o, norm_ffn_g[0], w_r, b_r,
                                    n_s)

    n_tok = t + n_s
    r_tok = -(-n_tok // tq_p) * tq_p
    bm = MOE_ROWS
    n_rows = -(-(n_tok * TOP_K + N_EXPERTS * (bm - 1)) // bm) * bm
    te_all = jnp.concatenate([te[:t], te_s, jnp.full((r_tok - n_tok, LANES), -1, jnp.int32)], axis=0)
    hn_all = jnp.concatenate([hn[:t], hn_s, jnp.zeros((r_tok - n_tok, d), F32)], axis=0)
    pos, block_e, n_valid = _route(te_all, n_rows, bm)
    pos_d = jnp.where(te_all[:, :1] >= 0, pos, n_rows)
    xs = _sc_scatter_rows(hn_all, pos_d.T, n_rows + 8)
    yb = _moe_gmm(block_e, n_valid, xs, w_gu, b_gu, w_dn, b_dn, bm)
    slot_rows = -(-(tp + n_s) // (SC_GATHER_WINDOW * 8)) * (SC_GATHER_WINDOW * 8)
    pos_slots = jnp.concatenate([pos[:t], jnp.zeros((tp - t, TOP_K), jnp.int32), pos[t:n_tok],
                                 jnp.zeros((slot_rows - tp - n_s, TOP_K), jnp.int32)], axis=0)
    contrib = _sc_gather(yb, pos_slots.T.reshape(-1))
    y_p = _final(h2, contrib, tg, final_norm_g, tq_p, slot_rows)
    y_s = _final(h2_s, contrib, tg_s, final_norm_g, n_s, slot_rows, c_row0=tp)

    y_prompt = y_p[N_META:t][None]
    y_sample = y_s.reshape(nb_s, s_len, d)

    def heads(a, nh):
        return a.reshape(1, *a.shape[:-1], nh, a.shape[-1] // nh)

    return (y_prompt, y_sample,
            heads(kf_sb[None], H_SB), heads(vf_sb[None], H_SB),
            heads(kf_d[None], H_DIFF), heads(vf_d[None], H_DIFF),
            heads(kf_s[0].reshape(nb_s, s_len, SEC), H_SB), heads(vf_s[0].reshape(nb_s, s_len, SEC), H_SB),
            heads(kf_s[1].reshape(nb_s, s_len, SEC), H_DIFF), heads(vf_s[1].reshape(nb_s, s_len, SEC), H_DIFF))
```

```python
import functools
import math

import jax
import jax.numpy as jnp
from jax import lax
from jax.experimental import pallas as pl
from jax.experimental.pallas import tpu as pltpu
from jax.experimental.pallas import tpu_sc as plsc

F32 = jnp.float32
BF16 = jnp.bfloat16

D_MODEL = 1024
CHUNK = 64
CHUNK_SHIFT = 6
N_META = 16
H_SB = 16
DH_SB = 64
H_DIFF = 8
DH_DIFF = 64
DV_DIFF = 128
N_EXPERTS = 32
TOP_K = 4
D_FF = 1024
SWIGLU_LIMIT = 7.0
SWIGLU_ALPHA = 1.702
NORM_EPS = 1e-5
LAMBDA_INIT = 0.8 - 0.6 * math.exp(-0.3 * 0)

LANES = 128
SEC = 1024
KEY_TILE = 256
Q_TILE_PROMPT = 256
Q_TILE_DIFF = 512
Q_TILE_DECODE = 128
MOE_ROWS = 256
VMEM_LIMIT = 56 * 1024 * 1024
NEG_INF = float("-inf")
SB_SKIP_LOG = 104.0
SB_ABSENT = 1e30
DIFF_GROUP = 2
DIFF_TAIL_GROUPS = 2
DIFF_SKIP_LOG = 105.0
DIFF_NORM_SLACK = 1.01
SC_GATHER_WINDOW = 128
SC_SUB_WINDOWS = 4


def _cparams(sem):
    return pltpu.CompilerParams(dimension_semantics=sem, vmem_limit_bytes=VMEM_LIMIT)


def _rmsnorm_kernel(x_ref, g_ref, o_ref):
    x = x_ref[...]
    y = x * lax.rsqrt(jnp.mean(x * x, axis=-1, keepdims=True) + NORM_EPS)
    o_ref[...] = (y * g_ref[...]).astype(o_ref.dtype)


def _rmsnorm(x, g, tm):
    r, d = x.shape
    return pl.pallas_call(
        _rmsnorm_kernel,
        grid=(r // tm,),
        in_specs=[pl.BlockSpec((tm, d), lambda i: (i, 0)),
                  pl.BlockSpec((1, d), lambda i: (0, 0))],
        out_specs=pl.BlockSpec((tm, d), lambda i: (i, 0)),
        out_shape=jax.ShapeDtypeStruct((r, d), BF16),
        compiler_params=_cparams(("parallel",)),
        name="rmsnorm",
    )(x, g.reshape(1, d))


def _proj_kernel(x_ref, w_ref, *out_refs, mode, scale, kt):
    x = x_ref[...]
    for s in range(2):
        acc = jnp.dot(x, w_ref[:, s * SEC:(s + 1) * SEC], preferred_element_type=F32)
        if mode == "q":
            out_refs[s][...] = (acc * scale).astype(BF16)
        elif mode == "k":
            out_refs[2 * s][...] = acc
            out_refs[2 * s + 1][...] = acc.astype(BF16)
        elif mode == "v":
            out_refs[2 * s][...] = acc
            for c in range(acc.shape[0] // kt):
                for hp in range(SEC // LANES):
                    tile = acc[c * kt:(c + 1) * kt, hp * LANES:(hp + 1) * LANES]
                    out_refs[2 * s + 1][hp, c] = tile.T.astype(BF16)
        else:
            out_refs[0][:, s * SEC:(s + 1) * SEC] = jax.nn.sigmoid(acc)


def _proj(xn, w2, mode, tm, r_out=None, kt=KEY_TILE, scale=1.0):
    r, d = xn.shape
    r_out = r if r_out is None else r_out
    in_specs = [pl.BlockSpec((tm, d), lambda i: (i, 0)),
                pl.BlockSpec((d, 2 * SEC), lambda i: (0, 0))]
    row_spec = pl.BlockSpec((tm, SEC), lambda i: (i, 0))
    f32_out = jax.ShapeDtypeStruct((r_out, SEC), F32)
    bf16_out = jax.ShapeDtypeStruct((r, SEC), BF16)
    if mode == "q":
        out_specs = [row_spec, row_spec]
        out_shape = [bf16_out, bf16_out]
    elif mode == "k":
        out_specs = [row_spec] * 4
        out_shape = [f32_out, bf16_out, f32_out, bf16_out]
    elif mode == "v":
        t_spec = pl.BlockSpec((SEC // LANES, tm // kt, LANES, kt), lambda i: (0, i, 0, 0))
        t_out = jax.ShapeDtypeStruct((SEC // LANES, r // kt, LANES, kt), BF16)
        out_specs = [row_spec, t_spec, row_spec, t_spec]
        out_shape = [f32_out, t_out, f32_out, t_out]
    else:
        out_specs = [pl.BlockSpec((tm, 2 * SEC), lambda i: (i, 0))]
        out_shape = [jax.ShapeDtypeStruct((r, 2 * SEC), F32)]
    return pl.pallas_call(
        functools.partial(_proj_kernel, mode=mode, scale=scale, kt=kt),
        grid=(r // tm,), in_specs=in_specs, out_specs=out_specs, out_shape=out_shape,
        compiler_params=_cparams(("parallel",)),
        name="proj_" + mode,
    )(xn, w2)


def _cache_tiles_kernel(kc_ref, vc_ref, kn_ref, vn_ref, k_out, vt_out, *, kt, keys_minor):
    j = pl.program_id(1)
    n_cache = pl.num_programs(1) - 1
    nhb = SEC // LANES

    @pl.when(j < n_cache)
    def _():
        for hb in range(nhb):
            cols = slice(hb * LANES, (hb + 1) * LANES)
            if keys_minor:
                k_out[0, :, cols] = kc_ref[0, cols, :].T.astype(BF16)
                vt_out[0, hb, 0] = vc_ref[0, cols, :].astype(BF16)
            else:
                k_out[0, :, cols] = kc_ref[0, :, hb, :].astype(BF16)
                vt_out[0, hb, 0] = vc_ref[0, :, hb, :].T.astype(BF16)

    @pl.when(j == n_cache)
    def _():
        s_len = kn_ref.shape[1]
        pad = jnp.zeros((kt - s_len, SEC), F32)
        k_out[0] = jnp.concatenate([kn_ref[0], pad], axis=0).astype(BF16)
        v_new = jnp.concatenate([vn_ref[0], pad], axis=0)
        for hb in range(nhb):
            vt_out[0, hb, 0] = v_new[:, hb * LANES:(hb + 1) * LANES].T.astype(BF16)


def _cache_tiles(k_cache, v_cache, k_new, v_new, kt, keys_minor):
    b = k_cache.shape[0]
    past = k_cache.shape[2] if keys_minor else k_cache.shape[1]
    s_len = k_new.shape[1]
    assert past % kt == 0 and s_len <= kt
    n_cache = past // kt
    if keys_minor:
        cache_spec = pl.BlockSpec((1, SEC, kt), lambda i, j: (i, 0, jnp.minimum(j, n_cache - 1)))
    else:
        cache_spec = pl.BlockSpec((1, kt, SEC // LANES, LANES),
                                  lambda i, j: (i, jnp.minimum(j, n_cache - 1), 0, 0))
    new_spec = pl.BlockSpec((1, s_len, SEC), lambda i, j: (i, 0, 0))
    return pl.pallas_call(
        functools.partial(_cache_tiles_kernel, kt=kt, keys_minor=keys_minor),
        grid=(b, n_cache + 1),
        in_specs=[cache_spec, cache_spec, new_spec, new_spec],
        out_specs=[pl.BlockSpec((1, kt, SEC), lambda i, j: (i, j, 0)),
                   pl.BlockSpec((1, SEC // LANES, 1, LANES, kt), lambda i, j: (i, 0, j, 0, 0))],
        out_shape=[jax.ShapeDtypeStruct((b, past + kt, SEC), BF16),
                   jax.ShapeDtypeStruct((b, SEC // LANES, n_cache + 1, LANES, kt), BF16)],
        compiler_params=_cparams(("parallel", "arbitrary")),
        name="cache_tiles",
    )(k_cache, v_cache, k_new, v_new)


def _sb_kernel(q_ref, k_ref, vt_ref, tri_ref, o_ref, acc_ref, *, tq, kt, q_pos0):
    qi = pl.program_id(2)
    q0 = q_pos0 + qi * tq
    jm = q0 // kt
    q = q_ref[0]
    lane = lax.broadcasted_iota(jnp.int32, (tq, LANES), 1)
    kpos_l = lax.broadcasted_iota(jnp.int32, (kt, tq), 0)
    qpos_l = lax.broadcasted_iota(jnp.int32, (kt, tq), 1)
    tri = tri_ref[...]
    zero = jnp.zeros_like(q)
    qz = (jnp.where(lane < DH_SB, q, zero), jnp.where(lane >= DH_SB, q, zero))
    acc_ref[...] = jnp.zeros_like(acc_ref)

    def tile(j, hh, carry, masked):
        off = pl.multiple_of(j * kt, kt)
        kk = k_ref[0, pl.ds(off, kt), :]
        z = lax.dot_general(kk, qz[hh], (((1,), (1,)), ((), ())),
                            preferred_element_type=F32)
        sp = jnp.maximum(z, 0.0) + jnp.log(1.0 + jnp.exp(-jnp.abs(z)))
        if masked:
            vis = (kpos_l + j * kt) < (qpos_l + q0)
            lk = jnp.where(vis, -sp, 0.0)
        else:
            lk = -sp
        later = jnp.dot(tri, lk.astype(BF16), preferred_element_type=F32)
        w = jnp.exp(z + lk + later + carry)
        if masked:
            w = jnp.where(vis, w, 0.0)
        acc_ref[hh] += jnp.dot(vt_ref[0, 0, j], w.astype(BF16), preferred_element_type=F32)
        return jnp.sum(lk, axis=0, keepdims=True)

    none_later = jnp.zeros((1, tq), F32)
    carries = tuple(tile(jm, hh, none_later, True) for hh in range(2))

    def pair(j0, ca, cb):
        j1 = j0 - 1
        out = []
        for hh, carry in ((0, ca), (1, cb)):
            s0 = tile(jnp.maximum(j0, 0), hh, jnp.where(j0 >= 0, carry, -SB_ABSENT), False)
            s1 = tile(jnp.maximum(j1, 0), hh, jnp.where(j1 >= 0, carry + s0, -SB_ABSENT), False)
            out.append(carry + s0 + s1)
        return out

    jb = jm - 1
    first = []
    for hh in range(2):
        s = tile(jnp.maximum(jb, 0), hh, jnp.where(jb >= 0, carries[hh], -SB_ABSENT), False)
        first.append(carries[hh] + s)

    def cond(state):
        t, ca, cb = state
        return jnp.logical_and(jm - 2 - 2 * t >= 0,
                               jnp.max(jnp.maximum(ca, cb)) > -SB_SKIP_LOG)

    def body(state):
        t, ca, cb = state
        ca, cb = pair(jm - 2 - 2 * t, ca, cb)
        return t + 1, ca, cb

    lax.while_loop(cond, body, (jnp.int32(0), first[0], first[1]))

    row = lax.broadcasted_iota(jnp.int32, (LANES, tq), 0)
    o_t = jnp.where(row < DH_SB, acc_ref[0], acc_ref[1])
    o_ref[0] = o_t.T.astype(o_ref.dtype)


def _sb_attention(q, k, vt, tri, *, batch, nq, tq, kt, q_pos0):
    rk = k.shape[1]
    nk = rk // kt
    nhp = SEC // LANES
    return pl.pallas_call(
        functools.partial(_sb_kernel, tq=tq, kt=kt, q_pos0=q_pos0),
        grid=(batch, nhp, nq),
        in_specs=[
            pl.BlockSpec((1, tq, LANES), lambda b, h, i: (b, i, h)),
            pl.BlockSpec((1, rk, LANES), lambda b, h, i: (b, 0, h)),
            pl.BlockSpec((1, 1, nk, LANES, kt), lambda b, h, i: (b, h, 0, 0, 0)),
            pl.BlockSpec((kt, kt), lambda b, h, i: (0, 0)),
        ],
        out_specs=pl.BlockSpec((1, tq, LANES), lambda b, h, i: (b, i, h)),
        out_shape=jax.ShapeDtypeStruct((batch, nq * tq, SEC), BF16),
        scratch_shapes=[pltpu.VMEM((2, LANES, tq), F32)],
        compiler_params=_cparams(("parallel", "parallel", "arbitrary")),
        name="sb_attention",
    )(q, k, vt, tri)


def _diff_kernel(slopes_ref, lam_ref, g_ref, q_ref, k_ref, vt_ref, o_ref,
                 acc_ref, bt_ref, s0_ref, s1_ref, p0_ref, p1_ref, kn_ref,
                 *, tq, kt, q_pos0, coff, n_keys, nk, q_rows, early_stop):
    h = pl.program_id(1)
    qi = pl.program_id(2)
    slope = slopes_ref[0, h]
    inv_slope = slopes_ref[1, h]
    q0 = q_pos0 + qi * tq
    c_lo = (q0 - coff + CHUNK) // CHUNK - 1
    c_hi = (q0 + tq - 1 - coff + CHUNK) // CHUNK - 1
    full_end = jnp.minimum(coff + CHUNK * (c_lo + 1), n_keys)
    vis_end = jnp.minimum(coff + CHUNK * (c_hi + 1), n_keys)
    n_full = full_end // kt
    n_vis = jnp.minimum((vis_end + kt - 1) // kt, nk)

    q = q_ref[0]
    qrow = lax.broadcasted_iota(jnp.int32, (tq, LANES), 0)
    q = jnp.where(qrow < q_rows - qi * tq, q, jnp.zeros_like(q))
    lane = lax.broadcasted_iota(jnp.int32, (tq, LANES), 1)
    kpos_l = lax.broadcasted_iota(jnp.int32, (kt, tq), 0)
    qpos_l = lax.broadcasted_iota(jnp.int32, (kt, tq), 1)
    gsz = DIFF_GROUP
    m_tiles = (n_full // gsz) * gsz
    n_fullg = m_tiles // gsz
    n_groups = n_fullg + DIFF_TAIL_GROUPS
    n_tail = DIFF_TAIL_GROUPS * gsz
    absent = gsz + n_tail

    for u in range(gsz):
        bt_ref[u] = slope * (kpos_l + u * kt).astype(F32)
    for u in range(n_tail):
        kpos = kpos_l + (m_tiles + u) * kt
        qpos = qpos_l + q0
        vis = (((kpos - coff + CHUNK) >> CHUNK_SHIFT) <= ((qpos - coff + CHUNK) >> CHUNK_SHIFT)) \
            & (kpos < n_keys)
        bias = slope * (qpos_l - jnp.abs(qpos - kpos)).astype(F32)
        bt_ref[gsz + u] = jnp.where(vis, bias, NEG_INF)
    bt_ref[absent] = jnp.full((kt, tq), NEG_INF, F32)

    zero = jnp.zeros_like(q)
    qz = (jnp.where(lane < DH_DIFF, q, zero), jnp.where(lane >= DH_DIFF, q, zero))
    acc_ref[...] = jnp.zeros_like(acc_ref)
    p1_ref[...] = jnp.zeros_like(p1_ref)

    if early_stop:
        hr = lax.broadcasted_iota(jnp.int32, (LANES, LANES), 0)
        hc = lax.broadcasted_iota(jnp.int32, (LANES, LANES), 1)
        half = (((hc == 0) & (hr < DH_DIFF)) | ((hc == 1) & (hr >= DH_DIFF))).astype(BF16)
        lane1 = lax.broadcasted_iota(jnp.int32, (1, LANES), 1)

        def max_half_norms(sq_max):
            nrm = jnp.sqrt(sq_max)
            return (jnp.max(jnp.where(lane1 == 0, nrm, 0.0)), jnp.max(jnp.where(lane1 == 1, nrm, 0.0)))

        @pl.when(qi == 0)
        def _():
            def body(j, mx):
                off = pl.multiple_of(j * kt, kt)
                kk = k_ref[0, pl.ds(off, kt), :]
                n2 = jnp.dot(kk * kk, half, preferred_element_type=F32)
                return jnp.maximum(mx, jnp.max(n2, axis=0, keepdims=True))
            kn = max_half_norms(lax.fori_loop(0, nk, body, jnp.zeros((1, LANES), F32)))
            kn_ref[0] = kn[0]
            kn_ref[1] = kn[1]

        qn = max_half_norms(jnp.max(jnp.dot(q * q, half, preferred_element_type=F32), axis=0, keepdims=True))
        qk_bound = [DIFF_NORM_SLACK * qn[mm] * kn_ref[mm] for mm in range(2)]
        q0_f = q0.astype(F32)

        def skip_below(state):
            need = jnp.maximum((DIFF_SKIP_LOG + qk_bound[0]) - state[0],
                               (DIFF_SKIP_LOG + qk_bound[1]) - state[2])
            return q0_f - jnp.max(need) * inv_slope
    else:
        def skip_below(state):
            return jnp.float32(NEG_INF)

    def seq_group(n):
        return n_groups - 1 - n

    def needed(g, thr):
        top = ((g + 1) * (gsz * kt)).astype(F32)
        return jnp.logical_and(g >= 0, jnp.logical_or(g >= n_fullg, top > thr))

    def tiles_of(g):
        return [jnp.clip(g * gsz + u, 0, nk - 1) for u in range(gsz)]

    def scores_stage(g, s_ref):
        js = tiles_of(g)
        for mm in range(2):
            for u in range(gsz):
                off = pl.multiple_of(js[u] * kt, kt)
                kk = k_ref[0, pl.ds(off, kt), :]
                s_ref[mm, u] =lax.dot_general(kk, qz[mm], (((1,), (1,)), ((), ())),
                                               preferred_element_type=F32)

    def softmax_stage(g, thr, s_ref, p_ref, state):
        ok = needed(g, thr)
        full = jnp.logical_and(ok, g < n_fullg)
        sj = jnp.where(full, slope * (g * gsz * kt - q0).astype(F32), 0.0)
        bidx = [jnp.where(full, u, jnp.where(ok, gsz + g * gsz + u - m_tiles, absent)) for u in range(gsz)]
        new_state, alphas = [], []
        for mm in range(2):
            m, l = state[2 * mm], state[2 * mm + 1]
            ss = [s_ref[mm, u] + bt_ref[bidx[u]] for u in range(gsz)]
            cmax = jnp.max(ss[0], axis=0, keepdims=True)
            for s in ss[1:]:
                cmax = jnp.maximum(cmax, jnp.max(s, axis=0, keepdims=True))
            m_new = jnp.maximum(m, cmax + sj)
            m_safe = jnp.where(m_new == NEG_INF, 0.0, m_new)
            alpha = jnp.exp(m - m_safe)
            r = m_safe - sj
            psum = jnp.zeros((1, tq), F32)
            for u in range(gsz):
                p = jnp.exp(ss[u] - r)
                psum = psum + jnp.sum(p, axis=0, keepdims=True)
                p_ref[mm, u] = p.astype(BF16)
            new_state += [m_new, alpha * l + psum]
            alphas.append(alpha)
        return tuple(new_state), tuple(alphas)

    def value_stage(g, p_ref, alphas):
        js = tiles_of(g)
        for mm in range(2):
            pv = jnp.dot(vt_ref[0, 0, js[0]], p_ref[mm, 0], preferred_element_type=F32)
            for u in range(1, gsz):
                pv = pv + jnp.dot(vt_ref[0, 0, js[u]], p_ref[mm, u], preferred_element_type=F32)
            acc_ref[mm] = alphas[mm] * acc_ref[mm] + pv

    def trip(carry):
        t, thr, state, alpha_prev = carry
        n0 = 2 * t
        scores_stage(seq_group(n0 + 1), s1_ref)
        state, alpha0 = softmax_stage(seq_group(n0), thr, s0_ref, p0_ref, state)
        value_stage(seq_group(n0 - 1), p1_ref, alpha_prev)
        scores_stage(seq_group(n0 + 2), s0_ref)
        state, alpha1 = softmax_stage(seq_group(n0 + 1), thr, s1_ref, p1_ref, state)
        value_stage(seq_group(n0), p0_ref, alpha0)
        return t + 1, skip_below(state), state, alpha1

    def more(carry):
        t, thr = carry[0], carry[1]
        return needed(seq_group(2 * t), thr)

    neg = jnp.full((1, tq), NEG_INF, F32)
    zer = jnp.zeros((1, tq), F32)
    one = jnp.ones((1, tq), F32)
    scores_stage(seq_group(0), s0_ref)
    n_done, _, state, alpha_last = lax.while_loop(
        more, trip, (jnp.int32(0), jnp.float32(NEG_INF), (neg, zer, neg, zer), (one, one)))
    value_stage(seq_group(2 * n_done - 1), p1_ref, alpha_last)
    stats = [(state[0], state[1]), (state[2], state[3])]

    lp = lam_ref[...]
    lam = (jnp.exp(jnp.sum(lp[0:1] * lp[1:2], axis=-1, keepdims=True))
           - jnp.exp(jnp.sum(lp[2:3] * lp[3:4], axis=-1, keepdims=True)) + LAMBDA_INIT)
    l1 = stats[0][1]
    l2 = stats[1][1]
    l1 = jnp.where(l1 == 0.0, 1.0, l1)
    l2 = jnp.where(l2 == 0.0, 1.0, l2)
    o_t = acc_ref[0] / l1 - lam * (acc_ref[1] / l2)
    o = o_t.T
    y = o * lax.rsqrt(jnp.mean(o * o, axis=-1, keepdims=True) + NORM_EPS)
    o_ref[0] = ((y * g_ref[...]) * (1.0 - LAMBDA_INIT)).astype(o_ref.dtype)


def _diff_scratch(tq, kt):
    return [pltpu.VMEM((2, LANES, tq), F32),
            pltpu.VMEM((DIFF_GROUP * (1 + DIFF_TAIL_GROUPS) + 1, kt, tq), F32),
            pltpu.VMEM((2, DIFF_GROUP, kt, tq), F32),
            pltpu.VMEM((2, DIFF_GROUP, kt, tq), F32),
            pltpu.VMEM((2, DIFF_GROUP, kt, tq), BF16),
            pltpu.VMEM((2, DIFF_GROUP, kt, tq), BF16),
            pltpu.SMEM((2,), F32)]


def _diff_param_specs():
    return [pl.BlockSpec(memory_space=pltpu.SMEM),
            pl.BlockSpec((4, DH_DIFF), lambda b, h, i: (0, 0)),
            pl.BlockSpec((1, DV_DIFF), lambda b, h, i: (0, 0))]


def _diff_attention(q, k, vt, slopes, lam_p, subln_g, *, batch, nq, tq, kt, q_pos0, coff, n_keys,
                    early_stop):
    rk = k.shape[1]
    nk = rk // kt
    q_rows = q.shape[1]
    assert (nq - 1) * tq < q_rows <= nq * tq
    return pl.pallas_call(
        functools.partial(_diff_kernel, tq=tq, kt=kt, q_pos0=q_pos0, coff=coff, n_keys=n_keys, nk=nk,
                          q_rows=q_rows, early_stop=early_stop),
        grid=(batch, SEC // LANES, nq),
        in_specs=_diff_param_specs() + [
            pl.BlockSpec((1, tq, LANES), lambda b, h, i: (b, i, h)),
            pl.BlockSpec((1, rk, LANES), lambda b, h, i: (b, 0, h)),
            pl.BlockSpec((1, 1, nk, LANES, kt), lambda b, h, i: (b, h, 0, 0, 0)),
        ],
        out_specs=pl.BlockSpec((1, tq, LANES), lambda b, h, i: (b, i, h)),
        out_shape=jax.ShapeDtypeStruct((batch, q_rows, SEC), BF16),
        scratch_shapes=_diff_scratch(tq, kt),
        compiler_params=_cparams(("parallel", "parallel", "arbitrary")),
        name="diff_attention",
    )(slopes, lam_p, subln_g.reshape(1, DV_DIFF), q, k, vt)


def _merge_kernel(h_ref, osb_ref, od_ref, gt_ref, wsb_ref, wd_ref, wo_ref, gn_ref, wr_ref, br_ref,
                  h2_ref, hn_ref, te_ref, tg_ref):
    y_sb = jnp.dot(osb_ref[...], wsb_ref[...], preferred_element_type=F32)
    y_d = jnp.dot(od_ref[...], wd_ref[...], preferred_element_type=F32)
    gt = gt_ref[...]
    mix = gt[:, :D_MODEL] * y_sb + gt[:, D_MODEL:] * y_d
    h2 = h_ref[...] + jnp.dot(mix.astype(BF16), wo_ref[...], preferred_element_type=F32)
    h2_ref[...] = h2
    hn = (h2 * lax.rsqrt(jnp.mean(h2 * h2, axis=-1, keepdims=True) + NORM_EPS)) * gn_ref[...]
    hn_ref[...] = hn
    logits = jnp.dot(hn, wr_ref[...], preferred_element_type=F32,
                     precision=lax.Precision.HIGHEST) + br_ref[...]
    tm = logits.shape[0]
    lane = lax.broadcasted_iota(jnp.int32, (tm, LANES), 1).astype(F32)
    tops, idxs = [], []
    l = logits
    for _ in range(TOP_K):
        m = jnp.max(l, axis=-1, keepdims=True)
        idx = jnp.min(jnp.where(l == m, lane, float(LANES)), axis=-1, keepdims=True)
        tops.append(m)
        idxs.append(idx)
        l = jnp.where(lane == idx, NEG_INF, l)
    ex = [jnp.exp(t - tops[0]) for t in tops]
    den = ex[0] + ex[1] + ex[2] + ex[3]
    te = jnp.zeros((tm, LANES), F32)
    tg = jnp.zeros((tm, LANES), F32)
    for k in range(TOP_K):
        te = jnp.where(lane == float(k), idxs[k], te)
        tg = jnp.where(lane == float(k), ex[k] / den, tg)
    te_ref[...] = te.astype(jnp.int32)
    tg_ref[...] = tg


def _merge(h, o_sb, o_d, gates, w_sb, w_d, w_o, g_ffn, w_r, b_r, tm):
    r, d = h.shape
    row = lambda w: pl.BlockSpec((tm, w), lambda i: (i, 0))
    full = lambda a, b: pl.BlockSpec((a, b), lambda i: (0, 0))
    return pl.pallas_call(
        _merge_kernel,
        grid=(r // tm,),
        in_specs=[row(d), row(d), row(d), row(2 * d), full(d, d), full(d, d), full(d, d),
                  full(1, d), full(d, LANES), full(1, LANES)],
        out_specs=[row(d), row(d), row(LANES), row(LANES)],
        out_shape=[jax.ShapeDtypeStruct((r, d), F32), jax.ShapeDtypeStruct((r, d), F32),
                   jax.ShapeDtypeStruct((r, LANES), jnp.int32), jax.ShapeDtypeStruct((r, LANES), F32)],
        compiler_params=_cparams(("parallel",)),
        name="merge_router",
    )(h, o_sb, o_d, gates, w_sb, w_d, w_o, g_ffn.reshape(1, d), w_r, b_r)


def _moe_kernel(be_ref, nv_ref, x_ref, wgu_ref, bgu_ref, wdn_ref, bdn_ref, o_ref, wgu_bf, wdn_bf):
    b = pl.program_id(0)
    used = nv_ref[b] > 0
    new_expert = jnp.logical_or(b == 0, be_ref[b] != be_ref[jnp.maximum(b - 1, 0)])

    @pl.when(jnp.logical_and(used, new_expert))
    def _():
        wgu_bf[...] = wgu_ref[0].astype(BF16)
        wdn_bf[...] = wdn_ref[0].astype(BF16)

    @pl.when(used)
    def _():
        row = lax.broadcasted_iota(jnp.int32, x_ref.shape, 0)
        x = jnp.where(row < nv_ref[b], x_ref[...], 0.0).astype(BF16)
        gu = jnp.dot(x, wgu_bf[...], preferred_element_type=F32) + bgu_ref[0]
        g = jnp.minimum(gu[:, :D_FF], SWIGLU_LIMIT)
        u = jnp.clip(gu[:, D_FF:], -SWIGLU_LIMIT, SWIGLU_LIMIT)
        act = (u + 1.0) * (g * jax.nn.sigmoid(SWIGLU_ALPHA * g))
        o_ref[...] = jnp.dot(act.astype(BF16), wdn_bf[...], preferred_element_type=F32) + bdn_ref[0]

    @pl.when(jnp.logical_not(used))
    def _():
        o_ref[...] = jnp.zeros_like(o_ref)


def _moe_gmm(block_e, n_valid, xs, w_gu, b_gu, w_dn, b_dn, bm):
    nb = block_e.shape[0]
    n_rows = nb * bm
    d = xs.shape[1]
    grid_spec = pltpu.PrefetchScalarGridSpec(
        num_scalar_prefetch=2,
        grid=(nb,),
        in_specs=[
            pl.BlockSpec((bm, d), lambda b, be, nu: (b, 0)),
            pl.BlockSpec((1, d, 2 * D_FF), lambda b, be, nu: (be[b], 0, 0)),
            pl.BlockSpec((1, 1, 2 * D_FF), lambda b, be, nu: (be[b], 0, 0)),
            pl.BlockSpec((1, D_FF, d), lambda b, be, nu: (be[b], 0, 0)),
            pl.BlockSpec((1, 1, d), lambda b, be, nu: (be[b], 0, 0)),
        ],
        out_specs=pl.BlockSpec((bm, d), lambda b, be, nu: (b, 0)),
        scratch_shapes=[pltpu.VMEM((d, 2 * D_FF), BF16), pltpu.VMEM((D_FF, d), BF16)],
    )
    return pl.pallas_call(
        _moe_kernel,
        grid_spec=grid_spec,
        out_shape=jax.ShapeDtypeStruct((n_rows, d), F32),
        compiler_params=_cparams(("arbitrary",)),
        name="moe_experts",
    )(block_e, n_valid, xs, w_gu, b_gu, w_dn, b_dn)


def _final_kernel(h_ref, *refs):
    c_refs, (tg_ref, g_ref, o_ref) = refs[:TOP_K], refs[TOP_K:]
    tg = tg_ref[...]
    y = c_refs[0][...] * tg[:, 0:1]
    for k in range(1, TOP_K):
        y = y + c_refs[k][...] * tg[:, k:k + 1]
    x = h_ref[...] + y
    o_ref[...] = (x * lax.rsqrt(jnp.mean(x * x, axis=-1, keepdims=True) + NORM_EPS)) * g_ref[...]


def _final(h2, contrib, tg, g, tm, slot_rows, c_row0=0):
    r, d = h2.shape

    def slot_spec(k):
        blk0 = (k * slot_rows + c_row0) // tm
        return pl.BlockSpec((tm, d), lambda i: (i + blk0, 0))

    return pl.pallas_call(
        _final_kernel,
        grid=(r // tm,),
        in_specs=[pl.BlockSpec((tm, d), lambda i: (i, 0))] + [slot_spec(k) for k in range(TOP_K)]
        + [pl.BlockSpec((tm, LANES), lambda i: (i, 0)), pl.BlockSpec((1, d), lambda i: (0, 0))],
        out_specs=pl.BlockSpec((tm, d), lambda i: (i, 0)),
        out_shape=jax.ShapeDtypeStruct((r, d), F32),
        compiler_params=_cparams(("parallel",)),
        name="combine_final_norm",
    )(h2, *([contrib] * TOP_K), tg, g.reshape(1, d))


def _rank_kernel(te_ref, tril_ref, rank_ref, cnt_ref, run_ref):
    @pl.when(pl.program_id(0) == 0)
    def _():
        run_ref[...] = jnp.zeros_like(run_ref)

    te = te_ref[...]
    tm = te.shape[0]
    lane = lax.broadcasted_iota(jnp.int32, (tm, LANES), 1)
    onehots = [(lane == te[:, k:k + 1]).astype(F32) for k in range(TOP_K)]
    member = onehots[0] + onehots[1] + onehots[2] + onehots[3]
    before = jnp.dot(tril_ref[...], member.astype(BF16), preferred_element_type=F32) + run_ref[...]
    rank = jnp.zeros((tm, LANES), F32)
    for k in range(TOP_K):
        r_k = jnp.sum(onehots[k] * before, axis=-1, keepdims=True)
        rank = jnp.where(lane == k, r_k, rank)
    rank_ref[...] = rank.astype(jnp.int32)
    run_ref[...] += jnp.sum(member, axis=0, keepdims=True)
    cnt_ref[...] = run_ref[...]


def _ranks(te, tm):
    r = te.shape[0]
    i = lax.broadcasted_iota(jnp.int32, (tm, tm), 0)
    j = lax.broadcasted_iota(jnp.int32, (tm, tm), 1)
    tril = (j < i).astype(BF16)
    return pl.pallas_call(
        _rank_kernel,
        grid=(r // tm,),
        in_specs=[pl.BlockSpec((tm, LANES), lambda b: (b, 0)), pl.BlockSpec((tm, tm), lambda b: (0, 0))],
        out_specs=[pl.BlockSpec((tm, LANES), lambda b: (b, 0)), pl.BlockSpec((1, LANES), lambda b: (0, 0))],
        out_shape=[jax.ShapeDtypeStruct((r, LANES), jnp.int32), jax.ShapeDtypeStruct((1, LANES), F32)],
        scratch_shapes=[pltpu.VMEM((1, LANES), F32)],
        compiler_params=_cparams(("arbitrary",)),
        name="route_ranks",
    )(te, tril)


def _route(te, n_rows, bm):
    rank, cnt = _ranks(te, Q_TILE_PROMPT)
    counts = cnt[0, :N_EXPERTS].astype(jnp.int32)
    padded = ((counts + bm - 1) // bm) * bm
    pend = jnp.cumsum(padded)
    pstart = pend - padded
    top_e = te[:, :TOP_K]
    onehot = top_e[:, :, None] == jnp.arange(N_EXPERTS, dtype=jnp.int32)[None, None, :]
    pos = jnp.sum(jnp.where(onehot, pstart[None, None, :], 0), axis=-1) + rank[:, :TOP_K]
    nb = n_rows // bm
    block_start = jnp.arange(nb, dtype=jnp.int32) * bm
    block_e = jnp.minimum(jnp.sum((pend[None, :] <= block_start[:, None]).astype(jnp.int32), axis=1),
                          N_EXPERTS - 1)
    e_hot = block_e[:, None] == jnp.arange(N_EXPERTS, dtype=jnp.int32)[None, :]
    last = jnp.sum(jnp.where(e_hot, (pstart + counts)[None, :], 0), axis=1)
    n_valid = jnp.clip(last - block_start, 0, bm).astype(jnp.int32)
    return pos, block_e.astype(jnp.int32), n_valid


def _sc_scatter_rows(src, idx_t, n_out):
    n_src, d = src.shape
    sub = SC_GATHER_WINDOW // SC_SUB_WINDOWS
    n_tiles = n_src // SC_GATHER_WINDOW
    idx_t = idx_t.reshape(1, TOP_K * n_src)
    mesh = plsc.VectorSubcoreMesh(core_axis_name="c", subcore_axis_name="s")

    @pl.kernel(out_type=jax.ShapeDtypeStruct((n_out, d), src.dtype), mesh=mesh)
    def scatter_kernel(x_hbm, i_hbm, o_hbm):
        def body(x_vmem, i_vmem):
            s = pl.program_id(2)
            pltpu.sync_copy(x_vmem, o_hbm.at[i_vmem.at[0, pl.ds(s * sub, sub)]])

        pltpu.emit_pipeline(
            body,
            grid=(n_tiles, TOP_K, SC_SUB_WINDOWS),
            in_specs=[pl.BlockSpec((sub, d), index_map=lambda i, k, s: (i * SC_SUB_WINDOWS + s, 0)),
                      pl.BlockSpec((1, SC_GATHER_WINDOW), index_map=lambda i, k, s: (0, k * n_tiles + i))],
            out_specs=[],
            core_axis_name=("c", "s"),
            dimension_semantics=(pltpu.PARALLEL, pltpu.ARBITRARY, pltpu.ARBITRARY),
        )(x_hbm, i_hbm)

    return scatter_kernel(src, idx_t)


def _sc_gather(table, idx):
    n_idx = idx.shape[0]
    d = table.shape[1]
    sub = SC_GATHER_WINDOW // SC_SUB_WINDOWS
    mesh = plsc.VectorSubcoreMesh(core_axis_name="c", subcore_axis_name="s")

    @pl.kernel(out_type=jax.ShapeDtypeStruct((n_idx, d), table.dtype), mesh=mesh)
    def gather_kernel(x_hbm, i_hbm, o_hbm):
        def body(i_vmem, o_vmem):
            j = pl.program_id(1)
            pltpu.sync_copy(x_hbm.at[i_vmem.at[0, pl.ds(j * sub, sub)]], o_vmem)

        pltpu.emit_pipeline(
            body,
            grid=(n_idx // SC_GATHER_WINDOW, SC_SUB_WINDOWS),
            in_specs=[pl.BlockSpec((1, SC_GATHER_WINDOW), index_map=lambda i, j: (0, i))],
            out_specs=[pl.BlockSpec((sub, d), index_map=lambda i, j: (i * SC_SUB_WINDOWS + j, 0))],
            core_axis_name=("c", "s"),
            dimension_semantics=(pltpu.PARALLEL, pltpu.ARBITRARY),
        )(i_hbm, o_hbm)

    return gather_kernel(table, idx.reshape(1, n_idx))


def _tri(kt):
    s = lax.broadcasted_iota(jnp.int32, (kt, kt), 0)
    j = lax.broadcasted_iota(jnp.int32, (kt, kt), 1)
    return (j > s).astype(BF16)


def kernel(x_prompt, x_sample, cache_sb_k, cache_sb_v, cache_diff_k, cache_diff_v, meta_tokens,
           norm_mix_g, w_in, diff_lambda, diff_subln_g, w_br_sb, w_br_diff, w_out, norm_ffn_g,
           w_router, b_router, w_gate_up, b_gate_up, w_down, b_down, final_norm_g):
    assert x_prompt.shape[0] == 1 and w_in.shape[0] == 1
    d = D_MODEL
    seq = x_prompt.shape[1]
    t = N_META + seq
    tq_p, kt = Q_TILE_PROMPT, KEY_TILE
    tp = -(-t // tq_p) * tq_p
    nq_p = tp // tq_p
    nb_s, s_len = x_sample.shape[:2]
    past = cache_sb_k.shape[2]
    n_s = nb_s * s_len
    scale = DH_SB ** -0.5

    w0 = w_in[0]
    wq = jnp.concatenate([w0[:, 0:SEC], w0[:, 3 * SEC:4 * SEC]], axis=1).astype(BF16)
    wk = jnp.concatenate([w0[:, SEC:2 * SEC], w0[:, 4 * SEC:5 * SEC]], axis=1).astype(BF16)
    wv = jnp.concatenate([w0[:, 2 * SEC:3 * SEC], w0[:, 5 * SEC:6 * SEC]], axis=1).astype(BF16)
    wg = w0[:, 6 * SEC:8 * SEC].astype(BF16)
    w_sb = w_br_sb[0].astype(BF16)
    w_d = w_br_diff[0].astype(BF16)
    w_o = w_out[0].astype(BF16)
    w_r = jnp.pad(w_router[0], ((0, 0), (0, LANES - N_EXPERTS)))
    b_r = jnp.pad(b_router[0], (0, LANES - N_EXPERTS), constant_values=NEG_INF).reshape(1, LANES)
    w_gu = w_gate_up[0]
    w_dn = w_down[0]
    b_gu = b_gate_up[0].reshape(N_EXPERTS, 1, 2 * D_FF)
    b_dn = b_down[0].reshape(N_EXPERTS, 1, d)
    slope = jnp.exp2(-8.0 * jnp.arange(1, H_DIFF + 1, dtype=F32) / H_DIFF)
    slopes = jnp.stack([slope, 1.0 / slope])
    tri = _tri(kt)
    lam_p = diff_lambda[0]
    g_sub = diff_subln_g[0]

    hp = jnp.concatenate([meta_tokens.astype(F32), x_prompt[0], jnp.zeros((tp - t, d), F32)], axis=0)
    tm_p = tq_p
    xn = _rmsnorm(hp, norm_mix_g[0], tm_p)
    q_sb, q_d = _proj(xn, wq, "q", tm_p, scale=scale)
    kf_sb, kb_sb, kf_d, kb_d = _proj(xn, wk, "k", tm_p, r_out=t)
    vf_sb, vt_sb, vf_d, vt_d = _proj(xn, wv, "v", tm_p, r_out=t)
    (gates,) = _proj(xn, wg, "g", tm_p)
    o_sb = _sb_attention(q_sb[None], kb_sb[None], vt_sb[None], tri, batch=1, nq=nq_p, tq=tq_p, kt=kt,
                         q_pos0=0)
    o_d = _diff_attention(q_d[None], kb_d[None], vt_d[None], slopes, lam_p, g_sub, batch=1,
                          nq=-(-tp // Q_TILE_DIFF), tq=Q_TILE_DIFF, kt=kt, q_pos0=0, coff=N_META, n_keys=t,
                          early_stop=True)
    tm_merge = 640 if tp % 640 == 0 else tq_p
    h2, hn, te, tg = _merge(hp, o_sb[0], o_d[0], gates, w_sb, w_d, w_o, norm_ffn_g[0], w_r, b_r, tm_merge)

    tq_s = Q_TILE_DECODE
    hs = x_sample.reshape(n_s, d)
    xn_s = _rmsnorm(hs, norm_mix_g[0], n_s)
    q2_s = _proj(xn_s, wq, "q", n_s, scale=scale)
    kf_s0, _, kf_s1, _ = _proj(xn_s, wk, "k", n_s)
    vf_s0, _, vf_s1, _ = _proj(xn_s, wv, "v", n_s, kt=n_s)
    kf_s = (kf_s0, kf_s1)
    vf_s = (vf_s0, vf_s1)
    (gates_s,) = _proj(xn_s, wg, "g", n_s)

    def pad_q(qs):
        return jnp.pad(qs.reshape(nb_s, s_len, SEC), ((0, 0), (0, tq_s - s_len), (0, 0)))

    def per_stream(a):
        return a.reshape(nb_s, -1, SEC)

    def keys_last(cache):
        return jnp.transpose(cache, (0, 2, 3, 1)).reshape(nb_s, SEC, past)

    k_sb_all, vt_sb_all = _cache_tiles(keys_last(cache_sb_k[0]), keys_last(cache_sb_v[0]),
                                       per_stream(kf_s[0]), per_stream(vf_s[0]), kt, True)
    k_d_all, vt_d_all = _cache_tiles(cache_diff_k[0], cache_diff_v[0], per_stream(kf_s[1]),
                                     per_stream(vf_s[1]), kt, False)
    o_sb_s = _sb_attention(pad_q(q2_s[0]), k_sb_all, vt_sb_all, tri, batch=nb_s, nq=1, tq=tq_s, kt=kt,
                           q_pos0=past)
    o_d_s = _diff_attention(pad_q(q2_s[1]), k_d_all, vt_d_all, slopes, lam_p, g_sub, batch=nb_s, nq=1,
                            tq=tq_s, kt=kt, q_pos0=past, coff=0, n_keys=past + s_len, early_stop=False)
    o_sb_s = o_sb_s[:, :s_len].reshape(n_s, SEC)
    o_d_s = o_d_s[:, :s_len].reshape(n_s, SEC)
    h2_s, hn_s, te_s, tg_s = _merge(hs, o_sb_s, o_d_s, gates_s, w_sb, w_d, w_o, norm_ffn_g[0], w_r, b_r,
                                    n_s)

    n_tok = t + n_s
    r_tok = -(-n_tok // tq_p) * tq_p
    bm = MOE_ROWS
    n_rows = -(-(n_tok * TOP_K + N_EXPERTS * (bm - 1)) // bm) * bm
    te_all = jnp.concatenate([te[:t], te_s, jnp.full((r_tok - n_tok, LANES), -1, jnp.int32)], axis=0)
    hn_all = jnp.concatenate([hn[:t], hn_s, jnp.zeros((r_tok - n_tok, d), F32)], axis=0)
    pos, block_e, n_valid = _route(te_all, n_rows, bm)
    pos_d = jnp.where(te_all[:, :1] >= 0, pos, n_rows)
    xs = _sc_scatter_rows(hn_all, pos_d.T, n_rows + 8)
    yb = _moe_gmm(block_e, n_valid, xs, w_gu, b_gu, w_dn, b_dn, bm)
    slot_rows = -(-(tp + n_s) // (SC_GATHER_WINDOW * 8)) * (SC_GATHER_WINDOW * 8)
    pos_slots = jnp.concatenate([pos[:t], jnp.zeros((tp - t, TOP_K), jnp.int32), pos[t:n_tok],
                                 jnp.zeros((slot_rows - tp - n_s, TOP_K), jnp.int32)], axis=0)
    contrib = _sc_gather(yb, pos_slots.T.reshape(-1))
    y_p = _final(h2, contrib, tg, final_norm_g, tq_p, slot_rows)
    y_s = _final(h2_s, contrib, tg_s, final_norm_g, n_s, slot_rows, c_row0=tp)

    y_prompt = y_p[N_META:t][None]
    y_sample = y_s.reshape(nb_s, s_len, d)

    def heads(a, nh):
        return a.reshape(1, *a.shape[:-1], nh, a.shape[-1] // nh)

    return (y_prompt, y_sample,
            heads(kf_sb[None], H_SB), heads(vf_sb[None], H_SB),
            heads(kf_d[None], H_DIFF), heads(vf_d[None], H_DIFF),
            heads(kf_s[0].reshape(nb_s, s_len, SEC), H_SB), heads(vf_s[0].reshape(nb_s, s_len, SEC), H_SB),
            heads(kf_s[1].reshape(nb_s, s_len, SEC), H_DIFF), heads(vf_s[1].reshape(nb_s, s_len, SEC), H_DIFF))
```

```python
import functools
import math

import jax
import jax.numpy as jnp
from jax import lax
from jax.experimental import pallas as pl
from jax.experimental.pallas import tpu as pltpu
from jax.experimental.pallas import tpu_sc as plsc

F32 = jnp.float32
BF16 = jnp.bfloat16

D_MODEL = 1024
CHUNK = 64
CHUNK_SHIFT = 6
N_META = 16
H_SB = 16
DH_SB = 64
H_DIFF = 8
DH_DIFF = 64
DV_DIFF = 128
N_EXPERTS = 32
TOP_K = 4
D_FF = 1024
SWIGLU_LIMIT = 7.0
SWIGLU_ALPHA = 1.702
NORM_EPS = 1e-5
LAMBDA_INIT = 0.8 - 0.6 * math.exp(-0.3 * 0)

LANES = 128
SEC = 1024
KEY_TILE = 256
Q_TILE_PROMPT = 256
Q_TILE_DIFF = 512
Q_TILE_DECODE = 128
MOE_ROWS = 256
VMEM_LIMIT = 56 * 1024 * 1024
NEG_INF = float("-inf")
SB_SKIP_LOG = 104.0
SB_ABSENT = 1e30
DIFF_GROUP = 2
DIFF_TAIL_GROUPS = 2
DIFF_SKIP_LOG = 105.0
DIFF_NORM_SLACK = 1.01
SC_GATHER_WINDOW = 128
SC_SUB_WINDOWS = 4


def _cparams(sem):
    return pltpu.CompilerParams(dimension_semantics=sem, vmem_limit_bytes=VMEM_LIMIT)


def _rmsnorm_kernel(x_ref, g_ref, o_ref):
    x = x_ref[...]
    y = x * lax.rsqrt(jnp.mean(x * x, axis=-1, keepdims=True) + NORM_EPS)
    o_ref[...] = (y * g_ref[...]).astype(o_ref.dtype)


def _rmsnorm(x, g, tm):
    r, d = x.shape
    return pl.pallas_call(
        _rmsnorm_kernel,
        grid=(r // tm,),
        in_specs=[pl.BlockSpec((tm, d), lambda i: (i, 0)),
                  pl.BlockSpec((1, d), lambda i: (0, 0))],
        out_specs=pl.BlockSpec((tm, d), lambda i: (i, 0)),
        out_shape=jax.ShapeDtypeStruct((r, d), BF16),
        compiler_params=_cparams(("parallel",)),
        name="rmsnorm",
    )(x, g.reshape(1, d))


def _proj_kernel(x_ref, w_ref, *out_refs, mode, scale, kt):
    x = x_ref[...]
    for s in range(2):
        acc = jnp.dot(x, w_ref[:, s * SEC:(s + 1) * SEC], preferred_element_type=F32)
        if mode == "q":
            out_refs[s][...] = (acc * scale).astype(BF16)
        elif mode == "k":
            out_refs[2 * s][...] = acc
            out_refs[2 * s + 1][...] = acc.astype(BF16)
        elif mode == "v":
            out_refs[2 * s][...] = acc
            for c in range(acc.shape[0] // kt):
                for hp in range(SEC // LANES):
                    tile = acc[c * kt:(c + 1) * kt, hp * LANES:(hp + 1) * LANES]
                    out_refs[2 * s + 1][hp, c] = tile.T.astype(BF16)
        else:
            out_refs[0][:, s * SEC:(s + 1) * SEC] = jax.nn.sigmoid(acc)


def _proj(xn, w2, mode, tm, r_out=None, kt=KEY_TILE, scale=1.0):
    r, d = xn.shape
    r_out = r if r_out is None else r_out
    in_specs = [pl.BlockSpec((tm, d), lambda i: (i, 0)),
                pl.BlockSpec((d, 2 * SEC), lambda i: (0, 0))]
    row_spec = pl.BlockSpec((tm, SEC), lambda i: (i, 0))
    f32_out = jax.ShapeDtypeStruct((r_out, SEC), F32)
    bf16_out = jax.ShapeDtypeStruct((r, SEC), BF16)
    if mode == "q":
        out_specs = [row_spec, row_spec]
        out_shape = [bf16_out, bf16_out]
    elif mode == "k":
        out_specs = [row_spec] * 4
        out_shape = [f32_out, bf16_out, f32_out, bf16_out]
    elif mode == "v":
        t_spec = pl.BlockSpec((SEC // LANES, tm // kt, LANES, kt), lambda i: (0, i, 0, 0))
        t_out = jax.ShapeDtypeStruct((SEC // LANES, r // kt, LANES, kt), BF16)
        out_specs = [row_spec, t_spec, row_spec, t_spec]
        out_shape = [f32_out, t_out, f32_out, t_out]
    else:
        out_specs = [pl.BlockSpec((tm, 2 * SEC), lambda i: (i, 0))]
        out_shape = [jax.ShapeDtypeStruct((r, 2 * SEC), F32)]
    return pl.pallas_call(
        functools.partial(_proj_kernel, mode=mode, scale=scale, kt=kt),
        grid=(r // tm,), in_specs=in_specs, out_specs=out_specs, out_shape=out_shape,
        compiler_params=_cparams(("parallel",)),
        name="proj_" + mode,
    )(xn, w2)


def _cache_tiles_kernel(kc_ref, vc_ref, kn_ref, vn_ref, k_out, vt_out, *, kt, keys_minor):
    j = pl.program_id(1)
    n_cache = pl.num_programs(1) - 1
    nhb = SEC // LANES

    @pl.when(j < n_cache)
    def _():
        for hb in range(nhb):
            cols = slice(hb * LANES, (hb + 1) * LANES)
            if keys_minor:
                k_out[0, :, cols] = kc_ref[0, cols, :].T.astype(BF16)
                vt_out[0, hb, 0] = vc_ref[0, cols, :].astype(BF16)
            else:
                k_out[0, :, cols] = kc_ref[0, :, hb, :].astype(BF16)
                vt_out[0, hb, 0] = vc_ref[0, :, hb, :].T.astype(BF16)

    @pl.when(j == n_cache)
    def _():
        s_len = kn_ref.shape[1]
        pad = jnp.zeros((kt - s_len, SEC), F32)
        k_out[0] = jnp.concatenate([kn_ref[0], pad], axis=0).astype(BF16)
        v_new = jnp.concatenate([vn_ref[0], pad], axis=0)
        for hb in range(nhb):
            vt_out[0, hb, 0] = v_new[:, hb * LANES:(hb + 1) * LANES].T.astype(BF16)


def _cache_tiles(k_cache, v_cache, k_new, v_new, kt, keys_minor):
    b = k_cache.shape[0]
    past = k_cache.shape[2] if keys_minor else k_cache.shape[1]
    s_len = k_new.shape[1]
    assert past % kt == 0 and s_len <= kt
    n_cache = past // kt
    if keys_minor:
        cache_spec = pl.BlockSpec((1, SEC, kt), lambda i, j: (i, 0, jnp.minimum(j, n_cache - 1)))
    else:
        cache_spec = pl.BlockSpec((1, kt, SEC // LANES, LANES),
                                  lambda i, j: (i, jnp.minimum(j, n_cache - 1), 0, 0))
    new_spec = pl.BlockSpec((1, s_len, SEC), lambda i, j: (i, 0, 0))
    return pl.pallas_call(
        functools.partial(_cache_tiles_kernel, kt=kt, keys_minor=keys_minor),
        grid=(b, n_cache + 1),
        in_specs=[cache_spec, cache_spec, new_spec, new_spec],
        out_specs=[pl.BlockSpec((1, kt, SEC), lambda i, j: (i, j, 0)),
                   pl.BlockSpec((1, SEC // LANES, 1, LANES, kt), lambda i, j: (i, 0, j, 0, 0))],
        out_shape=[jax.ShapeDtypeStruct((b, past + kt, SEC), BF16),
                   jax.ShapeDtypeStruct((b, SEC // LANES, n_cache + 1, LANES, kt), BF16)],
        compiler_params=_cparams(("parallel", "arbitrary")),
        name="cache_tiles",
    )(k_cache, v_cache, k_new, v_new)


def _sb_kernel(q_ref, k_ref, vt_ref, tri_ref, o_ref, acc_ref, *, tq, kt, q_pos0):
    qi = pl.program_id(2)
    q0 = q_pos0 + qi * tq
    jm = q0 // kt
    q = q_ref[0]
    lane = lax.broadcasted_iota(jnp.int32, (tq, LANES), 1)
    kpos_l = lax.broadcasted_iota(jnp.int32, (kt, tq), 0)
    qpos_l = lax.broadcasted_iota(jnp.int32, (kt, tq), 1)
    tri = tri_ref[...]
    zero = jnp.zeros_like(q)
    qz = (jnp.where(lane < DH_SB, q, zero), jnp.where(lane >= DH_SB, q, zero))
    acc_ref[...] = jnp.zeros_like(acc_ref)

    def tile(j, hh, carry, masked):
        off = pl.multiple_of(j * kt, kt)
        kk = k_ref[0, pl.ds(off, kt), :]
        z = lax.dot_general(kk, qz[hh], (((1,), (1,)), ((), ())),
                            preferred_element_type=F32)
        sp = jnp.maximum(z, 0.0) + jnp.log(1.0 + jnp.exp(-jnp.abs(z)))
        if masked:
            vis = (kpos_l + j * kt) < (qpos_l + q0)
            lk = jnp.where(vis, -sp, 0.0)
        else:
            lk = -sp
        later = jnp.dot(tri, lk.astype(BF16), preferred_element_type=F32)
        w = jnp.exp(z + lk + later + carry)
        if masked:
            w = jnp.where(vis, w, 0.0)
        acc_ref[hh] += jnp.dot(vt_ref[0, 0, j], w.astype(BF16), preferred_element_type=F32)
        return jnp.sum(lk, axis=0, keepdims=True)

    none_later = jnp.zeros((1, tq), F32)
    carries = tuple(tile(jm, hh, none_later, True) for hh in range(2))

    def pair(j0, ca, cb):
        j1 = j0 - 1
        out = []
        for hh, carry in ((0, ca), (1, cb)):
            s0 = tile(jnp.maximum(j0, 0), hh, jnp.where(j0 >= 0, carry, -SB_ABSENT), False)
            s1 = tile(jnp.maximum(j1, 0), hh, jnp.where(j1 >= 0, carry + s0, -SB_ABSENT), False)
            out.append(carry + s0 + s1)
        return out

    jb = jm - 1
    first = []
    for hh in range(2):
        s = tile(jnp.maximum(jb, 0), hh, jnp.where(jb >= 0, carries[hh], -SB_ABSENT), False)
        first.append(carries[hh] + s)

    def cond(state):
        t, ca, cb = state
        return jnp.logical_and(jm - 2 - 2 * t >= 0,
                               jnp.max(jnp.maximum(ca, cb)) > -SB_SKIP_LOG)

    def body(state):
        t, ca, cb = state
        ca, cb = pair(jm - 2 - 2 * t, ca, cb)
        return t + 1, ca, cb

    lax.while_loop(cond, body, (jnp.int32(0), first[0], first[1]))

    row = lax.broadcasted_iota(jnp.int32, (LANES, tq), 0)
    o_t = jnp.where(row < DH_SB, acc_ref[0], acc_ref[1])
    o_ref[0] = o_t.T.astype(o_ref.dtype)


def _sb_attention(q, k, vt, tri, *, batch, nq, tq, kt, q_pos0):
    rk = k.shape[1]
    nk = rk // kt
    nhp = SEC // LANES
    return pl.pallas_call(
        functools.partial(_sb_kernel, tq=tq, kt=kt, q_pos0=q_pos0),
        grid=(batch, nhp, nq),
        in_specs=[
            pl.BlockSpec((1, tq, LANES), lambda b, h, i: (b, i, h)),
            pl.BlockSpec((1, rk, LANES), lambda b, h, i: (b, 0, h)),
            pl.BlockSpec((1, 1, nk, LANES, kt), lambda b, h, i: (b, h, 0, 0, 0)),
            pl.BlockSpec((kt, kt), lambda b, h, i: (0, 0)),
        ],
        out_specs=pl.BlockSpec((1, tq, LANES), lambda b, h, i: (b, i, h)),
        out_shape=jax.ShapeDtypeStruct((batch, nq * tq, SEC), BF16),
        scratch_shapes=[pltpu.VMEM((2, LANES, tq), F32)],
        compiler_params=_cparams(("parallel", "parallel", "arbitrary")),
        name="sb_attention",
    )(q, k, vt, tri)


def _diff_kernel(slopes_ref, lam_ref, g_ref, q_ref, k_ref, vt_ref, o_ref,
                 acc_ref, bt_ref, s0_ref, s1_ref, p0_ref, p1_ref, kn_ref,
                 *, tq, kt, q_pos0, coff, n_keys, nk, q_rows, early_stop):
    h = pl.program_id(1)
    qi = pl.program_id(2)
    slope = slopes_ref[0, h]
    inv_slope = slopes_ref[1, h]
    q0 = q_pos0 + qi * tq
    c_lo = (q0 - coff + CHUNK) // CHUNK - 1
    c_hi = (q0 + tq - 1 - coff + CHUNK) // CHUNK - 1
    full_end = jnp.minimum(coff + CHUNK * (c_lo + 1), n_keys)
    vis_end = jnp.minimum(coff + CHUNK * (c_hi + 1), n_keys)
    n_full = full_end // kt
    n_vis = jnp.minimum((vis_end + kt - 1) // kt, nk)

    q = q_ref[0]
    qrow = lax.broadcasted_iota(jnp.int32, (tq, LANES), 0)
    q = jnp.where(qrow < q_rows - qi * tq, q, jnp.zeros_like(q))
    lane = lax.broadcasted_iota(jnp.int32, (tq, LANES), 1)
    kpos_l = lax.broadcasted_iota(jnp.int32, (kt, tq), 0)
    qpos_l = lax.broadcasted_iota(jnp.int32, (kt, tq), 1)
    gsz = DIFF_GROUP
    m_tiles = (n_full // gsz) * gsz
    n_fullg = m_tiles // gsz
    n_groups = n_fullg + DIFF_TAIL_GROUPS
    n_tail = DIFF_TAIL_GROUPS * gsz
    absent = gsz + n_tail

    for u in range(gsz):
        bt_ref[u] = slope * (kpos_l + u * kt).astype(F32)
    for u in range(n_tail):
        kpos = kpos_l + (m_tiles + u) * kt
        qpos = qpos_l + q0
        vis = (((kpos - coff + CHUNK) >> CHUNK_SHIFT) <= ((qpos - coff + CHUNK) >> CHUNK_SHIFT)) \
            & (kpos < n_keys)
        bias = slope * (qpos_l - jnp.abs(qpos - kpos)).astype(F32)
        bt_ref[gsz + u] = jnp.where(vis, bias, NEG_INF)
    bt_ref[absent] = jnp.full((kt, tq), NEG_INF, F32)

    zero = jnp.zeros_like(q)
    qz = (jnp.where(lane < DH_DIFF, q, zero), jnp.where(lane >= DH_DIFF, q, zero))
    acc_ref[...] = jnp.zeros_like(acc_ref)
    p1_ref[...] = jnp.zeros_like(p1_ref)

    if early_stop:
        hr = lax.broadcasted_iota(jnp.int32, (LANES, LANES), 0)
        hc = lax.broadcasted_iota(jnp.int32, (LANES, LANES), 1)
        half = (((hc == 0) & (hr < DH_DIFF)) | ((hc == 1) & (hr >= DH_DIFF))).astype(BF16)
        lane1 = lax.broadcasted_iota(jnp.int32, (1, LANES), 1)

        def max_half_norms(sq_max):
            nrm = jnp.sqrt(sq_max)
            return (jnp.max(jnp.where(lane1 == 0, nrm, 0.0)), jnp.max(jnp.where(lane1 == 1, nrm, 0.0)))

        @pl.when(qi == 0)
        def _():
            def body(j, mx):
                off = pl.multiple_of(j * kt, kt)
                kk = k_ref[0, pl.ds(off, kt), :]
                n2 = jnp.dot(kk * kk, half, preferred_element_type=F32)
                return jnp.maximum(mx, jnp.max(n2, axis=0, keepdims=True))
            kn = max_half_norms(lax.fori_loop(0, nk, body, jnp.zeros((1, LANES), F32)))
            kn_ref[0] = kn[0]
            kn_ref[1] = kn[1]

        qn = max_half_norms(jnp.max(jnp.dot(q * q, half, preferred_element_type=F32), axis=0, keepdims=True))
        qk_bound = [DIFF_NORM_SLACK * qn[mm] * kn_ref[mm] for mm in range(2)]
        q0_f = q0.astype(F32)

        def skip_below(state):
            need = jnp.maximum((DIFF_SKIP_LOG + qk_bound[0]) - state[0],
                               (DIFF_SKIP_LOG + qk_bound[1]) - state[2])
            return q0_f - jnp.max(need) * inv_slope
    else:
        def skip_below(state):
            return jnp.float32(NEG_INF)

    def seq_group(n):
        return n_groups - 1 - n

    def needed(g, thr):
        top = ((g + 1) * (gsz * kt)).astype(F32)
        return jnp.logical_and(g >= 0, jnp.logical_or(g >= n_fullg, top > thr))

    def tiles_of(g):
        return [jnp.clip(g * gsz + u, 0, nk - 1) for u in range(gsz)]

    def scores_stage(g, s_ref):
        js = tiles_of(g)
        for mm in range(2):
            for u in range(gsz):
                off = pl.multiple_of(js[u] * kt, kt)
                kk = k_ref[0, pl.ds(off, kt), :]
                s_ref[mm, u] =lax.dot_general(kk, qz[mm], (((1,), (1,)), ((), ())),
                                               preferred_element_type=F32)

    def softmax_stage(g, thr, s_ref, p_ref, state):
        ok = needed(g, thr)
        full = jnp.logical_and(ok, g < n_fullg)
        sj = jnp.where(full, slope * (g * gsz * kt - q0).astype(F32), 0.0)
        bidx = [jnp.where(full, u, jnp.where(ok, gsz + g * gsz + u - m_tiles, absent)) for u in range(gsz)]
        new_state, alphas = [], []
        for mm in range(2):
            m, l = state[2 * mm], state[2 * mm + 1]
            ss = [s_ref[mm, u] + bt_ref[bidx[u]] for u in range(gsz)]
            cmax = jnp.max(ss[0], axis=0, keepdims=True)
            for s in ss[1:]:
                cmax = jnp.maximum(cmax, jnp.max(s, axis=0, keepdims=True))
            m_new = jnp.maximum(m, cmax + sj)
            m_safe = jnp.where(m_new == NEG_INF, 0.0, m_new)
            alpha = jnp.exp(m - m_safe)
            r = m_safe - sj
            psum = jnp.zeros((1, tq), F32)
            for u in range(gsz):
                p = jnp.exp(ss[u] - r)
                psum = psum + jnp.sum(p, axis=0, keepdims=True)
                p_ref[mm, u] = p.astype(BF16)
            new_state += [m_new, alpha * l + psum]
            alphas.append(alpha)
        return tuple(new_state), tuple(alphas)

    def value_stage(g, p_ref, alphas):
        js = tiles_of(g)
        for mm in range(2):
            pv = jnp.dot(vt_ref[0, 0, js[0]], p_ref[mm, 0], preferred_element_type=F32)
            for u in range(1, gsz):
                pv = pv + jnp.dot(vt_ref[0, 0, js[u]], p_ref[mm, u], preferred_element_type=F32)
            acc_ref[mm] = alphas[mm] * acc_ref[mm] + pv

    def trip(carry):
        t, thr, state, alpha_prev = carry
        n0 = 2 * t
        scores_stage(seq_group(n0 + 1), s1_ref)
        state, alpha0 = softmax_stage(seq_group(n0), thr, s0_ref, p0_ref, state)
        value_stage(seq_group(n0 - 1), p1_ref, alpha_prev)
        scores_stage(seq_group(n0 + 2), s0_ref)
        state, alpha1 = softmax_stage(seq_group(n0 + 1), thr, s1_ref, p1_ref, state)
        value_stage(seq_group(n0), p0_ref, alpha0)
        return t + 1, skip_below(state), state, alpha1

    def more(carry):
        t, thr = carry[0], carry[1]
        return needed(seq_group(2 * t), thr)

    neg = jnp.full((1, tq), NEG_INF, F32)
    zer = jnp.zeros((1, tq), F32)
    one = jnp.ones((1, tq), F32)
    scores_stage(seq_group(0), s0_ref)
    n_done, _, state, alpha_last = lax.while_loop(
        more, trip, (jnp.int32(0), jnp.float32(NEG_INF), (neg, zer, neg, zer), (one, one)))
    value_stage(seq_group(2 * n_done - 1), p1_ref, alpha_last)
    stats = [(state[0], state[1]), (state[2], state[3])]

    lp = lam_ref[...]
    lam = (jnp.exp(jnp.sum(lp[0:1] * lp[1:2], axis=-1, keepdims=True))
           - jnp.exp(jnp.sum(lp[2:3] * lp[3:4], axis=-1, keepdims=True)) + LAMBDA_INIT)
    l1 = stats[0][1]
    l2 = stats[1][1]
    l1 = jnp.where(l1 == 0.0, 1.0, l1)
    l2 = jnp.where(l2 == 0.0, 1.0, l2)
    o_t = acc_ref[0] / l1 - lam * (acc_ref[1] / l2)
    o = o_t.T
    y = o * lax.rsqrt(jnp.mean(o * o, axis=-1, keepdims=True) + NORM_EPS)
    o_ref[0] = ((y * g_ref[...]) * (1.0 - LAMBDA_INIT)).astype(o_ref.dtype)


def _diff_scratch(tq, kt):
    return [pltpu.VMEM((2, LANES, tq), F32),
            pltpu.VMEM((DIFF_GROUP * (1 + DIFF_TAIL_GROUPS) + 1, kt, tq), F32),
            pltpu.VMEM((2, DIFF_GROUP, kt, tq), F32),
            pltpu.VMEM((2, DIFF_GROUP, kt, tq), F32),
            pltpu.VMEM((2, DIFF_GROUP, kt, tq), BF16),
            pltpu.VMEM((2, DIFF_GROUP, kt, tq), BF16),
            pltpu.SMEM((2,), F32)]


def _diff_param_specs():
    return [pl.BlockSpec(memory_space=pltpu.SMEM),
            pl.BlockSpec((4, DH_DIFF), lambda b, h, i: (0, 0)),
            pl.BlockSpec((1, DV_DIFF), lambda b, h, i: (0, 0))]


def _diff_attention(q, k, vt, slopes, lam_p, subln_g, *, batch, nq, tq, kt, q_pos0, coff, n_keys,
                    early_stop):
    rk = k.shape[1]
    nk = rk // kt
    q_rows = q.shape[1]
    assert (nq - 1) * tq < q_rows <= nq * tq
    return pl.pallas_call(
        functools.partial(_diff_kernel, tq=tq, kt=kt, q_pos0=q_pos0, coff=coff, n_keys=n_keys, nk=nk,
                          q_rows=q_rows, early_stop=early_stop),
        grid=(batch, SEC // LANES, nq),
        in_specs=_diff_param_specs() + [
            pl.BlockSpec((1, tq, LANES), lambda b, h, i: (b, i, h)),
            pl.BlockSpec((1, rk, LANES), lambda b, h, i: (b, 0, h)),
            pl.BlockSpec((1, 1, nk, LANES, kt), lambda b, h, i: (b, h, 0, 0, 0)),
        ],
        out_specs=pl.BlockSpec((1, tq, LANES), lambda b, h, i: (b, i, h)),
        out_shape=jax.ShapeDtypeStruct((batch, q_rows, SEC), BF16),
        scratch_shapes=_diff_scratch(tq, kt),
        compiler_params=_cparams(("parallel", "parallel", "arbitrary")),
        name="diff_attention",
    )(slopes, lam_p, subln_g.reshape(1, DV_DIFF), q, k, vt)


def _merge_kernel(h_ref, osb_ref, od_ref, gt_ref, wsb_ref, wd_ref, wo_ref, gn_ref, wr_ref, br_ref,
                  h2_ref, hn_ref, te_ref, tg_ref):
    y_sb = jnp.dot(osb_ref[...], wsb_ref[...], preferred_element_type=F32)
    y_d = jnp.dot(od_ref[...], wd_ref[...], preferred_element_type=F32)
    gt = gt_ref[...]
    mix = gt[:, :D_MODEL] * y_sb + gt[:, D_MODEL:] * y_d
    h2 = h_ref[...] + jnp.dot(mix.astype(BF16), wo_ref[...], preferred_element_type=F32)
    h2_ref[...] = h2
    hn = (h2 * lax.rsqrt(jnp.mean(h2 * h2, axis=-1, keepdims=True) + NORM_EPS)) * gn_ref[...]
    hn_ref[...] = hn
    logits = jnp.dot(hn, wr_ref[...], preferred_element_type=F32,
                     precision=lax.Precision.HIGHEST) + br_ref[...]
    tm = logits.shape[0]
    lane = lax.broadcasted_iota(jnp.int32, (tm, LANES), 1).astype(F32)
    tops, idxs = [], []
    l = logits
    for _ in range(TOP_K):
        m = jnp.max(l, axis=-1, keepdims=True)
        idx = jnp.min(jnp.where(l == m, lane, float(LANES)), axis=-1, keepdims=True)
        tops.append(m)
        idxs.append(idx)
        l = jnp.where(lane == idx, NEG_INF, l)
    ex = [jnp.exp(t - tops[0]) for t in tops]
    den = ex[0] + ex[1] + ex[2] + ex[3]
    te = jnp.zeros((tm, LANES), F32)
    tg = jnp.zeros((tm, LANES), F32)
    for k in range(TOP_K):
        te = jnp.where(lane == float(k), idxs[k], te)
        tg = jnp.where(lane == float(k), ex[k] / den, tg)
    te_ref[...] = te.astype(jnp.int32)
    tg_ref[...] = tg


def _merge(h, o_sb, o_d, gates, w_sb, w_d, w_o, g_ffn, w_r, b_r, tm):
    r, d = h.shape
    row = lambda w: pl.BlockSpec((tm, w), lambda i: (i, 0))
    full = lambda a, b: pl.BlockSpec((a, b), lambda i: (0, 0))
    return pl.pallas_call(
        _merge_kernel,
        grid=(r // tm,),
        in_specs=[row(d), row(d), row(d), row(2 * d), full(d, d), full(d, d), full(d, d),
                  full(1, d), full(d, LANES), full(1, LANES)],
        out_specs=[row(d), row(d), row(LANES), row(LANES)],
        out_shape=[jax.ShapeDtypeStruct((r, d), F32), jax.ShapeDtypeStruct((r, d), F32),
                   jax.ShapeDtypeStruct((r, LANES), jnp.int32), jax.ShapeDtypeStruct((r, LANES), F32)],
        compiler_params=_cparams(("parallel",)),
        name="merge_router",
    )(h, o_sb, o_d, gates, w_sb, w_d, w_o, g_ffn.reshape(1, d), w_r, b_r)


def _moe_kernel(be_ref, nv_ref, x_ref, wgu_ref, bgu_ref, wdn_ref, bdn_ref, o_ref, wgu_bf, wdn_bf):
    b = pl.program_id(0)
    used = nv_ref[b] > 0
    new_expert = jnp.logical_or(b == 0, be_ref[b] != be_ref[jnp.maximum(b - 1, 0)])

    @pl.when(jnp.logical_and(used, new_expert))
    def _():
        wgu_bf[...] = wgu_ref[0].astype(BF16)
        wdn_bf[...] = wdn_ref[0].astype(BF16)

    @pl.when(used)
    def _():
        row = lax.broadcasted_iota(jnp.int32, x_ref.shape, 0)
        x = jnp.where(row < nv_ref[b], x_ref[...], 0.0).astype(BF16)
        gu = jnp.dot(x, wgu_bf[...], preferred_element_type=F32) + bgu_ref[0]
        g = jnp.minimum(gu[:, :D_FF], SWIGLU_LIMIT)
        u = jnp.clip(gu[:, D_FF:], -SWIGLU_LIMIT, SWIGLU_LIMIT)
        act = (u + 1.0) * (g * jax.nn.sigmoid(SWIGLU_ALPHA * g))
        o_ref[...] = jnp.dot(act.astype(BF16), wdn_bf[...], preferred_element_type=F32) + bdn_ref[0]

    @pl.when(jnp.logical_not(used))
    def _():
        o_ref[...] = jnp.zeros_like(o_ref)


def _moe_gmm(block_e, n_valid, xs, w_gu, b_gu, w_dn, b_dn, bm):
    nb = block_e.shape[0]
    n_rows = nb * bm
    d = xs.shape[1]
    grid_spec = pltpu.PrefetchScalarGridSpec(
        num_scalar_prefetch=2,
        grid=(nb,),
        in_specs=[
            pl.BlockSpec((bm, d), lambda b, be, nu: (b, 0)),
            pl.BlockSpec((1, d, 2 * D_FF), lambda b, be, nu: (be[b], 0, 0)),
            pl.BlockSpec((1, 1, 2 * D_FF), lambda b, be, nu: (be[b], 0, 0)),
            pl.BlockSpec((1, D_FF, d), lambda b, be, nu: (be[b], 0, 0)),
            pl.BlockSpec((1, 1, d), lambda b, be, nu: (be[b], 0, 0)),
        ],
        out_specs=pl.BlockSpec((bm, d), lambda b, be, nu: (b, 0)),
        scratch_shapes=[pltpu.VMEM((d, 2 * D_FF), BF16), pltpu.VMEM((D_FF, d), BF16)],
    )
    return pl.pallas_call(
        _moe_kernel,
        grid_spec=grid_spec,
        out_shape=jax.ShapeDtypeStruct((n_rows, d), F32),
        compiler_params=_cparams(("arbitrary",)),
        name="moe_experts",
    )(block_e, n_valid, xs, w_gu, b_gu, w_dn, b_dn)


def _final_kernel(h_ref, c_ref, tg_ref, g_ref, o_ref):
    tm = h_ref.shape[0]
    tg = tg_ref[...]
    y = c_ref[0:tm, :] * tg[:, 0:1]
    for k in range(1, TOP_K):
        y = y + c_ref[k * tm:(k + 1) * tm, :] * tg[:, k:k + 1]
    x = h_ref[...] + y
    o_ref[...] = (x * lax.rsqrt(jnp.mean(x * x, axis=-1, keepdims=True) + NORM_EPS)) * g_ref[...]


def _final(h2, contrib, tg, g, tm, c_row0=0):
    r, d = h2.shape
    assert c_row0 % (TOP_K * tm) == 0
    c_blk0 = c_row0 // (TOP_K * tm)
    return pl.pallas_call(
        _final_kernel,
        grid=(r // tm,),
        in_specs=[pl.BlockSpec((tm, d), lambda i: (i, 0)),
                  pl.BlockSpec((TOP_K * tm, d), lambda i: (i + c_blk0, 0)),
                  pl.BlockSpec((tm, LANES), lambda i: (i, 0)),
                  pl.BlockSpec((1, d), lambda i: (0, 0))],
        out_specs=pl.BlockSpec((tm, d), lambda i: (i, 0)),
        out_shape=jax.ShapeDtypeStruct((r, d), F32),
        compiler_params=_cparams(("parallel",)),
        name="combine_final_norm",
    )(h2, contrib, tg, g.reshape(1, d))


def _rank_kernel(te_ref, tril_ref, rank_ref, cnt_ref, run_ref):
    @pl.when(pl.program_id(0) == 0)
    def _():
        run_ref[...] = jnp.zeros_like(run_ref)

    te = te_ref[...]
    tm = te.shape[0]
    lane = lax.broadcasted_iota(jnp.int32, (tm, LANES), 1)
    onehots = [(lane == te[:, k:k + 1]).astype(F32) for k in range(TOP_K)]
    member = onehots[0] + onehots[1] + onehots[2] + onehots[3]
    before = jnp.dot(tril_ref[...], member.astype(BF16), preferred_element_type=F32) + run_ref[...]
    rank = jnp.zeros((tm, LANES), F32)
    for k in range(TOP_K):
        r_k = jnp.sum(onehots[k] * before, axis=-1, keepdims=True)
        rank = jnp.where(lane == k, r_k, rank)
    rank_ref[...] = rank.astype(jnp.int32)
    run_ref[...] += jnp.sum(member, axis=0, keepdims=True)
    cnt_ref[...] = run_ref[...]


def _ranks(te, tm):
    r = te.shape[0]
    i = lax.broadcasted_iota(jnp.int32, (tm, tm), 0)
    j = lax.broadcasted_iota(jnp.int32, (tm, tm), 1)
    tril = (j < i).astype(BF16)
    return pl.pallas_call(
        _rank_kernel,
        grid=(r // tm,),
        in_specs=[pl.BlockSpec((tm, LANES), lambda b: (b, 0)), pl.BlockSpec((tm, tm), lambda b: (0, 0))],
        out_specs=[pl.BlockSpec((tm, LANES), lambda b: (b, 0)), pl.BlockSpec((1, LANES), lambda b: (0, 0))],
        out_shape=[jax.ShapeDtypeStruct((r, LANES), jnp.int32), jax.ShapeDtypeStruct((1, LANES), F32)],
        scratch_shapes=[pltpu.VMEM((1, LANES), F32)],
        compiler_params=_cparams(("arbitrary",)),
        name="route_ranks",
    )(te, tril)


def _route(te, n_rows, bm):
    rank, cnt = _ranks(te, Q_TILE_PROMPT)
    counts = cnt[0, :N_EXPERTS].astype(jnp.int32)
    padded = ((counts + bm - 1) // bm) * bm
    pend = jnp.cumsum(padded)
    pstart = pend - padded
    top_e = te[:, :TOP_K]
    onehot = top_e[:, :, None] == jnp.arange(N_EXPERTS, dtype=jnp.int32)[None, None, :]
    pos = jnp.sum(jnp.where(onehot, pstart[None, None, :], 0), axis=-1) + rank[:, :TOP_K]
    nb = n_rows // bm
    block_start = jnp.arange(nb, dtype=jnp.int32) * bm
    block_e = jnp.minimum(jnp.sum((pend[None, :] <= block_start[:, None]).astype(jnp.int32), axis=1),
                          N_EXPERTS - 1)
    e_hot = block_e[:, None] == jnp.arange(N_EXPERTS, dtype=jnp.int32)[None, :]
    last = jnp.sum(jnp.where(e_hot, (pstart + counts)[None, :], 0), axis=1)
    n_valid = jnp.clip(last - block_start, 0, bm).astype(jnp.int32)
    return pos, block_e.astype(jnp.int32), n_valid


def _sc_scatter_rows(src, idx_t, n_out):
    n_src, d = src.shape
    sub = SC_GATHER_WINDOW // SC_SUB_WINDOWS
    n_tiles = n_src // SC_GATHER_WINDOW
    idx_t = idx_t.reshape(1, TOP_K * n_src)
    mesh = plsc.VectorSubcoreMesh(core_axis_name="c", subcore_axis_name="s")

    @pl.kernel(out_type=jax.ShapeDtypeStruct((n_out, d), src.dtype), mesh=mesh)
    def scatter_kernel(x_hbm, i_hbm, o_hbm):
        def body(x_vmem, i_vmem):
            s = pl.program_id(2)
            pltpu.sync_copy(x_vmem, o_hbm.at[i_vmem.at[0, pl.ds(s * sub, sub)]])

        pltpu.emit_pipeline(
            body,
            grid=(n_tiles, TOP_K, SC_SUB_WINDOWS),
            in_specs=[pl.BlockSpec((sub, d), index_map=lambda i, k, s: (i * SC_SUB_WINDOWS + s, 0)),
                      pl.BlockSpec((1, SC_GATHER_WINDOW), index_map=lambda i, k, s: (0, k * n_tiles + i))],
            out_specs=[],
            core_axis_name=("c", "s"),
            dimension_semantics=(pltpu.PARALLEL, pltpu.ARBITRARY, pltpu.ARBITRARY),
        )(x_hbm, i_hbm)

    return scatter_kernel(src, idx_t)


def _sc_gather(table, idx):
    n_idx = idx.shape[0]
    d = table.shape[1]
    sub = SC_GATHER_WINDOW // SC_SUB_WINDOWS
    mesh = plsc.VectorSubcoreMesh(core_axis_name="c", subcore_axis_name="s")

    @pl.kernel(out_type=jax.ShapeDtypeStruct((n_idx, d), table.dtype), mesh=mesh)
    def gather_kernel(x_hbm, i_hbm, o_hbm):
        def body(i_vmem, o_vmem):
            j = pl.program_id(1)
            pltpu.sync_copy(x_hbm.at[i_vmem.at[0, pl.ds(j * sub, sub)]], o_vmem)

        pltpu.emit_pipeline(
            body,
            grid=(n_idx // SC_GATHER_WINDOW, SC_SUB_WINDOWS),
            in_specs=[pl.BlockSpec((1, SC_GATHER_WINDOW), index_map=lambda i, j: (0, i))],
            out_specs=[pl.BlockSpec((sub, d), index_map=lambda i, j: (i * SC_SUB_WINDOWS + j, 0))],
            core_axis_name=("c", "s"),
            dimension_semantics=(pltpu.PARALLEL, pltpu.ARBITRARY),
        )(i_hbm, o_hbm)

    return gather_kernel(table, idx.reshape(1, n_idx))


def _tri(kt):
    s = lax.broadcasted_iota(jnp.int32, (kt, kt), 0)
    j = lax.broadcasted_iota(jnp.int32, (kt, kt), 1)
    return (j > s).astype(BF16)


def kernel(x_prompt, x_sample, cache_sb_k, cache_sb_v, cache_diff_k, cache_diff_v, meta_tokens,
           norm_mix_g, w_in, diff_lambda, diff_subln_g, w_br_sb, w_br_diff, w_out, norm_ffn_g,
           w_router, b_router, w_gate_up, b_gate_up, w_down, b_down, final_norm_g):
    assert x_prompt.shape[0] == 1 and w_in.shape[0] == 1
    d = D_MODEL
    seq = x_prompt.shape[1]
    t = N_META + seq
    tq_p, kt = Q_TILE_PROMPT, KEY_TILE
    tp = -(-t // tq_p) * tq_p
    nq_p = tp // tq_p
    nb_s, s_len = x_sample.shape[:2]
    past = cache_sb_k.shape[2]
    n_s = nb_s * s_len
    scale = DH_SB ** -0.5

    w0 = w_in[0]
    wq = jnp.concatenate([w0[:, 0:SEC], w0[:, 3 * SEC:4 * SEC]], axis=1).astype(BF16)
    wk = jnp.concatenate([w0[:, SEC:2 * SEC], w0[:, 4 * SEC:5 * SEC]], axis=1).astype(BF16)
    wv = jnp.concatenate([w0[:, 2 * SEC:3 * SEC], w0[:, 5 * SEC:6 * SEC]], axis=1).astype(BF16)
    wg = w0[:, 6 * SEC:8 * SEC].astype(BF16)
    w_sb = w_br_sb[0].astype(BF16)
    w_d = w_br_diff[0].astype(BF16)
    w_o = w_out[0].astype(BF16)
    w_r = jnp.pad(w_router[0], ((0, 0), (0, LANES - N_EXPERTS)))
    b_r = jnp.pad(b_router[0], (0, LANES - N_EXPERTS), constant_values=NEG_INF).reshape(1, LANES)
    w_gu = w_gate_up[0]
    w_dn = w_down[0]
    b_gu = b_gate_up[0].reshape(N_EXPERTS, 1, 2 * D_FF)
    b_dn = b_down[0].reshape(N_EXPERTS, 1, d)
    slope = jnp.exp2(-8.0 * jnp.arange(1, H_DIFF + 1, dtype=F32) / H_DIFF)
    slopes = jnp.stack([slope, 1.0 / slope])
    tri = _tri(kt)
    lam_p = diff_lambda[0]
    g_sub = diff_subln_g[0]

    hp = jnp.concatenate([meta_tokens.astype(F32), x_prompt[0], jnp.zeros((tp - t, d), F32)], axis=0)
    tm_p = tq_p
    xn = _rmsnorm(hp, norm_mix_g[0], tm_p)
    q_sb, q_d = _proj(xn, wq, "q", tm_p, scale=scale)
    kf_sb, kb_sb, kf_d, kb_d = _proj(xn, wk, "k", tm_p, r_out=t)
    vf_sb, vt_sb, vf_d, vt_d = _proj(xn, wv, "v", tm_p, r_out=t)
    (gates,) = _proj(xn, wg, "g", tm_p)
    o_sb = _sb_attention(q_sb[None], kb_sb[None], vt_sb[None], tri, batch=1, nq=nq_p, tq=tq_p, kt=kt,
                         q_pos0=0)
    o_d = _diff_attention(q_d[None], kb_d[None], vt_d[None], slopes, lam_p, g_sub, batch=1,
                          nq=-(-tp // Q_TILE_DIFF), tq=Q_TILE_DIFF, kt=kt, q_pos0=0, coff=N_META, n_keys=t,
                          early_stop=True)
    tm_merge = 640 if tp % 640 == 0 else tq_p
    h2, hn, te, tg = _merge(hp, o_sb[0], o_d[0], gates, w_sb, w_d, w_o, norm_ffn_g[0], w_r, b_r, tm_merge)

    tq_s = Q_TILE_DECODE
    hs = x_sample.reshape(n_s, d)
    xn_s = _rmsnorm(hs, norm_mix_g[0], n_s)
    q2_s = _proj(xn_s, wq, "q", n_s, scale=scale)
    kf_s0, _, kf_s1, _ = _proj(xn_s, wk, "k", n_s)
    vf_s0, _, vf_s1, _ = _proj(xn_s, wv, "v", n_s, kt=n_s)
    kf_s = (kf_s0, kf_s1)
    vf_s = (vf_s0, vf_s1)
    (gates_s,) = _proj(xn_s, wg, "g", n_s)

    def pad_q(qs):
        return jnp.pad(qs.reshape(nb_s, s_len, SEC), ((0, 0), (0, tq_s - s_len), (0, 0)))

    def per_stream(a):
        return a.reshape(nb_s, -1, SEC)

    def keys_last(cache):
        return jnp.transpose(cache, (0, 2, 3, 1)).reshape(nb_s, SEC, past)

    k_sb_all, vt_sb_all = _cache_tiles(keys_last(cache_sb_k[0]), keys_last(cache_sb_v[0]),
                                       per_stream(kf_s[0]), per_stream(vf_s[0]), kt, True)
    k_d_all, vt_d_all = _cache_tiles(cache_diff_k[0], cache_diff_v[0], per_stream(kf_s[1]),
                                     per_stream(vf_s[1]), kt, False)
    o_sb_s = _sb_attention(pad_q(q2_s[0]), k_sb_all, vt_sb_all, tri, batch=nb_s, nq=1, tq=tq_s, kt=kt,
                           q_pos0=past)
    o_d_s = _diff_attention(pad_q(q2_s[1]), k_d_all, vt_d_all, slopes, lam_p, g_sub, batch=nb_s, nq=1,
                            tq=tq_s, kt=kt, q_pos0=past, coff=0, n_keys=past + s_len, early_stop=False)
    o_sb_s = o_sb_s[:, :s_len].reshape(n_s, SEC)
    o_d_s = o_d_s[:, :s_len].reshape(n_s, SEC)
    h2_s, hn_s, te_s, tg_s = _merge(hs, o_sb_s, o_d_s, gates_s, w_sb, w_d, w_o, norm_ffn_g[0], w_r, b_r,
                                    n_s)

    n_tok = t + n_s
    r_tok = -(-n_tok // tq_p) * tq_p
    bm = MOE_ROWS
    n_rows = -(-(n_tok * TOP_K + N_EXPERTS * (bm - 1)) // bm) * bm
    te_all = jnp.concatenate([te[:t], te_s, jnp.full((r_tok - n_tok, LANES), -1, jnp.int32)], axis=0)
    hn_all = jnp.concatenate([hn[:t], hn_s, jnp.zeros((r_tok - n_tok, d), F32)], axis=0)
    pos, block_e, n_valid = _route(te_all, n_rows, bm)
    pos_d = jnp.where(te_all[:, :1] >= 0, pos, n_rows)
    xs = _sc_scatter_rows(hn_all, pos_d.T, n_rows + 8)
    yb = _moe_gmm(block_e, n_valid, xs, w_gu, b_gu, w_dn, b_dn, bm)
    def by_block(p, tm):
        return p.reshape(-1, tm, TOP_K).transpose(0, 2, 1).reshape(-1)

    pos_p = by_block(jnp.concatenate([pos[:t], jnp.zeros((tp - t, TOP_K), jnp.int32)], axis=0), tq_p)
    pos_s = by_block(pos[t:n_tok], n_s)
    g_mult = SC_GATHER_WINDOW * 8
    g_pad = (-(pos_p.shape[0] + pos_s.shape[0])) % g_mult
    contrib = _sc_gather(yb, jnp.concatenate([pos_p, pos_s, jnp.zeros((g_pad,), jnp.int32)]))
    y_p = _final(h2, contrib, tg, final_norm_g, tq_p)
    y_s = _final(h2_s, contrib, tg_s, final_norm_g, n_s, c_row0=tp * TOP_K)

    y_prompt = y_p[N_META:t][None]
    y_sample = y_s.reshape(nb_s, s_len, d)

    def heads(a, nh):
        return a.reshape(1, *a.shape[:-1], nh, a.shape[-1] // nh)

    return (y_prompt, y_sample,
            heads(kf_sb[None], H_SB), heads(vf_sb[None], H_SB),
            heads(kf_d[None], H_DIFF), heads(vf_d[None], H_DIFF),
            heads(kf_s[0].reshape(nb_s, s_len, SEC), H_SB), heads(vf_s[0].reshape(nb_s, s_len, SEC), H_SB),
            heads(kf_s[1].reshape(nb_s, s_len, SEC), H_DIFF), heads(vf_s[1].reshape(nb_s, s_len, SEC), H_DIFF))
```

```python
import functools
import math

import jax
import jax.numpy as jnp
from jax import lax
from jax.experimental import pallas as pl
from jax.experimental.pallas import tpu as pltpu
from jax.experimental.pallas import tpu_sc as plsc

F32 = jnp.float32
BF16 = jnp.bfloat16

D_MODEL = 1024
CHUNK = 64
CHUNK_SHIFT = 6
N_META = 16
H_SB = 16
DH_SB = 64
H_DIFF = 8
DH_DIFF = 64
DV_DIFF = 128
N_EXPERTS = 32
TOP_K = 4
D_FF = 1024
SWIGLU_LIMIT = 7.0
SWIGLU_ALPHA = 1.702
NORM_EPS = 1e-5
LAMBDA_INIT = 0.8 - 0.6 * math.exp(-0.3 * 0)

LANES = 128
SEC = 1024
KEY_TILE = 256
Q_TILE_PROMPT = 256
Q_TILE_DIFF = 512
Q_TILE_DECODE = 128
MOE_ROWS = 512
VMEM_LIMIT = 56 * 1024 * 1024
NEG_INF = float("-inf")
SB_SKIP_LOG = 104.0
SB_ABSENT = 1e30
DIFF_GROUP = 2
DIFF_TAIL_GROUPS = 2
DIFF_SKIP_LOG = 105.0
DIFF_NORM_SLACK = 1.01
SC_GATHER_WINDOW = 128
SC_SUB_WINDOWS = 4


def _cparams(sem):
    return pltpu.CompilerParams(dimension_semantics=sem, vmem_limit_bytes=VMEM_LIMIT)


def _rmsnorm_kernel(x_ref, g_ref, o_ref):
    x = x_ref[...]
    y = x * lax.rsqrt(jnp.mean(x * x, axis=-1, keepdims=True) + NORM_EPS)
    o_ref[...] = (y * g_ref[...]).astype(o_ref.dtype)


def _rmsnorm(x, g, tm):
    r, d = x.shape
    return pl.pallas_call(
        _rmsnorm_kernel,
        grid=(r // tm,),
        in_specs=[pl.BlockSpec((tm, d), lambda i: (i, 0)),
                  pl.BlockSpec((1, d), lambda i: (0, 0))],
        out_specs=pl.BlockSpec((tm, d), lambda i: (i, 0)),
        out_shape=jax.ShapeDtypeStruct((r, d), BF16),
        compiler_params=_cparams(("parallel",)),
        name="rmsnorm",
    )(x, g.reshape(1, d))


def _proj_kernel(x_ref, w_ref, *out_refs, mode, scale, kt):
    x = x_ref[...]
    for s in range(2):
        acc = jnp.dot(x, w_ref[:, s * SEC:(s + 1) * SEC], preferred_element_type=F32)
        if mode == "q":
            out_refs[s][...] = (acc * scale).astype(BF16)
        elif mode == "k":
            out_refs[2 * s][...] = acc
            out_refs[2 * s + 1][...] = acc.astype(BF16)
        elif mode == "v":
            out_refs[2 * s][...] = acc
            for c in range(acc.shape[0] // kt):
                for hp in range(SEC // LANES):
                    tile = acc[c * kt:(c + 1) * kt, hp * LANES:(hp + 1) * LANES]
                    out_refs[2 * s + 1][hp, c] = tile.T.astype(BF16)
        else:
            out_refs[0][:, s * SEC:(s + 1) * SEC] = jax.nn.sigmoid(acc)


def _proj(xn, w2, mode, tm, r_out=None, kt=KEY_TILE, scale=1.0):
    r, d = xn.shape
    r_out = r if r_out is None else r_out
    in_specs = [pl.BlockSpec((tm, d), lambda i: (i, 0)),
                pl.BlockSpec((d, 2 * SEC), lambda i: (0, 0))]
    row_spec = pl.BlockSpec((tm, SEC), lambda i: (i, 0))
    f32_out = jax.ShapeDtypeStruct((r_out, SEC), F32)
    bf16_out = jax.ShapeDtypeStruct((r, SEC), BF16)
    if mode == "q":
        out_specs = [row_spec, row_spec]
        out_shape = [bf16_out, bf16_out]
    elif mode == "k":
        out_specs = [row_spec] * 4
        out_shape = [f32_out, bf16_out, f32_out, bf16_out]
    elif mode == "v":
        t_spec = pl.BlockSpec((SEC // LANES, tm // kt, LANES, kt), lambda i: (0, i, 0, 0))
        t_out = jax.ShapeDtypeStruct((SEC // LANES, r // kt, LANES, kt), BF16)
        out_specs = [row_spec, t_spec, row_spec, t_spec]
        out_shape = [f32_out, t_out, f32_out, t_out]
    else:
        out_specs = [pl.BlockSpec((tm, 2 * SEC), lambda i: (i, 0))]
        out_shape = [jax.ShapeDtypeStruct((r, 2 * SEC), F32)]
    return pl.pallas_call(
        functools.partial(_proj_kernel, mode=mode, scale=scale, kt=kt),
        grid=(r // tm,), in_specs=in_specs, out_specs=out_specs, out_shape=out_shape,
        compiler_params=_cparams(("parallel",)),
        name="proj_" + mode,
    )(xn, w2)


def _cache_tiles_kernel(kc_ref, vc_ref, kn_ref, vn_ref, k_out, vt_out, *, kt, keys_minor):
    j = pl.program_id(1)
    n_cache = pl.num_programs(1) - 1
    nhb = SEC // LANES

    @pl.when(j < n_cache)
    def _():
        for hb in range(nhb):
            cols = slice(hb * LANES, (hb + 1) * LANES)
            if keys_minor:
                k_out[0, :, cols] = kc_ref[0, cols, :].T.astype(BF16)
                vt_out[0, hb, 0] = vc_ref[0, cols, :].astype(BF16)
            else:
                k_out[0, :, cols] = kc_ref[0, :, hb, :].astype(BF16)
                vt_out[0, hb, 0] = vc_ref[0, :, hb, :].T.astype(BF16)

    @pl.when(j == n_cache)
    def _():
        s_len = kn_ref.shape[1]
        pad = jnp.zeros((kt - s_len, SEC), F32)
        k_out[0] = jnp.concatenate([kn_ref[0], pad], axis=0).astype(BF16)
        v_new = jnp.concatenate([vn_ref[0], pad], axis=0)
        for hb in range(nhb):
            vt_out[0, hb, 0] = v_new[:, hb * LANES:(hb + 1) * LANES].T.astype(BF16)


def _cache_tiles(k_cache, v_cache, k_new, v_new, kt, keys_minor):
    b = k_cache.shape[0]
    past = k_cache.shape[2] if keys_minor else k_cache.shape[1]
    s_len = k_new.shape[1]
    assert past % kt == 0 and s_len <= kt
    n_cache = past // kt
    if keys_minor:
        cache_spec = pl.BlockSpec((1, SEC, kt), lambda i, j: (i, 0, jnp.minimum(j, n_cache - 1)))
    else:
        cache_spec = pl.BlockSpec((1, kt, SEC // LANES, LANES),
                                  lambda i, j: (i, jnp.minimum(j, n_cache - 1), 0, 0))
    new_spec = pl.BlockSpec((1, s_len, SEC), lambda i, j: (i, 0, 0))
    return pl.pallas_call(
        functools.partial(_cache_tiles_kernel, kt=kt, keys_minor=keys_minor),
        grid=(b, n_cache + 1),
        in_specs=[cache_spec, cache_spec, new_spec, new_spec],
        out_specs=[pl.BlockSpec((1, kt, SEC), lambda i, j: (i, j, 0)),
                   pl.BlockSpec((1, SEC // LANES, 1, LANES, kt), lambda i, j: (i, 0, j, 0, 0))],
        out_shape=[jax.ShapeDtypeStruct((b, past + kt, SEC), BF16),
                   jax.ShapeDtypeStruct((b, SEC // LANES, n_cache + 1, LANES, kt), BF16)],
        compiler_params=_cparams(("parallel", "arbitrary")),
        name="cache_tiles",
    )(k_cache, v_cache, k_new, v_new)


def _sb_kernel(q_ref, k_ref, vt_ref, tri_ref, o_ref, acc_ref, *, tq, kt, q_pos0):
    qi = pl.program_id(2)
    q0 = q_pos0 + qi * tq
    jm = q0 // kt
    q = q_ref[0]
    lane = lax.broadcasted_iota(jnp.int32, (tq, LANES), 1)
    kpos_l = lax.broadcasted_iota(jnp.int32, (kt, tq), 0)
    qpos_l = lax.broadcasted_iota(jnp.int32, (kt, tq), 1)
    tri = tri_ref[...]
    zero = jnp.zeros_like(q)
    qz = (jnp.where(lane < DH_SB, q, zero), jnp.where(lane >= DH_SB, q, zero))
    acc_ref[...] = jnp.zeros_like(acc_ref)

    def tile(j, hh, carry, masked):
        off = pl.multiple_of(j * kt, kt)
        kk = k_ref[0, pl.ds(off, kt), :]
        z = lax.dot_general(kk, qz[hh], (((1,), (1,)), ((), ())),
                            preferred_element_type=F32)
        sp = jnp.maximum(z, 0.0) + jnp.log(1.0 + jnp.exp(-jnp.abs(z)))
        if masked:
            vis = (kpos_l + j * kt) < (qpos_l + q0)
            lk = jnp.where(vis, -sp, 0.0)
        else:
            lk = -sp
        later = jnp.dot(tri, lk.astype(BF16), preferred_element_type=F32)
        w = jnp.exp(z + lk + later + carry)
        if masked:
            w = jnp.where(vis, w, 0.0)
        acc_ref[hh] += jnp.dot(vt_ref[0, 0, j], w.astype(BF16), preferred_element_type=F32)
        return jnp.sum(lk, axis=0, keepdims=True)

    none_later = jnp.zeros((1, tq), F32)
    carries = tuple(tile(jm, hh, none_later, True) for hh in range(2))

    def pair(j0, ca, cb):
        j1 = j0 - 1
        out = []
        for hh, carry in ((0, ca), (1, cb)):
            s0 = tile(jnp.maximum(j0, 0), hh, jnp.where(j0 >= 0, carry, -SB_ABSENT), False)
            s1 = tile(jnp.maximum(j1, 0), hh, jnp.where(j1 >= 0, carry + s0, -SB_ABSENT), False)
            out.append(carry + s0 + s1)
        return out

    jb = jm - 1
    first = []
    for hh in range(2):
        s = tile(jnp.maximum(jb, 0), hh, jnp.where(jb >= 0, carries[hh], -SB_ABSENT), False)
        first.append(carries[hh] + s)

    def cond(state):
        t, ca, cb = state
        return jnp.logical_and(jm - 2 - 2 * t >= 0,
                               jnp.max(jnp.maximum(ca, cb)) > -SB_SKIP_LOG)

    def body(state):
        t, ca, cb = state
        ca, cb = pair(jm - 2 - 2 * t, ca, cb)
        return t + 1, ca, cb

    lax.while_loop(cond, body, (jnp.int32(0), first[0], first[1]))

    row = lax.broadcasted_iota(jnp.int32, (LANES, tq), 0)
    o_t = jnp.where(row < DH_SB, acc_ref[0], acc_ref[1])
    o_ref[0] = o_t.T.astype(o_ref.dtype)


def _sb_attention(q, k, vt, tri, *, batch, nq, tq, kt, q_pos0):
    rk = k.shape[1]
    nk = rk // kt
    nhp = SEC // LANES
    return pl.pallas_call(
        functools.partial(_sb_kernel, tq=tq, kt=kt, q_pos0=q_pos0),
        grid=(batch, nhp, nq),
        in_specs=[
            pl.BlockSpec((1, tq, LANES), lambda b, h, i: (b, i, h)),
            pl.BlockSpec((1, rk, LANES), lambda b, h, i: (b, 0, h)),
            pl.BlockSpec((1, 1, nk, LANES, kt), lambda b, h, i: (b, h, 0, 0, 0)),
            pl.BlockSpec((kt, kt), lambda b, h, i: (0, 0)),
        ],
        out_specs=pl.BlockSpec((1, tq, LANES), lambda b, h, i: (b, i, h)),
        out_shape=jax.ShapeDtypeStruct((batch, nq * tq, SEC), BF16),
        scratch_shapes=[pltpu.VMEM((2, LANES, tq), F32)],
        compiler_params=_cparams(("parallel", "parallel", "arbitrary")),
        name="sb_attention",
    )(q, k, vt, tri)


def _diff_kernel(slopes_ref, lam_ref, g_ref, q_ref, k_ref, vt_ref, o_ref,
                 acc_ref, bt_ref, s0_ref, s1_ref, p0_ref, p1_ref, kn_ref,
                 *, tq, kt, q_pos0, coff, n_keys, nk, q_rows, early_stop):
    h = pl.program_id(1)
    qi = pl.program_id(2)
    slope = slopes_ref[0, h]
    inv_slope = slopes_ref[1, h]
    q0 = q_pos0 + qi * tq
    c_lo = (q0 - coff + CHUNK) // CHUNK - 1
    c_hi = (q0 + tq - 1 - coff + CHUNK) // CHUNK - 1
    full_end = jnp.minimum(coff + CHUNK * (c_lo + 1), n_keys)
    vis_end = jnp.minimum(coff + CHUNK * (c_hi + 1), n_keys)
    n_full = full_end // kt
    n_vis = jnp.minimum((vis_end + kt - 1) // kt, nk)

    q = q_ref[0]
    qrow = lax.broadcasted_iota(jnp.int32, (tq, LANES), 0)
    q = jnp.where(qrow < q_rows - qi * tq, q, jnp.zeros_like(q))
    lane = lax.broadcasted_iota(jnp.int32, (tq, LANES), 1)
    kpos_l = lax.broadcasted_iota(jnp.int32, (kt, tq), 0)
    qpos_l = lax.broadcasted_iota(jnp.int32, (kt, tq), 1)
    gsz = DIFF_GROUP
    m_tiles = (n_full // gsz) * gsz
    n_fullg = m_tiles // gsz
    n_groups = n_fullg + DIFF_TAIL_GROUPS
    n_tail = DIFF_TAIL_GROUPS * gsz
    absent = gsz + n_tail

    for u in range(gsz):
        bt_ref[u] = slope * (kpos_l + u * kt).astype(F32)
    for u in range(n_tail):
        kpos = kpos_l + (m_tiles + u) * kt
        qpos = qpos_l + q0
        vis = (((kpos - coff + CHUNK) >> CHUNK_SHIFT) <= ((qpos - coff + CHUNK) >> CHUNK_SHIFT)) \
            & (kpos < n_keys)
        bias = slope * (qpos_l - jnp.abs(qpos - kpos)).astype(F32)
        bt_ref[gsz + u] = jnp.where(vis, bias, NEG_INF)
    bt_ref[absent] = jnp.full((kt, tq), NEG_INF, F32)

    zero = jnp.zeros_like(q)
    qz = (jnp.where(lane < DH_DIFF, q, zero), jnp.where(lane >= DH_DIFF, q, zero))
    acc_ref[...] = jnp.zeros_like(acc_ref)
    p1_ref[...] = jnp.zeros_like(p1_ref)

    if early_stop:
        hr = lax.broadcasted_iota(jnp.int32, (LANES, LANES), 0)
        hc = lax.broadcasted_iota(jnp.int32, (LANES, LANES), 1)
        half = (((hc == 0) & (hr < DH_DIFF)) | ((hc == 1) & (hr >= DH_DIFF))).astype(BF16)
        lane1 = lax.broadcasted_iota(jnp.int32, (1, LANES), 1)

        def max_half_norms(sq_max):
            nrm = jnp.sqrt(sq_max)
            return (jnp.max(jnp.where(lane1 == 0, nrm, 0.0)), jnp.max(jnp.where(lane1 == 1, nrm, 0.0)))

        @pl.when(qi == 0)
        def _():
            def body(j, mx):
                off = pl.multiple_of(j * kt, kt)
                kk = k_ref[0, pl.ds(off, kt), :]
                n2 = jnp.dot(kk * kk, half, preferred_element_type=F32)
                return jnp.maximum(mx, jnp.max(n2, axis=0, keepdims=True))
            kn = max_half_norms(lax.fori_loop(0, nk, body, jnp.zeros((1, LANES), F32)))
            kn_ref[0] = kn[0]
            kn_ref[1] = kn[1]

        qn = max_half_norms(jnp.max(jnp.dot(q * q, half, preferred_element_type=F32), axis=0, keepdims=True))
        qk_bound = [DIFF_NORM_SLACK * qn[mm] * kn_ref[mm] for mm in range(2)]
        q0_f = q0.astype(F32)

        def skip_below(state):
            need = jnp.maximum((DIFF_SKIP_LOG + qk_bound[0]) - state[0],
                               (DIFF_SKIP_LOG + qk_bound[1]) - state[2])
            return q0_f - jnp.max(need) * inv_slope
    else:
        def skip_below(state):
            return jnp.float32(NEG_INF)

    def seq_group(n):
        return n_groups - 1 - n

    def needed(g, thr):
        top = ((g + 1) * (gsz * kt)).astype(F32)
        return jnp.logical_and(g >= 0, jnp.logical_or(g >= n_fullg, top > thr))

    def tiles_of(g):
        return [jnp.clip(g * gsz + u, 0, nk - 1) for u in range(gsz)]

    def scores_stage(g, s_ref):
        js = tiles_of(g)
        for mm in range(2):
            for u in range(gsz):
                off = pl.multiple_of(js[u] * kt, kt)
                kk = k_ref[0, pl.ds(off, kt), :]
                s_ref[mm, u] =lax.dot_general(kk, qz[mm], (((1,), (1,)), ((), ())),
                                               preferred_element_type=F32)

    def softmax_stage(g, thr, s_ref, p_ref, state):
        ok = needed(g, thr)
        full = jnp.logical_and(ok, g < n_fullg)
        sj = jnp.where(full, slope * (g * gsz * kt - q0).astype(F32), 0.0)
        bidx = [jnp.where(full, u, jnp.where(ok, gsz + g * gsz + u - m_tiles, absent)) for u in range(gsz)]
        new_state, alphas = [], []
        for mm in range(2):
            m, l = state[2 * mm], state[2 * mm + 1]
            ss = [s_ref[mm, u] + bt_ref[bidx[u]] for u in range(gsz)]
            cmax = jnp.max(ss[0], axis=0, keepdims=True)
            for s in ss[1:]:
                cmax = jnp.maximum(cmax, jnp.max(s, axis=0, keepdims=True))
            m_new = jnp.maximum(m, cmax + sj)
            m_safe = jnp.where(m_new == NEG_INF, 0.0, m_new)
            alpha = jnp.exp(m - m_safe)
            r = m_safe - sj
            psum = jnp.zeros((1, tq), F32)
            for u in range(gsz):
                p = jnp.exp(ss[u] - r)
                psum = psum + jnp.sum(p, axis=0, keepdims=True)
                p_ref[mm, u] = p.astype(BF16)
            new_state += [m_new, alpha * l + psum]
            alphas.append(alpha)
        return tuple(new_state), tuple(alphas)

    def value_stage(g, p_ref, alphas):
        js = tiles_of(g)
        for mm in range(2):
            pv = jnp.dot(vt_ref[0, 0, js[0]], p_ref[mm, 0], preferred_element_type=F32)
            for u in range(1, gsz):
                pv = pv + jnp.dot(vt_ref[0, 0, js[u]], p_ref[mm, u], preferred_element_type=F32)
            acc_ref[mm] = alphas[mm] * acc_ref[mm] + pv

    def trip(carry):
        t, thr, state, alpha_prev = carry
        n0 = 2 * t
        scores_stage(seq_group(n0 + 1), s1_ref)
        state, alpha0 = softmax_stage(seq_group(n0), thr, s0_ref, p0_ref, state)
        value_stage(seq_group(n0 - 1), p1_ref, alpha_prev)
        scores_stage(seq_group(n0 + 2), s0_ref)
        state, alpha1 = softmax_stage(seq_group(n0 + 1), thr, s1_ref, p1_ref, state)
        value_stage(seq_group(n0), p0_ref, alpha0)
        return t + 1, skip_below(state), state, alpha1

    def more(carry):
        t, thr = carry[0], carry[1]
        return needed(seq_group(2 * t), thr)

    neg = jnp.full((1, tq), NEG_INF, F32)
    zer = jnp.zeros((1, tq), F32)
    one = jnp.ones((1, tq), F32)
    scores_stage(seq_group(0), s0_ref)
    n_done, _, state, alpha_last = lax.while_loop(
        more, trip, (jnp.int32(0), jnp.float32(NEG_INF), (neg, zer, neg, zer), (one, one)))
    value_stage(seq_group(2 * n_done - 1), p1_ref, alpha_last)
    stats = [(state[0], state[1]), (state[2], state[3])]

    lp = lam_ref[...]
    lam = (jnp.exp(jnp.sum(lp[0:1] * lp[1:2], axis=-1, keepdims=True))
           - jnp.exp(jnp.sum(lp[2:3] * lp[3:4], axis=-1, keepdims=True)) + LAMBDA_INIT)
    l1 = stats[0][1]
    l2 = stats[1][1]
    l1 = jnp.where(l1 == 0.0, 1.0, l1)
    l2 = jnp.where(l2 == 0.0, 1.0, l2)
    o_t = acc_ref[0] / l1 - lam * (acc_ref[1] / l2)
    o = o_t.T
    y = o * lax.rsqrt(jnp.mean(o * o, axis=-1, keepdims=True) + NORM_EPS)
    o_ref[0] = ((y * g_ref[...]) * (1.0 - LAMBDA_INIT)).astype(o_ref.dtype)


def _diff_scratch(tq, kt):
    return [pltpu.VMEM((2, LANES, tq), F32),
            pltpu.VMEM((DIFF_GROUP * (1 + DIFF_TAIL_GROUPS) + 1, kt, tq), F32),
            pltpu.VMEM((2, DIFF_GROUP, kt, tq), F32),
            pltpu.VMEM((2, DIFF_GROUP, kt, tq), F32),
            pltpu.VMEM((2, DIFF_GROUP, kt, tq), BF16),
            pltpu.VMEM((2, DIFF_GROUP, kt, tq), BF16),
            pltpu.SMEM((2,), F32)]


def _diff_param_specs():
    return [pl.BlockSpec(memory_space=pltpu.SMEM),
            pl.BlockSpec((4, DH_DIFF), lambda b, h, i: (0, 0)),
            pl.BlockSpec((1, DV_DIFF), lambda b, h, i: (0, 0))]


def _diff_attention(q, k, vt, slopes, lam_p, subln_g, *, batch, nq, tq, kt, q_pos0, coff, n_keys,
                    early_stop):
    rk = k.shape[1]
    nk = rk // kt
    q_rows = q.shape[1]
    assert (nq - 1) * tq < q_rows <= nq * tq
    return pl.pallas_call(
        functools.partial(_diff_kernel, tq=tq, kt=kt, q_pos0=q_pos0, coff=coff, n_keys=n_keys, nk=nk,
                          q_rows=q_rows, early_stop=early_stop),
        grid=(batch, SEC // LANES, nq),
        in_specs=_diff_param_specs() + [
            pl.BlockSpec((1, tq, LANES), lambda b, h, i: (b, i, h)),
            pl.BlockSpec((1, rk, LANES), lambda b, h, i: (b, 0, h)),
            pl.BlockSpec((1, 1, nk, LANES, kt), lambda b, h, i: (b, h, 0, 0, 0)),
        ],
        out_specs=pl.BlockSpec((1, tq, LANES), lambda b, h, i: (b, i, h)),
        out_shape=jax.ShapeDtypeStruct((batch, q_rows, SEC), BF16),
        scratch_shapes=_diff_scratch(tq, kt),
        compiler_params=_cparams(("parallel", "parallel", "arbitrary")),
        name="diff_attention",
    )(slopes, lam_p, subln_g.reshape(1, DV_DIFF), q, k, vt)


def _merge_kernel(h_ref, osb_ref, od_ref, gt_ref, wsb_ref, wd_ref, wo_ref, gn_ref, wr_ref, br_ref,
                  h2_ref, hn_ref, te_ref, tg_ref):
    y_sb = jnp.dot(osb_ref[...], wsb_ref[...], preferred_element_type=F32)
    y_d = jnp.dot(od_ref[...], wd_ref[...], preferred_element_type=F32)
    gt = gt_ref[...]
    mix = gt[:, :D_MODEL] * y_sb + gt[:, D_MODEL:] * y_d
    h2 = h_ref[...] + jnp.dot(mix.astype(BF16), wo_ref[...], preferred_element_type=F32)
    h2_ref[...] = h2
    hn = (h2 * lax.rsqrt(jnp.mean(h2 * h2, axis=-1, keepdims=True) + NORM_EPS)) * gn_ref[...]
    hn_ref[...] = hn
    logits = jnp.dot(hn, wr_ref[...], preferred_element_type=F32,
                     precision=lax.Precision.HIGHEST) + br_ref[...]
    tm = logits.shape[0]
    lane = lax.broadcasted_iota(jnp.int32, (tm, LANES), 1).astype(F32)
    tops, idxs = [], []
    l = logits
    for _ in range(TOP_K):
        m = jnp.max(l, axis=-1, keepdims=True)
        idx = jnp.min(jnp.where(l == m, lane, float(LANES)), axis=-1, keepdims=True)
        tops.append(m)
        idxs.append(idx)
        l = jnp.where(lane == idx, NEG_INF, l)
    ex = [jnp.exp(t - tops[0]) for t in tops]
    den = ex[0] + ex[1] + ex[2] + ex[3]
    te = jnp.zeros((tm, LANES), F32)
    tg = jnp.zeros((tm, LANES), F32)
    for k in range(TOP_K):
        te = jnp.where(lane == float(k), idxs[k], te)
        tg = jnp.where(lane == float(k), ex[k] / den, tg)
    te_ref[...] = te.astype(jnp.int32)
    tg_ref[...] = tg


def _merge(h, o_sb, o_d, gates, w_sb, w_d, w_o, g_ffn, w_r, b_r, tm):
    r, d = h.shape
    row = lambda w: pl.BlockSpec((tm, w), lambda i: (i, 0))
    full = lambda a, b: pl.BlockSpec((a, b), lambda i: (0, 0))
    return pl.pallas_call(
        _merge_kernel,
        grid=(r // tm,),
        in_specs=[row(d), row(d), row(d), row(2 * d), full(d, d), full(d, d), full(d, d),
                  full(1, d), full(d, LANES), full(1, LANES)],
        out_specs=[row(d), row(d), row(LANES), row(LANES)],
        out_shape=[jax.ShapeDtypeStruct((r, d), F32), jax.ShapeDtypeStruct((r, d), F32),
                   jax.ShapeDtypeStruct((r, LANES), jnp.int32), jax.ShapeDtypeStruct((r, LANES), F32)],
        compiler_params=_cparams(("parallel",)),
        name="merge_router",
    )(h, o_sb, o_d, gates, w_sb, w_d, w_o, g_ffn.reshape(1, d), w_r, b_r)


def _moe_kernel(be_ref, nv_ref, x_ref, wgu_ref, bgu_ref, wdn_ref, bdn_ref, o_ref, wgu_bf, wdn_bf):
    b = pl.program_id(0)
    used = nv_ref[b] > 0
    new_expert = jnp.logical_or(b == 0, be_ref[b] != be_ref[jnp.maximum(b - 1, 0)])

    @pl.when(jnp.logical_and(used, new_expert))
    def _():
        wgu_bf[...] = wgu_ref[0].astype(BF16)
        wdn_bf[...] = wdn_ref[0].astype(BF16)

    @pl.when(used)
    def _():
        row = lax.broadcasted_iota(jnp.int32, x_ref.shape, 0)
        x = jnp.where(row < nv_ref[b], x_ref[...], 0.0).astype(BF16)
        gu = jnp.dot(x, wgu_bf[...], preferred_element_type=F32) + bgu_ref[0]
        g = jnp.minimum(gu[:, :D_FF], SWIGLU_LIMIT)
        u = jnp.clip(gu[:, D_FF:], -SWIGLU_LIMIT, SWIGLU_LIMIT)
        act = (u + 1.0) * (g * jax.nn.sigmoid(SWIGLU_ALPHA * g))
        o_ref[...] = jnp.dot(act.astype(BF16), wdn_bf[...], preferred_element_type=F32) + bdn_ref[0]

    @pl.when(jnp.logical_not(used))
    def _():
        o_ref[...] = jnp.zeros_like(o_ref)


def _moe_gmm(block_e, n_valid, xs, w_gu, b_gu, w_dn, b_dn, bm):
    nb = block_e.shape[0]
    n_rows = nb * bm
    d = xs.shape[1]
    grid_spec = pltpu.PrefetchScalarGridSpec(
        num_scalar_prefetch=2,
        grid=(nb,),
        in_specs=[
            pl.BlockSpec((bm, d), lambda b, be, nu: (b, 0)),
            pl.BlockSpec((1, d, 2 * D_FF), lambda b, be, nu: (be[b], 0, 0)),
            pl.BlockSpec((1, 1, 2 * D_FF), lambda b, be, nu: (be[b], 0, 0)),
            pl.BlockSpec((1, D_FF, d), lambda b, be, nu: (be[b], 0, 0)),
            pl.BlockSpec((1, 1, d), lambda b, be, nu: (be[b], 0, 0)),
        ],
        out_specs=pl.BlockSpec((bm, d), lambda b, be, nu: (b, 0)),
        scratch_shapes=[pltpu.VMEM((d, 2 * D_FF), BF16), pltpu.VMEM((D_FF, d), BF16)],
    )
    return pl.pallas_call(
        _moe_kernel,
        grid_spec=grid_spec,
        out_shape=jax.ShapeDtypeStruct((n_rows, d), F32),
        compiler_params=_cparams(("arbitrary",)),
        name="moe_experts",
    )(block_e, n_valid, xs, w_gu, b_gu, w_dn, b_dn)


def _final_kernel(h_ref, c_ref, tg_ref, g_ref, o_ref):
    tm = h_ref.shape[0]
    tg = tg_ref[...]
    y = c_ref[0:tm, :] * tg[:, 0:1]
    for k in range(1, TOP_K):
        y = y + c_ref[k * tm:(k + 1) * tm, :] * tg[:, k:k + 1]
    x = h_ref[...] + y
    o_ref[...] = (x * lax.rsqrt(jnp.mean(x * x, axis=-1, keepdims=True) + NORM_EPS)) * g_ref[...]


def _final(h2, contrib, tg, g, tm, c_row0=0):
    r, d = h2.shape
    assert c_row0 % (TOP_K * tm) == 0
    c_blk0 = c_row0 // (TOP_K * tm)
    return pl.pallas_call(
        _final_kernel,
        grid=(r // tm,),
        in_specs=[pl.BlockSpec((tm, d), lambda i: (i, 0)),
                  pl.BlockSpec((TOP_K * tm, d), lambda i: (i + c_blk0, 0)),
                  pl.BlockSpec((tm, LANES), lambda i: (i, 0)),
                  pl.BlockSpec((1, d), lambda i: (0, 0))],
        out_specs=pl.BlockSpec((tm, d), lambda i: (i, 0)),
        out_shape=jax.ShapeDtypeStruct((r, d), F32),
        compiler_params=_cparams(("parallel",)),
        name="combine_final_norm",
    )(h2, contrib, tg, g.reshape(1, d))


def _rank_kernel(te_ref, tril_ref, rank_ref, cnt_ref, run_ref):
    @pl.when(pl.program_id(0) == 0)
    def _():
        run_ref[...] = jnp.zeros_like(run_ref)

    te = te_ref[...]
    tm = te.shape[0]
    lane = lax.broadcasted_iota(jnp.int32, (tm, LANES), 1)
    onehots = [(lane == te[:, k:k + 1]).astype(F32) for k in range(TOP_K)]
    member = onehots[0] + onehots[1] + onehots[2] + onehots[3]
    before = jnp.dot(tril_ref[...], member.astype(BF16), preferred_element_type=F32) + run_ref[...]
    rank = jnp.zeros((tm, LANES), F32)
    for k in range(TOP_K):
        r_k = jnp.sum(onehots[k] * before, axis=-1, keepdims=True)
        rank = jnp.where(lane == k, r_k, rank)
    rank_ref[...] = rank.astype(jnp.int32)
    run_ref[...] += jnp.sum(member, axis=0, keepdims=True)
    cnt_ref[...] = run_ref[...]


def _ranks(te, tm):
    r = te.shape[0]
    i = lax.broadcasted_iota(jnp.int32, (tm, tm), 0)
    j = lax.broadcasted_iota(jnp.int32, (tm, tm), 1)
    tril = (j < i).astype(BF16)
    return pl.pallas_call(
        _rank_kernel,
        grid=(r // tm,),
        in_specs=[pl.BlockSpec((tm, LANES), lambda b: (b, 0)), pl.BlockSpec((tm, tm), lambda b: (0, 0))],
        out_specs=[pl.BlockSpec((tm, LANES), lambda b: (b, 0)), pl.BlockSpec((1, LANES), lambda b: (0, 0))],
        out_shape=[jax.ShapeDtypeStruct((r, LANES), jnp.int32), jax.ShapeDtypeStruct((1, LANES), F32)],
        scratch_shapes=[pltpu.VMEM((1, LANES), F32)],
        compiler_params=_cparams(("arbitrary",)),
        name="route_ranks",
    )(te, tril)


def _route(te, n_rows, bm):
    rank, cnt = _ranks(te, Q_TILE_PROMPT)
    counts = cnt[0, :N_EXPERTS].astype(jnp.int32)
    padded = ((counts + bm - 1) // bm) * bm
    pend = jnp.cumsum(padded)
    pstart = pend - padded
    top_e = te[:, :TOP_K]
    onehot = top_e[:, :, None] == jnp.arange(N_EXPERTS, dtype=jnp.int32)[None, None, :]
    pos = jnp.sum(jnp.where(onehot, pstart[None, None, :], 0), axis=-1) + rank[:, :TOP_K]
    nb = n_rows // bm
    block_start = jnp.arange(nb, dtype=jnp.int32) * bm
    block_e = jnp.minimum(jnp.sum((pend[None, :] <= block_start[:, None]).astype(jnp.int32), axis=1),
                          N_EXPERTS - 1)
    e_hot = block_e[:, None] == jnp.arange(N_EXPERTS, dtype=jnp.int32)[None, :]
    last = jnp.sum(jnp.where(e_hot, (pstart + counts)[None, :], 0), axis=1)
    n_valid = jnp.clip(last - block_start, 0, bm).astype(jnp.int32)
    return pos, block_e.astype(jnp.int32), n_valid


def _sc_scatter_rows(src, idx_t, n_out):
    n_src, d = src.shape
    sub = SC_GATHER_WINDOW // SC_SUB_WINDOWS
    n_tiles = n_src // SC_GATHER_WINDOW
    idx_t = idx_t.reshape(1, TOP_K * n_src)
    mesh = plsc.VectorSubcoreMesh(core_axis_name="c", subcore_axis_name="s")

    @pl.kernel(out_type=jax.ShapeDtypeStruct((n_out, d), src.dtype), mesh=mesh)
    def scatter_kernel(x_hbm, i_hbm, o_hbm):
        def body(x_vmem, i_vmem):
            s = pl.program_id(2)
            pltpu.sync_copy(x_vmem, o_hbm.at[i_vmem.at[0, pl.ds(s * sub, sub)]])

        pltpu.emit_pipeline(
            body,
            grid=(n_tiles, TOP_K, SC_SUB_WINDOWS),
            in_specs=[pl.BlockSpec((sub, d), index_map=lambda i, k, s: (i * SC_SUB_WINDOWS + s, 0)),
                      pl.BlockSpec((1, SC_GATHER_WINDOW), index_map=lambda i, k, s: (0, k * n_tiles + i))],
            out_specs=[],
            core_axis_name=("c", "s"),
            dimension_semantics=(pltpu.PARALLEL, pltpu.ARBITRARY, pltpu.ARBITRARY),
        )(x_hbm, i_hbm)

    return scatter_kernel(src, idx_t)


def _sc_gather(table, idx):
    n_idx = idx.shape[0]
    d = table.shape[1]
    sub = SC_GATHER_WINDOW // SC_SUB_WINDOWS
    mesh = plsc.VectorSubcoreMesh(core_axis_name="c", subcore_axis_name="s")

    @pl.kernel(out_type=jax.ShapeDtypeStruct((n_idx, d), table.dtype), mesh=mesh)
    def gather_kernel(x_hbm, i_hbm, o_hbm):
        def body(i_vmem, o_vmem):
            j = pl.program_id(1)
            pltpu.sync_copy(x_hbm.at[i_vmem.at[0, pl.ds(j * sub, sub)]], o_vmem)

        pltpu.emit_pipeline(
            body,
            grid=(n_idx // SC_GATHER_WINDOW, SC_SUB_WINDOWS),
            in_specs=[pl.BlockSpec((1, SC_GATHER_WINDOW), index_map=lambda i, j: (0, i))],
            out_specs=[pl.BlockSpec((sub, d), index_map=lambda i, j: (i * SC_SUB_WINDOWS + j, 0))],
            core_axis_name=("c", "s"),
            dimension_semantics=(pltpu.PARALLEL, pltpu.ARBITRARY),
        )(i_hbm, o_hbm)

    return gather_kernel(table, idx.reshape(1, n_idx))


def _tri(kt):
    s = lax.broadcasted_iota(jnp.int32, (kt, kt), 0)
    j = lax.broadcasted_iota(jnp.int32, (kt, kt), 1)
    return (j > s).astype(BF16)


def kernel(x_prompt, x_sample, cache_sb_k, cache_sb_v, cache_diff_k, cache_diff_v, meta_tokens,
           norm_mix_g, w_in, diff_lambda, diff_subln_g, w_br_sb, w_br_diff, w_out, norm_ffn_g,
           w_router, b_router, w_gate_up, b_gate_up, w_down, b_down, final_norm_g):
    assert x_prompt.shape[0] == 1 and w_in.shape[0] == 1
    d = D_MODEL
    seq = x_prompt.shape[1]
    t = N_META + seq
    tq_p, kt = Q_TILE_PROMPT, KEY_TILE
    tp = -(-t // tq_p) * tq_p
    nq_p = tp // tq_p
    nb_s, s_len = x_sample.shape[:2]
    past = cache_sb_k.shape[2]
    n_s = nb_s * s_len
    scale = DH_SB ** -0.5

    w0 = w_in[0]
    wq = jnp.concatenate([w0[:, 0:SEC], w0[:, 3 * SEC:4 * SEC]], axis=1).astype(BF16)
    wk = jnp.concatenate([w0[:, SEC:2 * SEC], w0[:, 4 * SEC:5 * SEC]], axis=1).astype(BF16)
    wv = jnp.concatenate([w0[:, 2 * SEC:3 * SEC], w0[:, 5 * SEC:6 * SEC]], axis=1).astype(BF16)
    wg = w0[:, 6 * SEC:8 * SEC].astype(BF16)
    w_sb = w_br_sb[0].astype(BF16)
    w_d = w_br_diff[0].astype(BF16)
    w_o = w_out[0].astype(BF16)
    w_r = jnp.pad(w_router[0], ((0, 0), (0, LANES - N_EXPERTS)))
    b_r = jnp.pad(b_router[0], (0, LANES - N_EXPERTS), constant_values=NEG_INF).reshape(1, LANES)
    w_gu = w_gate_up[0]
    w_dn = w_down[0]
    b_gu = b_gate_up[0].reshape(N_EXPERTS, 1, 2 * D_FF)
    b_dn = b_down[0].reshape(N_EXPERTS, 1, d)
    slope = jnp.exp2(-8.0 * jnp.arange(1, H_DIFF + 1, dtype=F32) / H_DIFF)
    slopes = jnp.stack([slope, 1.0 / slope])
    tri = _tri(kt)
    lam_p = diff_lambda[0]
    g_sub = diff_subln_g[0]

    hp = jnp.concatenate([meta_tokens.astype(F32), x_prompt[0], jnp.zeros((tp - t, d), F32)], axis=0)
    tm_p = tq_p
    xn = _rmsnorm(hp, norm_mix_g[0], tm_p)
    q_sb, q_d = _proj(xn, wq, "q", tm_p, scale=scale)
    kf_sb, kb_sb, kf_d, kb_d = _proj(xn, wk, "k", tm_p, r_out=t)
    vf_sb, vt_sb, vf_d, vt_d = _proj(xn, wv, "v", tm_p, r_out=t)
    (gates,) = _proj(xn, wg, "g", tm_p)
    o_sb = _sb_attention(q_sb[None], kb_sb[None], vt_sb[None], tri, batch=1, nq=nq_p, tq=tq_p, kt=kt,
                         q_pos0=0)
    o_d = _diff_attention(q_d[None], kb_d[None], vt_d[None], slopes, lam_p, g_sub, batch=1,
                          nq=-(-tp // Q_TILE_DIFF), tq=Q_TILE_DIFF, kt=kt, q_pos0=0, coff=N_META, n_keys=t,
                          early_stop=True)
    tm_merge = 640 if tp % 640 == 0 else tq_p
    h2, hn, te, tg = _merge(hp, o_sb[0], o_d[0], gates, w_sb, w_d, w_o, norm_ffn_g[0], w_r, b_r, tm_merge)

    tq_s = Q_TILE_DECODE
    hs = x_sample.reshape(n_s, d)
    xn_s = _rmsnorm(hs, norm_mix_g[0], n_s)
    q2_s = _proj(xn_s, wq, "q", n_s, scale=scale)
    kf_s0, _, kf_s1, _ = _proj(xn_s, wk, "k", n_s)
    vf_s0, _, vf_s1, _ = _proj(xn_s, wv, "v", n_s, kt=n_s)
    kf_s = (kf_s0, kf_s1)
    vf_s = (vf_s0, vf_s1)
    (gates_s,) = _proj(xn_s, wg, "g", n_s)

    def pad_q(qs):
        return jnp.pad(qs.reshape(nb_s, s_len, SEC), ((0, 0), (0, tq_s - s_len), (0, 0)))

    def per_stream(a):
        return a.reshape(nb_s, -1, SEC)

    def keys_last(cache):
        return jnp.transpose(cache, (0, 2, 3, 1)).reshape(nb_s, SEC, past)

    k_sb_all, vt_sb_all = _cache_tiles(keys_last(cache_sb_k[0]), keys_last(cache_sb_v[0]),
                                       per_stream(kf_s[0]), per_stream(vf_s[0]), kt, True)
    k_d_all, vt_d_all = _cache_tiles(cache_diff_k[0], cache_diff_v[0], per_stream(kf_s[1]),
                                     per_stream(vf_s[1]), kt, False)
    o_sb_s = _sb_attention(pad_q(q2_s[0]), k_sb_all, vt_sb_all, tri, batch=nb_s, nq=1, tq=tq_s, kt=kt,
                           q_pos0=past)
    o_d_s = _diff_attention(pad_q(q2_s[1]), k_d_all, vt_d_all, slopes, lam_p, g_sub, batch=nb_s, nq=1,
                            tq=tq_s, kt=kt, q_pos0=past, coff=0, n_keys=past + s_len, early_stop=False)
    o_sb_s = o_sb_s[:, :s_len].reshape(n_s, SEC)
    o_d_s = o_d_s[:, :s_len].reshape(n_s, SEC)
    h2_s, hn_s, te_s, tg_s = _merge(hs, o_sb_s, o_d_s, gates_s, w_sb, w_d, w_o, norm_ffn_g[0], w_r, b_r,
                                    n_s)

    n_tok = t + n_s
    r_tok = -(-n_tok // tq_p) * tq_p
    bm = MOE_ROWS
    n_rows = -(-(n_tok * TOP_K + N_EXPERTS * (bm - 1)) // bm) * bm
    te_all = jnp.concatenate([te[:t], te_s, jnp.full((r_tok - n_tok, LANES), -1, jnp.int32)], axis=0)
    hn_all = jnp.concatenate([hn[:t], hn_s, jnp.zeros((r_tok - n_tok, d), F32)], axis=0)
    pos, block_e, n_valid = _route(te_all, n_rows, bm)
    pos_d = jnp.where(te_all[:, :1] >= 0, pos, n_rows)
    xs = _sc_scatter_rows(hn_all, pos_d.T, n_rows + 8)
    yb = _moe_gmm(block_e, n_valid, xs, w_gu, b_gu, w_dn, b_dn, bm)
    def by_block(p, tm):
        return p.reshape(-1, tm, TOP_K).transpose(0, 2, 1).reshape(-1)

    pos_p = by_block(jnp.concatenate([pos[:t], jnp.zeros((tp - t, TOP_K), jnp.int32)], axis=0), tq_p)
    pos_s = by_block(pos[t:n_tok], n_s)
    g_mult = SC_GATHER_WINDOW * 8
    g_pad = (-(pos_p.shape[0] + pos_s.shape[0])) % g_mult
    contrib = _sc_gather(yb, jnp.concatenate([pos_p, pos_s, jnp.zeros((g_pad,), jnp.int32)]))
    y_p = _final(h2, contrib, tg, final_norm_g, tq_p)
    y_s = _final(h2_s, contrib, tg_s, final_norm_g, n_s, c_row0=tp * TOP_K)

    y_prompt = y_p[N_META:t][None]
    y_sample = y_s.reshape(nb_s, s_len, d)

    def heads(a, nh):
        return a.reshape(1, *a.shape[:-1], nh, a.shape[-1] // nh)

    return (y_prompt, y_sample,
            heads(kf_sb[None], H_SB), heads(vf_sb[None], H_SB),
            heads(kf_d[None], H_DIFF), heads(vf_d[None], H_DIFF),
            heads(kf_s[0].reshape(nb_s, s_len, SEC), H_SB), heads(vf_s[0].reshape(nb_s, s_len, SEC), H_SB),
            heads(kf_s[1].reshape(nb_s, s_len, SEC), H_DIFF), heads(vf_s[1].reshape(nb_s, s_len, SEC), H_DIFF))
```

```python
import functools
import math

import jax
import jax.numpy as jnp
from jax import lax
from jax.experimental import pallas as pl
from jax.experimental.pallas import tpu as pltpu
from jax.experimental.pallas import tpu_sc as plsc

F32 = jnp.float32
BF16 = jnp.bfloat16

D_MODEL = 1024
CHUNK = 64
CHUNK_SHIFT = 6
N_META = 16
H_SB = 16
DH_SB = 64
H_DIFF = 8
DH_DIFF = 64
DV_DIFF = 128
N_EXPERTS = 32
TOP_K = 4
D_FF = 1024
SWIGLU_LIMIT = 7.0
SWIGLU_ALPHA = 1.702
NORM_EPS = 1e-5
LAMBDA_INIT = 0.8 - 0.6 * math.exp(-0.3 * 0)

LANES = 128
SEC = 1024
KEY_TILE = 256
Q_TILE_PROMPT = 256
Q_TILE_DIFF = 512
Q_TILE_DECODE = 128
MOE_ROWS = 512
VMEM_LIMIT = 56 * 1024 * 1024
NEG_INF = float("-inf")
SB_SKIP_LOG = 104.0
SB_ABSENT = 1e30
DIFF_GROUP = 2
DIFF_TAIL_GROUPS = 2
DIFF_SKIP_LOG = 105.0
DIFF_NORM_SLACK = 1.01
SC_GATHER_WINDOW = 128
SC_SUB_WINDOWS = 4


def _cparams(sem):
    return pltpu.CompilerParams(dimension_semantics=sem, vmem_limit_bytes=VMEM_LIMIT)


def _rmsnorm_kernel(x_ref, g_ref, o_ref):
    x = x_ref[...]
    y = x * lax.rsqrt(jnp.mean(x * x, axis=-1, keepdims=True) + NORM_EPS)
    o_ref[...] = (y * g_ref[...]).astype(o_ref.dtype)


def _rmsnorm(x, g, tm):
    r, d = x.shape
    return pl.pallas_call(
        _rmsnorm_kernel,
        grid=(r // tm,),
        in_specs=[pl.BlockSpec((tm, d), lambda i: (i, 0)),
                  pl.BlockSpec((1, d), lambda i: (0, 0))],
        out_specs=pl.BlockSpec((tm, d), lambda i: (i, 0)),
        out_shape=jax.ShapeDtypeStruct((r, d), BF16),
        compiler_params=_cparams(("parallel",)),
        name="rmsnorm",
    )(x, g.reshape(1, d))


def _proj_kernel(x_ref, w_ref, *out_refs, mode, scale, kt):
    x = x_ref[...]
    for s in range(2):
        acc = jnp.dot(x, w_ref[:, s * SEC:(s + 1) * SEC], preferred_element_type=F32)
        if mode == "q":
            out_refs[s][...] = (acc * scale).astype(BF16)
        elif mode == "f":
            out_refs[s][...] = acc
        elif mode == "k":
            out_refs[2 * s][...] = acc
            out_refs[2 * s + 1][...] = acc.astype(BF16)
        elif mode == "v":
            out_refs[2 * s][...] = acc
            for c in range(acc.shape[0] // kt):
                for hp in range(SEC // LANES):
                    tile = acc[c * kt:(c + 1) * kt, hp * LANES:(hp + 1) * LANES]
                    out_refs[2 * s + 1][hp, c] = tile.T.astype(BF16)
        else:
            out_refs[0][:, s * SEC:(s + 1) * SEC] = jax.nn.sigmoid(acc)


def _proj(xn, w2, mode, tm, r_out=None, kt=KEY_TILE, scale=1.0):
    r, d = xn.shape
    r_out = r if r_out is None else r_out
    in_specs = [pl.BlockSpec((tm, d), lambda i: (i, 0)),
                pl.BlockSpec((d, 2 * SEC), lambda i: (0, 0))]
    row_spec = pl.BlockSpec((tm, SEC), lambda i: (i, 0))
    f32_out = jax.ShapeDtypeStruct((r_out, SEC), F32)
    bf16_out = jax.ShapeDtypeStruct((r, SEC), BF16)
    if mode == "q":
        out_specs = [row_spec, row_spec]
        out_shape = [bf16_out, bf16_out]
    elif mode == "f":
        out_specs = [row_spec, row_spec]
        out_shape = [f32_out, f32_out]
    elif mode == "k":
        out_specs = [row_spec] * 4
        out_shape = [f32_out, bf16_out, f32_out, bf16_out]
    elif mode == "v":
        t_spec = pl.BlockSpec((SEC // LANES, tm // kt, LANES, kt), lambda i: (0, i, 0, 0))
        t_out = jax.ShapeDtypeStruct((SEC // LANES, r // kt, LANES, kt), BF16)
        out_specs = [row_spec, t_spec, row_spec, t_spec]
        out_shape = [f32_out, t_out, f32_out, t_out]
    else:
        out_specs = [pl.BlockSpec((tm, 2 * SEC), lambda i: (i, 0))]
        out_shape = [jax.ShapeDtypeStruct((r, 2 * SEC), F32)]
    return pl.pallas_call(
        functools.partial(_proj_kernel, mode=mode, scale=scale, kt=kt),
        grid=(r // tm,), in_specs=in_specs, out_specs=out_specs, out_shape=out_shape,
        compiler_params=_cparams(("parallel",)),
        name="proj_" + mode,
    )(xn, w2)


def _cache_tiles_kernel(kc_ref, vc_ref, kn_ref, vn_ref, k_out, vt_out, *, kt, keys_minor):
    j = pl.program_id(1)
    n_cache = pl.num_programs(1) - 1
    nhb = SEC // LANES

    @pl.when(j < n_cache)
    def _():
        for hb in range(nhb):
            cols = slice(hb * LANES, (hb + 1) * LANES)
            if keys_minor:
                k_out[0, :, cols] = kc_ref[0, cols, :].T.astype(BF16)
                vt_out[0, hb, 0] = vc_ref[0, cols, :].astype(BF16)
            else:
                k_out[0, :, cols] = kc_ref[0, :, hb, :].astype(BF16)
                vt_out[0, hb, 0] = vc_ref[0, :, hb, :].T.astype(BF16)

    @pl.when(j == n_cache)
    def _():
        s_len = kn_ref.shape[1]
        pad = jnp.zeros((kt - s_len, SEC), F32)
        k_out[0] = jnp.concatenate([kn_ref[0], pad], axis=0).astype(BF16)
        v_new = jnp.concatenate([vn_ref[0], pad], axis=0)
        for hb in range(nhb):
            vt_out[0, hb, 0] = v_new[:, hb * LANES:(hb + 1) * LANES].T.astype(BF16)


def _cache_tiles(k_cache, v_cache, k_new, v_new, kt, keys_minor):
    b = k_cache.shape[0]
    past = k_cache.shape[2] if keys_minor else k_cache.shape[1]
    s_len = k_new.shape[1]
    assert past % kt == 0 and s_len <= kt
    n_cache = past // kt
    if keys_minor:
        cache_spec = pl.BlockSpec((1, SEC, kt), lambda i, j: (i, 0, jnp.minimum(j, n_cache - 1)))
    else:
        cache_spec = pl.BlockSpec((1, kt, SEC // LANES, LANES),
                                  lambda i, j: (i, jnp.minimum(j, n_cache - 1), 0, 0))
    new_spec = pl.BlockSpec((1, s_len, SEC), lambda i, j: (i, 0, 0))
    return pl.pallas_call(
        functools.partial(_cache_tiles_kernel, kt=kt, keys_minor=keys_minor),
        grid=(b, n_cache + 1),
        in_specs=[cache_spec, cache_spec, new_spec, new_spec],
        out_specs=[pl.BlockSpec((1, kt, SEC), lambda i, j: (i, j, 0)),
                   pl.BlockSpec((1, SEC // LANES, 1, LANES, kt), lambda i, j: (i, 0, j, 0, 0))],
        out_shape=[jax.ShapeDtypeStruct((b, past + kt, SEC), BF16),
                   jax.ShapeDtypeStruct((b, SEC // LANES, n_cache + 1, LANES, kt), BF16)],
        compiler_params=_cparams(("parallel", "arbitrary")),
        name="cache_tiles",
    )(k_cache, v_cache, k_new, v_new)


def _sb_kernel(q_ref, k_ref, vt_ref, tri_ref, o_ref, acc_ref, *, tq, kt, q_pos0):
    qi = pl.program_id(2)
    q0 = q_pos0 + qi * tq
    jm = q0 // kt
    q = q_ref[0]
    lane = lax.broadcasted_iota(jnp.int32, (tq, LANES), 1)
    kpos_l = lax.broadcasted_iota(jnp.int32, (kt, tq), 0)
    qpos_l = lax.broadcasted_iota(jnp.int32, (kt, tq), 1)
    tri = tri_ref[...]
    zero = jnp.zeros_like(q)
    qz = (jnp.where(lane < DH_SB, q, zero), jnp.where(lane >= DH_SB, q, zero))
    acc_ref[...] = jnp.zeros_like(acc_ref)

    def tile(j, hh, carry, masked):
        off = pl.multiple_of(j * kt, kt)
        kk = k_ref[0, pl.ds(off, kt), :]
        z = lax.dot_general(kk, qz[hh], (((1,), (1,)), ((), ())),
                            preferred_element_type=F32)
        sp = jnp.maximum(z, 0.0) + jnp.log(1.0 + jnp.exp(-jnp.abs(z)))
        if masked:
            vis = (kpos_l + j * kt) < (qpos_l + q0)
            lk = jnp.where(vis, -sp, 0.0)
        else:
            lk = -sp
        later = jnp.dot(tri, lk.astype(BF16), preferred_element_type=F32)
        w = jnp.exp(z + lk + later + carry)
        if masked:
            w = jnp.where(vis, w, 0.0)
        acc_ref[hh] += jnp.dot(vt_ref[0, 0, j], w.astype(BF16), preferred_element_type=F32)
        return jnp.sum(lk, axis=0, keepdims=True)

    none_later = jnp.zeros((1, tq), F32)
    carries = tuple(tile(jm, hh, none_later, True) for hh in range(2))

    def pair(j0, ca, cb):
        j1 = j0 - 1
        out = []
        for hh, carry in ((0, ca), (1, cb)):
            s0 = tile(jnp.maximum(j0, 0), hh, jnp.where(j0 >= 0, carry, -SB_ABSENT), False)
            s1 = tile(jnp.maximum(j1, 0), hh, jnp.where(j1 >= 0, carry + s0, -SB_ABSENT), False)
            out.append(carry + s0 + s1)
        return out

    jb = jm - 1
    first = []
    for hh in range(2):
        s = tile(jnp.maximum(jb, 0), hh, jnp.where(jb >= 0, carries[hh], -SB_ABSENT), False)
        first.append(carries[hh] + s)

    def cond(state):
        t, ca, cb = state
        return jnp.logical_and(jm - 2 - 2 * t >= 0,
                               jnp.max(jnp.maximum(ca, cb)) > -SB_SKIP_LOG)

    def body(state):
        t, ca, cb = state
        ca, cb = pair(jm - 2 - 2 * t, ca, cb)
        return t + 1, ca, cb

    lax.while_loop(cond, body, (jnp.int32(0), first[0], first[1]))

    row = lax.broadcasted_iota(jnp.int32, (LANES, tq), 0)
    o_t = jnp.where(row < DH_SB, acc_ref[0], acc_ref[1])
    o_ref[0] = o_t.T.astype(o_ref.dtype)


def _sb_attention(q, k, vt, tri, *, batch, nq, tq, kt, q_pos0):
    rk = k.shape[1]
    nk = rk // kt
    nhp = SEC // LANES
    return pl.pallas_call(
        functools.partial(_sb_kernel, tq=tq, kt=kt, q_pos0=q_pos0),
        grid=(batch, nhp, nq),
        in_specs=[
            pl.BlockSpec((1, tq, LANES), lambda b, h, i: (b, i, h)),
            pl.BlockSpec((1, rk, LANES), lambda b, h, i: (b, 0, h)),
            pl.BlockSpec((1, 1, nk, LANES, kt), lambda b, h, i: (b, h, 0, 0, 0)),
            pl.BlockSpec((kt, kt), lambda b, h, i: (0, 0)),
        ],
        out_specs=pl.BlockSpec((1, tq, LANES), lambda b, h, i: (b, i, h)),
        out_shape=jax.ShapeDtypeStruct((batch, nq * tq, SEC), BF16),
        scratch_shapes=[pltpu.VMEM((2, LANES, tq), F32)],
        compiler_params=_cparams(("parallel", "parallel", "arbitrary")),
        name="sb_attention",
    )(q, k, vt, tri)


def _diff_kernel(slopes_ref, lam_ref, g_ref, q_ref, k_ref, vt_ref, o_ref,
                 acc_ref, bt_ref, s0_ref, s1_ref, p0_ref, p1_ref, kn_ref,
                 *, tq, kt, q_pos0, coff, n_keys, nk, q_rows, early_stop):
    h = pl.program_id(1)
    qi = pl.program_id(2)
    slope = slopes_ref[0, h]
    inv_slope = slopes_ref[1, h]
    q0 = q_pos0 + qi * tq
    c_lo = (q0 - coff + CHUNK) // CHUNK - 1
    c_hi = (q0 + tq - 1 - coff + CHUNK) // CHUNK - 1
    full_end = jnp.minimum(coff + CHUNK * (c_lo + 1), n_keys)
    vis_end = jnp.minimum(coff + CHUNK * (c_hi + 1), n_keys)
    n_full = full_end // kt
    n_vis = jnp.minimum((vis_end + kt - 1) // kt, nk)

    q = q_ref[0]
    qrow = lax.broadcasted_iota(jnp.int32, (tq, LANES), 0)
    q = jnp.where(qrow < q_rows - qi * tq, q, jnp.zeros_like(q))
    lane = lax.broadcasted_iota(jnp.int32, (tq, LANES), 1)
    kpos_l = lax.broadcasted_iota(jnp.int32, (kt, tq), 0)
    qpos_l = lax.broadcasted_iota(jnp.int32, (kt, tq), 1)
    gsz = DIFF_GROUP
    m_tiles = (n_full // gsz) * gsz
    n_fullg = m_tiles // gsz
    n_groups = n_fullg + DIFF_TAIL_GROUPS
    n_tail = DIFF_TAIL_GROUPS * gsz
    absent = gsz + n_tail

    for u in range(gsz):
        bt_ref[u] = slope * (kpos_l + u * kt).astype(F32)
    for u in range(n_tail):
        kpos = kpos_l + (m_tiles + u) * kt
        qpos = qpos_l + q0
        vis = (((kpos - coff + CHUNK) >> CHUNK_SHIFT) <= ((qpos - coff + CHUNK) >> CHUNK_SHIFT)) \
            & (kpos < n_keys)
        bias = slope * (qpos_l - jnp.abs(qpos - kpos)).astype(F32)
        bt_ref[gsz + u] = jnp.where(vis, bias, NEG_INF)
    bt_ref[absent] = jnp.full((kt, tq), NEG_INF, F32)

    zero = jnp.zeros_like(q)
    qz = (jnp.where(lane < DH_DIFF, q, zero), jnp.where(lane >= DH_DIFF, q, zero))
    acc_ref[...] = jnp.zeros_like(acc_ref)
    p1_ref[...] = jnp.zeros_like(p1_ref)

    if early_stop:
        hr = lax.broadcasted_iota(jnp.int32, (LANES, LANES), 0)
        hc = lax.broadcasted_iota(jnp.int32, (LANES, LANES), 1)
        half = (((hc == 0) & (hr < DH_DIFF)) | ((hc == 1) & (hr >= DH_DIFF))).astype(BF16)
        lane1 = lax.broadcasted_iota(jnp.int32, (1, LANES), 1)

        def max_half_norms(sq_max):
            nrm = jnp.sqrt(sq_max)
            return (jnp.max(jnp.where(lane1 == 0, nrm, 0.0)), jnp.max(jnp.where(lane1 == 1, nrm, 0.0)))

        @pl.when(qi == 0)
        def _():
            def body(j, mx):
                off = pl.multiple_of(j * kt, kt)
                kk = k_ref[0, pl.ds(off, kt), :]
                n2 = jnp.dot(kk * kk, half, preferred_element_type=F32)
                return jnp.maximum(mx, jnp.max(n2, axis=0, keepdims=True))
            kn = max_half_norms(lax.fori_loop(0, nk, body, jnp.zeros((1, LANES), F32)))
            kn_ref[0] = kn[0]
            kn_ref[1] = kn[1]

        qn = max_half_norms(jnp.max(jnp.dot(q * q, half, preferred_element_type=F32), axis=0, keepdims=True))
        qk_bound = [DIFF_NORM_SLACK * qn[mm] * kn_ref[mm] for mm in range(2)]
        q0_f = q0.astype(F32)

        def skip_below(state):
            need = jnp.maximum((DIFF_SKIP_LOG + qk_bound[0]) - state[0],
                               (DIFF_SKIP_LOG + qk_bound[1]) - state[2])
            return q0_f - jnp.max(need) * inv_slope
    else:
        def skip_below(state):
            return jnp.float32(NEG_INF)

    def seq_group(n):
        return n_groups - 1 - n

    def needed(g, thr):
        top = ((g + 1) * (gsz * kt)).astype(F32)
        return jnp.logical_and(g >= 0, jnp.logical_or(g >= n_fullg, top > thr))

    def tiles_of(g):
        return [jnp.clip(g * gsz + u, 0, nk - 1) for u in range(gsz)]

    def scores_stage(g, s_ref):
        js = tiles_of(g)
        for mm in range(2):
            for u in range(gsz):
                off = pl.multiple_of(js[u] * kt, kt)
                kk = k_ref[0, pl.ds(off, kt), :]
                s_ref[mm, u] =lax.dot_general(kk, qz[mm], (((1,), (1,)), ((), ())),
                                               preferred_element_type=F32)

    def softmax_stage(g, thr, s_ref, p_ref, state):
        ok = needed(g, thr)
        full = jnp.logical_and(ok, g < n_fullg)
        sj = jnp.where(full, slope * (g * gsz * kt - q0).astype(F32), 0.0)
        bidx = [jnp.where(full, u, jnp.where(ok, gsz + g * gsz + u - m_tiles, absent)) for u in range(gsz)]
        new_state, alphas = [], []
        for mm in range(2):
            m, l = state[2 * mm], state[2 * mm + 1]
            ss = [s_ref[mm, u] + bt_ref[bidx[u]] for u in range(gsz)]
            cmax = jnp.max(ss[0], axis=0, keepdims=True)
            for s in ss[1:]:
                cmax = jnp.maximum(cmax, jnp.max(s, axis=0, keepdims=True))
            m_new = jnp.maximum(m, cmax + sj)
            m_safe = jnp.where(m_new == NEG_INF, 0.0, m_new)
            alpha = jnp.exp(m - m_safe)
            r = m_safe - sj
            psum = jnp.zeros((1, tq), F32)
            for u in range(gsz):
                p = jnp.exp(ss[u] - r)
                psum = psum + jnp.sum(p, axis=0, keepdims=True)
                p_ref[mm, u] = p.astype(BF16)
            new_state += [m_new, alpha * l + psum]
            alphas.append(alpha)
        return tuple(new_state), tuple(alphas)

    def value_stage(g, p_ref, alphas):
        js = tiles_of(g)
        for mm in range(2):
            pv = jnp.dot(vt_ref[0, 0, js[0]], p_ref[mm, 0], preferred_element_type=F32)
            for u in range(1, gsz):
                pv = pv + jnp.dot(vt_ref[0, 0, js[u]], p_ref[mm, u], preferred_element_type=F32)
            acc_ref[mm] = alphas[mm] * acc_ref[mm] + pv

    def trip(carry):
        t, thr, state, alpha_prev = carry
        n0 = 2 * t
        scores_stage(seq_group(n0 + 1), s1_ref)
        state, alpha0 = softmax_stage(seq_group(n0), thr, s0_ref, p0_ref, state)
        value_stage(seq_group(n0 - 1), p1_ref, alpha_prev)
        scores_stage(seq_group(n0 + 2), s0_ref)
        state, alpha1 = softmax_stage(seq_group(n0 + 1), thr, s1_ref, p1_ref, state)
        value_stage(seq_group(n0), p0_ref, alpha0)
        return t + 1, skip_below(state), state, alpha1

    def more(carry):
        t, thr = carry[0], carry[1]
        return needed(seq_group(2 * t), thr)

    neg = jnp.full((1, tq), NEG_INF, F32)
    zer = jnp.zeros((1, tq), F32)
    one = jnp.ones((1, tq), F32)
    scores_stage(seq_group(0), s0_ref)
    n_done, _, state, alpha_last = lax.while_loop(
        more, trip, (jnp.int32(0), jnp.float32(NEG_INF), (neg, zer, neg, zer), (one, one)))
    value_stage(seq_group(2 * n_done - 1), p1_ref, alpha_last)
    stats = [(state[0], state[1]), (state[2], state[3])]

    lp = lam_ref[...]
    lam = (jnp.exp(jnp.sum(lp[0:1] * lp[1:2], axis=-1, keepdims=True))
           - jnp.exp(jnp.sum(lp[2:3] * lp[3:4], axis=-1, keepdims=True)) + LAMBDA_INIT)
    l1 = stats[0][1]
    l2 = stats[1][1]
    l1 = jnp.where(l1 == 0.0, 1.0, l1)
    l2 = jnp.where(l2 == 0.0, 1.0, l2)
    o_t = acc_ref[0] / l1 - lam * (acc_ref[1] / l2)
    o = o_t.T
    y = o * lax.rsqrt(jnp.mean(o * o, axis=-1, keepdims=True) + NORM_EPS)
    o_ref[0] = ((y * g_ref[...]) * (1.0 - LAMBDA_INIT)).astype(o_ref.dtype)


def _diff_scratch(tq, kt):
    return [pltpu.VMEM((2, LANES, tq), F32),
            pltpu.VMEM((DIFF_GROUP * (1 + DIFF_TAIL_GROUPS) + 1, kt, tq), F32),
            pltpu.VMEM((2, DIFF_GROUP, kt, tq), F32),
            pltpu.VMEM((2, DIFF_GROUP, kt, tq), F32),
            pltpu.VMEM((2, DIFF_GROUP, kt, tq), BF16),
            pltpu.VMEM((2, DIFF_GROUP, kt, tq), BF16),
            pltpu.SMEM((2,), F32)]


def _diff_param_specs():
    return [pl.BlockSpec(memory_space=pltpu.SMEM),
            pl.BlockSpec((4, DH_DIFF), lambda b, h, i: (0, 0)),
            pl.BlockSpec((1, DV_DIFF), lambda b, h, i: (0, 0))]


def _diff_attention(q, k, vt, slopes, lam_p, subln_g, *, batch, nq, tq, kt, q_pos0, coff, n_keys,
                    early_stop):
    rk = k.shape[1]
    nk = rk // kt
    q_rows = q.shape[1]
    assert (nq - 1) * tq < q_rows <= nq * tq
    return pl.pallas_call(
        functools.partial(_diff_kernel, tq=tq, kt=kt, q_pos0=q_pos0, coff=coff, n_keys=n_keys, nk=nk,
                          q_rows=q_rows, early_stop=early_stop),
        grid=(batch, SEC // LANES, nq),
        in_specs=_diff_param_specs() + [
            pl.BlockSpec((1, tq, LANES), lambda b, h, i: (b, i, h)),
            pl.BlockSpec((1, rk, LANES), lambda b, h, i: (b, 0, h)),
            pl.BlockSpec((1, 1, nk, LANES, kt), lambda b, h, i: (b, h, 0, 0, 0)),
        ],
        out_specs=pl.BlockSpec((1, tq, LANES), lambda b, h, i: (b, i, h)),
        out_shape=jax.ShapeDtypeStruct((batch, q_rows, SEC), BF16),
        scratch_shapes=_diff_scratch(tq, kt),
        compiler_params=_cparams(("parallel", "parallel", "arbitrary")),
        name="diff_attention",
    )(slopes, lam_p, subln_g.reshape(1, DV_DIFF), q, k, vt)


def _merge_kernel(h_ref, osb_ref, od_ref, gt_ref, wsb_ref, wd_ref, wo_ref, gn_ref, wr_ref, br_ref,
                  h2_ref, hn_ref, te_ref, tg_ref):
    y_sb = jnp.dot(osb_ref[...], wsb_ref[...], preferred_element_type=F32)
    y_d = jnp.dot(od_ref[...], wd_ref[...], preferred_element_type=F32)
    gt = gt_ref[...]
    mix = gt[:, :D_MODEL] * y_sb + gt[:, D_MODEL:] * y_d
    h2 = h_ref[...] + jnp.dot(mix.astype(BF16), wo_ref[...], preferred_element_type=F32)
    h2_ref[...] = h2
    hn = (h2 * lax.rsqrt(jnp.mean(h2 * h2, axis=-1, keepdims=True) + NORM_EPS)) * gn_ref[...]
    hn_ref[...] = hn
    logits = jnp.dot(hn, wr_ref[...], preferred_element_type=F32,
                     precision=lax.Precision.HIGHEST) + br_ref[...]
    tm = logits.shape[0]
    lane = lax.broadcasted_iota(jnp.int32, (tm, LANES), 1).astype(F32)
    tops, idxs = [], []
    l = logits
    for _ in range(TOP_K):
        m = jnp.max(l, axis=-1, keepdims=True)
        idx = jnp.min(jnp.where(l == m, lane, float(LANES)), axis=-1, keepdims=True)
        tops.append(m)
        idxs.append(idx)
        l = jnp.where(lane == idx, NEG_INF, l)
    ex = [jnp.exp(t - tops[0]) for t in tops]
    den = ex[0] + ex[1] + ex[2] + ex[3]
    te = jnp.zeros((tm, LANES), F32)
    tg = jnp.zeros((tm, LANES), F32)
    for k in range(TOP_K):
        te = jnp.where(lane == float(k), idxs[k], te)
        tg = jnp.where(lane == float(k), ex[k] / den, tg)
    te_ref[...] = te.astype(jnp.int32)
    tg_ref[...] = tg


def _merge(h, o_sb, o_d, gates, w_sb, w_d, w_o, g_ffn, w_r, b_r, tm):
    r, d = h.shape
    row = lambda w: pl.BlockSpec((tm, w), lambda i: (i, 0))
    full = lambda a, b: pl.BlockSpec((a, b), lambda i: (0, 0))
    return pl.pallas_call(
        _merge_kernel,
        grid=(r // tm,),
        in_specs=[row(d), row(d), row(d), row(2 * d), full(d, d), full(d, d), full(d, d),
                  full(1, d), full(d, LANES), full(1, LANES)],
        out_specs=[row(d), row(d), row(LANES), row(LANES)],
        out_shape=[jax.ShapeDtypeStruct((r, d), F32), jax.ShapeDtypeStruct((r, d), F32),
                   jax.ShapeDtypeStruct((r, LANES), jnp.int32), jax.ShapeDtypeStruct((r, LANES), F32)],
        compiler_params=_cparams(("parallel",)),
        name="merge_router",
    )(h, o_sb, o_d, gates, w_sb, w_d, w_o, g_ffn.reshape(1, d), w_r, b_r)


def _moe_kernel(be_ref, nv_ref, x_ref, wgu_ref, bgu_ref, wdn_ref, bdn_ref, o_ref, wgu_bf, wdn_bf):
    b = pl.program_id(0)
    used = nv_ref[b] > 0
    new_expert = jnp.logical_or(b == 0, be_ref[b] != be_ref[jnp.maximum(b - 1, 0)])

    @pl.when(jnp.logical_and(used, new_expert))
    def _():
        wgu_bf[...] = wgu_ref[0].astype(BF16)
        wdn_bf[...] = wdn_ref[0].astype(BF16)

    @pl.when(used)
    def _():
        row = lax.broadcasted_iota(jnp.int32, x_ref.shape, 0)
        x = jnp.where(row < nv_ref[b], x_ref[...], 0.0).astype(BF16)
        gu = jnp.dot(x, wgu_bf[...], preferred_element_type=F32) + bgu_ref[0]
        g = jnp.minimum(gu[:, :D_FF], SWIGLU_LIMIT)
        u = jnp.clip(gu[:, D_FF:], -SWIGLU_LIMIT, SWIGLU_LIMIT)
        act = (u + 1.0) * (g * jax.nn.sigmoid(SWIGLU_ALPHA * g))
        o_ref[...] = jnp.dot(act.astype(BF16), wdn_bf[...], preferred_element_type=F32) + bdn_ref[0]

    @pl.when(jnp.logical_not(used))
    def _():
        o_ref[...] = jnp.zeros_like(o_ref)


def _moe_gmm(block_e, n_valid, xs, w_gu, b_gu, w_dn, b_dn, bm):
    nb = block_e.shape[0]
    n_rows = nb * bm
    d = xs.shape[1]
    grid_spec = pltpu.PrefetchScalarGridSpec(
        num_scalar_prefetch=2,
        grid=(nb,),
        in_specs=[
            pl.BlockSpec((bm, d), lambda b, be, nu: (b, 0)),
            pl.BlockSpec((1, d, 2 * D_FF), lambda b, be, nu: (be[b], 0, 0)),
            pl.BlockSpec((1, 1, 2 * D_FF), lambda b, be, nu: (be[b], 0, 0)),
            pl.BlockSpec((1, D_FF, d), lambda b, be, nu: (be[b], 0, 0)),
            pl.BlockSpec((1, 1, d), lambda b, be, nu: (be[b], 0, 0)),
        ],
        out_specs=pl.BlockSpec((bm, d), lambda b, be, nu: (b, 0)),
        scratch_shapes=[pltpu.VMEM((d, 2 * D_FF), BF16), pltpu.VMEM((D_FF, d), BF16)],
    )
    return pl.pallas_call(
        _moe_kernel,
        grid_spec=grid_spec,
        out_shape=jax.ShapeDtypeStruct((n_rows, d), F32),
        compiler_params=_cparams(("arbitrary",)),
        name="moe_experts",
    )(block_e, n_valid, xs, w_gu, b_gu, w_dn, b_dn)


def _final_kernel(h_ref, c_ref, tg_ref, g_ref, o_ref):
    tm = h_ref.shape[0]
    tg = tg_ref[...]
    y = c_ref[0:tm, :] * tg[:, 0:1]
    for k in range(1, TOP_K):
        y = y + c_ref[k * tm:(k + 1) * tm, :] * tg[:, k:k + 1]
    x = h_ref[...] + y
    o_ref[...] = (x * lax.rsqrt(jnp.mean(x * x, axis=-1, keepdims=True) + NORM_EPS)) * g_ref[...]


def _final(h2, contrib, tg, g, tm, c_row0=0):
    r, d = h2.shape
    assert c_row0 % (TOP_K * tm) == 0
    c_blk0 = c_row0 // (TOP_K * tm)
    return pl.pallas_call(
        _final_kernel,
        grid=(r // tm,),
        in_specs=[pl.BlockSpec((tm, d), lambda i: (i, 0)),
                  pl.BlockSpec((TOP_K * tm, d), lambda i: (i + c_blk0, 0)),
                  pl.BlockSpec((tm, LANES), lambda i: (i, 0)),
                  pl.BlockSpec((1, d), lambda i: (0, 0))],
        out_specs=pl.BlockSpec((tm, d), lambda i: (i, 0)),
        out_shape=jax.ShapeDtypeStruct((r, d), F32),
        compiler_params=_cparams(("parallel",)),
        name="combine_final_norm",
    )(h2, contrib, tg, g.reshape(1, d))


def _rank_kernel(te_ref, tril_ref, rank_ref, cnt_ref, run_ref):
    @pl.when(pl.program_id(0) == 0)
    def _():
        run_ref[...] = jnp.zeros_like(run_ref)

    te = te_ref[...]
    tm = te.shape[0]
    lane = lax.broadcasted_iota(jnp.int32, (tm, LANES), 1)
    onehots = [(lane == te[:, k:k + 1]).astype(F32) for k in range(TOP_K)]
    member = onehots[0] + onehots[1] + onehots[2] + onehots[3]
    before = jnp.dot(tril_ref[...], member.astype(BF16), preferred_element_type=F32) + run_ref[...]
    rank = jnp.zeros((tm, LANES), F32)
    for k in range(TOP_K):
        r_k = jnp.sum(onehots[k] * before, axis=-1, keepdims=True)
        rank = jnp.where(lane == k, r_k, rank)
    rank_ref[...] = rank.astype(jnp.int32)
    run_ref[...] += jnp.sum(member, axis=0, keepdims=True)
    cnt_ref[...] = run_ref[...]


def _ranks(te, tm):
    r = te.shape[0]
    i = lax.broadcasted_iota(jnp.int32, (tm, tm), 0)
    j = lax.broadcasted_iota(jnp.int32, (tm, tm), 1)
    tril = (j < i).astype(BF16)
    return pl.pallas_call(
        _rank_kernel,
        grid=(r // tm,),
        in_specs=[pl.BlockSpec((tm, LANES), lambda b: (b, 0)), pl.BlockSpec((tm, tm), lambda b: (0, 0))],
        out_specs=[pl.BlockSpec((tm, LANES), lambda b: (b, 0)), pl.BlockSpec((1, LANES), lambda b: (0, 0))],
        out_shape=[jax.ShapeDtypeStruct((r, LANES), jnp.int32), jax.ShapeDtypeStruct((1, LANES), F32)],
        scratch_shapes=[pltpu.VMEM((1, LANES), F32)],
        compiler_params=_cparams(("arbitrary",)),
        name="route_ranks",
    )(te, tril)


def _route(te, n_rows, bm):
    rank, cnt = _ranks(te, Q_TILE_PROMPT)
    counts = cnt[0, :N_EXPERTS].astype(jnp.int32)
    padded = ((counts + bm - 1) // bm) * bm
    pend = jnp.cumsum(padded)
    pstart = pend - padded
    top_e = te[:, :TOP_K]
    onehot = top_e[:, :, None] == jnp.arange(N_EXPERTS, dtype=jnp.int32)[None, None, :]
    pos = jnp.sum(jnp.where(onehot, pstart[None, None, :], 0), axis=-1) + rank[:, :TOP_K]
    nb = n_rows // bm
    block_start = jnp.arange(nb, dtype=jnp.int32) * bm
    block_e = jnp.minimum(jnp.sum((pend[None, :] <= block_start[:, None]).astype(jnp.int32), axis=1),
                          N_EXPERTS - 1)
    e_hot = block_e[:, None] == jnp.arange(N_EXPERTS, dtype=jnp.int32)[None, :]
    last = jnp.sum(jnp.where(e_hot, (pstart + counts)[None, :], 0), axis=1)
    n_valid = jnp.clip(last - block_start, 0, bm).astype(jnp.int32)
    return pos, block_e.astype(jnp.int32), n_valid


def _sc_scatter_rows(src, idx_t, n_out):
    n_src, d = src.shape
    sub = SC_GATHER_WINDOW // SC_SUB_WINDOWS
    n_tiles = n_src // SC_GATHER_WINDOW
    idx_t = idx_t.reshape(1, TOP_K * n_src)
    mesh = plsc.VectorSubcoreMesh(core_axis_name="c", subcore_axis_name="s")

    @pl.kernel(out_type=jax.ShapeDtypeStruct((n_out, d), src.dtype), mesh=mesh)
    def scatter_kernel(x_hbm, i_hbm, o_hbm):
        def body(x_vmem, i_vmem):
            s = pl.program_id(2)
            pltpu.sync_copy(x_vmem, o_hbm.at[i_vmem.at[0, pl.ds(s * sub, sub)]])

        pltpu.emit_pipeline(
            body,
            grid=(n_tiles, TOP_K, SC_SUB_WINDOWS),
            in_specs=[pl.BlockSpec((sub, d), index_map=lambda i, k, s: (i * SC_SUB_WINDOWS + s, 0)),
                      pl.BlockSpec((1, SC_GATHER_WINDOW), index_map=lambda i, k, s: (0, k * n_tiles + i))],
            out_specs=[],
            core_axis_name=("c", "s"),
            dimension_semantics=(pltpu.PARALLEL, pltpu.ARBITRARY, pltpu.ARBITRARY),
        )(x_hbm, i_hbm)

    return scatter_kernel(src, idx_t)


def _sc_gather(table, idx):
    n_idx = idx.shape[0]
    d = table.shape[1]
    sub = SC_GATHER_WINDOW // SC_SUB_WINDOWS
    mesh = plsc.VectorSubcoreMesh(core_axis_name="c", subcore_axis_name="s")

    @pl.kernel(out_type=jax.ShapeDtypeStruct((n_idx, d), table.dtype), mesh=mesh)
    def gather_kernel(x_hbm, i_hbm, o_hbm):
        def body(i_vmem, o_vmem):
            j = pl.program_id(1)
            pltpu.sync_copy(x_hbm.at[i_vmem.at[0, pl.ds(j * sub, sub)]], o_vmem)

        pltpu.emit_pipeline(
            body,
            grid=(n_idx // SC_GATHER_WINDOW, SC_SUB_WINDOWS),
            in_specs=[pl.BlockSpec((1, SC_GATHER_WINDOW), index_map=lambda i, j: (0, i))],
            out_specs=[pl.BlockSpec((sub, d), index_map=lambda i, j: (i * SC_SUB_WINDOWS + j, 0))],
            core_axis_name=("c", "s"),
            dimension_semantics=(pltpu.PARALLEL, pltpu.ARBITRARY),
        )(i_hbm, o_hbm)

    return gather_kernel(table, idx.reshape(1, n_idx))


def _tri(kt):
    s = lax.broadcasted_iota(jnp.int32, (kt, kt), 0)
    j = lax.broadcasted_iota(jnp.int32, (kt, kt), 1)
    return (j > s).astype(BF16)


def kernel(x_prompt, x_sample, cache_sb_k, cache_sb_v, cache_diff_k, cache_diff_v, meta_tokens,
           norm_mix_g, w_in, diff_lambda, diff_subln_g, w_br_sb, w_br_diff, w_out, norm_ffn_g,
           w_router, b_router, w_gate_up, b_gate_up, w_down, b_down, final_norm_g):
    assert x_prompt.shape[0] == 1 and w_in.shape[0] == 1
    d = D_MODEL
    seq = x_prompt.shape[1]
    t = N_META + seq
    tq_p, kt = Q_TILE_PROMPT, KEY_TILE
    tp = -(-t // tq_p) * tq_p
    nq_p = tp // tq_p
    nb_s, s_len = x_sample.shape[:2]
    past = cache_sb_k.shape[2]
    n_s = nb_s * s_len
    scale = DH_SB ** -0.5

    w0 = w_in[0]
    wq = jnp.concatenate([w0[:, 0:SEC], w0[:, 3 * SEC:4 * SEC]], axis=1).astype(BF16)
    wk = jnp.concatenate([w0[:, SEC:2 * SEC], w0[:, 4 * SEC:5 * SEC]], axis=1).astype(BF16)
    wv = jnp.concatenate([w0[:, 2 * SEC:3 * SEC], w0[:, 5 * SEC:6 * SEC]], axis=1).astype(BF16)
    wg = w0[:, 6 * SEC:8 * SEC].astype(BF16)
    w_sb = w_br_sb[0].astype(BF16)
    w_d = w_br_diff[0].astype(BF16)
    w_o = w_out[0].astype(BF16)
    w_r = jnp.pad(w_router[0], ((0, 0), (0, LANES - N_EXPERTS)))
    b_r = jnp.pad(b_router[0], (0, LANES - N_EXPERTS), constant_values=NEG_INF).reshape(1, LANES)
    w_gu = w_gate_up[0]
    w_dn = w_down[0]
    b_gu = b_gate_up[0].reshape(N_EXPERTS, 1, 2 * D_FF)
    b_dn = b_down[0].reshape(N_EXPERTS, 1, d)
    slope = jnp.exp2(-8.0 * jnp.arange(1, H_DIFF + 1, dtype=F32) / H_DIFF)
    slopes = jnp.stack([slope, 1.0 / slope])
    tri = _tri(kt)
    lam_p = diff_lambda[0]
    g_sub = diff_subln_g[0]

    hp = jnp.concatenate([meta_tokens.astype(F32), x_prompt[0], jnp.zeros((tp - t, d), F32)], axis=0)
    tm_p = tq_p
    xn = _rmsnorm(hp, norm_mix_g[0], tm_p)
    q_sb, q_d = _proj(xn, wq, "q", tm_p, scale=scale)
    kf_sb, kb_sb, kf_d, kb_d = _proj(xn, wk, "k", tm_p, r_out=t)
    vf_sb, vt_sb, vf_d, vt_d = _proj(xn, wv, "v", tm_p, r_out=t)
    (gates,) = _proj(xn, wg, "g", tm_p)
    o_sb = _sb_attention(q_sb[None], kb_sb[None], vt_sb[None], tri, batch=1, nq=nq_p, tq=tq_p, kt=kt,
                         q_pos0=0)
    o_d = _diff_attention(q_d[None], kb_d[None], vt_d[None], slopes, lam_p, g_sub, batch=1,
                          nq=-(-tp // Q_TILE_DIFF), tq=Q_TILE_DIFF, kt=kt, q_pos0=0, coff=N_META, n_keys=t,
                          early_stop=True)
    tm_merge = 640 if tp % 640 == 0 else tq_p
    h2, hn, te, tg = _merge(hp, o_sb[0], o_d[0], gates, w_sb, w_d, w_o, norm_ffn_g[0], w_r, b_r, tm_merge)

    tq_s = Q_TILE_DECODE
    hs = x_sample.reshape(n_s, d)
    xn_s = _rmsnorm(hs, norm_mix_g[0], n_s)
    q2_s = _proj(xn_s, wq, "q", n_s, scale=scale)
    kf_s = _proj(xn_s, wk, "f", n_s)
    vf_s = _proj(xn_s, wv, "f", n_s)
    (gates_s,) = _proj(xn_s, wg, "g", n_s)

    def pad_q(qs):
        return jnp.pad(qs.reshape(nb_s, s_len, SEC), ((0, 0), (0, tq_s - s_len), (0, 0)))

    def per_stream(a):
        return a.reshape(nb_s, -1, SEC)

    def keys_last(cache):
        return jnp.transpose(cache, (0, 2, 3, 1)).reshape(nb_s, SEC, past)

    k_sb_all, vt_sb_all = _cache_tiles(keys_last(cache_sb_k[0]), keys_last(cache_sb_v[0]),
                                       per_stream(kf_s[0]), per_stream(vf_s[0]), kt, True)
    k_d_all, vt_d_all = _cache_tiles(cache_diff_k[0], cache_diff_v[0], per_stream(kf_s[1]),
                                     per_stream(vf_s[1]), kt, False)
    o_sb_s = _sb_attention(pad_q(q2_s[0]), k_sb_all, vt_sb_all, tri, batch=nb_s, nq=1, tq=tq_s, kt=kt,
                           q_pos0=past)
    o_d_s = _diff_attention(pad_q(q2_s[1]), k_d_all, vt_d_all, slopes, lam_p, g_sub, batch=nb_s, nq=1,
                            tq=tq_s, kt=kt, q_pos0=past, coff=0, n_keys=past + s_len, early_stop=False)
    o_sb_s = o_sb_s[:, :s_len].reshape(n_s, SEC)
    o_d_s = o_d_s[:, :s_len].reshape(n_s, SEC)
    h2_s, hn_s, te_s, tg_s = _merge(hs, o_sb_s, o_d_s, gates_s, w_sb, w_d, w_o, norm_ffn_g[0], w_r, b_r,
                                    n_s)

    n_tok = t + n_s
    r_tok = -(-n_tok // tq_p) * tq_p
    bm = MOE_ROWS
    n_rows = -(-(n_tok * TOP_K + N_EXPERTS * (bm - 1)) // bm) * bm
    te_all = jnp.concatenate([te[:t], te_s, jnp.full((r_tok - n_tok, LANES), -1, jnp.int32)], axis=0)
    hn_all = jnp.concatenate([hn[:t], hn_s, jnp.zeros((r_tok - n_tok, d), F32)], axis=0)
    pos, block_e, n_valid = _route(te_all, n_rows, bm)
    pos_d = jnp.where(te_all[:, :1] >= 0, pos, n_rows)
    xs = _sc_scatter_rows(hn_all, pos_d.T, n_rows + 8)
    yb = _moe_gmm(block_e, n_valid, xs, w_gu, b_gu, w_dn, b_dn, bm)
    def by_block(p, tm):
        return p.reshape(-1, tm, TOP_K).transpose(0, 2, 1).reshape(-1)

    pos_p = by_block(jnp.concatenate([pos[:t], jnp.zeros((tp - t, TOP_K), jnp.int32)], axis=0), tq_p)
    pos_s = by_block(pos[t:n_tok], n_s)
    g_mult = SC_GATHER_WINDOW * 8
    g_pad = (-(pos_p.shape[0] + pos_s.shape[0])) % g_mult
    contrib = _sc_gather(yb, jnp.concatenate([pos_p, pos_s, jnp.zeros((g_pad,), jnp.int32)]))
    y_p = _final(h2, contrib, tg, final_norm_g, tq_p)
    y_s = _final(h2_s, contrib, tg_s, final_norm_g, n_s, c_row0=tp * TOP_K)

    y_prompt = y_p[N_META:t][None]
    y_sample = y_s.reshape(nb_s, s_len, d)

    def heads(a, nh):
        return a.reshape(1, *a.shape[:-1], nh, a.shape[-1] // nh)

    return (y_prompt, y_sample,
            heads(kf_sb[None], H_SB), heads(vf_sb[None], H_SB),
            heads(kf_d[None], H_DIFF), heads(vf_d[None], H_DIFF),
            heads(kf_s[0].reshape(nb_s, s_len, SEC), H_SB), heads(vf_s[0].reshape(nb_s, s_len, SEC), H_SB),
            heads(kf_s[1].reshape(nb_s, s_len, SEC), H_DIFF), heads(vf_s[1].reshape(nb_s, s_len, SEC), H_DIFF))
```
